```python
import jax, jax.numpy as jnp
from jax import lax
import numpy as np

D_MODEL = 2048
BATCH = 8
SEQ = 4096
DEPTH = 1

N_META = 16
HEAD_DIM = 64
RWKV_HEADS = 16
RWKV_WIDTH = RWKV_HEADS * HEAD_DIM
FOX_HEADS = 16
FOX_WIDTH = FOX_HEADS * HEAD_DIM
DECAY_LORA = 96
AAA_LORA = 96
GATE_LORA = 256
D_FF = -(-8 * D_MODEL // (3 * 256)) * 256
Q_BLOCK = 128
RMS_EPS = 1e-6
GN_EPS = 64e-5
ATTN_SCALE = HEAD_DIM ** -0.5

RWKV_SPLITS = (RWKV_WIDTH, 2 * RWKV_WIDTH, 3 * RWKV_WIDTH,
               3 * RWKV_WIDTH + DECAY_LORA, 3 * RWKV_WIDTH + DECAY_LORA + AAA_LORA)
RWKV_COLS = 3 * RWKV_WIDTH + DECAY_LORA + AAA_LORA + GATE_LORA
FOX_SPLITS = (FOX_WIDTH, 2 * FOX_WIDTH, 3 * FOX_WIDTH)
FOX_COLS = 3 * FOX_WIDTH + FOX_HEADS
N_IN = RWKV_COLS + FOX_COLS + 2 * D_MODEL

kernel_name = 'hybrid_rwkv7_fox_meta_gated_block'

F32 = jnp.float32


def _rms(x, g):
    xf = x.astype(F32)
    xf = xf * lax.rsqrt(jnp.mean(xf * xf, axis=-1, keepdims=True) + RMS_EPS)
    return xf.astype(x.dtype) * g


def _wkv7_scan(r, w, k, v, a, b):
    B, L, H, N = r.shape

    def step(S, inp):
        r_t, w_t, k_t, v_t, a_t, b_t = inp
        sa = jnp.einsum('bhvk,bhk->bhv', S, a_t)
        S = (S * w_t[:, :, None, :] + sa[..., None] * b_t[:, :, None, :]
             + v_t[..., None] * k_t[:, :, None, :])
        return S, jnp.einsum('bhvk,bhk->bhv', S, r_t)

    xs = tuple(jnp.swapaxes(t, 0, 1) for t in (r, w, k, v, a, b))
    _, y = lax.scan(step, jnp.zeros((B, H, N, N), F32), xs)
    return jnp.swapaxes(y, 0, 1)


def _rwkv7(z, w0, w2, a0, a2, g2, k_k, k_a, r_k, gn_w, gn_b):
    B, L, _ = z.shape
    r, k, v, wd, ad, gd = jnp.split(z, RWKV_SPLITS, axis=-1)
    w_log = -jax.nn.softplus(-(w0 + jnp.tanh(wd) @ w2).astype(F32)) - 0.5
    decay = jnp.exp(-jnp.exp(w_log))
    a = jax.nn.sigmoid((a0 + ad @ a2).astype(F32))
    g = jax.nn.sigmoid(gd) @ g2
    heads = lambda t: t.reshape(B, L, RWKV_HEADS, HEAD_DIM)
    kk = heads((k * k_k).astype(F32))
    kk = kk / jnp.maximum(jnp.sqrt(jnp.sum(kk * kk, axis=-1, keepdims=True)), 1e-12)
    kf = k.astype(F32) * (1.0 + (a - 1.0) * k_a.astype(F32))
    rh, kh, vh, ah, dh = heads(r.astype(F32)), heads(kf), heads(v.astype(F32)), heads(a), heads(decay)
    y = _wkv7_scan(rh, dh, kh, vh, -kk, kk * ah)
    mu = jnp.mean(y, axis=-1, keepdims=True)
    var = jnp.mean(jnp.square(y - mu), axis=-1, keepdims=True)
    y = (y - mu) * lax.rsqrt(var + GN_EPS)
    y = y * gn_w.astype(F32).reshape(RWKV_HEADS, HEAD_DIM) + gn_b.astype(F32).reshape(RWKV_HEADS, HEAD_DIM)
    y = y + jnp.sum(rh * kh * r_k.astype(F32), axis=-1, keepdims=True) * vh
    return y.reshape(B, L, RWKV_WIDTH).astype(z.dtype) * g


def _fox(z, q_g, k_g, f_bias):
    B, L, _ = z.shape
    q, k, v, fl = jnp.split(z, FOX_SPLITS, axis=-1)
    q = _rms(q.reshape(B, L, FOX_HEADS, HEAD_DIM), q_g)
    k = _rms(k.reshape(B, L, FOX_HEADS, HEAD_DIM), k_g)
    v = v.reshape(B, L, FOX_HEADS, HEAD_DIM)
    log_f = jax.nn.log_sigmoid(fl.astype(F32) + f_bias.astype(F32))
    c = jnp.swapaxes(jnp.cumsum(log_f, axis=1), 1, 2)
    bounds = [(0, N_META)] + [(s, min(s + Q_BLOCK, L)) for s in range(N_META, L, Q_BLOCK)]
    outs = []
    for s, e in bounds:
        sc = jnp.einsum('bqhd,bkhd->bhqk', q[:, s:e], k[:, :e]).astype(F32) * ATTN_SCALE
        sc = sc + c[:, :, s:e, None] - c[:, :, None, :e]
        causal = jnp.arange(s, e)[:, None] >= jnp.arange(e)[None, :]
        p = jax.nn.softmax(jnp.where(causal, sc, -jnp.inf), axis=-1)
        outs.append(jnp.einsum('bhqk,bkhd->bqhd', p.astype(v.dtype), v[:, :e]))
    return jnp.concatenate(outs, axis=1).reshape(B, L, FOX_WIDTH)


def _layer(h, n1, w_in, mu, w0, w2, a0, a2, g2, k_k, k_a, r_k, gn_w, gn_b,
           q_g, k_g, f_bias, w_a, w_b, w_o, n2, w_gu, w_dn):
    xn = _rms(h, n1)
    proj = xn @ w_in
    z_rwkv, z_fox, z_gate = jnp.split(proj, (RWKV_COLS, RWKV_COLS + FOX_COLS), axis=-1)
    z_prev = jnp.pad(z_rwkv, ((0, 0), (1, 0), (0, 0)))[:, :-1]
    z_rwkv = z_rwkv + (z_prev - z_rwkv) * mu
    y_a = _rwkv7(z_rwkv, w0, w2, a0, a2, g2, k_k, k_a, r_k, gn_w, gn_b)
    y_b = _fox(z_fox, q_g, k_g, f_bias)
    gates = jax.nn.sigmoid(z_gate.astype(F32)).astype(h.dtype)
    g_a, g_b = jnp.split(gates, 2, axis=-1)
    merged = g_a * (y_a @ w_a) + g_b * (y_b @ w_b)
    h = h + merged @ w_o
    gate, up = jnp.split(_rms(h, n2) @ w_gu, 2, axis=-1)
    return h + (jax.nn.silu(gate) * up) @ w_dn


def _fwd_setup_inputs(seed: int = 0) -> dict:
    key = jax.random.key(seed)
    ks = jax.random.split(key, 24)
    nrm = lambda k, shape, scale: jax.random.normal(k, shape, F32) * scale
    Dp = DEPTH
    return {
        'x': nrm(ks[0], (BATCH, SEQ, D_MODEL), 1.0),
        'meta_tokens': nrm(ks[1], (N_META, D_MODEL), 1.0),
        'norm1_g': 1.0 + nrm(ks[2], (Dp, D_MODEL), 0.05),
        'w_in': nrm(ks[3], (Dp, D_MODEL, N_IN), D_MODEL ** -0.5),
        'rwkv_mu': jax.random.uniform(ks[4], (Dp, RWKV_COLS), F32, 0.0, 1.0),
        'rwkv_w0': jax.random.uniform(ks[5], (Dp, RWKV_WIDTH), F32, -5.0, -1.0),
        'rwkv_w2': nrm(ks[6], (Dp, DECAY_LORA, RWKV_WIDTH), 0.5 * DECAY_LORA ** -0.5),
        'rwkv_a0': nrm(ks[7], (Dp, RWKV_WIDTH), 0.1),
        'rwkv_a2': nrm(ks[8], (Dp, AAA_LORA, RWKV_WIDTH), 0.5 * AAA_LORA ** -0.5),
        'rwkv_g2': nrm(ks[9], (Dp, GATE_LORA, RWKV_WIDTH), GATE_LORA ** -0.5),
        'rwkv_k_k': 0.85 + nrm(ks[10], (Dp, RWKV_WIDTH), 0.05),
        'rwkv_k_a': 1.0 + nrm(ks[11], (Dp, RWKV_WIDTH), 0.05),
        'rwkv_r_k': nrm(ks[12], (Dp, RWKV_HEADS, HEAD_DIM), 0.1),
        'rwkv_gn_w': 1.0 + nrm(ks[13], (Dp, RWKV_WIDTH), 0.05),
        'rwkv_gn_b': nrm(ks[14], (Dp, RWKV_WIDTH), 0.01),
        'fox_q_norm_g': 1.0 + nrm(ks[15], (Dp, HEAD_DIM), 0.05),
        'fox_k_norm_g': 1.0 + nrm(ks[16], (Dp, HEAD_DIM), 0.05),
        'fox_f_bias': jax.random.uniform(ks[17], (Dp, FOX_HEADS), F32, 1.0, 4.0),
        'w_branch_a': nrm(ks[18], (Dp, RWKV_WIDTH, D_MODEL), RWKV_WIDTH ** -0.5),
        'w_branch_b': nrm(ks[19], (Dp, FOX_WIDTH, D_MODEL), FOX_WIDTH ** -0.5),
        'w_o': nrm(ks[20], (Dp, D_MODEL, D_MODEL), D_MODEL ** -0.5),
        'norm2_g': 1.0 + nrm(ks[21], (Dp, D_MODEL), 0.05),
        'w_gate_up': nrm(ks[22], (Dp, D_MODEL, 2 * D_FF), D_MODEL ** -0.5),
        'w_down': nrm(ks[23], (Dp, D_FF, D_MODEL), D_FF ** -0.5),
    }


def _fwd_reference(x, meta_tokens, norm1_g, w_in, rwkv_mu, rwkv_w0, rwkv_w2, rwkv_a0, rwkv_a2,
              rwkv_g2, rwkv_k_k, rwkv_k_a, rwkv_r_k, rwkv_gn_w, rwkv_gn_b, fox_q_norm_g,
              fox_k_norm_g, fox_f_bias, w_branch_a, w_branch_b, w_o, norm2_g, w_gate_up, w_down):
    B = x.shape[0]
    meta = jnp.broadcast_to(meta_tokens.astype(x.dtype)[None], (B, N_META, D_MODEL))
    h = jnp.concatenate([meta, x], axis=1)
    for l in range(DEPTH):
        h = _layer(h, norm1_g[l], w_in[l], rwkv_mu[l], rwkv_w0[l], rwkv_w2[l], rwkv_a0[l],
                   rwkv_a2[l], rwkv_g2[l], rwkv_k_k[l], rwkv_k_a[l], rwkv_r_k[l], rwkv_gn_w[l],
                   rwkv_gn_b[l], fox_q_norm_g[l], fox_k_norm_g[l], fox_f_bias[l], w_branch_a[l],
                   w_branch_b[l], w_o[l], norm2_g[l], w_gate_up[l], w_down[l])
    return h[:, N_META:]


import jax as _jax
import jax.numpy as _jnp

TWIN_FORMAT = 'train_step'
FWD_PARAMS = ['x', 'meta_tokens', 'norm1_g', 'w_in', 'rwkv_mu', 'rwkv_w0', 'rwkv_w2', 'rwkv_a0', 'rwkv_a2', 'rwkv_g2', 'rwkv_k_k', 'rwkv_k_a', 'rwkv_r_k', 'rwkv_gn_w', 'rwkv_gn_b', 'fox_q_norm_g', 'fox_k_norm_g', 'fox_f_bias', 'w_branch_a', 'w_branch_b', 'w_o', 'norm2_g', 'w_gate_up', 'w_down']
TWIN_WEIGHTS = ['meta_tokens', 'norm1_g', 'w_in', 'rwkv_mu', 'rwkv_w0', 'rwkv_w2', 'rwkv_a0', 'rwkv_a2', 'rwkv_g2', 'rwkv_k_k', 'rwkv_k_a', 'rwkv_r_k', 'rwkv_gn_w', 'rwkv_gn_b', 'fox_q_norm_g', 'fox_k_norm_g', 'fox_f_bias', 'w_branch_a', 'w_branch_b', 'w_o', 'norm2_g', 'w_gate_up', 'w_down']
TWIN_DIFF_INPUT = 'x'
TWIN_INPUTS = ['x', 'meta_tokens', 'norm1_g', 'w_in', 'rwkv_mu', 'rwkv_w0', 'rwkv_w2', 'rwkv_a0', 'rwkv_a2', 'rwkv_g2', 'rwkv_k_k', 'rwkv_k_a', 'rwkv_r_k', 'rwkv_gn_w', 'rwkv_gn_b', 'fox_q_norm_g', 'fox_k_norm_g', 'fox_f_bias', 'w_branch_a', 'w_branch_b', 'w_o', 'norm2_g', 'w_gate_up', 'w_down', 'loss_target', 'm_meta_tokens', 'm_norm1_g', 'm_w_in', 'm_rwkv_mu', 'm_rwkv_w0', 'm_rwkv_w2', 'm_rwkv_a0', 'm_rwkv_a2', 'm_rwkv_g2', 'm_rwkv_k_k', 'm_rwkv_k_a', 'm_rwkv_r_k', 'm_rwkv_gn_w', 'm_rwkv_gn_b', 'm_fox_q_norm_g', 'm_fox_k_norm_g', 'm_fox_f_bias', 'm_w_branch_a', 'm_w_branch_b', 'm_w_o', 'm_norm2_g', 'm_w_gate_up', 'm_w_down', 'v_meta_tokens', 'v_norm1_g', 'v_w_in', 'v_rwkv_mu', 'v_rwkv_w0', 'v_rwkv_w2', 'v_rwkv_a0', 'v_rwkv_a2', 'v_rwkv_g2', 'v_rwkv_k_k', 'v_rwkv_k_a', 'v_rwkv_r_k', 'v_rwkv_gn_w', 'v_rwkv_gn_b', 'v_fox_q_norm_g', 'v_fox_k_norm_g', 'v_fox_f_bias', 'v_w_branch_a', 'v_w_branch_b', 'v_w_o', 'v_norm2_g', 'v_w_gate_up', 'v_w_down']
TWIN_OUTPUTS = ['loss', 'grad_x', 'grad_meta_tokens', 'grad_norm1_g', 'grad_w_in', 'grad_rwkv_mu', 'grad_rwkv_w0', 'grad_rwkv_w2', 'grad_rwkv_a0', 'grad_rwkv_a2', 'grad_rwkv_g2', 'grad_rwkv_k_k', 'grad_rwkv_k_a', 'grad_rwkv_r_k', 'grad_rwkv_gn_w', 'grad_rwkv_gn_b', 'grad_fox_q_norm_g', 'grad_fox_k_norm_g', 'grad_fox_f_bias', 'grad_w_branch_a', 'grad_w_branch_b', 'grad_w_o', 'grad_norm2_g', 'grad_w_gate_up', 'grad_w_down', 'delta_meta_tokens', 'delta_norm1_g', 'delta_w_in', 'delta_rwkv_mu', 'delta_rwkv_w0', 'delta_rwkv_w2', 'delta_rwkv_a0', 'delta_rwkv_a2', 'delta_rwkv_g2', 'delta_rwkv_k_k', 'delta_rwkv_k_a', 'delta_rwkv_r_k', 'delta_rwkv_gn_w', 'delta_rwkv_gn_b', 'delta_fox_q_norm_g', 'delta_fox_k_norm_g', 'delta_fox_f_bias', 'delta_w_branch_a', 'delta_w_branch_b', 'delta_w_o', 'delta_norm2_g', 'delta_w_gate_up', 'delta_w_down', 'new_m_meta_tokens', 'new_m_norm1_g', 'new_m_w_in', 'new_m_rwkv_mu', 'new_m_rwkv_w0', 'new_m_rwkv_w2', 'new_m_rwkv_a0', 'new_m_rwkv_a2', 'new_m_rwkv_g2', 'new_m_rwkv_k_k', 'new_m_rwkv_k_a', 'new_m_rwkv_r_k', 'new_m_rwkv_gn_w', 'new_m_rwkv_gn_b', 'new_m_fox_q_norm_g', 'new_m_fox_k_norm_g', 'new_m_fox_f_bias', 'new_m_w_branch_a', 'new_m_w_branch_b', 'new_m_w_o', 'new_m_norm2_g', 'new_m_w_gate_up', 'new_m_w_down', 'new_v_meta_tokens', 'new_v_norm1_g', 'new_v_w_in', 'new_v_rwkv_mu', 'new_v_rwkv_w0', 'new_v_rwkv_w2', 'new_v_rwkv_a0', 'new_v_rwkv_a2', 'new_v_rwkv_g2', 'new_v_rwkv_k_k', 'new_v_rwkv_k_a', 'new_v_rwkv_r_k', 'new_v_rwkv_gn_w', 'new_v_rwkv_gn_b', 'new_v_fox_q_norm_g', 'new_v_fox_k_norm_g', 'new_v_fox_f_bias', 'new_v_w_branch_a', 'new_v_w_branch_b', 'new_v_w_o', 'new_v_norm2_g', 'new_v_w_gate_up', 'new_v_w_down']
TWIN_LEAF_KINDS = {'loss': 'loss', 'grad_x': 'grad_x', 'grad_meta_tokens': 'grad_w', 'grad_norm1_g': 'grad_w', 'grad_w_in': 'grad_w', 'grad_rwkv_mu': 'grad_w', 'grad_rwkv_w0': 'grad_w', 'grad_rwkv_w2': 'grad_w', 'grad_rwkv_a0': 'grad_w', 'grad_rwkv_a2': 'grad_w', 'grad_rwkv_g2': 'grad_w', 'grad_rwkv_k_k': 'grad_w', 'grad_rwkv_k_a': 'grad_w', 'grad_rwkv_r_k': 'grad_w', 'grad_rwkv_gn_w': 'grad_w', 'grad_rwkv_gn_b': 'grad_w', 'grad_fox_q_norm_g': 'grad_w', 'grad_fox_k_norm_g': 'grad_w', 'grad_fox_f_bias': 'grad_w', 'grad_w_branch_a': 'grad_w', 'grad_w_branch_b': 'grad_w', 'grad_w_o': 'grad_w', 'grad_norm2_g': 'grad_w', 'grad_w_gate_up': 'grad_w', 'grad_w_down': 'grad_w', 'delta_meta_tokens': 'delta_w', 'delta_norm1_g': 'delta_w', 'delta_w_in': 'delta_w', 'delta_rwkv_mu': 'delta_w', 'delta_rwkv_w0': 'delta_w', 'delta_rwkv_w2': 'delta_w', 'delta_rwkv_a0': 'delta_w', 'delta_rwkv_a2': 'delta_w', 'delta_rwkv_g2': 'delta_w', 'delta_rwkv_k_k': 'delta_w', 'delta_rwkv_k_a': 'delta_w', 'delta_rwkv_r_k': 'delta_w', 'delta_rwkv_gn_w': 'delta_w', 'delta_rwkv_gn_b': 'delta_w', 'delta_fox_q_norm_g': 'delta_w', 'delta_fox_k_norm_g': 'delta_w', 'delta_fox_f_bias': 'delta_w', 'delta_w_branch_a': 'delta_w', 'delta_w_branch_b': 'delta_w', 'delta_w_o': 'delta_w', 'delta_norm2_g': 'delta_w', 'delta_w_gate_up': 'delta_w', 'delta_w_down': 'delta_w', 'new_m_meta_tokens': 'new_m', 'new_m_norm1_g': 'new_m', 'new_m_w_in': 'new_m', 'new_m_rwkv_mu': 'new_m', 'new_m_rwkv_w0': 'new_m', 'new_m_rwkv_w2': 'new_m', 'new_m_rwkv_a0': 'new_m', 'new_m_rwkv_a2': 'new_m', 'new_m_rwkv_g2': 'new_m', 'new_m_rwkv_k_k': 'new_m', 'new_m_rwkv_k_a': 'new_m', 'new_m_rwkv_r_k': 'new_m', 'new_m_rwkv_gn_w': 'new_m', 'new_m_rwkv_gn_b': 'new_m', 'new_m_fox_q_norm_g': 'new_m', 'new_m_fox_k_norm_g': 'new_m', 'new_m_fox_f_bias': 'new_m', 'new_m_w_branch_a': 'new_m', 'new_m_w_branch_b': 'new_m', 'new_m_w_o': 'new_m', 'new_m_norm2_g': 'new_m', 'new_m_w_gate_up': 'new_m', 'new_m_w_down': 'new_m', 'new_v_meta_tokens': 'new_v', 'new_v_norm1_g': 'new_v', 'new_v_w_in': 'new_v', 'new_v_rwkv_mu': 'new_v', 'new_v_rwkv_w0': 'new_v', 'new_v_rwkv_w2': 'new_v', 'new_v_rwkv_a0': 'new_v', 'new_v_rwkv_a2': 'new_v', 'new_v_rwkv_g2': 'new_v', 'new_v_rwkv_k_k': 'new_v', 'new_v_rwkv_k_a': 'new_v', 'new_v_rwkv_r_k': 'new_v', 'new_v_rwkv_gn_w': 'new_v', 'new_v_rwkv_gn_b': 'new_v', 'new_v_fox_q_norm_g': 'new_v', 'new_v_fox_k_norm_g': 'new_v', 'new_v_fox_f_bias': 'new_v', 'new_v_w_branch_a': 'new_v', 'new_v_w_branch_b': 'new_v', 'new_v_w_o': 'new_v', 'new_v_norm2_g': 'new_v', 'new_v_w_gate_up': 'new_v', 'new_v_w_down': 'new_v'}


def _forward(args):
    return _fwd_reference(*[args[k] for k in FWD_PARAMS])


def _output_shape():
    def fwd():
        inp = _fwd_setup_inputs(0)
        return _fwd_reference(*[inp[k] for k in FWD_PARAMS])
    out = _jax.eval_shape(fwd)
    return out.shape, out.dtype

N_MICROBATCH = 1
ADAM_LR = 0.001
ADAM_B1 = 0.9
ADAM_B2 = 0.999
ADAM_EPS = 1e-08
ADAM_WD = 0.01
ADAM_STEP = 10
PER_EXAMPLE_BATCH_AXIS = {'x': 0, 'loss_target': 0}
SHARED_INPUTS = []
_WEIGHT_DTYPES = {'meta_tokens': _jnp.float32, 'norm1_g': _jnp.float32, 'w_in': _jnp.float32, 'rwkv_mu': _jnp.float32, 'rwkv_w0': _jnp.float32, 'rwkv_w2': _jnp.float32, 'rwkv_a0': _jnp.float32, 'rwkv_a2': _jnp.float32, 'rwkv_g2': _jnp.float32, 'rwkv_k_k': _jnp.float32, 'rwkv_k_a': _jnp.float32, 'rwkv_r_k': _jnp.float32, 'rwkv_gn_w': _jnp.float32, 'rwkv_gn_b': _jnp.float32, 'fox_q_norm_g': _jnp.float32, 'fox_k_norm_g': _jnp.float32, 'fox_f_bias': _jnp.float32, 'w_branch_a': _jnp.float32, 'w_branch_b': _jnp.float32, 'w_o': _jnp.float32, 'norm2_g': _jnp.float32, 'w_gate_up': _jnp.float32, 'w_down': _jnp.float32}
MOMENT_SCALE = {'meta_tokens': 6.831388e-03, 'norm1_g': 1.211523e+00, 'w_in': 6.324476e-02, 'rwkv_mu': 1.214848e+00, 'rwkv_w0': 4.320378e-02, 'rwkv_w2': 5.067400e-03, 'rwkv_a0': 1.547177e-01, 'rwkv_a2': 3.481021e-02, 'rwkv_g2': 2.467270e+00, 'rwkv_k_k': 8.751388e-02, 'rwkv_k_a': 3.122836e-01, 'rwkv_r_k': 1.882777e+00, 'rwkv_gn_w': 4.150972e+00, 'rwkv_gn_b': 8.284331e-01, 'fox_q_norm_g': 7.551591e+00, 'fox_k_norm_g': 7.528346e+00, 'fox_f_bias': 2.536208e+01, 'w_branch_a': 8.874107e-02, 'w_branch_b': 4.698631e-02, 'w_o': 9.062966e-02, 'norm2_g': 1.247322e+01, 'w_gate_up': 7.424182e-02, 'w_down': 1.147352e-01}


def _to_microbatches(a, axis):
    t = _jnp.moveaxis(a, axis, 0)
    t = t.reshape((N_MICROBATCH, t.shape[0] // N_MICROBATCH) + t.shape[1:])
    return _jnp.moveaxis(t, 1, axis + 1)


def setup_inputs(seed: int = 0) -> dict:
    inp = _fwd_setup_inputs(seed)
    key = _jax.random.fold_in(_jax.random.key(seed), 7919)
    shape, _ = _output_shape()
    out = dict(inp)
    out["loss_target"] = _jax.random.normal(_jax.random.fold_in(key, 0), shape, _jnp.float32)
    for i, name in enumerate(TWIN_WEIGHTS):
        w = inp[name].astype(_jnp.float32)
        if MOMENT_SCALE is None:
            s = _jnp.sqrt(_jnp.mean(_jnp.square(w)) + 1e-30)
        else:
            s = MOMENT_SCALE[name]
        km, kv = _jax.random.split(_jax.random.fold_in(key, i + 1))
        out[name] = w
        out["m_" + name] = s * _jax.random.normal(km, w.shape, _jnp.float32)
        out["v_" + name] = (s * s) * _jax.random.uniform(kv, w.shape, _jnp.float32, 0.5, 1.5)
    if N_MICROBATCH > 1:
        for name, axis in PER_EXAMPLE_BATCH_AXIS.items():
            out[name] = _to_microbatches(out[name], axis)
    return {'x': out['x'], 'meta_tokens': out['meta_tokens'], 'norm1_g': out['norm1_g'], 'w_in': out['w_in'], 'rwkv_mu': out['rwkv_mu'], 'rwkv_w0': out['rwkv_w0'], 'rwkv_w2': out['rwkv_w2'], 'rwkv_a0': out['rwkv_a0'], 'rwkv_a2': out['rwkv_a2'], 'rwkv_g2': out['rwkv_g2'], 'rwkv_k_k': out['rwkv_k_k'], 'rwkv_k_a': out['rwkv_k_a'], 'rwkv_r_k': out['rwkv_r_k'], 'rwkv_gn_w': out['rwkv_gn_w'], 'rwkv_gn_b': out['rwkv_gn_b'], 'fox_q_norm_g': out['fox_q_norm_g'], 'fox_k_norm_g': out['fox_k_norm_g'], 'fox_f_bias': out['fox_f_bias'], 'w_branch_a': out['w_branch_a'], 'w_branch_b': out['w_branch_b'], 'w_o': out['w_o'], 'norm2_g': out['norm2_g'], 'w_gate_up': out['w_gate_up'], 'w_down': out['w_down'], 'loss_target': out['loss_target'], 'm_meta_tokens': out['m_meta_tokens'], 'm_norm1_g': out['m_norm1_g'], 'm_w_in': out['m_w_in'], 'm_rwkv_mu': out['m_rwkv_mu'], 'm_rwkv_w0': out['m_rwkv_w0'], 'm_rwkv_w2': out['m_rwkv_w2'], 'm_rwkv_a0': out['m_rwkv_a0'], 'm_rwkv_a2': out['m_rwkv_a2'], 'm_rwkv_g2': out['m_rwkv_g2'], 'm_rwkv_k_k': out['m_rwkv_k_k'], 'm_rwkv_k_a': out['m_rwkv_k_a'], 'm_rwkv_r_k': out['m_rwkv_r_k'], 'm_rwkv_gn_w': out['m_rwkv_gn_w'], 'm_rwkv_gn_b': out['m_rwkv_gn_b'], 'm_fox_q_norm_g': out['m_fox_q_norm_g'], 'm_fox_k_norm_g': out['m_fox_k_norm_g'], 'm_fox_f_bias': out['m_fox_f_bias'], 'm_w_branch_a': out['m_w_branch_a'], 'm_w_branch_b': out['m_w_branch_b'], 'm_w_o': out['m_w_o'], 'm_norm2_g': out['m_norm2_g'], 'm_w_gate_up': out['m_w_gate_up'], 'm_w_down': out['m_w_down'], 'v_meta_tokens': out['v_meta_tokens'], 'v_norm1_g': out['v_norm1_g'], 'v_w_in': out['v_w_in'], 'v_rwkv_mu': out['v_rwkv_mu'], 'v_rwkv_w0': out['v_rwkv_w0'], 'v_rwkv_w2': out['v_rwkv_w2'], 'v_rwkv_a0': out['v_rwkv_a0'], 'v_rwkv_a2': out['v_rwkv_a2'], 'v_rwkv_g2': out['v_rwkv_g2'], 'v_rwkv_k_k': out['v_rwkv_k_k'], 'v_rwkv_k_a': out['v_rwkv_k_a'], 'v_rwkv_r_k': out['v_rwkv_r_k'], 'v_rwkv_gn_w': out['v_rwkv_gn_w'], 'v_rwkv_gn_b': out['v_rwkv_gn_b'], 'v_fox_q_norm_g': out['v_fox_q_norm_g'], 'v_fox_k_norm_g': out['v_fox_k_norm_g'], 'v_fox_f_bias': out['v_fox_f_bias'], 'v_w_branch_a': out['v_w_branch_a'], 'v_w_branch_b': out['v_w_branch_b'], 'v_w_o': out['v_w_o'], 'v_norm2_g': out['v_norm2_g'], 'v_w_gate_up': out['v_w_gate_up'], 'v_w_down': out['v_w_down']}


def _loss(weights, diff, rest, loss_target):
    with _jax.named_scope("forward"):
        args = {**rest, TWIN_DIFF_INPUT: diff, **{k: w.astype(_WEIGHT_DTYPES[k]) for k, w in weights.items()}}
        y = _forward(args)
    with _jax.named_scope("loss_head"):
        err = _jnp.square(y.astype(_jnp.float32) - loss_target)
        return 0.5 * _jnp.sum(_jnp.mean(err, axis=-1)) if err.ndim else 0.5 * err


def _adamw(w, g, m, v):
    m = ADAM_B1 * m + (1.0 - ADAM_B1) * g
    v = ADAM_B2 * v + (1.0 - ADAM_B2) * _jnp.square(g)
    m_hat = m / (1.0 - ADAM_B1 ** ADAM_STEP)
    v_hat = v / (1.0 - ADAM_B2 ** ADAM_STEP)
    delta = -ADAM_LR * (m_hat / (_jnp.sqrt(v_hat) + ADAM_EPS) + ADAM_WD * w)
    return delta, m, v


def reference(x, meta_tokens, norm1_g, w_in, rwkv_mu, rwkv_w0, rwkv_w2, rwkv_a0, rwkv_a2, rwkv_g2, rwkv_k_k, rwkv_k_a, rwkv_r_k, rwkv_gn_w, rwkv_gn_b, fox_q_norm_g, fox_k_norm_g, fox_f_bias, w_branch_a, w_branch_b, w_o, norm2_g, w_gate_up, w_down, loss_target, m_meta_tokens, m_norm1_g, m_w_in, m_rwkv_mu, m_rwkv_w0, m_rwkv_w2, m_rwkv_a0, m_rwkv_a2, m_rwkv_g2, m_rwkv_k_k, m_rwkv_k_a, m_rwkv_r_k, m_rwkv_gn_w, m_rwkv_gn_b, m_fox_q_norm_g, m_fox_k_norm_g, m_fox_f_bias, m_w_branch_a, m_w_branch_b, m_w_o, m_norm2_g, m_w_gate_up, m_w_down, v_meta_tokens, v_norm1_g, v_w_in, v_rwkv_mu, v_rwkv_w0, v_rwkv_w2, v_rwkv_a0, v_rwkv_a2, v_rwkv_g2, v_rwkv_k_k, v_rwkv_k_a, v_rwkv_r_k, v_rwkv_gn_w, v_rwkv_gn_b, v_fox_q_norm_g, v_fox_k_norm_g, v_fox_f_bias, v_w_branch_a, v_w_branch_b, v_w_o, v_norm2_g, v_w_gate_up, v_w_down):
    given = dict(x=x, meta_tokens=meta_tokens, norm1_g=norm1_g, w_in=w_in, rwkv_mu=rwkv_mu, rwkv_w0=rwkv_w0, rwkv_w2=rwkv_w2, rwkv_a0=rwkv_a0, rwkv_a2=rwkv_a2, rwkv_g2=rwkv_g2, rwkv_k_k=rwkv_k_k, rwkv_k_a=rwkv_k_a, rwkv_r_k=rwkv_r_k, rwkv_gn_w=rwkv_gn_w, rwkv_gn_b=rwkv_gn_b, fox_q_norm_g=fox_q_norm_g, fox_k_norm_g=fox_k_norm_g, fox_f_bias=fox_f_bias, w_branch_a=w_branch_a, w_branch_b=w_branch_b, w_o=w_o, norm2_g=norm2_g, w_gate_up=w_gate_up, w_down=w_down, loss_target=loss_target, m_meta_tokens=m_meta_tokens, m_norm1_g=m_norm1_g, m_w_in=m_w_in, m_rwkv_mu=m_rwkv_mu, m_rwkv_w0=m_rwkv_w0, m_rwkv_w2=m_rwkv_w2, m_rwkv_a0=m_rwkv_a0, m_rwkv_a2=m_rwkv_a2, m_rwkv_g2=m_rwkv_g2, m_rwkv_k_k=m_rwkv_k_k, m_rwkv_k_a=m_rwkv_k_a, m_rwkv_r_k=m_rwkv_r_k, m_rwkv_gn_w=m_rwkv_gn_w, m_rwkv_gn_b=m_rwkv_gn_b, m_fox_q_norm_g=m_fox_q_norm_g, m_fox_k_norm_g=m_fox_k_norm_g, m_fox_f_bias=m_fox_f_bias, m_w_branch_a=m_w_branch_a, m_w_branch_b=m_w_branch_b, m_w_o=m_w_o, m_norm2_g=m_norm2_g, m_w_gate_up=m_w_gate_up, m_w_down=m_w_down, v_meta_tokens=v_meta_tokens, v_norm1_g=v_norm1_g, v_w_in=v_w_in, v_rwkv_mu=v_rwkv_mu, v_rwkv_w0=v_rwkv_w0, v_rwkv_w2=v_rwkv_w2, v_rwkv_a0=v_rwkv_a0, v_rwkv_a2=v_rwkv_a2, v_rwkv_g2=v_rwkv_g2, v_rwkv_k_k=v_rwkv_k_k, v_rwkv_k_a=v_rwkv_k_a, v_rwkv_r_k=v_rwkv_r_k, v_rwkv_gn_w=v_rwkv_gn_w, v_rwkv_gn_b=v_rwkv_gn_b, v_fox_q_norm_g=v_fox_q_norm_g, v_fox_k_norm_g=v_fox_k_norm_g, v_fox_f_bias=v_fox_f_bias, v_w_branch_a=v_w_branch_a, v_w_branch_b=v_w_branch_b, v_w_o=v_w_o, v_norm2_g=v_norm2_g, v_w_gate_up=v_w_gate_up, v_w_down=v_w_down)
    weights = {n: given[n] for n in TWIN_WEIGHTS}
    shared = {n: given[n] for n in SHARED_INPUTS}
    per_example = {n: given[n] for n in ['x']}
    grad_fn = _jax.value_and_grad(_loss, argnums=(0, 1))

    def one_microbatch(ex, loss_target):
        ex = dict(ex)
        diff = ex.pop(TWIN_DIFF_INPUT)
        return grad_fn(weights, diff, {**shared, **ex}, loss_target)

    if N_MICROBATCH == 1:
        loss, (grad_w, grad_x) = one_microbatch(per_example, given["loss_target"])
    else:
        def body(carry, xs):
            loss_sum, grad_sum = carry
            l_k, (gw_k, gx_k) = one_microbatch(xs[0], xs[1])
            with _jax.named_scope("update"):
                return (loss_sum + l_k, _jax.tree.map(_jnp.add, grad_sum, gw_k)), gx_k

        init = (_jnp.zeros((), _jnp.float32), _jax.tree.map(_jnp.zeros_like, weights))
        (loss, grad_w), grad_x = _jax.lax.scan(body, init, (per_example, given["loss_target"]))
    with _jax.named_scope("update"):
        delta_w, new_m, new_v = {}, {}, {}
        for n in TWIN_WEIGHTS:
            delta_w[n], new_m[n], new_v[n] = _adamw(weights[n], grad_w[n], given["m_" + n], given["v_" + n])
    return (loss, grad_x, *[grad_w[n] for n in TWIN_WEIGHTS], *[delta_w[n] for n in TWIN_WEIGHTS],
            *[new_m[n] for n in TWIN_WEIGHTS], *[new_v[n] for n in TWIN_WEIGHTS])
```

```python
import functools

import jax
import jax.numpy as jnp
import numpy as np
from jax import lax
from jax.experimental import pallas as pl
from jax.experimental.pallas import tpu as pltpu

F32 = jnp.float32
BF16 = jnp.bfloat16
HI = lax.Precision.HIGHEST
MESH = pl.DeviceIdType.MESH
ANY = pl.BlockSpec(memory_space=pl.ANY)

N_META = 16
HEAD = 64
ROW_TILE = 128
PAD_ROWS = ROW_TILE - N_META
SEQ_ROW0 = ROW_TILE
CHUNK = 64
LANES = 128
PACK_W = 1024
RMS_EPS = 1e-6
GN_EPS = 64e-5
ATTN_SCALE = HEAD ** -0.5
NEG = -1e30
VMEM_LIMIT_V7X = 56 * 1024 * 1024

ADAM_LR = 0.001
ADAM_B1 = 0.9
ADAM_B2 = 0.999
ADAM_EPS = 1e-08
ADAM_WD = 0.01
ADAM_STEP = 10


def _params(sem=None):
    return pltpu.CompilerParams(dimension_semantics=sem, vmem_limit_bytes=VMEM_LIMIT_V7X)


def _pick(n, cands):
    for c in cands:
        if n % c == 0:
            return c
    return n


def _bf(t):
    return t.astype(BF16)


def _dotb(a, b):
    return jnp.dot(_bf(a), _bf(b), preferred_element_type=F32)


def _doth(a, b):
    return jnp.dot(a, b, precision=HI, preferred_element_type=F32)


_BIG = (2048, 1536, 1408, 1024, 768, 704, 512, 384, 256, 128)


def _matmul(a, b, mode, out_dtype, name, add=None):
    if mode == "nn":
        (M, R), (_, N) = a.shape, b.shape
        dims = (((1,), (0,)), ((), ()))
    elif mode == "nt":
        (M, R), (N, _) = a.shape, b.shape
        dims = (((1,), (1,)), ((), ()))
    else:
        (R, M), (_, N) = a.shape, b.shape
        dims = (((0,), (0,)), ((), ()))
    tm = _pick(M, (1408, 1024, 768, 512, 384, 256, 128))
    tn = _pick(N, (1024, 768, 512, 384, 256, 128)) if mode == "tn" else _pick(N, (512, 384, 256, 128))
    tr = _pick(R, (1408, 1056, 768, 512, 384, 256, 128)) if mode == "tn" else _pick(R, _BIG)
    nr = R // tr

    if mode == "nn":
        a_spec = pl.BlockSpec((tm, tr), lambda i, j, r: (i, r))
        b_spec = pl.BlockSpec((tr, tn), lambda i, j, r: (r, j))
    elif mode == "nt":
        a_spec = pl.BlockSpec((tm, tr), lambda i, j, r: (i, r))
        b_spec = pl.BlockSpec((tn, tr), lambda i, j, r: (j, r))
    else:
        a_spec = pl.BlockSpec((tr, tm), lambda i, j, r: (r, i))
        b_spec = pl.BlockSpec((tr, tn), lambda i, j, r: (r, j))
    o_spec = pl.BlockSpec((tm, tn), lambda i, j, r: (i, j))
    has_add = add is not None

    def body(*refs):
        if has_add:
            a_ref, b_ref, add_ref, o_ref, acc = refs
        else:
            a_ref, b_ref, o_ref, acc = refs
        r = pl.program_id(2)

        @pl.when(r == 0)
        def _():
            acc[...] = jnp.zeros_like(acc)

        acc[...] += lax.dot_general(_bf(a_ref[...]), _bf(b_ref[...]), dims, preferred_element_type=F32)

        @pl.when(r == nr - 1)
        def _():
            res = acc[...]
            if has_add:
                res = res + add_ref[...]
            o_ref[...] = res.astype(o_ref.dtype)

    ins = [a, b] + ([add] if has_add else [])
    specs = [a_spec, b_spec] + ([o_spec] if has_add else [])
    return pl.pallas_call(
        body, name=name, out_shape=jax.ShapeDtypeStruct((M, N), out_dtype),
        grid=(M // tm, N // tn, nr), in_specs=specs, out_specs=o_spec,
        scratch_shapes=[pltpu.VMEM((tm, tn), F32)],
        compiler_params=_params(("parallel", "parallel", "arbitrary")),
    )(*ins)


def _rowcall(name, fn, L, row_ins, bc_ins, row_outs, acc_outs, tm=ROW_TILE):
    nt = L // tm
    specs = []
    for arr, w, cb, kind in row_ins:
        if kind == "row":
            specs.append(pl.BlockSpec((tm, w), lambda i, cb=cb: (i, cb)))
        elif kind == "lag":
            specs.append(pl.BlockSpec((tm, w), lambda i, cb=cb: (jnp.maximum(i - 1, 0), cb)))
        elif kind == "prev":
            specs.append(pl.BlockSpec((8, w), lambda i, cb=cb: (jnp.maximum(i * (tm // 8) - 1, 0), cb)))
        else:
            specs.append(pl.BlockSpec((8, w), lambda i, cb=cb: (jnp.minimum((i + 1) * (tm // 8), L // 8 - 1), cb)))
    for arr in bc_ins:
        specs.append(pl.BlockSpec(arr.shape, lambda i, nd=arr.ndim: (0,) * nd))
    out_shapes = [jax.ShapeDtypeStruct((L, w), dt) for w, dt in row_outs]
    out_specs = [pl.BlockSpec((tm, w), lambda i: (i, 0)) for w, dt in row_outs]
    out_shapes += [jax.ShapeDtypeStruct(s, F32) for s in acc_outs]
    out_specs += [pl.BlockSpec(s, lambda i, nd=len(s): (0,) * nd) for s in acc_outs]
    n_row, n_bc, n_ro = len(row_ins), len(bc_ins), len(row_outs)

    def body(*refs):
        i = pl.program_id(0)
        vals = [r[...] for r in refs[: n_row + n_bc]]
        outs, sums = fn(i, vals[:n_row], vals[n_row:])
        o_refs = refs[n_row + n_bc:]
        for r, v in zip(o_refs[:n_ro], outs):
            r[...] = v.astype(r.dtype)

        @pl.when(i == 0)
        def _():
            for r in o_refs[n_ro:]:
                r[...] = jnp.zeros_like(r)

        for r, v in zip(o_refs[n_ro:], sums):
            r[...] += v

    res = pl.pallas_call(
        body, name=name, out_shape=out_shapes, grid=(nt,), in_specs=specs, out_specs=out_specs,
        compiler_params=_params(("arbitrary",)),
    )(*[a for a, _, _, _ in row_ins], *bc_ins)
    return list(res[:n_ro]), list(res[n_ro:])


def _rowsum(t):
    return jnp.sum(t, axis=0, keepdims=True)


def _head_mats(width):
    e = (np.arange(width)[:, None] // HEAD == np.arange(LANES)[None, :]).astype(np.float32)
    return jnp.asarray(e), jnp.asarray(e.T)


def _fold_mat(width):
    ft = (np.arange(HEAD)[:, None] == np.arange(width)[None, :] % HEAD).astype(np.float32)
    return jnp.asarray(ft)


def _rms_f(h, g):
    return (h * lax.rsqrt(jnp.mean(h * h, axis=-1, keepdims=True) + RMS_EPS)) * g


def _prep_f(xr, xk, xv, xl, w0, w2p, a0, a2p, g2p, k_k, k_a, e, et):
    w_log = -jax.nn.softplus(-(w0 + _dotb(jnp.tanh(xl), w2p))) - 0.5
    lw = -jnp.exp(w_log)
    a = jax.nn.sigmoid(a0 + _dotb(xl, a2p))
    g = _dotb(jax.nn.sigmoid(xl), g2p)
    kkr = xk * k_k
    inv = lax.rsqrt(jnp.maximum(_doth(kkr * kkr, e), 1e-24))
    kk = kkr * _doth(inv, et)
    kf = xk * (1.0 + (a - 1.0) * k_a)
    return xr, lw, kf, xv, -kk, kk * a, g


def _post_f(y, r, kf, v, g, gn_w, gn_b, r_k, e, et):
    mu = _doth(y, e) * (1.0 / HEAD)
    yc = y - _doth(mu, et)
    var = _doth(yc * yc, e) * (1.0 / HEAD)
    yn = yc * _doth(lax.rsqrt(var + GN_EPS), et) * gn_w + gn_b
    bonus = _doth(r * kf * r_k, e)
    return (yn + _doth(bonus, et) * v) * g


def _foxprep_f(q, k, fl, qg8, kg8, fb, e, et, ft, fmask):
    def norm(t, g8):
        ms = _doth(t * t, e) * (1.0 / HEAD)
        return t * _doth(lax.rsqrt(ms + RMS_EPS), et) * _doth(g8, ft)[0:1]
    logf = jax.nn.log_sigmoid(fl + fb) * fmask
    return norm(q, qg8), norm(k, kg8), logf


def _merge_f(za, zb, pa, pb):
    return jax.nn.sigmoid(za) * pa + jax.nn.sigmoid(zb) * pb


def _swiglu_f(gate, up):
    return jax.nn.silu(gate) * up


def _tri(n, strict):
    row = lax.broadcasted_iota(jnp.int32, (n, n), 0)
    col = lax.broadcasted_iota(jnp.int32, (n, n), 1)
    return (row > col) if strict else (row >= col)


def _dot_nt(a, b):
    return lax.dot_general(a, b, (((1,), (1,)), ((), ())), precision=HI, preferred_element_type=F32)


def _dot_tn(a, b):
    return lax.dot_general(a, b, (((0,), (0,)), ((), ())), precision=HI, preferred_element_type=F32)


def _scan_chunk(r, lw, k, v, a, b, s0):
    c = r.shape[0]
    incl = _tri(c, False)
    strict = _tri(c, True)
    cl = _doth(incl.astype(F32), lw)
    pe = jnp.exp(cl)
    pinv = jnp.exp(-cl)
    rt = r * pe
    at = a * jnp.exp(cl - lw)
    bt = b * pinv
    kt = k * pinv
    a_ab = jnp.where(strict, _dot_nt(at, bt), 0.0)
    a_ak = jnp.where(strict, _dot_nt(at, kt), 0.0)
    a_rb = jnp.where(incl, _dot_nt(rt, bt), 0.0)
    a_rk = jnp.where(incl, _dot_nt(rt, kt), 0.0)
    u = _dot_nt(at, s0) + _doth(a_ak, v)
    m = a_ab
    steps = max(1, int(np.ceil(np.log2(c))))
    for i in range(steps):
        u = u + _doth(m, u)
        if i < steps - 1:
            m = _doth(m, m)
    y = _dot_nt(rt, s0) + _doth(a_rb, u) + _doth(a_rk, v)
    s1 = (s0 + _dot_tn(u, bt) + _dot_tn(v, kt)) * pe[c - 1:c, :]
    return y, s1


def _scan_fwd(r, lw, k, v, a, b):
    L, W = r.shape
    nc, npair = L // CHUNK, W // LANES
    spec = pl.BlockSpec((CHUNK, LANES), lambda p, c: (c, p))

    def body(r_ref, lw_ref, k_ref, v_ref, a_ref, b_ref, y_ref, s_ref, st):
        @pl.when(pl.program_id(1) == 0)
        def _():
            st[...] = jnp.zeros_like(st)

        for hh in range(2):
            sl = slice(hh * HEAD, (hh + 1) * HEAD)
            s0 = st[hh]
            y, s1 = _scan_chunk(r_ref[:, sl], lw_ref[:, sl], k_ref[:, sl], v_ref[:, sl], a_ref[:, sl],
                                b_ref[:, sl], s0)
            s_ref[0, hh] = s0
            st[hh] = s1
            y_ref[:, sl] = y

    return pl.pallas_call(
        body, name="wkv7_fwd",
        out_shape=[jax.ShapeDtypeStruct((L, W), F32), jax.ShapeDtypeStruct((nc, 2 * npair, HEAD, HEAD), F32)],
        grid=(npair, nc), in_specs=[spec] * 6,
        out_specs=[spec, pl.BlockSpec((1, 2, HEAD, HEAD), lambda p, c: (c, p, 0, 0))],
        scratch_shapes=[pltpu.VMEM((2, HEAD, HEAD), F32)],
        compiler_params=_params(("parallel", "arbitrary")),
    )(r, lw, k, v, a, b)


def _scan_bwd(r, lw, k, v, a, b, s_all, dy):
    L, W = r.shape
    nc, npair = L // CHUNK, W // LANES
    spec = pl.BlockSpec((CHUNK, LANES), lambda p, c: (nc - 1 - c, p))

    def body(r_ref, lw_ref, k_ref, v_ref, a_ref, b_ref, s_ref, dy_ref,
             dr_ref, dlw_ref, dk_ref, dv_ref, da_ref, db_ref, dst):
        @pl.when(pl.program_id(1) == 0)
        def _():
            dst[...] = jnp.zeros_like(dst)

        for hh in range(2):
            sl = slice(hh * HEAD, (hh + 1) * HEAD)
            _, vjp = jax.vjp(_scan_chunk, r_ref[:, sl], lw_ref[:, sl], k_ref[:, sl], v_ref[:, sl],
                             a_ref[:, sl], b_ref[:, sl], s_ref[0, hh])
            g = vjp((dy_ref[:, sl], dst[hh]))
            for ref, val in zip((dr_ref, dlw_ref, dk_ref, dv_ref, da_ref, db_ref), g[:6]):
                ref[:, sl] = val
            dst[hh] = g[6]

    return pl.pallas_call(
        body, name="wkv7_bwd", out_shape=[jax.ShapeDtypeStruct((L, W), F32)] * 6,
        grid=(npair, nc),
        in_specs=[spec] * 6 + [pl.BlockSpec((1, 2, HEAD, HEAD), lambda p, c: (nc - 1 - c, p, 0, 0)), spec],
        out_specs=[spec] * 6,
        scratch_shapes=[pltpu.VMEM((2, HEAD, HEAD), F32)],
        compiler_params=_params(("parallel", "arbitrary")),
    )(r, lw, k, v, a, b, s_all, dy)


def _cumsum_rows(logf):
    L = logf.shape[0]
    t = LANES

    def body(x_ref, o_ref, carry):
        @pl.when(pl.program_id(0) == 0)
        def _():
            carry[...] = jnp.zeros_like(carry)

        c = _doth(_tri(t, False).astype(F32), x_ref[...]) + carry[...]
        carry[...] = c[t - 1:t, :]
        o_ref[...] = c.T

    return pl.pallas_call(
        body, name="fox_cumsum", out_shape=jax.ShapeDtypeStruct((LANES, L), F32), grid=(L // t,),
        in_specs=[pl.BlockSpec((t, LANES), lambda i: (i, 0))],
        out_specs=pl.BlockSpec((LANES, t), lambda i: (0, i)),
        scratch_shapes=[pltpu.VMEM((1, LANES), F32)], compiler_params=_params(("arbitrary",)),
    )(logf)


def _rcumsum_cols(dct):
    L = dct.shape[1]
    t = LANES
    n = L // t

    def body(x_ref, o_ref, carry):
        @pl.when(pl.program_id(0) == 0)
        def _():
            carry[...] = jnp.zeros_like(carry)

        rc = _doth(x_ref[...], _tri(t, False).astype(F32)) + carry[...]
        carry[...] = rc[:, 0:1]
        o_ref[...] = rc.T

    return pl.pallas_call(
        body, name="fox_rcumsum", out_shape=jax.ShapeDtypeStruct((L, LANES), F32), grid=(n,),
        in_specs=[pl.BlockSpec((LANES, t), lambda i: (0, n - 1 - i))],
        out_specs=pl.BlockSpec((t, LANES), lambda i: (n - 1 - i, 0)),
        scratch_shapes=[pltpu.VMEM((LANES, 1), F32)], compiler_params=_params(("arbitrary",)),
    )(dct)


def _attn_tile(L):
    return _pick(L, (384, 256, 128))


def _attn_scores(q, k, ck, qi, kj, t):
    s = _dot_bnt(q, k) * ATTN_SCALE - ck
    qpos = qi * t + lax.broadcasted_iota(jnp.int32, (t, t), 0)
    kpos = kj * t + lax.broadcasted_iota(jnp.int32, (t, t), 1)
    mask = (kpos <= qpos) & (kpos >= PAD_ROWS)
    return jnp.where(mask, s, NEG), mask


def _dot_bnt(a, b):
    return lax.dot_general(_bf(a), _bf(b), (((1,), (1,)), ((), ())), preferred_element_type=F32)


def _dot_btn(a, b):
    return lax.dot_general(_bf(a), _bf(b), (((0,), (0,)), ((), ())), preferred_element_type=F32)


def _ck_rows(ct_ref, p):
    r0 = 2 * (p % 4)
    return ct_ref[pl.ds(r0, 1), :], ct_ref[pl.ds(r0 + 1, 1), :]


def _attn_fwd(q, k, v, ct):
    L, W = q.shape
    t = _attn_tile(L)
    nt, npair = L // t, W // LANES
    qspec = pl.BlockSpec((t, LANES), lambda p, i, j: (i, p))
    kspec = pl.BlockSpec((t, LANES), lambda p, i, j: (jnp.minimum(i, j), p))
    cspec = pl.BlockSpec((8, t), lambda p, i, j: (p // 4, jnp.minimum(i, j)))

    def body(q_ref, k_ref, v_ref, ct_ref, o_ref, lse_ref, m_s, l_s, acc):
        p, i, j = pl.program_id(0), pl.program_id(1), pl.program_id(2)

        @pl.when(j == 0)
        def _():
            m_s[...] = jnp.full_like(m_s, NEG)
            l_s[...] = jnp.zeros_like(l_s)
            acc[...] = jnp.zeros_like(acc)

        @pl.when(j <= i)
        def _():
            cks = _ck_rows(ct_ref, p)
            for hh in range(2):
                sl = slice(hh * HEAD, (hh + 1) * HEAD)
                s, _ = _attn_scores(q_ref[:, sl], k_ref[:, sl], cks[hh], i, j, t)
                m_old = m_s[hh]
                m_new = jnp.maximum(m_old, jnp.max(s, axis=-1, keepdims=True))
                alpha = jnp.exp(m_old - m_new)
                pr = jnp.exp(s - m_new)
                l_s[hh] = alpha * l_s[hh] + jnp.sum(pr, axis=-1, keepdims=True)
                acc[hh] = alpha * acc[hh] + _dotb(pr, v_ref[:, sl])
                m_s[hh] = m_new

        @pl.when(j == i)
        def _():
            lane = lax.broadcasted_iota(jnp.int32, (t, LANES), 1)
            lse = jnp.zeros((t, LANES), F32)
            for hh in range(2):
                o_ref[:, hh * HEAD:(hh + 1) * HEAD] = acc[hh] / l_s[hh]
                lse = jnp.where(lane == hh, m_s[hh] + jnp.log(l_s[hh]), lse)
            lse_ref[0] = lse

    return pl.pallas_call(
        body, name="fox_attn_fwd",
        out_shape=[jax.ShapeDtypeStruct((L, W), F32), jax.ShapeDtypeStruct((npair, L, LANES), F32)],
        grid=(npair, nt, nt), in_specs=[qspec, kspec, kspec, cspec],
        out_specs=[qspec, pl.BlockSpec((1, t, LANES), lambda p, i, j: (p, i, 0))],
        scratch_shapes=[pltpu.VMEM((2, t, 1), F32), pltpu.VMEM((2, t, 1), F32), pltpu.VMEM((2, t, HEAD), F32)],
        compiler_params=_params(("parallel", "parallel", "arbitrary")),
    )(q, k, v, ct)


def _attn_bwd_delta(q, k, v, ct, lse, do):
    L, W = q.shape
    t = _attn_tile(L)
    nt, npair = L // t, W // LANES
    qspec = pl.BlockSpec((t, LANES), lambda p, i, j: (i, p))
    kspec = pl.BlockSpec((t, LANES), lambda p, i, j: (jnp.minimum(i, j), p))
    cspec = pl.BlockSpec((8, t), lambda p, i, j: (p // 4, jnp.minimum(i, j)))
    lspec = pl.BlockSpec((1, t, LANES), lambda p, i, j: (p, i, 0))

    def body(q_ref, k_ref, v_ref, ct_ref, lse_ref, do_ref, dl_ref, acc):
        p, i, j = pl.program_id(0), pl.program_id(1), pl.program_id(2)

        @pl.when(j == 0)
        def _():
            acc[...] = jnp.zeros_like(acc)

        @pl.when(j <= i)
        def _():
            cks = _ck_rows(ct_ref, p)
            for hh in range(2):
                sl = slice(hh * HEAD, (hh + 1) * HEAD)
                s, mask = _attn_scores(q_ref[:, sl], k_ref[:, sl], cks[hh], i, j, t)
                pr = jnp.where(mask, jnp.exp(s - lse_ref[0, :, hh:hh + 1]), 0.0)
                acc[hh] += jnp.sum(pr * _dot_bnt(do_ref[:, sl], v_ref[:, sl]), axis=-1, keepdims=True)

        @pl.when(j == i)
        def _():
            lane = lax.broadcasted_iota(jnp.int32, (t, LANES), 1)
            dl = jnp.zeros((t, LANES), F32)
            for hh in range(2):
                dl = jnp.where(lane == hh, acc[hh], dl)
            dl_ref[0] = dl

    return pl.pallas_call(
        body, name="fox_attn_bwd_delta", out_shape=jax.ShapeDtypeStruct((npair, L, LANES), F32),
        grid=(npair, nt, nt), in_specs=[qspec, kspec, kspec, cspec, lspec, qspec],
        out_specs=lspec, scratch_shapes=[pltpu.VMEM((2, t, 1), F32)],
        compiler_params=_params(("parallel", "parallel", "arbitrary")),
    )(q, k, v, ct, lse, do)


def _attn_bwd_dq(q, k, v, ct, delta, lse, do):
    L, W = q.shape
    t = _attn_tile(L)
    nt, npair = L // t, W // LANES
    qspec = pl.BlockSpec((t, LANES), lambda p, i, j: (i, p))
    kspec = pl.BlockSpec((t, LANES), lambda p, i, j: (jnp.minimum(i, j), p))
    cspec = pl.BlockSpec((8, t), lambda p, i, j: (p // 4, jnp.minimum(i, j)))
    lspec = pl.BlockSpec((1, t, LANES), lambda p, i, j: (p, i, 0))

    def body(q_ref, k_ref, v_ref, ct_ref, dl_ref, lse_ref, do_ref, dq_ref, acc):
        p, i, j = pl.program_id(0), pl.program_id(1), pl.program_id(2)

        @pl.when(j == 0)
        def _():
            acc[...] = jnp.zeros_like(acc)

        @pl.when(j <= i)
        def _():
            cks = _ck_rows(ct_ref, p)
            for hh in range(2):
                sl = slice(hh * HEAD, (hh + 1) * HEAD)
                s, mask = _attn_scores(q_ref[:, sl], k_ref[:, sl], cks[hh], i, j, t)
                pr = jnp.where(mask, jnp.exp(s - lse_ref[0, :, hh:hh + 1]), 0.0)
                ds = pr * (_dot_bnt(do_ref[:, sl], v_ref[:, sl]) - dl_ref[0, :, hh:hh + 1])
                acc[hh] += _dotb(ds, k_ref[:, sl])

        @pl.when(j == i)
        def _():
            for hh in range(2):
                dq_ref[:, hh * HEAD:(hh + 1) * HEAD] = acc[hh] * ATTN_SCALE

    return pl.pallas_call(
        body, name="fox_attn_bwd_dq", out_shape=jax.ShapeDtypeStruct((L, W), F32),
        grid=(npair, nt, nt), in_specs=[qspec, kspec, kspec, cspec, lspec, lspec, qspec],
        out_specs=qspec, scratch_shapes=[pltpu.VMEM((2, t, HEAD), F32)],
        compiler_params=_params(("parallel", "parallel", "arbitrary")),
    )(q, k, v, ct, delta, lse, do)


def _attn_bwd_dkv(q, k, v, ct, delta, lse, do):
    L, W = q.shape
    t = _attn_tile(L)
    nt, npair = L // t, W // LANES
    kspec = pl.BlockSpec((t, LANES), lambda p, j, i: (j, p))
    qspec = pl.BlockSpec((t, LANES), lambda p, j, i: (jnp.maximum(i, j), p))
    cspec = pl.BlockSpec((8, t), lambda p, j, i: (p // 4, j))
    lspec = pl.BlockSpec((1, t, LANES), lambda p, j, i: (p, jnp.maximum(i, j), 0))

    def body(q_ref, k_ref, v_ref, ct_ref, dl_ref, lse_ref, do_ref, dk_ref, dv_ref, dc_ref, dk_s, dv_s, dc_s):
        p, j, i = pl.program_id(0), pl.program_id(1), pl.program_id(2)

        @pl.when(i == 0)
        def _():
            dk_s[...] = jnp.zeros_like(dk_s)
            dv_s[...] = jnp.zeros_like(dv_s)
            dc_s[...] = jnp.zeros_like(dc_s)

        @pl.when(i >= j)
        def _():
            cks = _ck_rows(ct_ref, p)
            for hh in range(2):
                sl = slice(hh * HEAD, (hh + 1) * HEAD)
                s, mask = _attn_scores(q_ref[:, sl], k_ref[:, sl], cks[hh], i, j, t)
                pr = jnp.where(mask, jnp.exp(s - lse_ref[0, :, hh:hh + 1]), 0.0)
                doh = do_ref[:, sl]
                ds = pr * (_dot_bnt(doh, v_ref[:, sl]) - dl_ref[0, :, hh:hh + 1])
                dv_s[hh] += _dot_btn(pr, doh)
                dk_s[hh] += _dot_btn(ds, q_ref[:, sl])
                dc_s[hh] += -jnp.sum(ds, axis=0, keepdims=True)

        @pl.when(i == nt - 1)
        def _():
            row = lax.broadcasted_iota(jnp.int32, (8, t), 0)
            dc = jnp.zeros((8, t), F32)
            for hh in range(2):
                dk_ref[:, hh * HEAD:(hh + 1) * HEAD] = dk_s[hh] * ATTN_SCALE
                dv_ref[:, hh * HEAD:(hh + 1) * HEAD] = dv_s[hh]
                dc = jnp.where(row == hh, dc_s[hh], dc)
            dc_ref[0] = dc

    return pl.pallas_call(
        body, name="fox_attn_bwd_dkv",
        out_shape=[jax.ShapeDtypeStruct((L, W), F32), jax.ShapeDtypeStruct((L, W), F32),
                   jax.ShapeDtypeStruct((npair, 8, L), F32)],
        grid=(npair, nt, nt), in_specs=[qspec, kspec, kspec, cspec, lspec, lspec, qspec],
        out_specs=[kspec, kspec, pl.BlockSpec((1, 8, t), lambda p, j, i: (p, 0, j))],
        scratch_shapes=[pltpu.VMEM((2, t, HEAD), F32), pltpu.VMEM((2, t, HEAD), F32), pltpu.VMEM((2, 1, t), F32)],
        compiler_params=_params(("parallel", "parallel", "arbitrary")),
    )(q, k, v, ct, delta, lse, do)


def _place():
    x, y, c = lax.axis_index("x"), lax.axis_index("y"), lax.axis_index("c")
    chips = [(1 - x, y), (x, 1 - y), (1 - x, 1 - y)]
    return x, y, c, chips


def _remote(src, dst, send_sems, recv_sems, k, to):
    return pltpu.make_async_remote_copy(src_ref=src, dst_ref=dst, send_sem=send_sems.at[k],
                                        recv_sem=recv_sems.at[k], device_id=to, device_id_type=MESH)


def _all_gather_shards(shard):
    R = shard.shape[0]
    h = R // 2

    def body(x_ref, out_ref, send_sems, recv_sems, local_sem):
        x, y, c, chips = _place()
        me_chip = 2 * x + y
        sibling = (x, y, 1 - c)

        def half(chip, cc):
            return out_ref.at[chip, pl.ds(cc * h, h), :]

        mine = pltpu.make_async_copy(x_ref, out_ref.at[me_chip], local_sem)
        mine.start()
        first = [_remote(x_ref.at[pl.ds(c * h, h), :], half(me_chip, c), send_sems, recv_sems, k, (cx, cy, c))
                 for k, (cx, cy) in enumerate(chips)]
        for cp in first:
            cp.start()
        passed = []
        for k, (cx, cy) in enumerate(chips):
            landed = half(2 * cx + cy, c)
            _remote(landed, landed, send_sems, recv_sems, k, sibling).wait_recv()
            fwd = _remote(landed, landed, send_sems, recv_sems, 3 + k, sibling)
            fwd.start()
            passed.append(fwd)
        for k, (cx, cy) in enumerate(chips):
            other = half(2 * cx + cy, 1 - c)
            _remote(other, other, send_sems, recv_sems, 3 + k, sibling).wait_recv()
        for cp in first + passed:
            cp.wait_send()
        mine.wait()

    return pl.pallas_call(
        body, name="gather_weights", out_shape=jax.ShapeDtypeStruct((4, R, shard.shape[1]), shard.dtype),
        in_specs=[ANY], out_specs=ANY,
        scratch_shapes=[pltpu.SemaphoreType.DMA((6,)), pltpu.SemaphoreType.DMA((6,)), pltpu.SemaphoreType.DMA],
    )(shard)


def _pair_exchange(contrib):
    n, _, h, w = contrib.shape

    def body(x_ref, out_ref, send_sems, recv_sems):
        x, y, c, _ = _place()
        cp = _remote(x_ref.at[:, 1 - c], out_ref, send_sems, recv_sems, 0, (x, y, 1 - c))
        cp.start()
        cp.wait()

    return pl.pallas_call(
        body, name="reduce_pair_exchange", out_shape=jax.ShapeDtypeStruct((n, h, w), contrib.dtype),
        in_specs=[ANY], out_specs=ANY,
        scratch_shapes=[pltpu.SemaphoreType.DMA((1,)), pltpu.SemaphoreType.DMA((1,))],
    )(contrib)


def _chip_exchange(part):
    def body(x_ref, out_ref, send_sems, recv_sems, local_sem):
        x, y, c, chips = _place()
        me_chip = 2 * x + y
        mine = pltpu.make_async_copy(x_ref.at[me_chip], out_ref.at[me_chip], local_sem)
        mine.start()
        sends = [_remote(x_ref.at[2 * cx + cy], out_ref.at[me_chip], send_sems, recv_sems, k, (cx, cy, c))
                 for k, (cx, cy) in enumerate(chips)]
        for cp in sends:
            cp.start()
        for k, (cx, cy) in enumerate(chips):
            slot = out_ref.at[2 * cx + cy]
            _remote(slot, slot, send_sems, recv_sems, k, (cx, cy, c)).wait_recv()
        for cp in sends:
            cp.wait_send()
        mine.wait()

    return pl.pallas_call(
        body, name="reduce_chip_exchange", out_shape=jax.ShapeDtypeStruct(part.shape, part.dtype),
        in_specs=[ANY], out_specs=ANY,
        scratch_shapes=[pltpu.SemaphoreType.DMA((3,)), pltpu.SemaphoreType.DMA((3,)), pltpu.SemaphoreType.DMA],
    )(part)


def _pair_join(tot):
    h, w = tot.shape

    def body(x_ref, out_ref, send_sems, recv_sems, local_sem):
        x, y, c, _ = _place()
        mine = pltpu.make_async_copy(x_ref, out_ref.at[c], local_sem)
        mine.start()
        cp = _remote(x_ref, out_ref.at[c], send_sems, recv_sems, 0, (x, y, 1 - c))
        cp.start()
        other = out_ref.at[1 - c]
        _remote(other, other, send_sems, recv_sems, 0, (x, y, 1 - c)).wait_recv()
        cp.wait_send()
        mine.wait()

    return pl.pallas_call(
        body, name="reduce_pair_join", out_shape=jax.ShapeDtypeStruct((2, h, w), tot.dtype),
        in_specs=[ANY], out_specs=ANY,
        scratch_shapes=[pltpu.SemaphoreType.DMA((1,)), pltpu.SemaphoreType.DMA((1,)), pltpu.SemaphoreType.DMA],
    )(tot)


def _pair_add(contrib, recv):
    n, _, h, w = contrib.shape
    tr = _pick(h, (512, 256, 128, 64, 32, 16, 8))
    c = lax.axis_index("c").astype(jnp.int32).reshape((1,))

    def body(c_ref, a_ref, b_ref, o_ref):
        o_ref[...] = a_ref[0] + b_ref[...]

    return pl.pallas_call(
        body, name="reduce_pair_add", out_shape=jax.ShapeDtypeStruct((n, h, w), F32),
        grid_spec=pltpu.PrefetchScalarGridSpec(
            num_scalar_prefetch=1, grid=(n, h // tr),
            in_specs=[pl.BlockSpec((1, 1, tr, w), lambda a, i, cr: (a, cr[0], i, 0)),
                      pl.BlockSpec((1, tr, w), lambda a, i, cr: (a, i, 0))],
            out_specs=pl.BlockSpec((1, tr, w), lambda a, i, cr: (a, i, 0))),
        compiler_params=_params(("parallel", "parallel")),
    )(c, contrib, recv)


def _chip_add(parts):
    n, h, w = parts.shape
    tr = _pick(h, (512, 256, 128, 64, 32, 16, 8))

    def body(a_ref, o_ref):
        o_ref[...] = ((a_ref[0] + a_ref[1]) + a_ref[2]) + a_ref[3]

    return pl.pallas_call(
        body, name="reduce_chip_add", out_shape=jax.ShapeDtypeStruct((h, w), F32), grid=(h // tr,),
        in_specs=[pl.BlockSpec((n, tr, w), lambda i: (0, i, 0))],
        out_specs=pl.BlockSpec((tr, w), lambda i: (i, 0)),
        compiler_params=_params(("parallel",)),
    )(parts)


def _adamw(name, w, g, m, v):
    R, C = w.shape
    tr = _pick(R, (128, 64, 32, 16, 8))
    spec = pl.BlockSpec((tr, C), lambda i: (i, 0))

    def body(w_ref, g_ref, m_ref, v_ref, d_ref, mo_ref, vo_ref):
        gr = g_ref[...]
        mn = ADAM_B1 * m_ref[...] + (1.0 - ADAM_B1) * gr
        vn = ADAM_B2 * v_ref[...] + (1.0 - ADAM_B2) * jnp.square(gr)
        m_hat = mn / (1.0 - ADAM_B1 ** ADAM_STEP)
        v_hat = vn / (1.0 - ADAM_B2 ** ADAM_STEP)
        d_ref[...] = -ADAM_LR * (m_hat / (jnp.sqrt(v_hat) + ADAM_EPS) + ADAM_WD * w_ref[...])
        mo_ref[...] = mn
        vo_ref[...] = vn

    return pl.pallas_call(
        body, name=name, out_shape=[jax.ShapeDtypeStruct((R, C), F32)] * 3, grid=(R // tr,),
        in_specs=[spec] * 4, out_specs=[spec] * 3, compiler_params=_params(("parallel",)),
    )(w, g, m, v)


def _pack(parts, dtype, row_mult):
    flat = jnp.concatenate([p.reshape(-1).astype(dtype) for p in parts])
    unit = row_mult * PACK_W
    pad = (-flat.shape[0]) % unit
    if pad:
        flat = jnp.concatenate([flat, jnp.zeros((pad,), dtype)])
    return flat.reshape(-1, PACK_W)


def _unpack(flat, shapes):
    out, off = [], 0
    for s in shapes:
        n = int(np.prod(s))
        out.append(flat[off:off + n].reshape(s))
        off += n
    return out


SHARDED = ("w_in", "rwkv_w2", "rwkv_a2", "rwkv_g2", "w_branch_a", "w_branch_b", "w_o", "w_gate_up", "w_down",
           "meta_tokens")
SHARD_AXIS = {"w_in": 1, "rwkv_w2": 1, "rwkv_a2": 1, "rwkv_g2": 1, "w_branch_a": 1, "w_branch_b": 1, "w_o": 0,
              "w_gate_up": 1, "w_down": 0, "meta_tokens": 1}
SMALL = ("norm1_g", "rwkv_mu", "rwkv_w0", "rwkv_a0", "rwkv_k_k", "rwkv_k_a", "rwkv_r_k", "rwkv_gn_w",
         "rwkv_gn_b", "fox_q_norm_g", "fox_k_norm_g", "fox_f_bias", "norm2_g")
WEIGHTS = ("meta_tokens", "norm1_g", "w_in", "rwkv_mu", "rwkv_w0", "rwkv_w2", "rwkv_a0", "rwkv_a2", "rwkv_g2",
           "rwkv_k_k", "rwkv_k_a", "rwkv_r_k", "rwkv_gn_w", "rwkv_gn_b", "fox_q_norm_g", "fox_k_norm_g",
           "fox_f_bias", "w_branch_a", "w_branch_b", "w_o", "norm2_g", "w_gate_up", "w_down")


def _pad_rows(t, rows):
    return jnp.concatenate([t, jnp.zeros((rows - t.shape[0],) + t.shape[1:], t.dtype)], axis=0)


def _pad_cols(t, cols):
    return jnp.concatenate([t, jnp.zeros(t.shape[:-1] + (cols - t.shape[-1],), t.dtype)], axis=-1)


def _step(x, tgt, wts, mom1, mom2):
    seq, D = x.shape
    L = SEQ_ROW0 + seq
    RW = wts["rwkv_w0"].shape[-1]
    DL, AL, GL = wts["rwkv_w2"].shape[0], wts["rwkv_a2"].shape[0], wts["rwkv_g2"].shape[0]
    FW = wts["w_branch_b"].shape[0]
    FH = wts["fox_f_bias"].shape[-1]
    DFF = wts["w_down"].shape[0] * 4
    LORA = DL + AL + GL
    LW = -(-(LORA + FH) // 512) * 512
    assert RW == FW and (6 * RW) % D == 0 and (6 * RW + 2 * D) % LW == 0 and LORA % 8 == 0
    xj = lax.axis_index("x")
    yj = lax.axis_index("y")
    chip = 2 * xj + yj

    meta_bits = lax.bitcast_convert_type(wts["meta_tokens"], BF16)
    send = [wts[n] for n in SHARDED[:-1]] + [meta_bits]
    gathered = _all_gather_shards(_pack(send, BF16, 32))
    shapes = [wts[n].shape for n in SHARDED[:-1]] + [meta_bits.shape]
    per_chip = [_unpack(gathered[j].reshape(-1), shapes) for j in range(4)]
    full = {}
    for idx, n in enumerate(SHARDED):
        full[n] = jnp.concatenate([per_chip[j][idx] for j in range(4)], axis=SHARD_AXIS[n])
    meta = lax.bitcast_convert_type(full["meta_tokens"], F32)
    w_in = full["w_in"]
    o = 0
    segs = {}
    for nm, wd in (("r", RW), ("k", RW), ("v", RW), ("wd", DL), ("ad", AL), ("gd", GL),
                   ("fq", FW), ("fk", FW), ("fv", FW), ("ff", FH), ("ga", D), ("gb", D)):
        segs[nm] = w_in[:, o:o + wd]
        o += wd
    lora_w = _pad_cols(jnp.concatenate([segs["wd"], segs["ad"], segs["gd"], segs["ff"]], axis=1), LW)
    w1 = jnp.concatenate([segs["r"], segs["k"], segs["v"], segs["fq"], segs["fk"], segs["fv"],
                          segs["ga"], segs["gb"], lora_w], axis=1)
    cb_f = 3
    cb_gate = (6 * RW) // D
    cb_lora = (6 * RW + 2 * D) // LW

    e_m, et_m = _head_mats(RW)
    ft_m = _fold_mat(RW)
    mu = wts["rwkv_mu"]
    mu_rkv = mu[:, :3 * RW]
    mu_l = _pad_cols(mu[:, 3 * RW:], LW)
    w2p = _pad_rows(full["rwkv_w2"].astype(F32), LW)
    a2p = _pad_rows(jnp.concatenate([jnp.zeros((DL, RW), F32), full["rwkv_a2"].astype(F32)], axis=0), LW)
    g2p = _pad_rows(jnp.concatenate([jnp.zeros((DL + AL, RW), F32), full["rwkv_g2"].astype(F32)], axis=0), LW)
    r_k = wts["rwkv_r_k"].reshape(1, RW)
    qg8 = jnp.broadcast_to(wts["fox_q_norm_g"], (8, HEAD))
    kg8 = jnp.broadcast_to(wts["fox_k_norm_g"], (8, HEAD))
    fb = _pad_cols(wts["fox_f_bias"], LANES)
    fmask = (jnp.arange(LANES) < FH).astype(F32).reshape(1, LANES)
    lmask = ((jnp.arange(LW) >= LORA) & (jnp.arange(LW) < LORA + FH)).astype(F32).reshape(1, LW)

    h0 = jnp.concatenate([jnp.zeros((PAD_ROWS, D), F32), meta, x], axis=0)
    n1 = wts["norm1_g"]

    (xn,), _ = _rowcall("rms1_fwd", lambda i, r, b: ([_rms_f(r[0], b[0])], []), L,
                        [(h0, D, 0, "row")], [n1], [(D, BF16)], [])
    proj = _matmul(xn, w1, "nn", F32, "proj_fwd")

    def shift_fn(i, r, b):
        rows = lax.broadcasted_iota(jnp.int32, (ROW_TILE, 1), 0)
        outs = []
        for z, halo, m_ in ((r[0], r[1], b[0]), (r[2], r[3], b[1])):
            first = jnp.where(i == 0, 0.0, halo[7:8, :])
            zp = jnp.where(rows == 0, first, pltpu.roll(z, 1, 0))
            outs.append(z + (zp - z) * m_)
        return outs, []

    rkv_w = 3 * RW
    (x_rkv, x_l), _ = _rowcall(
        "shift_fwd", shift_fn, L,
        [(proj, rkv_w, 0, "row"), (proj, rkv_w, 0, "prev"), (proj, LW, cb_lora, "row"), (proj, LW, cb_lora, "prev")],
        [mu_rkv, mu_l], [(rkv_w, F32), (LW, F32)], [])

    prep_p = [wts["rwkv_w0"], w2p, wts["rwkv_a0"], a2p, g2p, wts["rwkv_k_k"], wts["rwkv_k_a"], e_m, et_m]
    prep_rows = [(x_rkv, RW, 0, "row"), (x_rkv, RW, 1, "row"), (x_rkv, RW, 2, "row"), (x_l, LW, 0, "row")]
    (s_r, s_lw, s_k, s_v, s_a, s_b, gate_g), _ = _rowcall(
        "rwkv_prep_fwd", lambda i, r, b: (list(_prep_f(*r, *b)), []), L, prep_rows, prep_p,
        [(RW, F32)] * 7, [])
    y_scan, s_all = _scan_fwd(s_r, s_lw, s_k, s_v, s_a, s_b)
    post_p = [wts["rwkv_gn_w"], wts["rwkv_gn_b"], r_k, e_m, et_m]
    post_rows = [(y_scan, RW, 0, "row"), (s_r, RW, 0, "row"), (s_k, RW, 0, "row"), (s_v, RW, 0, "row"),
                 (gate_g, RW, 0, "row")]
    (y_a,), _ = _rowcall("rwkv_post_fwd", lambda i, r, b: ([_post_f(*r, *b)], []), L, post_rows, post_p,
                         [(RW, BF16)], [])

    fox_p = [qg8, kg8, fb, e_m, et_m, ft_m, fmask]

    def foxprep_fn(i, r, b):
        fl = _doth(r[2] * b[-1], b[-2])
        return list(_foxprep_f(r[0], r[1], fl, *b[:-2])), []

    sel = (np.arange(LW)[:, None] - LORA == np.arange(LANES)[None, :]).astype(np.float32)
    sel = jnp.asarray(sel)
    fox_rows = [(proj, FW, cb_f, "row"), (proj, FW, cb_f + 1, "row"), (proj, LW, cb_lora, "row")]
    (f_q, f_k, logf), _ = _rowcall("fox_prep_fwd", foxprep_fn, L, fox_rows, fox_p + [sel, lmask],
                                   [(FW, BF16), (FW, BF16), (LANES, F32)], [])
    ct = _cumsum_rows(logf)
    f_v = proj[:, (cb_f + 2) * FW:(cb_f + 3) * FW]
    y_b32, lse = _attn_fwd(f_q, f_k, f_v, ct)
    y_b = y_b32.astype(BF16)

    p_a = _matmul(y_a, full["w_branch_a"], "nn", F32, "branch_a_fwd")
    p_b = _matmul(y_b, full["w_branch_b"], "nn", F32, "branch_b_fwd")
    merge_rows = [(proj, D, cb_gate, "row"), (proj, D, cb_gate + 1, "row"), (p_a, D, 0, "row"), (p_b, D, 0, "row")]
    (merged,), _ = _rowcall("merge_fwd", lambda i, r, b: ([_merge_f(*r)], []), L, merge_rows, [], [(D, BF16)], [])
    h1 = _matmul(merged, full["w_o"], "nn", F32, "wo_fwd", add=h0)
    n2 = wts["norm2_g"]
    (xn2,), _ = _rowcall("rms2_fwd", lambda i, r, b: ([_rms_f(r[0], b[0])], []), L,
                         [(h1, D, 0, "row")], [n2], [(D, BF16)], [])
    gu = _matmul(xn2, full["w_gate_up"], "nn", F32, "gate_up_fwd")
    gu_rows = [(gu, DFF, 0, "row"), (gu, DFF, 1, "row")]
    (act,), _ = _rowcall("swiglu_fwd", lambda i, r, b: ([_swiglu_f(*r)], []), L, gu_rows, [], [(DFF, BF16)], [])
    h2 = _matmul(act, full["w_down"], "nn", F32, "down_fwd", add=h1)

    def loss_fn(i, r, b):
        err = jnp.where(i == 0, 0.0, r[0] - r[1])
        return [err * (1.0 / D)], [jnp.zeros((8, LANES), F32) + 0.5 / D * jnp.sum(err * err)]

    (dh2,), (loss_acc,) = _rowcall("loss", loss_fn, L, [(h2, D, 0, "row"), (tgt, D, 0, "lag")], [],
                                   [(D, F32)], [(8, LANES)])
    loss = lax.psum(loss_acc[0, 0], ("x", "y", "c"))

    dh2b = dh2.astype(BF16)
    g_w_down = _matmul(act, dh2b, "tn", F32, "down_dw")
    d_act = _matmul(dh2b, full["w_down"], "nt", F32, "down_dx")

    def swiglu_bwd(i, r, b):
        _, vjp = jax.vjp(_swiglu_f, r[0], r[1])
        return list(vjp(r[2])), []

    (d_gate, d_up), _ = _rowcall("swiglu_bwd", swiglu_bwd, L, gu_rows + [(d_act, DFF, 0, "row")], [],
                                 [(DFF, BF16), (DFF, BF16)], [])
    d_gu = jnp.concatenate([d_gate, d_up], axis=1)
    g_w_gu = _matmul(xn2, d_gu, "tn", F32, "gate_up_dw")
    d_xn2 = _matmul(d_gu, full["w_gate_up"], "nt", F32, "gate_up_dx")

    def rms_bwd(i, r, b):
        _, vjp = jax.vjp(_rms_f, r[0], b[0])
        dh, dg = vjp(r[1])
        return [dh + r[2]], [dg]

    (dh1,), (g_n2,) = _rowcall("rms2_bwd", rms_bwd, L,
                               [(h1, D, 0, "row"), (d_xn2, D, 0, "row"), (dh2, D, 0, "row")], [n2],
                               [(D, F32)], [(1, D)])
    dh1b = dh1.astype(BF16)
    g_w_o = _matmul(merged, dh1b, "tn", F32, "wo_dw")
    d_merged = _matmul(dh1b, full["w_o"], "nt", F32, "wo_dx")

    def merge_bwd(i, r, b):
        _, vjp = jax.vjp(_merge_f, *r[:4])
        return list(vjp(r[4])), []

    (d_za, d_zb, d_pa, d_pb), _ = _rowcall("merge_bwd", merge_bwd, L, merge_rows + [(d_merged, D, 0, "row")], [],
                                           [(D, BF16)] * 4, [])
    g_w_a = _matmul(y_a, d_pa, "tn", F32, "branch_a_dw")
    g_w_b = _matmul(y_b, d_pb, "tn", F32, "branch_b_dw")
    d_ya = _matmul(d_pa, full["w_branch_a"], "nt", F32, "branch_a_dx")
    d_yb = _matmul(d_pb, full["w_branch_b"], "nt", F32, "branch_b_dx")

    delta = _attn_bwd_delta(f_q, f_k, f_v, ct, lse, d_yb)
    d_fq = _attn_bwd_dq(f_q, f_k, f_v, ct, delta, lse, d_yb)
    d_fk, d_fv, dc_rows = _attn_bwd_dkv(f_q, f_k, f_v, ct, delta, lse, d_yb)
    dct = _pad_rows(dc_rows[:, :2, :].reshape(-1, L), LANES)
    d_logf = _rcumsum_cols(dct)

    def foxprep_bwd(i, r, b):
        def f(q, k, xl, qg, kg, fbias):
            return _foxprep_f(q, k, _doth(xl * b[-1], b[-2]), qg, kg, fbias, *b[3:7])
        _, vjp = jax.vjp(f, r[0], r[1], r[2], b[0], b[1], b[2])
        dq, dk, dxl, dqg, dkg, dfb = vjp((r[3], r[4], r[5]))
        return [dq, dk, dxl], [dqg, dkg, dfb]

    (d_zfq, d_zfk, d_zl_f), (g_qg8, g_kg8, g_fb) = _rowcall(
        "fox_prep_bwd", foxprep_bwd, L,
        fox_rows + [(d_fq, FW, 0, "row"), (d_fk, FW, 0, "row"), (d_logf, LANES, 0, "row")],
        fox_p + [sel, lmask], [(FW, BF16), (FW, BF16), (LW, F32)], [(8, HEAD), (8, HEAD), (1, LANES)])

    def post_bwd(i, r, b):
        _, vjp = jax.vjp(lambda *a: _post_f(*a, b[3], b[4]), *r[:5], b[0], b[1], b[2])
        g = vjp(r[5])
        return list(g[:5]), list(g[5:])

    (d_y, d_r1, d_k1, d_v1, d_g), (g_gn_w, g_gn_b, g_r_k) = _rowcall(
        "rwkv_post_bwd", post_bwd, L, post_rows + [(d_ya, RW, 0, "row")], post_p,
        [(RW, F32)] * 5, [(1, RW)] * 3)
    d_r2, d_lw, d_k2, d_v2, d_a, d_b = _scan_bwd(s_r, s_lw, s_k, s_v, s_a, s_b, s_all, d_y)

    def prep_bwd(i, r, b):
        _, vjp = jax.vjp(lambda *a: _prep_f(*a, b[7], b[8]), *r[:4], *b[:7])
        cts = (r[4] + r[10], r[5], r[6] + r[11], r[7] + r[12], r[8], r[9], r[13])
        g = vjp(cts)
        return list(g[:4]), list(g[4:])

    bwd_rows = prep_rows + [(d_r2, RW, 0, "row"), (d_lw, RW, 0, "row"), (d_k2, RW, 0, "row"), (d_v2, RW, 0, "row"),
                            (d_a, RW, 0, "row"), (d_b, RW, 0, "row"), (d_r1, RW, 0, "row"), (d_k1, RW, 0, "row"),
                            (d_v1, RW, 0, "row"), (d_g, RW, 0, "row")]
    (d_xr, d_xk, d_xv, d_xl), (g_w0, g_w2p, g_a0, g_a2p, g_g2p, g_kk, g_ka) = _rowcall(
        "rwkv_prep_bwd", prep_bwd, L, bwd_rows, prep_p, [(RW, F32)] * 3 + [(LW, F32)],
        [(1, RW), (LW, RW), (1, RW), (LW, RW), (LW, RW), (1, RW), (1, RW)])

    def shift_bwd(i, r, b):
        last = pl.num_programs(0) - 1
        rows = lax.broadcasted_iota(jnp.int32, (ROW_TILE, 1), 0)
        outs, sums = [], []
        groups = ((r[0], r[1], r[2], r[3], b[0], None), (r[4], r[5], r[6], r[7], b[1], r[8]))
        for d, dnext, z, zhalo, m_, extra in groups:
            nxt = jnp.where(i == last, 0.0, dnext[0:1, :])
            d_up = jnp.where(rows == ROW_TILE - 1, nxt, pltpu.roll(d, ROW_TILE - 1, 0))
            dz = d * (1.0 - m_) + d_up * m_
            if extra is not None:
                dz = dz + extra
            first = jnp.where(i == 0, 0.0, zhalo[7:8, :])
            zp = jnp.where(rows == 0, first, pltpu.roll(z, 1, 0))
            outs.append(dz)
            sums.append(_rowsum(d * (zp - z)))
        return outs, sums

    d_xrkv = jnp.concatenate([d_xr, d_xk, d_xv], axis=1)
    (d_zrkv, d_zl), (g_mu_rkv, g_mu_l) = _rowcall(
        "shift_bwd", shift_bwd, L,
        [(d_xrkv, rkv_w, 0, "row"), (d_xrkv, rkv_w, 0, "next"), (proj, rkv_w, 0, "row"), (proj, rkv_w, 0, "prev"),
         (d_xl, LW, 0, "row"), (d_xl, LW, 0, "next"), (proj, LW, cb_lora, "row"), (proj, LW, cb_lora, "prev"),
         (d_zl_f, LW, 0, "row")],
        [mu_rkv, mu_l], [(rkv_w, BF16), (LW, BF16)], [(1, rkv_w), (1, LW)])

    d_proj = jnp.concatenate([d_zrkv, d_zfq, d_zfk, d_fv.astype(BF16), d_za, d_zb, d_zl], axis=1)
    g_w1 = _matmul(xn, d_proj, "tn", F32, "proj_dw")
    d_xn = _matmul(d_proj, w1, "nt", F32, "proj_dx")
    (dh0,), (g_n1,) = _rowcall("rms1_bwd", rms_bwd, L,
                               [(h0, D, 0, "row"), (d_xn, D, 0, "row"), (dh1, D, 0, "row")], [n1],
                               [(D, F32)], [(1, D)])
    grad_x = dh0[SEQ_ROW0:]
    g_meta = dh0[PAD_ROWS:SEQ_ROW0]

    o = 0
    gs = {}
    for nm, wd in (("r", RW), ("k", RW), ("v", RW), ("fq", FW), ("fk", FW), ("fv", FW), ("ga", D), ("gb", D),
                   ("wd", DL), ("ad", AL), ("gd", GL), ("ff", FH)):
        gs[nm] = g_w1[:, o:o + wd]
        o += wd
    g_w_in = jnp.concatenate([gs[nm] for nm in ("r", "k", "v", "wd", "ad", "gd", "fq", "fk", "fv", "ff", "ga", "gb")],
                             axis=1)
    gfull = {
        "w_in": g_w_in, "rwkv_w2": g_w2p[:DL], "rwkv_a2": g_a2p[DL:DL + AL], "rwkv_g2": g_g2p[DL + AL:LORA],
        "w_branch_a": g_w_a, "w_branch_b": g_w_b, "w_o": g_w_o, "w_gate_up": g_w_gu, "w_down": g_w_down,
        "meta_tokens": g_meta,
    }
    g_mu = jnp.concatenate([g_mu_rkv, g_mu_l[:, :LORA]], axis=1)
    gsmall = {
        "norm1_g": g_n1, "rwkv_mu": g_mu, "rwkv_w0": g_w0, "rwkv_a0": g_a0, "rwkv_k_k": g_kk, "rwkv_k_a": g_ka,
        "rwkv_r_k": g_r_k.reshape(wts["rwkv_r_k"].shape), "rwkv_gn_w": g_gn_w, "rwkv_gn_b": g_gn_b,
        "fox_q_norm_g": g_qg8[0:1], "fox_k_norm_g": g_kg8[0:1], "fox_f_bias": g_fb[:, :FH], "norm2_g": g_n2,
    }
    small_flat = jnp.concatenate([gsmall[n].reshape(-1) for n in SMALL])

    def shard_of(n, j):
        t = gfull[n]
        w = t.shape[SHARD_AXIS[n]] // 4
        return lax.slice_in_dim(t, j * w, (j + 1) * w, axis=SHARD_AXIS[n])

    contrib = jnp.stack([_pack([shard_of(n, j) for n in SHARDED] + [small_flat], F32, 16) for j in range(4)])
    rows = contrib.shape[1]
    contrib = contrib.reshape(4, 2, rows // 2, PACK_W)
    pair = _pair_add(contrib, _pair_exchange(contrib))
    tot = _chip_add(_chip_exchange(pair))
    gsum = _pair_join(tot).reshape(-1)
    red = _unpack(gsum, [wts[n].shape for n in SHARDED] + [small_flat.shape])
    grads = {n: red[i] for i, n in enumerate(SHARDED)}
    for n, t in zip(SMALL, _unpack(red[-1], [wts[n].shape for n in SMALL])):
        grads[n] = t

    delta, new_m, new_v = {}, {}, {}
    for n in SHARDED:
        delta[n], new_m[n], new_v[n] = _adamw("adamw_" + n, wts[n], grads[n], mom1[n], mom2[n])
    pk = lambda d: _pack([d[n] for n in SMALL], F32, 8)
    ds, ms, vs = _adamw("adamw_small", pk(wts), pk(grads), pk(mom1), pk(mom2))
    small_shapes = [wts[n].shape for n in SMALL]
    for dst, src in ((delta, ds), (new_m, ms), (new_v, vs)):
        for n, t in zip(SMALL, _unpack(src.reshape(-1), small_shapes)):
            dst[n] = t
    return loss, grad_x, grads, delta, new_m, new_v


def kernel(x, meta_tokens, norm1_g, w_in, rwkv_mu, rwkv_w0, rwkv_w2, rwkv_a0, rwkv_a2, rwkv_g2, rwkv_k_k, rwkv_k_a, rwkv_r_k, rwkv_gn_w, rwkv_gn_b, fox_q_norm_g, fox_k_norm_g, fox_f_bias, w_branch_a, w_branch_b, w_o, norm2_g, w_gate_up, w_down, loss_target, m_meta_tokens, m_norm1_g, m_w_in, m_rwkv_mu, m_rwkv_w0, m_rwkv_w2, m_rwkv_a0, m_rwkv_a2, m_rwkv_g2, m_rwkv_k_k, m_rwkv_k_a, m_rwkv_r_k, m_rwkv_gn_w, m_rwkv_gn_b, m_fox_q_norm_g, m_fox_k_norm_g, m_fox_f_bias, m_w_branch_a, m_w_branch_b, m_w_o, m_norm2_g, m_w_gate_up, m_w_down, v_meta_tokens, v_norm1_g, v_w_in, v_rwkv_mu, v_rwkv_w0, v_rwkv_w2, v_rwkv_a0, v_rwkv_a2, v_rwkv_g2, v_rwkv_k_k, v_rwkv_k_a, v_rwkv_r_k, v_rwkv_gn_w, v_rwkv_gn_b, v_fox_q_norm_g, v_fox_k_norm_g, v_fox_f_bias, v_w_branch_a, v_w_branch_b, v_w_o, v_norm2_g, v_w_gate_up, v_w_down):
    args = dict(locals())
    shapes = {n: args[n].shape for n in WEIGHTS}

    def drop_depth(t, n):
        if n == "meta_tokens":
            return t
        if n == "rwkv_r_k":
            return t.reshape(1, -1)
        return t.reshape(t.shape[1:]) if t.ndim == 3 else t

    wts = {n: drop_depth(args[n], n) for n in WEIGHTS}
    mom1 = {n: drop_depth(args["m_" + n], n) for n in WEIGHTS}
    mom2 = {n: drop_depth(args["v_" + n], n) for n in WEIGHTS}
    loss, grad_x, grads, delta, new_m, new_v = _step(x[0], loss_target[0], wts, mom1, mom2)
    outs = [loss, grad_x[None]]
    for d in (grads, delta, new_m, new_v):
        outs += [d[n].reshape(shapes[n]) for n in WEIGHTS]
    return tuple(outs)
```

```python
import functools

import jax
import jax.numpy as jnp
import numpy as np
from jax import lax
from jax.experimental import pallas as pl
from jax.experimental.pallas import tpu as pltpu

F32 = jnp.float32
BF16 = jnp.bfloat16
HI = lax.Precision.HIGHEST
MESH = pl.DeviceIdType.MESH
ANY = pl.BlockSpec(memory_space=pl.ANY)

N_META = 16
HEAD = 64
ROW_TILE = 128
PAD_ROWS = ROW_TILE - N_META
SEQ_ROW0 = ROW_TILE
CHUNK = 64
LANES = 128
PACK_W = 1024
RMS_EPS = 1e-6
GN_EPS = 64e-5
ATTN_SCALE = HEAD ** -0.5
NEG = -1e30
VMEM_LIMIT_V7X = 56 * 1024 * 1024

ADAM_LR = 0.001
ADAM_B1 = 0.9
ADAM_B2 = 0.999
ADAM_EPS = 1e-08
ADAM_WD = 0.01
ADAM_STEP = 10


def _params(sem=None):
    return pltpu.CompilerParams(dimension_semantics=sem, vmem_limit_bytes=VMEM_LIMIT_V7X)


def _pick(n, cands):
    for c in cands:
        if n % c == 0:
            return c
    return n


def _bf(t):
    return t.astype(BF16)


def _dotb(a, b):
    return jnp.dot(_bf(a), _bf(b), preferred_element_type=F32)


def _doth(a, b):
    return jnp.dot(a, b, precision=HI, preferred_element_type=F32)


_BIG = (2048, 1536, 1408, 1024, 768, 704, 512, 384, 256, 128)


def _matmul(a, b, mode, out_dtype, name, add=None):
    if mode == "nn":
        (M, R), (_, N) = a.shape, b.shape
        dims = (((1,), (0,)), ((), ()))
    elif mode == "nt":
        (M, R), (N, _) = a.shape, b.shape
        dims = (((1,), (1,)), ((), ()))
    else:
        (R, M), (_, N) = a.shape, b.shape
        dims = (((0,), (0,)), ((), ()))
    tm = _pick(M, (1408, 1024, 768, 512, 384, 256, 128))
    tn = _pick(N, (1024, 768, 512, 384, 256, 128)) if mode == "tn" else _pick(N, (512, 384, 256, 128))
    tr = _pick(R, (1408, 1056, 768, 512, 384, 256, 128)) if mode == "tn" else _pick(R, _BIG)
    nr = R // tr

    if mode == "nn":
        a_spec = pl.BlockSpec((tm, tr), lambda i, j, r: (i, r))
        b_spec = pl.BlockSpec((tr, tn), lambda i, j, r: (r, j))
    elif mode == "nt":
        a_spec = pl.BlockSpec((tm, tr), lambda i, j, r: (i, r))
        b_spec = pl.BlockSpec((tn, tr), lambda i, j, r: (j, r))
    else:
        a_spec = pl.BlockSpec((tr, tm), lambda i, j, r: (r, i))
        b_spec = pl.BlockSpec((tr, tn), lambda i, j, r: (r, j))
    o_spec = pl.BlockSpec((tm, tn), lambda i, j, r: (i, j))
    has_add = add is not None

    def body(*refs):
        if has_add:
            a_ref, b_ref, add_ref, o_ref, acc = refs
        else:
            a_ref, b_ref, o_ref, acc = refs
        r = pl.program_id(2)

        @pl.when(r == 0)
        def _():
            acc[...] = jnp.zeros_like(acc)

        acc[...] += lax.dot_general(_bf(a_ref[...]), _bf(b_ref[...]), dims, preferred_element_type=F32)

        @pl.when(r == nr - 1)
        def _():
            res = acc[...]
            if has_add:
                res = res + add_ref[...]
            o_ref[...] = res.astype(o_ref.dtype)

    ins = [a, b] + ([add] if has_add else [])
    specs = [a_spec, b_spec] + ([o_spec] if has_add else [])
    return pl.pallas_call(
        body, name=name, out_shape=jax.ShapeDtypeStruct((M, N), out_dtype),
        grid=(M // tm, N // tn, nr), in_specs=specs, out_specs=o_spec,
        scratch_shapes=[pltpu.VMEM((tm, tn), F32)],
        compiler_params=_params(("parallel", "parallel", "arbitrary")),
    )(*ins)


def _rowcall(name, fn, L, row_ins, bc_ins, row_outs, acc_outs, tm=ROW_TILE):
    nt = L // tm
    specs = []
    for arr, w, cb, kind in row_ins:
        if kind == "row":
            specs.append(pl.BlockSpec((tm, w), lambda i, cb=cb: (i, cb)))
        elif kind == "lag":
            specs.append(pl.BlockSpec((tm, w), lambda i, cb=cb: (jnp.maximum(i - 1, 0), cb)))
        elif kind == "prev":
            specs.append(pl.BlockSpec((8, w), lambda i, cb=cb: (jnp.maximum(i * (tm // 8) - 1, 0), cb)))
        else:
            specs.append(pl.BlockSpec((8, w), lambda i, cb=cb: (jnp.minimum((i + 1) * (tm // 8), L // 8 - 1), cb)))
    for arr in bc_ins:
        specs.append(pl.BlockSpec(arr.shape, lambda i, nd=arr.ndim: (0,) * nd))
    out_shapes = [jax.ShapeDtypeStruct((L, w), dt) for w, dt in row_outs]
    out_specs = [pl.BlockSpec((tm, w), lambda i: (i, 0)) for w, dt in row_outs]
    out_shapes += [jax.ShapeDtypeStruct(s, F32) for s in acc_outs]
    out_specs += [pl.BlockSpec(s, lambda i, nd=len(s): (0,) * nd) for s in acc_outs]
    n_row, n_bc, n_ro = len(row_ins), len(bc_ins), len(row_outs)

    def body(*refs):
        i = pl.program_id(0)
        vals = [r[...] for r in refs[: n_row + n_bc]]
        outs, sums = fn(i, vals[:n_row], vals[n_row:])
        o_refs = refs[n_row + n_bc:]
        for r, v in zip(o_refs[:n_ro], outs):
            r[...] = v.astype(r.dtype)

        @pl.when(i == 0)
        def _():
            for r in o_refs[n_ro:]:
                r[...] = jnp.zeros_like(r)

        for r, v in zip(o_refs[n_ro:], sums):
            r[...] += v

    res = pl.pallas_call(
        body, name=name, out_shape=out_shapes, grid=(nt,), in_specs=specs, out_specs=out_specs,
        compiler_params=_params(("arbitrary",)),
    )(*[a for a, _, _, _ in row_ins], *bc_ins)
    return list(res[:n_ro]), list(res[n_ro:])


def _rowsum(t):
    return jnp.sum(t, axis=0, keepdims=True)


def _head_mats(width):
    e = (np.arange(width)[:, None] // HEAD == np.arange(LANES)[None, :]).astype(np.float32)
    return jnp.asarray(e), jnp.asarray(e.T)


def _fold_mat(width):
    ft = (np.arange(HEAD)[:, None] == np.arange(width)[None, :] % HEAD).astype(np.float32)
    return jnp.asarray(ft)


def _rms_f(h, g):
    return (h * lax.rsqrt(jnp.mean(h * h, axis=-1, keepdims=True) + RMS_EPS)) * g


def _prep_f(xr, xk, xv, xl, w0, w2p, a0, a2p, g2p, k_k, k_a, e, et):
    w_log = -jax.nn.softplus(-(w0 + _dotb(jnp.tanh(xl), w2p))) - 0.5
    lw = -jnp.exp(w_log)
    a = jax.nn.sigmoid(a0 + _dotb(xl, a2p))
    g = _dotb(jax.nn.sigmoid(xl), g2p)
    kkr = xk * k_k
    inv = lax.rsqrt(jnp.maximum(_doth(kkr * kkr, e), 1e-24))
    kk = kkr * _doth(inv, et)
    kf = xk * (1.0 + (a - 1.0) * k_a)
    return xr, lw, kf, xv, -kk, kk * a, g


def _post_f(y, r, kf, v, g, gn_w, gn_b, r_k, e, et):
    mu = _doth(y, e) * (1.0 / HEAD)
    yc = y - _doth(mu, et)
    var = _doth(yc * yc, e) * (1.0 / HEAD)
    yn = yc * _doth(lax.rsqrt(var + GN_EPS), et) * gn_w + gn_b
    bonus = _doth(r * kf * r_k, e)
    return (yn + _doth(bonus, et) * v) * g


def _foxprep_f(q, k, fl, qg8, kg8, fb, e, et, ft, fmask):
    def norm(t, g8):
        ms = _doth(t * t, e) * (1.0 / HEAD)
        return t * _doth(lax.rsqrt(ms + RMS_EPS), et) * _doth(g8, ft)[0:1]
    logf = jax.nn.log_sigmoid(fl + fb) * fmask
    return norm(q, qg8), norm(k, kg8), logf


def _merge_f(za, zb, pa, pb):
    return jax.nn.sigmoid(za) * pa + jax.nn.sigmoid(zb) * pb


def _swiglu_f(gate, up):
    return jax.nn.silu(gate) * up


def _tri(n, strict):
    row = lax.broadcasted_iota(jnp.int32, (n, n), 0)
    col = lax.broadcasted_iota(jnp.int32, (n, n), 1)
    return (row > col) if strict else (row >= col)


def _split2(x):
    hi = x.astype(BF16)
    return hi, (x - hi.astype(F32)).astype(BF16)


def _mm3_raw(a, b, ca, cb):
    dn = (((ca,), (cb,)), ((), ()))
    ah, al = _split2(a)
    bh, bl = _split2(b)
    dot = lambda p, q: lax.dot_general(p, q, dn, preferred_element_type=F32)
    return dot(ah, bh) + (dot(al, bh) + dot(ah, bl))


@functools.partial(jax.custom_vjp, nondiff_argnums=(2, 3))
def _mm3(a, b, ca, cb):
    return _mm3_raw(a, b, ca, cb)


def _mm3_fwd(a, b, ca, cb):
    return _mm3_raw(a, b, ca, cb), (a, b)


def _mm3_bwd(ca, cb, res, ct):
    a, b = res
    da = _mm3_raw(ct, b, 1, 1 - cb) if ca == 1 else _mm3_raw(b, ct, 1 - cb, 1)
    db = _mm3_raw(a, ct, 1 - ca, 0) if cb == 0 else _mm3_raw(ct, a, 0, 1 - ca)
    return da, db


_mm3.defvjp(_mm3_fwd, _mm3_bwd)


def _mmx_raw(t, x, ct):
    x1 = x.astype(BF16)
    r1 = x - x1.astype(F32)
    x2 = r1.astype(BF16)
    x3 = (r1 - x2.astype(F32)).astype(BF16)
    tb = t.astype(BF16)
    dot = lambda q: lax.dot_general(tb, q, (((ct,), (0,)), ((), ())), preferred_element_type=F32)
    return dot(x1) + (dot(x2) + dot(x3))


@jax.custom_vjp
def _mmx(t, x):
    return _mmx_raw(t, x, 1)


def _mmx_fwd(t, x):
    return _mmx_raw(t, x, 1), t


def _mmx_bwd(t, ct):
    return jnp.zeros_like(t), _mmx_raw(t, ct, 0)


_mmx.defvjp(_mmx_fwd, _mmx_bwd)

SCAN_HEADS = 16


def _scan_step(r, lw, k, v, a, b, st):
    c = r.shape[0]
    nh = r.shape[1] // HEAD
    incl = _tri(c, False)
    row2 = lax.broadcasted_iota(jnp.int32, (2 * c, 2 * c), 0)
    col2 = lax.broadcasted_iota(jnp.int32, (2 * c, 2 * c), 1)
    t_row = jnp.where(row2 >= c, row2 - c, row2)
    t_col = jnp.where(col2 >= c, col2 - c, col2)
    mask2 = (t_row > t_col) | ((row2 >= c) & (t_row == t_col))
    right =lax.broadcasted_iota(jnp.int32, (c, 2 * c), 1) >= c
    eye = lax.broadcasted_iota(jnp.int32, (HEAD, HEAD), 0) == lax.broadcasted_iota(jnp.int32, (HEAD, HEAD), 1)
    cl = _mmx(incl.astype(F32), lw)
    last = cl[c - 1:c, :]
    rt = r * jnp.exp(cl)
    at = a * jnp.exp(cl - lw)
    pinv = jnp.exp(-cl)
    bt = b * pinv
    kt = k * pinv
    pend = jnp.exp(last - cl)
    bl = b * pend
    kl = k * pend
    pe_last = jnp.exp(last)
    hs = range(nh)
    sl = [slice(h * HEAD, (h + 1) * HEAD) for h in hs]
    ar = [jnp.concatenate([at[:, sl[h]], rt[:, sl[h]]], axis=0) for h in hs]
    bk = [jnp.concatenate([bt[:, sl[h]], kt[:, sl[h]]], axis=0) for h in hs]
    amat = [jnp.where(mask2, _mm3(ar[h], bk[h], 1, 1), 0.0) for h in hs]
    res = [_mm3(jnp.concatenate([ar[h], amat[h][:, c:]], axis=1),
                jnp.concatenate([st[h], v[:, sl[h]]], axis=0), 1, 0) for h in hs]
    z = [jnp.concatenate([amat[h][:c, :c], res[h][:c]], axis=1) for h in hs]
    for _ in range(max(1, int(np.ceil(np.log2(c))))):
        z = [_mm3(z[h][:, :c], z[h], 1, 0) + jnp.where(right, z[h], 0.0) for h in hs]
    u = [z[h][:, c:] for h in hs]
    ys = [res[h][c:] + _mm3(amat[h][c:, :c], u[h], 1, 0) for h in hs]
    s1s = [_mm3(jnp.concatenate([bl[:, sl[h]], kl[:, sl[h]], jnp.where(eye, pe_last[:, sl[h]], 0.0)], axis=0),
                jnp.concatenate([u[h], v[:, sl[h]], st[h]], axis=0), 0, 0) for h in hs]
    return tuple(ys), tuple(s1s)


def _scan_heads(W):
    return SCAN_HEADS if W % (SCAN_HEADS * HEAD) == 0 else 2


def _scan_fwd(r, lw, k, v, a, b):
    L, W = r.shape
    nh = _scan_heads(W)
    nc, ng = L // CHUNK, W // (nh * HEAD)
    spec = pl.BlockSpec((CHUNK, nh * HEAD), lambda p, c: (c, p))

    def body(r_ref, lw_ref, k_ref, v_ref, a_ref, b_ref, y_ref, s_ref, st):
        @pl.when(pl.program_id(1) == 0)
        def _():
            st[...] = jnp.zeros_like(st)

        s0 = st[...]
        ys, s1s = _scan_step(r_ref[...], lw_ref[...], k_ref[...], v_ref[...], a_ref[...], b_ref[...], s0)
        s_ref[0] = s0
        st[...] = jnp.stack(s1s)
        y_ref[...] = jnp.concatenate(ys, axis=1)

    return pl.pallas_call(
        body, name="wkv7_fwd",
        out_shape=[jax.ShapeDtypeStruct((L, W), F32), jax.ShapeDtypeStruct((nc, nh * ng, HEAD, HEAD), F32)],
        grid=(ng, nc), in_specs=[spec] * 6,
        out_specs=[spec, pl.BlockSpec((1, nh, HEAD, HEAD), lambda p, c: (c, p, 0, 0))],
        scratch_shapes=[pltpu.VMEM((nh, HEAD, HEAD), F32)],
        compiler_params=_params(("parallel", "arbitrary")),
    )(r, lw, k, v, a, b)


def _scan_bwd(r, lw, k, v, a, b, s_all, dy):
    L, W = r.shape
    nh = _scan_heads(W)
    nc, ng = L // CHUNK, W // (nh * HEAD)
    spec = pl.BlockSpec((CHUNK, nh * HEAD), lambda p, c: (nc - 1 - c, p))

    def body(r_ref, lw_ref, k_ref, v_ref, a_ref, b_ref, s_ref, dy_ref,
             dr_ref, dlw_ref, dk_ref, dv_ref, da_ref, db_ref, dst):
        @pl.when(pl.program_id(1) == 0)
        def _():
            dst[...] = jnp.zeros_like(dst)

        _, vjp = jax.vjp(_scan_step, r_ref[...], lw_ref[...], k_ref[...], v_ref[...], a_ref[...], b_ref[...],
                         s_ref[0])
        dys = tuple(dy_ref[:, h * HEAD:(h + 1) * HEAD] for h in range(nh))
        g = vjp((dys, tuple(dst[h] for h in range(nh))))
        for ref, val in zip((dr_ref, dlw_ref, dk_ref, dv_ref, da_ref, db_ref), g[:6]):
            ref[...] = val
        dst[...] = g[6]

    return pl.pallas_call(
        body, name="wkv7_bwd", out_shape=[jax.ShapeDtypeStruct((L, W), F32)] * 6,
        grid=(ng, nc),
        in_specs=[spec] * 6 + [pl.BlockSpec((1, nh, HEAD, HEAD), lambda p, c: (nc - 1 - c, p, 0, 0)), spec],
        out_specs=[spec] * 6,
        scratch_shapes=[pltpu.VMEM((nh, HEAD, HEAD), F32)],
        compiler_params=_params(("parallel", "arbitrary")),
    )(r, lw, k, v, a, b, s_all, dy)


def _cumsum_rows(logf):
    L = logf.shape[0]
    t = LANES

    def body(x_ref, o_ref, carry):
        @pl.when(pl.program_id(0) == 0)
        def _():
            carry[...] = jnp.zeros_like(carry)

        c = _doth(_tri(t, False).astype(F32), x_ref[...]) + carry[...]
        carry[...] = c[t - 1:t, :]
        o_ref[...] = c.T

    return pl.pallas_call(
        body, name="fox_cumsum", out_shape=jax.ShapeDtypeStruct((LANES, L), F32), grid=(L // t,),
        in_specs=[pl.BlockSpec((t, LANES), lambda i: (i, 0))],
        out_specs=pl.BlockSpec((LANES, t), lambda i: (0, i)),
        scratch_shapes=[pltpu.VMEM((1, LANES), F32)], compiler_params=_params(("arbitrary",)),
    )(logf)


def _rcumsum_cols(dct):
    L = dct.shape[1]
    t = LANES
    n = L // t

    def body(x_ref, o_ref, carry):
        @pl.when(pl.program_id(0) == 0)
        def _():
            carry[...] = jnp.zeros_like(carry)

        rc = _doth(x_ref[...], _tri(t, False).astype(F32)) + carry[...]
        carry[...] = rc[:, 0:1]
        o_ref[...] = rc.T

    return pl.pallas_call(
        body, name="fox_rcumsum", out_shape=jax.ShapeDtypeStruct((L, LANES), F32), grid=(n,),
        in_specs=[pl.BlockSpec((LANES, t), lambda i: (0, n - 1 - i))],
        out_specs=pl.BlockSpec((t, LANES), lambda i: (n - 1 - i, 0)),
        scratch_shapes=[pltpu.VMEM((LANES, 1), F32)], compiler_params=_params(("arbitrary",)),
    )(dct)


def _attn_tile(L):
    return _pick(L, (384, 256, 128))


def _attn_scores(q, k, ck, qi, kj, t):
    s = _dot_bnt(q, k) * ATTN_SCALE - ck
    qpos = qi * t + lax.broadcasted_iota(jnp.int32, (t, t), 0)
    kpos = kj * t + lax.broadcasted_iota(jnp.int32, (t, t), 1)
    mask = (kpos <= qpos) & (kpos >= PAD_ROWS)
    return jnp.where(mask, s, NEG), mask


def _dot_bnt(a, b):
    return lax.dot_general(_bf(a), _bf(b), (((1,), (1,)), ((), ())), preferred_element_type=F32)


def _dot_btn(a, b):
    return lax.dot_general(_bf(a), _bf(b), (((0,), (0,)), ((), ())), preferred_element_type=F32)


def _ck_rows(ct_ref, p):
    r0 = 2 * (p % 4)
    return ct_ref[pl.ds(r0, 1), :], ct_ref[pl.ds(r0 + 1, 1), :]


def _attn_fwd(q, k, v, ct):
    L, W = q.shape
    t = _attn_tile(L)
    nt, npair = L // t, W // LANES
    qspec = pl.BlockSpec((t, LANES), lambda p, i, j: (i, p))
    kspec = pl.BlockSpec((t, LANES), lambda p, i, j: (jnp.minimum(i, j), p))
    cspec = pl.BlockSpec((8, t), lambda p, i, j: (p // 4, jnp.minimum(i, j)))

    def body(q_ref, k_ref, v_ref, ct_ref, o_ref, lse_ref, m_s, l_s, acc):
        p, i, j = pl.program_id(0), pl.program_id(1), pl.program_id(2)

        @pl.when(j == 0)
        def _():
            m_s[...] = jnp.full_like(m_s, NEG)
            l_s[...] = jnp.zeros_like(l_s)
            acc[...] = jnp.zeros_like(acc)

        @pl.when(j <= i)
        def _():
            cks = _ck_rows(ct_ref, p)
            for hh in range(2):
                sl = slice(hh * HEAD, (hh + 1) * HEAD)
                s, _ = _attn_scores(q_ref[:, sl], k_ref[:, sl], cks[hh], i, j, t)
                m_old = m_s[hh]
                m_new = jnp.maximum(m_old, jnp.max(s, axis=-1, keepdims=True))
                alpha = jnp.exp(m_old - m_new)
                pr = jnp.exp(s - m_new)
                l_s[hh] = alpha * l_s[hh] + jnp.sum(pr, axis=-1, keepdims=True)
                acc[hh] = alpha * acc[hh] + _dotb(pr, v_ref[:, sl])
                m_s[hh] = m_new

        @pl.when(j == i)
        def _():
            lane = lax.broadcasted_iota(jnp.int32, (t, LANES), 1)
            lse = jnp.zeros((t, LANES), F32)
            for hh in range(2):
                o_ref[:, hh * HEAD:(hh + 1) * HEAD] = acc[hh] / l_s[hh]
                lse = jnp.where(lane == hh, m_s[hh] + jnp.log(l_s[hh]), lse)
            lse_ref[0] = lse

    return pl.pallas_call(
        body, name="fox_attn_fwd",
        out_shape=[jax.ShapeDtypeStruct((L, W), F32), jax.ShapeDtypeStruct((npair, L, LANES), F32)],
        grid=(npair, nt, nt), in_specs=[qspec, kspec, kspec, cspec],
        out_specs=[qspec, pl.BlockSpec((1, t, LANES), lambda p, i, j: (p, i, 0))],
        scratch_shapes=[pltpu.VMEM((2, t, 1), F32), pltpu.VMEM((2, t, 1), F32), pltpu.VMEM((2, t, HEAD), F32)],
        compiler_params=_params(("parallel", "parallel", "arbitrary")),
    )(q, k, v, ct)


def _attn_bwd_delta(q, k, v, ct, lse, do):
    L, W = q.shape
    t = _attn_tile(L)
    nt, npair = L // t, W // LANES
    qspec = pl.BlockSpec((t, LANES), lambda p, i, j: (i, p))
    kspec = pl.BlockSpec((t, LANES), lambda p, i, j: (jnp.minimum(i, j), p))
    cspec = pl.BlockSpec((8, t), lambda p, i, j: (p // 4, jnp.minimum(i, j)))
    lspec = pl.BlockSpec((1, t, LANES), lambda p, i, j: (p, i, 0))

    def body(q_ref, k_ref, v_ref, ct_ref, lse_ref, do_ref, dl_ref, acc):
        p, i, j = pl.program_id(0), pl.program_id(1), pl.program_id(2)

        @pl.when(j == 0)
        def _():
            acc[...] = jnp.zeros_like(acc)

        @pl.when(j <= i)
        def _():
            cks = _ck_rows(ct_ref, p)
            for hh in range(2):
                sl = slice(hh * HEAD, (hh + 1) * HEAD)
                s, mask = _attn_scores(q_ref[:, sl], k_ref[:, sl], cks[hh], i, j, t)
                pr = jnp.where(mask, jnp.exp(s - lse_ref[0, :, hh:hh + 1]), 0.0)
                acc[hh] += jnp.sum(pr * _dot_bnt(do_ref[:, sl], v_ref[:, sl]), axis=-1, keepdims=True)

        @pl.when(j == i)
        def _():
            lane = lax.broadcasted_iota(jnp.int32, (t, LANES), 1)
            dl = jnp.zeros((t, LANES), F32)
            for hh in range(2):
                dl = jnp.where(lane == hh, acc[hh], dl)
            dl_ref[0] = dl

    return pl.pallas_call(
        body, name="fox_attn_bwd_delta", out_shape=jax.ShapeDtypeStruct((npair, L, LANES), F32),
        grid=(npair, nt, nt), in_specs=[qspec, kspec, kspec, cspec, lspec, qspec],
        out_specs=lspec, scratch_shapes=[pltpu.VMEM((2, t, 1), F32)],
        compiler_params=_params(("parallel", "parallel", "arbitrary")),
    )(q, k, v, ct, lse, do)


def _attn_bwd_dq(q, k, v, ct, delta, lse, do):
    L, W = q.shape
    t = _attn_tile(L)
    nt, npair = L // t, W // LANES
    qspec = pl.BlockSpec((t, LANES), lambda p, i, j: (i, p))
    kspec = pl.BlockSpec((t, LANES), lambda p, i, j: (jnp.minimum(i, j), p))
    cspec = pl.BlockSpec((8, t), lambda p, i, j: (p // 4, jnp.minimum(i, j)))
    lspec = pl.BlockSpec((1, t, LANES), lambda p, i, j: (p, i, 0))

    def body(q_ref, k_ref, v_ref, ct_ref, dl_ref, lse_ref, do_ref, dq_ref, acc):
        p, i, j = pl.program_id(0), pl.program_id(1), pl.program_id(2)

        @pl.when(j == 0)
        def _():
            acc[...] = jnp.zeros_like(acc)

        @pl.when(j <= i)
        def _():
            cks = _ck_rows(ct_ref, p)
            for hh in range(2):
                sl = slice(hh * HEAD, (hh + 1) * HEAD)
                s, mask = _attn_scores(q_ref[:, sl], k_ref[:, sl], cks[hh], i, j, t)
                pr = jnp.where(mask, jnp.exp(s - lse_ref[0, :, hh:hh + 1]), 0.0)
                ds = pr * (_dot_bnt(do_ref[:, sl], v_ref[:, sl]) - dl_ref[0, :, hh:hh + 1])
                acc[hh] += _dotb(ds, k_ref[:, sl])

        @pl.when(j == i)
        def _():
            for hh in range(2):
                dq_ref[:, hh * HEAD:(hh + 1) * HEAD] = acc[hh] * ATTN_SCALE

    return pl.pallas_call(
        body, name="fox_attn_bwd_dq", out_shape=jax.ShapeDtypeStruct((L, W), F32),
        grid=(npair, nt, nt), in_specs=[qspec, kspec, kspec, cspec, lspec, lspec, qspec],
        out_specs=qspec, scratch_shapes=[pltpu.VMEM((2, t, HEAD), F32)],
        compiler_params=_params(("parallel", "parallel", "arbitrary")),
    )(q, k, v, ct, delta, lse, do)


def _attn_bwd_dkv(q, k, v, ct, delta, lse, do):
    L, W = q.shape
    t = _attn_tile(L)
    nt, npair = L // t, W // LANES
    kspec = pl.BlockSpec((t, LANES), lambda p, j, i: (j, p))
    qspec = pl.BlockSpec((t, LANES), lambda p, j, i: (jnp.maximum(i, j), p))
    cspec = pl.BlockSpec((8, t), lambda p, j, i: (p // 4, j))
    lspec = pl.BlockSpec((1, t, LANES), lambda p, j, i: (p, jnp.maximum(i, j), 0))

    def body(q_ref, k_ref, v_ref, ct_ref, dl_ref, lse_ref, do_ref, dk_ref, dv_ref, dc_ref, dk_s, dv_s, dc_s):
        p, j, i = pl.program_id(0), pl.program_id(1), pl.program_id(2)

        @pl.when(i == 0)
        def _():
            dk_s[...] = jnp.zeros_like(dk_s)
            dv_s[...] = jnp.zeros_like(dv_s)
            dc_s[...] = jnp.zeros_like(dc_s)

        @pl.when(i >= j)
        def _():
            cks = _ck_rows(ct_ref, p)
            for hh in range(2):
                sl = slice(hh * HEAD, (hh + 1) * HEAD)
                s, mask = _attn_scores(q_ref[:, sl], k_ref[:, sl], cks[hh], i, j, t)
                pr = jnp.where(mask, jnp.exp(s - lse_ref[0, :, hh:hh + 1]), 0.0)
                doh = do_ref[:, sl]
                ds = pr * (_dot_bnt(doh, v_ref[:, sl]) - dl_ref[0, :, hh:hh + 1])
                dv_s[hh] += _dot_btn(pr, doh)
                dk_s[hh] += _dot_btn(ds, q_ref[:, sl])
                dc_s[hh] += -jnp.sum(ds, axis=0, keepdims=True)

        @pl.when(i == nt - 1)
        def _():
            row = lax.broadcasted_iota(jnp.int32, (8, t), 0)
            dc = jnp.zeros((8, t), F32)
            for hh in range(2):
                dk_ref[:, hh * HEAD:(hh + 1) * HEAD] = dk_s[hh] * ATTN_SCALE
                dv_ref[:, hh * HEAD:(hh + 1) * HEAD] = dv_s[hh]
                dc = jnp.where(row == hh, dc_s[hh], dc)
            dc_ref[0] = dc

    return pl.pallas_call(
        body, name="fox_attn_bwd_dkv",
        out_shape=[jax.ShapeDtypeStruct((L, W), F32), jax.ShapeDtypeStruct((L, W), F32),
                   jax.ShapeDtypeStruct((npair, 8, L), F32)],
        grid=(npair, nt, nt), in_specs=[qspec, kspec, kspec, cspec, lspec, lspec, qspec],
        out_specs=[kspec, kspec, pl.BlockSpec((1, 8, t), lambda p, j, i: (p, 0, j))],
        scratch_shapes=[pltpu.VMEM((2, t, HEAD), F32), pltpu.VMEM((2, t, HEAD), F32), pltpu.VMEM((2, 1, t), F32)],
        compiler_params=_params(("parallel", "parallel", "arbitrary")),
    )(q, k, v, ct, delta, lse, do)


def _place():
    x, y, c = lax.axis_index("x"), lax.axis_index("y"), lax.axis_index("c")
    chips = [(1 - x, y), (x, 1 - y), (1 - x, 1 - y)]
    return x, y, c, chips


def _remote(src, dst, send_sems, recv_sems, k, to):
    return pltpu.make_async_remote_copy(src_ref=src, dst_ref=dst, send_sem=send_sems.at[k],
                                        recv_sem=recv_sems.at[k], device_id=to, device_id_type=MESH)


def _all_gather_shards(shard):
    R = shard.shape[0]
    h = R // 2

    def body(x_ref, out_ref, send_sems, recv_sems, local_sem):
        x, y, c, chips = _place()
        me_chip = 2 * x + y
        sibling = (x, y, 1 - c)

        def half(chip, cc):
            return out_ref.at[chip, pl.ds(cc * h, h), :]

        mine = pltpu.make_async_copy(x_ref, out_ref.at[me_chip], local_sem)
        mine.start()
        first = [_remote(x_ref.at[pl.ds(c * h, h), :], half(me_chip, c), send_sems, recv_sems, k, (cx, cy, c))
                 for k, (cx, cy) in enumerate(chips)]
        for cp in first:
            cp.start()
        passed = []
        for k, (cx, cy) in enumerate(chips):
            landed = half(2 * cx + cy, c)
            _remote(landed, landed, send_sems, recv_sems, k, sibling).wait_recv()
            fwd = _remote(landed, landed, send_sems, recv_sems, 3 + k, sibling)
            fwd.start()
            passed.append(fwd)
        for k, (cx, cy) in enumerate(chips):
            other = half(2 * cx + cy, 1 - c)
            _remote(other, other, send_sems, recv_sems, 3 + k, sibling).wait_recv()
        for cp in first + passed:
            cp.wait_send()
        mine.wait()

    return pl.pallas_call(
        body, name="gather_weights", out_shape=jax.ShapeDtypeStruct((4, R, shard.shape[1]), shard.dtype),
        in_specs=[ANY], out_specs=ANY,
        scratch_shapes=[pltpu.SemaphoreType.DMA((6,)), pltpu.SemaphoreType.DMA((6,)), pltpu.SemaphoreType.DMA],
    )(shard)


def _pair_exchange(contrib):
    n, _, h, w = contrib.shape

    def body(x_ref, out_ref, send_sems, recv_sems):
        x, y, c, _ = _place()
        cp = _remote(x_ref.at[:, 1 - c], out_ref, send_sems, recv_sems, 0, (x, y, 1 - c))
        cp.start()
        cp.wait()

    return pl.pallas_call(
        body, name="reduce_pair_exchange", out_shape=jax.ShapeDtypeStruct((n, h, w), contrib.dtype),
        in_specs=[ANY], out_specs=ANY,
        scratch_shapes=[pltpu.SemaphoreType.DMA((1,)), pltpu.SemaphoreType.DMA((1,))],
    )(contrib)


def _chip_exchange(part):
    def body(x_ref, out_ref, send_sems, recv_sems, local_sem):
        x, y, c, chips = _place()
        me_chip = 2 * x + y
        mine = pltpu.make_async_copy(x_ref.at[me_chip], out_ref.at[me_chip], local_sem)
        mine.start()
        sends = [_remote(x_ref.at[2 * cx + cy], out_ref.at[me_chip], send_sems, recv_sems, k, (cx, cy, c))
                 for k, (cx, cy) in enumerate(chips)]
        for cp in sends:
            cp.start()
        for k, (cx, cy) in enumerate(chips):
            slot = out_ref.at[2 * cx + cy]
            _remote(slot, slot, send_sems, recv_sems, k, (cx, cy, c)).wait_recv()
        for cp in sends:
            cp.wait_send()
        mine.wait()

    return pl.pallas_call(
        body, name="reduce_chip_exchange", out_shape=jax.ShapeDtypeStruct(part.shape, part.dtype),
        in_specs=[ANY], out_specs=ANY,
        scratch_shapes=[pltpu.SemaphoreType.DMA((3,)), pltpu.SemaphoreType.DMA((3,)), pltpu.SemaphoreType.DMA],
    )(part)


def _pair_join(tot):
    h, w = tot.shape

    def body(x_ref, out_ref, send_sems, recv_sems, local_sem):
        x, y, c, _ = _place()
        mine = pltpu.make_async_copy(x_ref, out_ref.at[c], local_sem)
        mine.start()
        cp = _remote(x_ref, out_ref.at[c], send_sems, recv_sems, 0, (x, y, 1 - c))
        cp.start()
        other = out_ref.at[1 - c]
        _remote(other, other, send_sems, recv_sems, 0, (x, y, 1 - c)).wait_recv()
        cp.wait_send()
        mine.wait()

    return pl.pallas_call(
        body, name="reduce_pair_join", out_shape=jax.ShapeDtypeStruct((2, h, w), tot.dtype),
        in_specs=[ANY], out_specs=ANY,
        scratch_shapes=[pltpu.SemaphoreType.DMA((1,)), pltpu.SemaphoreType.DMA((1,)), pltpu.SemaphoreType.DMA],
    )(tot)


def _pair_add(contrib, recv):
    n, _, h, w = contrib.shape
    tr = _pick(h, (512, 256, 128, 64, 32, 16, 8))
    c = lax.axis_index("c").astype(jnp.int32).reshape((1,))

    def body(c_ref, a_ref, b_ref, o_ref):
        o_ref[...] = (a_ref[0] + b_ref[...]).astype(o_ref.dtype)

    return pl.pallas_call(
        body, name="reduce_pair_add", out_shape=jax.ShapeDtypeStruct((n, h, w), BF16),
        grid_spec=pltpu.PrefetchScalarGridSpec(
            num_scalar_prefetch=1, grid=(n, h // tr),
            in_specs=[pl.BlockSpec((1, 1, tr, w), lambda a, i, cr: (a, cr[0], i, 0)),
                      pl.BlockSpec((1, tr, w), lambda a, i, cr: (a, i, 0))],
            out_specs=pl.BlockSpec((1, tr, w), lambda a, i, cr: (a, i, 0))),
        compiler_params=_params(("parallel", "parallel")),
    )(c, contrib, recv)


def _chip_add(parts):
    n, h, w = parts.shape
    tr = _pick(h, (512, 256, 128, 64, 32, 16, 8))

    def body(a_ref, o_ref):
        f = lambda i: a_ref[i].astype(F32)
        o_ref[...] = ((f(0) + f(1)) + f(2)) + f(3)

    return pl.pallas_call(
        body, name="reduce_chip_add", out_shape=jax.ShapeDtypeStruct((h, w), F32), grid=(h // tr,),
        in_specs=[pl.BlockSpec((n, tr, w), lambda i: (0, i, 0))],
        out_specs=pl.BlockSpec((tr, w), lambda i: (i, 0)),
        compiler_params=_params(("parallel",)),
    )(parts)


def _adamw(name, w, g, m, v):
    R, C = w.shape
    tr = _pick(R, (128, 64, 32, 16, 8))
    spec = pl.BlockSpec((tr, C), lambda i: (i, 0))

    def body(w_ref, g_ref, m_ref, v_ref, d_ref, mo_ref, vo_ref):
        gr = g_ref[...]
        mn = ADAM_B1 * m_ref[...] + (1.0 - ADAM_B1) * gr
        vn = ADAM_B2 * v_ref[...] + (1.0 - ADAM_B2) * jnp.square(gr)
        m_hat = mn / (1.0 - ADAM_B1 ** ADAM_STEP)
        v_hat = vn / (1.0 - ADAM_B2 ** ADAM_STEP)
        d_ref[...] = -ADAM_LR * (m_hat / (jnp.sqrt(v_hat) + ADAM_EPS) + ADAM_WD * w_ref[...])
        mo_ref[...] = mn
        vo_ref[...] = vn

    return pl.pallas_call(
        body, name=name, out_shape=[jax.ShapeDtypeStruct((R, C), F32)] * 3, grid=(R // tr,),
        in_specs=[spec] * 4, out_specs=[spec] * 3, compiler_params=_params(("parallel",)),
    )(w, g, m, v)


def _pack(parts, dtype, row_mult):
    flat = jnp.concatenate([p.reshape(-1).astype(dtype) for p in parts])
    unit = row_mult * PACK_W
    pad = (-flat.shape[0]) % unit
    if pad:
        flat = jnp.concatenate([flat, jnp.zeros((pad,), dtype)])
    return flat.reshape(-1, PACK_W)


def _unpack(flat, shapes):
    out, off = [], 0
    for s in shapes:
        n = int(np.prod(s))
        out.append(flat[off:off + n].reshape(s))
        off += n
    return out


SHARDED = ("w_in", "rwkv_w2", "rwkv_a2", "rwkv_g2", "w_branch_a", "w_branch_b", "w_o", "w_gate_up", "w_down",
           "meta_tokens")
SHARD_AXIS = {"w_in": 1, "rwkv_w2": 1, "rwkv_a2": 1, "rwkv_g2": 1, "w_branch_a": 1, "w_branch_b": 1, "w_o": 0,
              "w_gate_up": 1, "w_down": 0, "meta_tokens": 1}
SMALL = ("norm1_g", "rwkv_mu", "rwkv_w0", "rwkv_a0", "rwkv_k_k", "rwkv_k_a", "rwkv_r_k", "rwkv_gn_w",
         "rwkv_gn_b", "fox_q_norm_g", "fox_k_norm_g", "fox_f_bias", "norm2_g")
WEIGHTS = ("meta_tokens", "norm1_g", "w_in", "rwkv_mu", "rwkv_w0", "rwkv_w2", "rwkv_a0", "rwkv_a2", "rwkv_g2",
           "rwkv_k_k", "rwkv_k_a", "rwkv_r_k", "rwkv_gn_w", "rwkv_gn_b", "fox_q_norm_g", "fox_k_norm_g",
           "fox_f_bias", "w_branch_a", "w_branch_b", "w_o", "norm2_g", "w_gate_up", "w_down")


def _pad_rows(t, rows):
    return jnp.concatenate([t, jnp.zeros((rows - t.shape[0],) + t.shape[1:], t.dtype)], axis=0)


def _pad_cols(t, cols):
    return jnp.concatenate([t, jnp.zeros(t.shape[:-1] + (cols - t.shape[-1],), t.dtype)], axis=-1)


def _step(x, tgt, wts, mom1, mom2):
    seq, D = x.shape
    L = SEQ_ROW0 + seq
    RW = wts["rwkv_w0"].shape[-1]
    DL, AL, GL = wts["rwkv_w2"].shape[0], wts["rwkv_a2"].shape[0], wts["rwkv_g2"].shape[0]
    FW = wts["w_branch_b"].shape[0]
    FH = wts["fox_f_bias"].shape[-1]
    DFF = wts["w_down"].shape[0] * 4
    LORA = DL + AL + GL
    LW = -(-(LORA + FH) // 512) * 512
    assert RW == FW and (6 * RW) % D == 0 and (6 * RW + 2 * D) % LW == 0 and LORA % 8 == 0
    xj = lax.axis_index("x")
    yj = lax.axis_index("y")
    chip = 2 * xj + yj

    meta_bits = lax.bitcast_convert_type(wts["meta_tokens"], BF16)
    send = [wts[n] for n in SHARDED[:-1]] + [meta_bits]
    gathered = _all_gather_shards(_pack(send, BF16, 32))
    shapes = [wts[n].shape for n in SHARDED[:-1]] + [meta_bits.shape]
    per_chip = [_unpack(gathered[j].reshape(-1), shapes) for j in range(4)]
    full = {}
    for idx, n in enumerate(SHARDED):
        full[n] = jnp.concatenate([per_chip[j][idx] for j in range(4)], axis=SHARD_AXIS[n])
    meta = lax.bitcast_convert_type(full["meta_tokens"], F32)
    w_in = full["w_in"]
    o = 0
    segs = {}
    for nm, wd in (("r", RW), ("k", RW), ("v", RW), ("wd", DL), ("ad", AL), ("gd", GL),
                   ("fq", FW), ("fk", FW), ("fv", FW), ("ff", FH), ("ga", D), ("gb", D)):
        segs[nm] = w_in[:, o:o + wd]
        o += wd
    lora_w = _pad_cols(jnp.concatenate([segs["wd"], segs["ad"], segs["gd"], segs["ff"]], axis=1), LW)
    w1 = jnp.concatenate([segs["r"], segs["k"], segs["v"], segs["fq"], segs["fk"], segs["fv"],
                          segs["ga"], segs["gb"], lora_w], axis=1)
    cb_f = 3
    cb_gate = (6 * RW) // D
    cb_lora = (6 * RW + 2 * D) // LW

    e_m, et_m = _head_mats(RW)
    ft_m = _fold_mat(RW)
    mu = wts["rwkv_mu"]
    mu_rkv = mu[:, :3 * RW]
    mu_l = _pad_cols(mu[:, 3 * RW:], LW)
    w2p = _pad_rows(full["rwkv_w2"].astype(F32), LW)
    a2p = _pad_rows(jnp.concatenate([jnp.zeros((DL, RW), F32), full["rwkv_a2"].astype(F32)], axis=0), LW)
    g2p = _pad_rows(jnp.concatenate([jnp.zeros((DL + AL, RW), F32), full["rwkv_g2"].astype(F32)], axis=0), LW)
    r_k = wts["rwkv_r_k"].reshape(1, RW)
    qg8 = jnp.broadcast_to(wts["fox_q_norm_g"], (8, HEAD))
    kg8 = jnp.broadcast_to(wts["fox_k_norm_g"], (8, HEAD))
    fb = _pad_cols(wts["fox_f_bias"], LANES)
    fmask = (jnp.arange(LANES) < FH).astype(F32).reshape(1, LANES)
    lmask = ((jnp.arange(LW) >= LORA) & (jnp.arange(LW) < LORA + FH)).astype(F32).reshape(1, LW)

    h0 = jnp.concatenate([jnp.zeros((PAD_ROWS, D), F32), meta, x], axis=0)
    n1 = wts["norm1_g"]

    (xn,), _ = _rowcall("rms1_fwd", lambda i, r, b: ([_rms_f(r[0], b[0])], []), L,
                        [(h0, D, 0, "row")], [n1], [(D, BF16)], [])
    proj = _matmul(xn, w1, "nn", F32, "proj_fwd")

    def shift_fn(i, r, b):
        rows = lax.broadcasted_iota(jnp.int32, (ROW_TILE, 1), 0)
        outs = []
        for z, halo, m_ in ((r[0], r[1], b[0]), (r[2], r[3], b[1])):
            first = jnp.where(i == 0, 0.0, halo[7:8, :])
            zp = jnp.where(rows == 0, first, pltpu.roll(z, 1, 0))
            outs.append(z + (zp - z) * m_)
        return outs, []

    rkv_w = 3 * RW
    (x_rkv, x_l), _ = _rowcall(
        "shift_fwd", shift_fn, L,
        [(proj, rkv_w, 0, "row"), (proj, rkv_w, 0, "prev"), (proj, LW, cb_lora, "row"), (proj, LW, cb_lora, "prev")],
        [mu_rkv, mu_l], [(rkv_w, F32), (LW, F32)], [])

    prep_p = [wts["rwkv_w0"], w2p, wts["rwkv_a0"], a2p, g2p, wts["rwkv_k_k"], wts["rwkv_k_a"], e_m, et_m]
    prep_rows = [(x_rkv, RW, 0, "row"), (x_rkv, RW, 1, "row"), (x_rkv, RW, 2, "row"), (x_l, LW, 0, "row")]
    (s_r, s_lw, s_k, s_v, s_a, s_b, gate_g), _ = _rowcall(
        "rwkv_prep_fwd", lambda i, r, b: (list(_prep_f(*r, *b)), []), L, prep_rows, prep_p,
        [(RW, F32)] * 7, [])
    y_scan, s_all = _scan_fwd(s_r, s_lw, s_k, s_v, s_a, s_b)
    post_p = [wts["rwkv_gn_w"], wts["rwkv_gn_b"], r_k, e_m, et_m]
    post_rows = [(y_scan, RW, 0, "row"), (s_r, RW, 0, "row"), (s_k, RW, 0, "row"), (s_v, RW, 0, "row"),
                 (gate_g, RW, 0, "row")]
    (y_a,), _ = _rowcall("rwkv_post_fwd", lambda i, r, b: ([_post_f(*r, *b)], []), L, post_rows, post_p,
                         [(RW, BF16)], [])

    fox_p = [qg8, kg8, fb, e_m, et_m, ft_m, fmask]

    def foxprep_fn(i, r, b):
        fl = _doth(r[2] * b[-1], b[-2])
        return list(_foxprep_f(r[0], r[1], fl, *b[:-2])), []

    sel = (np.arange(LW)[:, None] - LORA == np.arange(LANES)[None, :]).astype(np.float32)
    sel = jnp.asarray(sel)
    fox_rows = [(proj, FW, cb_f, "row"), (proj, FW, cb_f + 1, "row"), (proj, LW, cb_lora, "row")]
    (f_q, f_k, logf), _ = _rowcall("fox_prep_fwd", foxprep_fn, L, fox_rows, fox_p + [sel, lmask],
                                   [(FW, BF16), (FW, BF16), (LANES, F32)], [])
    ct = _cumsum_rows(logf)
    f_v = proj[:, (cb_f + 2) * FW:(cb_f + 3) * FW]
    y_b32, lse = _attn_fwd(f_q, f_k, f_v, ct)
    y_b = y_b32.astype(BF16)

    p_a = _matmul(y_a, full["w_branch_a"], "nn", F32, "branch_a_fwd")
    p_b = _matmul(y_b, full["w_branch_b"], "nn", F32, "branch_b_fwd")
    merge_rows = [(proj, D, cb_gate, "row"), (proj, D, cb_gate + 1, "row"), (p_a, D, 0, "row"), (p_b, D, 0, "row")]
    (merged,), _ = _rowcall("merge_fwd", lambda i, r, b: ([_merge_f(*r)], []), L, merge_rows, [], [(D, BF16)], [])
    h1 = _matmul(merged, full["w_o"], "nn", F32, "wo_fwd", add=h0)
    n2 = wts["norm2_g"]
    (xn2,), _ = _rowcall("rms2_fwd", lambda i, r, b: ([_rms_f(r[0], b[0])], []), L,
                         [(h1, D, 0, "row")], [n2], [(D, BF16)], [])
    gu = _matmul(xn2, full["w_gate_up"], "nn", F32, "gate_up_fwd")
    gu_rows = [(gu, DFF, 0, "row"), (gu, DFF, 1, "row")]
    (act,), _ = _rowcall("swiglu_fwd", lambda i, r, b: ([_swiglu_f(*r)], []), L, gu_rows, [], [(DFF, BF16)], [])
    h2 = _matmul(act, full["w_down"], "nn", F32, "down_fwd", add=h1)

    def loss_fn(i, r, b):
        err = jnp.where(i == 0, 0.0, r[0] - r[1])
        return [err * (1.0 / D)], [jnp.zeros((8, LANES), F32) + 0.5 / D * jnp.sum(err * err)]

    (dh2,), (loss_acc,) = _rowcall("loss", loss_fn, L, [(h2, D, 0, "row"), (tgt, D, 0, "lag")], [],
                                   [(D, F32)], [(8, LANES)])
    loss = lax.psum(loss_acc[0, 0], ("x", "y", "c"))

    dh2b = dh2.astype(BF16)
    g_w_down = _matmul(act, dh2b, "tn", F32, "down_dw")
    d_act = _matmul(dh2b, full["w_down"], "nt", F32, "down_dx")

    def swiglu_bwd(i, r, b):
        _, vjp = jax.vjp(_swiglu_f, r[0], r[1])
        return list(vjp(r[2])), []

    (d_gate, d_up), _ = _rowcall("swiglu_bwd", swiglu_bwd, L, gu_rows + [(d_act, DFF, 0, "row")], [],
                                 [(DFF, BF16), (DFF, BF16)], [])
    d_gu = jnp.concatenate([d_gate, d_up], axis=1)
    g_w_gu = _matmul(xn2, d_gu, "tn", F32, "gate_up_dw")
    d_xn2 = _matmul(d_gu, full["w_gate_up"], "nt", F32, "gate_up_dx")

    def rms_bwd(i, r, b):
        _, vjp = jax.vjp(_rms_f, r[0], b[0])
        dh, dg = vjp(r[1])
        return [dh + r[2]], [dg]

    (dh1,), (g_n2,) = _rowcall("rms2_bwd", rms_bwd, L,
                               [(h1, D, 0, "row"), (d_xn2, D, 0, "row"), (dh2, D, 0, "row")], [n2],
                               [(D, F32)], [(1, D)])
    dh1b = dh1.astype(BF16)
    g_w_o = _matmul(merged, dh1b, "tn", F32, "wo_dw")
    d_merged = _matmul(dh1b, full["w_o"], "nt", F32, "wo_dx")

    def merge_bwd(i, r, b):
        _, vjp = jax.vjp(_merge_f, *r[:4])
        return list(vjp(r[4])), []

    (d_za, d_zb, d_pa, d_pb), _ = _rowcall("merge_bwd", merge_bwd, L, merge_rows + [(d_merged, D, 0, "row")], [],
                                           [(D, BF16)] * 4, [])
    g_w_a = _matmul(y_a, d_pa, "tn", F32, "branch_a_dw")
    g_w_b = _matmul(y_b, d_pb, "tn", F32, "branch_b_dw")
    d_ya = _matmul(d_pa, full["w_branch_a"], "nt", F32, "branch_a_dx")
    d_yb = _matmul(d_pb, full["w_branch_b"], "nt", F32, "branch_b_dx")

    delta = _attn_bwd_delta(f_q, f_k, f_v, ct, lse, d_yb)
    d_fq = _attn_bwd_dq(f_q, f_k, f_v, ct, delta, lse, d_yb)
    d_fk, d_fv, dc_rows = _attn_bwd_dkv(f_q, f_k, f_v, ct, delta, lse, d_yb)
    dct = _pad_rows(dc_rows[:, :2, :].reshape(-1, L), LANES)
    d_logf = _rcumsum_cols(dct)

    def foxprep_bwd(i, r, b):
        def f(q, k, xl, qg, kg, fbias):
            return _foxprep_f(q, k, _doth(xl * b[-1], b[-2]), qg, kg, fbias, *b[3:7])
        _, vjp = jax.vjp(f, r[0], r[1], r[2], b[0], b[1], b[2])
        dq, dk, dxl, dqg, dkg, dfb = vjp((r[3], r[4], r[5]))
        return [dq, dk, dxl], [dqg, dkg, dfb]

    (d_zfq, d_zfk, d_zl_f), (g_qg8, g_kg8, g_fb) = _rowcall(
        "fox_prep_bwd", foxprep_bwd, L,
        fox_rows + [(d_fq, FW, 0, "row"), (d_fk, FW, 0, "row"), (d_logf, LANES, 0, "row")],
        fox_p + [sel, lmask], [(FW, BF16), (FW, BF16), (LW, F32)], [(8, HEAD), (8, HEAD), (1, LANES)])

    def post_bwd(i, r, b):
        _, vjp = jax.vjp(lambda *a: _post_f(*a, b[3], b[4]), *r[:5], b[0], b[1], b[2])
        g = vjp(r[5])
        return list(g[:5]), list(g[5:])

    (d_y, d_r1, d_k1, d_v1, d_g), (g_gn_w, g_gn_b, g_r_k) = _rowcall(
        "rwkv_post_bwd", post_bwd, L, post_rows + [(d_ya, RW, 0, "row")], post_p,
        [(RW, F32)] * 5, [(1, RW)] * 3)
    d_r2, d_lw, d_k2, d_v2, d_a, d_b = _scan_bwd(s_r, s_lw, s_k, s_v, s_a, s_b, s_all, d_y)

    def prep_bwd(i, r, b):
        _, vjp = jax.vjp(lambda *a: _prep_f(*a, b[7], b[8]), *r[:4], *b[:7])
        cts = (r[4] + r[10], r[5], r[6] + r[11], r[7] + r[12], r[8], r[9], r[13])
        g = vjp(cts)
        return list(g[:4]), list(g[4:])

    bwd_rows = prep_rows + [(d_r2, RW, 0, "row"), (d_lw, RW, 0, "row"), (d_k2, RW, 0, "row"), (d_v2, RW, 0, "row"),
                            (d_a, RW, 0, "row"), (d_b, RW, 0, "row"), (d_r1, RW, 0, "row"), (d_k1, RW, 0, "row"),
                            (d_v1, RW, 0, "row"), (d_g, RW, 0, "row")]
    (d_xr, d_xk, d_xv, d_xl), (g_w0, g_w2p, g_a0, g_a2p, g_g2p, g_kk, g_ka) = _rowcall(
        "rwkv_prep_bwd", prep_bwd, L, bwd_rows, prep_p, [(RW, F32)] * 3 + [(LW, F32)],
        [(1, RW), (LW, RW), (1, RW), (LW, RW), (LW, RW), (1, RW), (1, RW)])

    def shift_bwd(i, r, b):
        last = pl.num_programs(0) - 1
        rows = lax.broadcasted_iota(jnp.int32, (ROW_TILE, 1), 0)
        outs, sums = [], []
        groups = ((r[0], r[1], r[2], r[3], b[0], None), (r[4], r[5], r[6], r[7], b[1], r[8]))
        for d, dnext, z, zhalo, m_, extra in groups:
            nxt = jnp.where(i == last, 0.0, dnext[0:1, :])
            d_up = jnp.where(rows == ROW_TILE - 1, nxt, pltpu.roll(d, ROW_TILE - 1, 0))
            dz = d * (1.0 - m_) + d_up * m_
            if extra is not None:
                dz = dz + extra
            first = jnp.where(i == 0, 0.0, zhalo[7:8, :])
            zp = jnp.where(rows == 0, first, pltpu.roll(z, 1, 0))
            outs.append(dz)
            sums.append(_rowsum(d * (zp - z)))
        return outs, sums

    d_xrkv = jnp.concatenate([d_xr, d_xk, d_xv], axis=1)
    (d_zrkv, d_zl), (g_mu_rkv, g_mu_l) = _rowcall(
        "shift_bwd", shift_bwd, L,
        [(d_xrkv, rkv_w, 0, "row"), (d_xrkv, rkv_w, 0, "next"), (proj, rkv_w, 0, "row"), (proj, rkv_w, 0, "prev"),
         (d_xl, LW, 0, "row"), (d_xl, LW, 0, "next"), (proj, LW, cb_lora, "row"), (proj, LW, cb_lora, "prev"),
         (d_zl_f, LW, 0, "row")],
        [mu_rkv, mu_l], [(rkv_w, BF16), (LW, BF16)], [(1, rkv_w), (1, LW)])

    d_proj = jnp.concatenate([d_zrkv, d_zfq, d_zfk, d_fv.astype(BF16), d_za, d_zb, d_zl], axis=1)
    g_w1 = _matmul(xn, d_proj, "tn", F32, "proj_dw")
    d_xn = _matmul(d_proj, w1, "nt", F32, "proj_dx")
    (dh0,), (g_n1,) = _rowcall("rms1_bwd", rms_bwd, L,
                               [(h0, D, 0, "row"), (d_xn, D, 0, "row"), (dh1, D, 0, "row")], [n1],
                               [(D, F32)], [(1, D)])
    grad_x = dh0[SEQ_ROW0:]
    g_meta = dh0[PAD_ROWS:SEQ_ROW0]

    o = 0
    gs = {}
    for nm, wd in (("r", RW), ("k", RW), ("v", RW), ("fq", FW), ("fk", FW), ("fv", FW), ("ga", D), ("gb", D),
                   ("wd", DL), ("ad", AL), ("gd", GL), ("ff", FH)):
        gs[nm] = g_w1[:, o:o + wd]
        o += wd
    g_w_in = jnp.concatenate([gs[nm] for nm in ("r", "k", "v", "wd", "ad", "gd", "fq", "fk", "fv", "ff", "ga", "gb")],
                             axis=1)
    gfull = {
        "w_in": g_w_in, "rwkv_w2": g_w2p[:DL], "rwkv_a2": g_a2p[DL:DL + AL], "rwkv_g2": g_g2p[DL + AL:LORA],
        "w_branch_a": g_w_a, "w_branch_b": g_w_b, "w_o": g_w_o, "w_gate_up": g_w_gu, "w_down": g_w_down,
        "meta_tokens": g_meta,
    }
    g_mu = jnp.concatenate([g_mu_rkv, g_mu_l[:, :LORA]], axis=1)
    gsmall = {
        "norm1_g": g_n1, "rwkv_mu": g_mu, "rwkv_w0": g_w0, "rwkv_a0": g_a0, "rwkv_k_k": g_kk, "rwkv_k_a": g_ka,
        "rwkv_r_k": g_r_k.reshape(wts["rwkv_r_k"].shape), "rwkv_gn_w": g_gn_w, "rwkv_gn_b": g_gn_b,
        "fox_q_norm_g": g_qg8[0:1], "fox_k_norm_g": g_kg8[0:1], "fox_f_bias": g_fb[:, :FH], "norm2_g": g_n2,
    }
    small_flat = jnp.concatenate([gsmall[n].reshape(-1) for n in SMALL])

    def shard_of(n, j):
        t = gfull[n]
        w = t.shape[SHARD_AXIS[n]] // 4
        return lax.slice_in_dim(t, j * w, (j + 1) * w, axis=SHARD_AXIS[n])

    contrib = jnp.stack([_pack([shard_of(n, j) for n in SHARDED] + [small_flat], F32, 2048) for j in range(4)])
    rows = contrib.shape[1]
    contrib = contrib.reshape(4, 2, rows // 2, PACK_W)
    pair = _pair_add(contrib, _pair_exchange(contrib))
    tot = _chip_add(_chip_exchange(pair))
    gsum = _pair_join(tot).reshape(-1)
    red = _unpack(gsum, [wts[n].shape for n in SHARDED] + [small_flat.shape])
    grads = {n: red[i] for i, n in enumerate(SHARDED)}
    for n, t in zip(SMALL, _unpack(red[-1], [wts[n].shape for n in SMALL])):
        grads[n] = t

    delta, new_m, new_v = {}, {}, {}
    for n in SHARDED:
        delta[n], new_m[n], new_v[n] = _adamw("adamw_" + n, wts[n], grads[n], mom1[n], mom2[n])
    pk = lambda d: _pack([d[n] for n in SMALL], F32, 8)
    ds, ms, vs = _adamw("adamw_small", pk(wts), pk(grads), pk(mom1), pk(mom2))
    small_shapes = [wts[n].shape for n in SMALL]
    for dst, src in ((delta, ds), (new_m, ms), (new_v, vs)):
        for n, t in zip(SMALL, _unpack(src.reshape(-1), small_shapes)):
            dst[n] = t
    return loss, grad_x, grads, delta, new_m, new_v


def kernel(x, meta_tokens, norm1_g, w_in, rwkv_mu, rwkv_w0, rwkv_w2, rwkv_a0, rwkv_a2, rwkv_g2, rwkv_k_k, rwkv_k_a, rwkv_r_k, rwkv_gn_w, rwkv_gn_b, fox_q_norm_g, fox_k_norm_g, fox_f_bias, w_branch_a, w_branch_b, w_o, norm2_g, w_gate_up, w_down, loss_target, m_meta_tokens, m_norm1_g, m_w_in, m_rwkv_mu, m_rwkv_w0, m_rwkv_w2, m_rwkv_a0, m_rwkv_a2, m_rwkv_g2, m_rwkv_k_k, m_rwkv_k_a, m_rwkv_r_k, m_rwkv_gn_w, m_rwkv_gn_b, m_fox_q_norm_g, m_fox_k_norm_g, m_fox_f_bias, m_w_branch_a, m_w_branch_b, m_w_o, m_norm2_g, m_w_gate_up, m_w_down, v_meta_tokens, v_norm1_g, v_w_in, v_rwkv_mu, v_rwkv_w0, v_rwkv_w2, v_rwkv_a0, v_rwkv_a2, v_rwkv_g2, v_rwkv_k_k, v_rwkv_k_a, v_rwkv_r_k, v_rwkv_gn_w, v_rwkv_gn_b, v_fox_q_norm_g, v_fox_k_norm_g, v_fox_f_bias, v_w_branch_a, v_w_branch_b, v_w_o, v_norm2_g, v_w_gate_up, v_w_down):
    args = dict(locals())
    shapes = {n: args[n].shape for n in WEIGHTS}

    def drop_depth(t, n):
        if n == "meta_tokens":
            return t
        if n == "rwkv_r_k":
            return t.reshape(1, -1)
        return t.reshape(t.shape[1:]) if t.ndim == 3 else t

    wts = {n: drop_depth(args[n], n) for n in WEIGHTS}
    mom1 = {n: drop_depth(args["m_" + n], n) for n in WEIGHTS}
    mom2 = {n: drop_depth(args["v_" + n], n) for n in WEIGHTS}
    loss, grad_x, grads, delta, new_m, new_v = _step(x[0], loss_target[0], wts, mom1, mom2)
    outs = [loss, grad_x[None]]
    for d in (grads, delta, new_m, new_v):
        outs += [d[n].reshape(shapes[n]) for n in WEIGHTS]
    return tuple(outs)
```

```python
import functools

import jax
import jax.numpy as jnp
import numpy as np
from jax import lax
from jax.experimental import pallas as pl
from jax.experimental.pallas import tpu as pltpu

F32 = jnp.float32
BF16 = jnp.bfloat16
HI = lax.Precision.HIGHEST
MESH = pl.DeviceIdType.MESH
ANY = pl.BlockSpec(memory_space=pl.ANY)

N_META = 16
HEAD = 64
ROW_TILE = 128
PAD_ROWS = ROW_TILE - N_META
SEQ_ROW0 = ROW_TILE
CHUNK = 64
LANES = 128
PACK_W = 1024
RMS_EPS = 1e-6
GN_EPS = 64e-5
ATTN_SCALE = HEAD ** -0.5
NEG = -1e30
VMEM_LIMIT_V7X = 56 * 1024 * 1024

ADAM_LR = 0.001
ADAM_B1 = 0.9
ADAM_B2 = 0.999
ADAM_EPS = 1e-08
ADAM_WD = 0.01
ADAM_STEP = 10


def _params(sem=None):
    return pltpu.CompilerParams(dimension_semantics=sem, vmem_limit_bytes=VMEM_LIMIT_V7X)


def _pick(n, cands):
    for c in cands:
        if n % c == 0:
            return c
    return n


def _bf(t):
    return t.astype(BF16)


def _dotb(a, b):
    return jnp.dot(_bf(a), _bf(b), preferred_element_type=F32)


def _doth(a, b):
    return jnp.dot(a, b, precision=HI, preferred_element_type=F32)


_BIG = (2048, 1536, 1408, 1024, 768, 704, 512, 384, 256, 128)


def _matmul(a, b, mode, out_dtype, name, add=None, col_blocks=1):
    if mode == "nn":
        (M, R), (_, N) = a.shape, b.shape
        dims = (((1,), (0,)), ((), ()))
    elif mode == "nt":
        (M, R), (N, _) = a.shape, b.shape
        dims = (((1,), (1,)), ((), ()))
    else:
        (R, M), (_, N) = a.shape, b.shape
        dims = (((0,), (0,)), ((), ()))
    tm = _pick(M, (1408, 1024, 768, 512, 384, 256, 128))
    nb = N // col_blocks
    tn = _pick(nb, (1024, 896, 768, 512, 384, 256, 128)) if mode == "tn" else _pick(nb, (512, 384, 256, 128))
    per = nb // tn
    tr = _pick(R, (1408, 1056, 768, 512, 384, 256, 128)) if mode == "tn" else _pick(R, _BIG)
    nr = R // tr

    if mode == "nn":
        a_spec = pl.BlockSpec((tm, tr), lambda i, j, r: (i, r))
        b_spec = pl.BlockSpec((tr, tn), lambda i, j, r: (r, j))
    elif mode == "nt":
        a_spec = pl.BlockSpec((tm, tr), lambda i, j, r: (i, r))
        b_spec = pl.BlockSpec((tn, tr), lambda i, j, r: (j, r))
    else:
        a_spec = pl.BlockSpec((tr, tm), lambda i, j, r: (r, i))
        b_spec = pl.BlockSpec((tr, tn), lambda i, j, r: (r, j))
    if col_blocks == 1:
        o_spec = pl.BlockSpec((tm, tn), lambda i, j, r: (i, j))
        o_shape = (M, N)
    else:
        o_spec = pl.BlockSpec((1, tm, tn), lambda i, j, r: (j // per, i, j % per))
        o_shape = (col_blocks, M, nb)
    has_add = add is not None

    def body(*refs):
        if has_add:
            a_ref, b_ref, add_ref, o_ref, acc = refs
        else:
            a_ref, b_ref, o_ref, acc = refs
        r = pl.program_id(2)

        @pl.when(r == 0)
        def _():
            acc[...] = jnp.zeros_like(acc)

        acc[...] += lax.dot_general(_bf(a_ref[...]), _bf(b_ref[...]), dims, preferred_element_type=F32)

        @pl.when(r == nr - 1)
        def _():
            res = acc[...]
            if has_add:
                res = res + add_ref[...]
            o_ref[...] = res.astype(o_ref.dtype).reshape(o_ref.shape)

    ins = [a, b] + ([add] if has_add else [])
    specs = [a_spec, b_spec] + ([o_spec] if has_add else [])
    return pl.pallas_call(
        body, name=name, out_shape=jax.ShapeDtypeStruct(o_shape, out_dtype),
        grid=(M // tm, N // tn, nr), in_specs=specs, out_specs=o_spec,
        scratch_shapes=[pltpu.VMEM((tm, tn), F32)],
        compiler_params=_params(("parallel", "parallel", "arbitrary")),
    )(*ins)


def _rowcall(name, fn, L, row_ins, bc_ins, row_outs, acc_outs, tm=ROW_TILE):
    nt = L // tm
    specs = []
    for arr, w, cb, kind in row_ins:
        if kind == "row":
            specs.append(pl.BlockSpec((tm, w), lambda i, cb=cb: (i, cb)))
        elif kind == "lag":
            specs.append(pl.BlockSpec((tm, w), lambda i, cb=cb: (jnp.maximum(i - 1, 0), cb)))
        elif kind == "prev":
            specs.append(pl.BlockSpec((8, w), lambda i, cb=cb: (jnp.maximum(i * (tm // 8) - 1, 0), cb)))
        else:
            specs.append(pl.BlockSpec((8, w), lambda i, cb=cb: (jnp.minimum((i + 1) * (tm // 8), L // 8 - 1), cb)))
    for arr in bc_ins:
        specs.append(pl.BlockSpec(arr.shape, lambda i, nd=arr.ndim: (0,) * nd))
    out_shapes = [jax.ShapeDtypeStruct((L, w), dt) for w, dt in row_outs]
    out_specs = [pl.BlockSpec((tm, w), lambda i: (i, 0)) for w, dt in row_outs]
    out_shapes += [jax.ShapeDtypeStruct(s, F32) for s in acc_outs]
    out_specs += [pl.BlockSpec(s, lambda i, nd=len(s): (0,) * nd) for s in acc_outs]
    n_row, n_bc, n_ro = len(row_ins), len(bc_ins), len(row_outs)

    def body(*refs):
        i = pl.program_id(0)
        vals = [r[...] for r in refs[: n_row + n_bc]]
        outs, sums = fn(i, vals[:n_row], vals[n_row:])
        o_refs = refs[n_row + n_bc:]
        for r, v in zip(o_refs[:n_ro], outs):
            r[...] = v.astype(r.dtype)

        @pl.when(i == 0)
        def _():
            for r in o_refs[n_ro:]:
                r[...] = jnp.zeros_like(r)

        for r, v in zip(o_refs[n_ro:], sums):
            r[...] += v

    res = pl.pallas_call(
        body, name=name, out_shape=out_shapes, grid=(nt,), in_specs=specs, out_specs=out_specs,
        compiler_params=_params(("arbitrary",)),
    )(*[a for a, _, _, _ in row_ins], *bc_ins)
    return list(res[:n_ro]), list(res[n_ro:])


def _rowsum(t):
    return jnp.sum(t, axis=0, keepdims=True)


def _head_mats(width):
    e = (np.arange(width)[:, None] // HEAD == np.arange(LANES)[None, :]).astype(np.float32)
    return jnp.asarray(e), jnp.asarray(e.T)


def _fold_mat(width):
    ft = (np.arange(HEAD)[:, None] == np.arange(width)[None, :] % HEAD).astype(np.float32)
    return jnp.asarray(ft)


def _rms_f(h, g):
    return (h * lax.rsqrt(jnp.mean(h * h, axis=-1, keepdims=True) + RMS_EPS)) * g


def _prep_f(xr, xk, xv, xl, w0, w2p, a0, a2p, g2p, k_k, k_a, e, et):
    w_log = -jax.nn.softplus(-(w0 + _dotb(jnp.tanh(xl), w2p))) - 0.5
    lw = -jnp.exp(w_log)
    a = jax.nn.sigmoid(a0 + _dotb(xl, a2p))
    g = _dotb(jax.nn.sigmoid(xl), g2p)
    kkr = xk * k_k
    inv = lax.rsqrt(jnp.maximum(_doth(kkr * kkr, e), 1e-24))
    kk = kkr * _doth(inv, et)
    kf = xk * (1.0 + (a - 1.0) * k_a)
    return xr, lw, kf, xv, -kk, kk * a, g


def _post_f(y, r, kf, v, g, gn_w, gn_b, r_k, e, et):
    mu = _doth(y, e) * (1.0 / HEAD)
    yc = y - _doth(mu, et)
    var = _doth(yc * yc, e) * (1.0 / HEAD)
    yn = yc * _doth(lax.rsqrt(var + GN_EPS), et) * gn_w + gn_b
    bonus = _doth(r * kf * r_k, e)
    return (yn + _doth(bonus, et) * v) * g


def _foxprep_f(q, k, fl, qg8, kg8, fb, e, et, ft, fmask):
    def norm(t, g8):
        ms = _doth(t * t, e) * (1.0 / HEAD)
        return t * _doth(lax.rsqrt(ms + RMS_EPS), et) * _doth(g8, ft)[0:1]
    logf = jax.nn.log_sigmoid(fl + fb) * fmask
    return norm(q, qg8), norm(k, kg8), logf


def _merge_f(za, zb, pa, pb):
    return jax.nn.sigmoid(za) * pa + jax.nn.sigmoid(zb) * pb


def _swiglu_f(gate, up):
    return jax.nn.silu(gate) * up


def _tri(n, strict):
    row = lax.broadcasted_iota(jnp.int32, (n, n), 0)
    col = lax.broadcasted_iota(jnp.int32, (n, n), 1)
    return (row > col) if strict else (row >= col)


def _split2(x):
    hi = x.astype(BF16)
    return hi, (x - hi.astype(F32)).astype(BF16)


def _mm3_raw(a, b, ca, cb):
    dn = (((ca,), (cb,)), ((), ()))
    ah, al = _split2(a)
    bh, bl = _split2(b)
    dot = lambda p, q: lax.dot_general(p, q, dn, preferred_element_type=F32)
    return dot(ah, bh) + (dot(al, bh) + dot(ah, bl))


@functools.partial(jax.custom_vjp, nondiff_argnums=(2, 3))
def _mm3(a, b, ca, cb):
    return _mm3_raw(a, b, ca, cb)


def _mm3_fwd(a, b, ca, cb):
    return _mm3_raw(a, b, ca, cb), (a, b)


def _mm3_bwd(ca, cb, res, ct):
    a, b = res
    da = _mm3_raw(ct, b, 1, 1 - cb) if ca == 1 else _mm3_raw(b, ct, 1 - cb, 1)
    db = _mm3_raw(a, ct, 1 - ca, 0) if cb == 0 else _mm3_raw(ct, a, 0, 1 - ca)
    return da, db


_mm3.defvjp(_mm3_fwd, _mm3_bwd)


def _mmx_raw(t, x, ct):
    x1 = x.astype(BF16)
    r1 = x - x1.astype(F32)
    x2 = r1.astype(BF16)
    x3 = (r1 - x2.astype(F32)).astype(BF16)
    tb = t.astype(BF16)
    dot = lambda q: lax.dot_general(tb, q, (((ct,), (0,)), ((), ())), preferred_element_type=F32)
    return dot(x1) + (dot(x2) + dot(x3))


@jax.custom_vjp
def _mmx(t, x):
    return _mmx_raw(t, x, 1)


def _mmx_fwd(t, x):
    return _mmx_raw(t, x, 1), t


def _mmx_bwd(t, ct):
    return jnp.zeros_like(t), _mmx_raw(t, ct, 0)


_mmx.defvjp(_mmx_fwd, _mmx_bwd)

SCAN_HEADS = 16


def _scan_step(r, lw, k, v, a, b, st):
    c = r.shape[0]
    nh = r.shape[1] // HEAD
    incl = _tri(c, False)
    row2 = lax.broadcasted_iota(jnp.int32, (2 * c, 2 * c), 0)
    col2 = lax.broadcasted_iota(jnp.int32, (2 * c, 2 * c), 1)
    t_row = jnp.where(row2 >= c, row2 - c, row2)
    t_col = jnp.where(col2 >= c, col2 - c, col2)
    mask2 = (t_row > t_col) | ((row2 >= c) & (t_row == t_col))
    right =lax.broadcasted_iota(jnp.int32, (c, 2 * c), 1) >= c
    eye = lax.broadcasted_iota(jnp.int32, (HEAD, HEAD), 0) == lax.broadcasted_iota(jnp.int32, (HEAD, HEAD), 1)
    cl = _mmx(incl.astype(F32), lw)
    last = cl[c - 1:c, :]
    rt = r * jnp.exp(cl)
    at = a * jnp.exp(cl - lw)
    pinv = jnp.exp(-cl)
    bt = b * pinv
    kt = k * pinv
    pend = jnp.exp(last - cl)
    bl = b * pend
    kl = k * pend
    pe_last = jnp.exp(last)
    hs = range(nh)
    sl = [slice(h * HEAD, (h + 1) * HEAD) for h in hs]
    ar = [jnp.concatenate([at[:, sl[h]], rt[:, sl[h]]], axis=0) for h in hs]
    bk = [jnp.concatenate([bt[:, sl[h]], kt[:, sl[h]]], axis=0) for h in hs]
    amat = [jnp.where(mask2, _mm3(ar[h], bk[h], 1, 1), 0.0) for h in hs]
    res = [_mm3(jnp.concatenate([ar[h], amat[h][:, c:]], axis=1),
                jnp.concatenate([st[h], v[:, sl[h]]], axis=0), 1, 0) for h in hs]
    z = [jnp.concatenate([amat[h][:c, :c], res[h][:c]], axis=1) for h in hs]
    for _ in range(max(1, int(np.ceil(np.log2(c))))):
        z = [_mm3(z[h][:, :c], z[h], 1, 0) + jnp.where(right, z[h], 0.0) for h in hs]
    u = [z[h][:, c:] for h in hs]
    ys = [res[h][c:] + _mm3(amat[h][c:, :c], u[h], 1, 0) for h in hs]
    s1s = [_mm3(jnp.concatenate([bl[:, sl[h]], kl[:, sl[h]], jnp.where(eye, pe_last[:, sl[h]], 0.0)], axis=0),
                jnp.concatenate([u[h], v[:, sl[h]], st[h]], axis=0), 0, 0) for h in hs]
    return tuple(ys), tuple(s1s)


def _scan_heads(W):
    return SCAN_HEADS if W % (SCAN_HEADS * HEAD) == 0 else 2


def _scan_fwd(r, lw, k, v, a, b):
    L, W = r.shape
    nh = _scan_heads(W)
    nc, ng = L // CHUNK, W // (nh * HEAD)
    spec = pl.BlockSpec((CHUNK, nh * HEAD), lambda p, c: (c, p))

    def body(r_ref, lw_ref, k_ref, v_ref, a_ref, b_ref, y_ref, s_ref, st):
        @pl.when(pl.program_id(1) == 0)
        def _():
            st[...] = jnp.zeros_like(st)

        s0 = st[...]
        ys, s1s = _scan_step(r_ref[...], lw_ref[...], k_ref[...], v_ref[...], a_ref[...], b_ref[...], s0)
        s_ref[0] = s0
        st[...] = jnp.stack(s1s)
        y_ref[...] = jnp.concatenate(ys, axis=1)

    return pl.pallas_call(
        body, name="wkv7_fwd",
        out_shape=[jax.ShapeDtypeStruct((L, W), F32), jax.ShapeDtypeStruct((nc, nh * ng, HEAD, HEAD), F32)],
        grid=(ng, nc), in_specs=[spec] * 6,
        out_specs=[spec, pl.BlockSpec((1, nh, HEAD, HEAD), lambda p, c: (c, p, 0, 0))],
        scratch_shapes=[pltpu.VMEM((nh, HEAD, HEAD), F32)],
        compiler_params=_params(("parallel", "arbitrary")),
    )(r, lw, k, v, a, b)


def _scan_bwd(r, lw, k, v, a, b, s_all, dy):
    L, W = r.shape
    nh = _scan_heads(W)
    nc, ng = L // CHUNK, W // (nh * HEAD)
    spec = pl.BlockSpec((CHUNK, nh * HEAD), lambda p, c: (nc - 1 - c, p))

    def body(r_ref, lw_ref, k_ref, v_ref, a_ref, b_ref, s_ref, dy_ref,
             dr_ref, dlw_ref, dk_ref, dv_ref, da_ref, db_ref, dst):
        @pl.when(pl.program_id(1) == 0)
        def _():
            dst[...] = jnp.zeros_like(dst)

        _, vjp = jax.vjp(_scan_step, r_ref[...], lw_ref[...], k_ref[...], v_ref[...], a_ref[...], b_ref[...],
                         s_ref[0])
        dys = tuple(dy_ref[:, h * HEAD:(h + 1) * HEAD] for h in range(nh))
        g = vjp((dys, tuple(dst[h] for h in range(nh))))
        for ref, val in zip((dr_ref, dlw_ref, dk_ref, dv_ref, da_ref, db_ref), g[:6]):
            ref[...] = val
        dst[...] = g[6]

    return pl.pallas_call(
        body, name="wkv7_bwd", out_shape=[jax.ShapeDtypeStruct((L, W), F32)] * 6,
        grid=(ng, nc),
        in_specs=[spec] * 6 + [pl.BlockSpec((1, nh, HEAD, HEAD), lambda p, c: (nc - 1 - c, p, 0, 0)), spec],
        out_specs=[spec] * 6,
        scratch_shapes=[pltpu.VMEM((nh, HEAD, HEAD), F32)],
        compiler_params=_params(("parallel", "arbitrary")),
    )(r, lw, k, v, a, b, s_all, dy)


def _cumsum_rows(logf):
    L = logf.shape[0]
    t = LANES

    def body(x_ref, o_ref, carry):
        @pl.when(pl.program_id(0) == 0)
        def _():
            carry[...] = jnp.zeros_like(carry)

        c = _doth(_tri(t, False).astype(F32), x_ref[...]) + carry[...]
        carry[...] = c[t - 1:t, :]
        o_ref[...] = c.T

    return pl.pallas_call(
        body, name="fox_cumsum", out_shape=jax.ShapeDtypeStruct((LANES, L), F32), grid=(L // t,),
        in_specs=[pl.BlockSpec((t, LANES), lambda i: (i, 0))],
        out_specs=pl.BlockSpec((LANES, t), lambda i: (0, i)),
        scratch_shapes=[pltpu.VMEM((1, LANES), F32)], compiler_params=_params(("arbitrary",)),
    )(logf)


def _rcumsum_cols(dct):
    L = dct.shape[1]
    t = LANES
    n = L // t

    def body(x_ref, o_ref, carry):
        @pl.when(pl.program_id(0) == 0)
        def _():
            carry[...] = jnp.zeros_like(carry)

        rc = _doth(x_ref[...], _tri(t, False).astype(F32)) + carry[...]
        carry[...] = rc[:, 0:1]
        o_ref[...] = rc.T

    return pl.pallas_call(
        body, name="fox_rcumsum", out_shape=jax.ShapeDtypeStruct((L, LANES), F32), grid=(n,),
        in_specs=[pl.BlockSpec((LANES, t), lambda i: (0, n - 1 - i))],
        out_specs=pl.BlockSpec((t, LANES), lambda i: (n - 1 - i, 0)),
        scratch_shapes=[pltpu.VMEM((LANES, 1), F32)], compiler_params=_params(("arbitrary",)),
    )(dct)


def _attn_tile(L):
    return _pick(L, (384, 256, 128))


def _attn_scores(q, k, ck, qi, kj, t):
    s = _dot_bnt(q, k) * ATTN_SCALE - ck
    qpos = qi * t + lax.broadcasted_iota(jnp.int32, (t, t), 0)
    kpos = kj * t + lax.broadcasted_iota(jnp.int32, (t, t), 1)
    mask = (kpos <= qpos) & (kpos >= PAD_ROWS)
    return jnp.where(mask, s, NEG), mask


def _dot_bnt(a, b):
    return lax.dot_general(_bf(a), _bf(b), (((1,), (1,)), ((), ())), preferred_element_type=F32)


def _dot_btn(a, b):
    return lax.dot_general(_bf(a), _bf(b), (((0,), (0,)), ((), ())), preferred_element_type=F32)


def _ck_rows(ct_ref, p):
    r0 = 2 * (p % 4)
    return ct_ref[pl.ds(r0, 1), :], ct_ref[pl.ds(r0 + 1, 1), :]


def _attn_fwd(q, k, v, ct):
    L, W = q.shape
    t = _attn_tile(L)
    nt, npair = L // t, W // LANES
    qspec = pl.BlockSpec((t, LANES), lambda p, i, j: (i, p))
    kspec = pl.BlockSpec((t, LANES), lambda p, i, j: (jnp.minimum(i, j), p))
    cspec = pl.BlockSpec((8, t), lambda p, i, j: (p // 4, jnp.minimum(i, j)))

    def body(q_ref, k_ref, v_ref, ct_ref, o_ref, lse_ref, m_s, l_s, acc):
        p, i, j = pl.program_id(0), pl.program_id(1), pl.program_id(2)

        @pl.when(j == 0)
        def _():
            m_s[...] = jnp.full_like(m_s, NEG)
            l_s[...] = jnp.zeros_like(l_s)
            acc[...] = jnp.zeros_like(acc)

        @pl.when(j <= i)
        def _():
            cks = _ck_rows(ct_ref, p)
            for hh in range(2):
                sl = slice(hh * HEAD, (hh + 1) * HEAD)
                s, _ = _attn_scores(q_ref[:, sl], k_ref[:, sl], cks[hh], i, j, t)
                m_old = m_s[hh]
                m_new = jnp.maximum(m_old, jnp.max(s, axis=-1, keepdims=True))
                alpha = jnp.exp(m_old - m_new)
                pr = jnp.exp(s - m_new)
                l_s[hh] = alpha * l_s[hh] + jnp.sum(pr, axis=-1, keepdims=True)
                acc[hh] = alpha * acc[hh] + _dotb(pr, v_ref[:, sl])
                m_s[hh] = m_new

        @pl.when(j == i)
        def _():
            lane = lax.broadcasted_iota(jnp.int32, (t, LANES), 1)
            lse = jnp.zeros((t, LANES), F32)
            for hh in range(2):
                o_ref[:, hh * HEAD:(hh + 1) * HEAD] = acc[hh] / l_s[hh]
                lse = jnp.where(lane == hh, m_s[hh] + jnp.log(l_s[hh]), lse)
            lse_ref[0] = lse

    return pl.pallas_call(
        body, name="fox_attn_fwd",
        out_shape=[jax.ShapeDtypeStruct((L, W), F32), jax.ShapeDtypeStruct((npair, L, LANES), F32)],
        grid=(npair, nt, nt), in_specs=[qspec, kspec, kspec, cspec],
        out_specs=[qspec, pl.BlockSpec((1, t, LANES), lambda p, i, j: (p, i, 0))],
        scratch_shapes=[pltpu.VMEM((2, t, 1), F32), pltpu.VMEM((2, t, 1), F32), pltpu.VMEM((2, t, HEAD), F32)],
        compiler_params=_params(("parallel", "parallel", "arbitrary")),
    )(q, k, v, ct)


def _attn_bwd_delta(q, k, v, ct, lse, do):
    L, W = q.shape
    t = _attn_tile(L)
    nt, npair = L // t, W // LANES
    qspec = pl.BlockSpec((t, LANES), lambda p, i, j: (i, p))
    kspec = pl.BlockSpec((t, LANES), lambda p, i, j: (jnp.minimum(i, j), p))
    cspec = pl.BlockSpec((8, t), lambda p, i, j: (p // 4, jnp.minimum(i, j)))
    lspec = pl.BlockSpec((1, t, LANES), lambda p, i, j: (p, i, 0))

    def body(q_ref, k_ref, v_ref, ct_ref, lse_ref, do_ref, dl_ref, acc):
        p, i, j = pl.program_id(0), pl.program_id(1), pl.program_id(2)

        @pl.when(j == 0)
        def _():
            acc[...] = jnp.zeros_like(acc)

        @pl.when(j <= i)
        def _():
            cks = _ck_rows(ct_ref, p)
            for hh in range(2):
                sl = slice(hh * HEAD, (hh + 1) * HEAD)
                s, mask = _attn_scores(q_ref[:, sl], k_ref[:, sl], cks[hh], i, j, t)
                pr = jnp.where(mask, jnp.exp(s - lse_ref[0, :, hh:hh + 1]), 0.0)
                acc[hh] += jnp.sum(pr * _dot_bnt(do_ref[:, sl], v_ref[:, sl]), axis=-1, keepdims=True)

        @pl.when(j == i)
        def _():
            lane = lax.broadcasted_iota(jnp.int32, (t, LANES), 1)
            dl = jnp.zeros((t, LANES), F32)
            for hh in range(2):
                dl = jnp.where(lane == hh, acc[hh], dl)
            dl_ref[0] = dl

    return pl.pallas_call(
        body, name="fox_attn_bwd_delta", out_shape=jax.ShapeDtypeStruct((npair, L, LANES), F32),
        grid=(npair, nt, nt), in_specs=[qspec, kspec, kspec, cspec, lspec, qspec],
        out_specs=lspec, scratch_shapes=[pltpu.VMEM((2, t, 1), F32)],
        compiler_params=_params(("parallel", "parallel", "arbitrary")),
    )(q, k, v, ct, lse, do)


def _attn_bwd_dq(q, k, v, ct, delta, lse, do):
    L, W = q.shape
    t = _attn_tile(L)
    nt, npair = L // t, W // LANES
    qspec = pl.BlockSpec((t, LANES), lambda p, i, j: (i, p))
    kspec = pl.BlockSpec((t, LANES), lambda p, i, j: (jnp.minimum(i, j), p))
    cspec = pl.BlockSpec((8, t), lambda p, i, j: (p // 4, jnp.minimum(i, j)))
    lspec = pl.BlockSpec((1, t, LANES), lambda p, i, j: (p, i, 0))

    def body(q_ref, k_ref, v_ref, ct_ref, dl_ref, lse_ref, do_ref, dq_ref, acc):
        p, i, j = pl.program_id(0), pl.program_id(1), pl.program_id(2)

        @pl.when(j == 0)
        def _():
            acc[...] = jnp.zeros_like(acc)

        @pl.when(j <= i)
        def _():
            cks = _ck_rows(ct_ref, p)
            for hh in range(2):
                sl = slice(hh * HEAD, (hh + 1) * HEAD)
                s, mask = _attn_scores(q_ref[:, sl], k_ref[:, sl], cks[hh], i, j, t)
                pr = jnp.where(mask, jnp.exp(s - lse_ref[0, :, hh:hh + 1]), 0.0)
                ds = pr * (_dot_bnt(do_ref[:, sl], v_ref[:, sl]) - dl_ref[0, :, hh:hh + 1])
                acc[hh] += _dotb(ds, k_ref[:, sl])

        @pl.when(j == i)
        def _():
            for hh in range(2):
                dq_ref[:, hh * HEAD:(hh + 1) * HEAD] = acc[hh] * ATTN_SCALE

    return pl.pallas_call(
        body, name="fox_attn_bwd_dq", out_shape=jax.ShapeDtypeStruct((L, W), F32),
        grid=(npair, nt, nt), in_specs=[qspec, kspec, kspec, cspec, lspec, lspec, qspec],
        out_specs=qspec, scratch_shapes=[pltpu.VMEM((2, t, HEAD), F32)],
        compiler_params=_params(("parallel", "parallel", "arbitrary")),
    )(q, k, v, ct, delta, lse, do)


def _attn_bwd_dkv(q, k, v, ct, delta, lse, do):
    L, W = q.shape
    t = _attn_tile(L)
    nt, npair = L // t, W // LANES
    kspec = pl.BlockSpec((t, LANES), lambda p, j, i: (j, p))
    qspec = pl.BlockSpec((t, LANES), lambda p, j, i: (jnp.maximum(i, j), p))
    cspec = pl.BlockSpec((8, t), lambda p, j, i: (p // 4, j))
    lspec = pl.BlockSpec((1, t, LANES), lambda p, j, i: (p, jnp.maximum(i, j), 0))

    def body(q_ref, k_ref, v_ref, ct_ref, dl_ref, lse_ref, do_ref, dk_ref, dv_ref, dc_ref, dk_s, dv_s, dc_s):
        p, j, i = pl.program_id(0), pl.program_id(1), pl.program_id(2)

        @pl.when(i == 0)
        def _():
            dk_s[...] = jnp.zeros_like(dk_s)
            dv_s[...] = jnp.zeros_like(dv_s)
            dc_s[...] = jnp.zeros_like(dc_s)

        @pl.when(i >= j)
        def _():
            cks = _ck_rows(ct_ref, p)
            for hh in range(2):
                sl = slice(hh * HEAD, (hh + 1) * HEAD)
                s, mask = _attn_scores(q_ref[:, sl], k_ref[:, sl], cks[hh], i, j, t)
                pr = jnp.where(mask, jnp.exp(s - lse_ref[0, :, hh:hh + 1]), 0.0)
                doh = do_ref[:, sl]
                ds = pr * (_dot_bnt(doh, v_ref[:, sl]) - dl_ref[0, :, hh:hh + 1])
                dv_s[hh] += _dot_btn(pr, doh)
                dk_s[hh] += _dot_btn(ds, q_ref[:, sl])
                dc_s[hh] += -jnp.sum(ds, axis=0, keepdims=True)

        @pl.when(i == nt - 1)
        def _():
            row = lax.broadcasted_iota(jnp.int32, (8, t), 0)
            dc = jnp.zeros((8, t), F32)
            for hh in range(2):
                dk_ref[:, hh * HEAD:(hh + 1) * HEAD] = dk_s[hh] * ATTN_SCALE
                dv_ref[:, hh * HEAD:(hh + 1) * HEAD] = dv_s[hh]
                dc = jnp.where(row == hh, dc_s[hh], dc)
            dc_ref[0] = dc

    return pl.pallas_call(
        body, name="fox_attn_bwd_dkv",
        out_shape=[jax.ShapeDtypeStruct((L, W), F32), jax.ShapeDtypeStruct((L, W), F32),
                   jax.ShapeDtypeStruct((npair, 8, L), F32)],
        grid=(npair, nt, nt), in_specs=[qspec, kspec, kspec, cspec, lspec, lspec, qspec],
        out_specs=[kspec, kspec, pl.BlockSpec((1, 8, t), lambda p, j, i: (p, 0, j))],
        scratch_shapes=[pltpu.VMEM((2, t, HEAD), F32), pltpu.VMEM((2, t, HEAD), F32), pltpu.VMEM((2, 1, t), F32)],
        compiler_params=_params(("parallel", "parallel", "arbitrary")),
    )(q, k, v, ct, delta, lse, do)


def _place():
    x, y, c = lax.axis_index("x"), lax.axis_index("y"), lax.axis_index("c")
    chips = [(1 - x, y), (x, 1 - y), (1 - x, 1 - y)]
    return x, y, c, chips


def _remote(src, dst, send_sems, recv_sems, k, to):
    return pltpu.make_async_remote_copy(src_ref=src, dst_ref=dst, send_sem=send_sems.at[k],
                                        recv_sem=recv_sems.at[k], device_id=to, device_id_type=MESH)


def _dma_sems(n):
    return [pltpu.SemaphoreType.DMA((n,)), pltpu.SemaphoreType.DMA((n,))]


def _all_gather(shards, modes):
    n = len(shards)

    def out_shape(s, mode):
        r, c = s.shape
        return {"stack": (4, r, c), "cols": (r, 4 * c), "rows": (4 * r, c)}[mode]

    def body(*refs):
        ins, outs, (send_sems, recv_sems) = refs[:n], refs[n:2 * n], refs[2 * n:]
        x, y, c, chips = _place()
        me_chip = 2 * x + y
        sibling = (x, y, 1 - c)

        def window(i, chip, cc):
            r, cw = shards[i].shape
            h = r // 2
            if modes[i] == "stack":
                return outs[i].at[chip, pl.ds(cc * h, h), :]
            if modes[i] == "cols":
                return outs[i].at[pl.ds(cc * h, h), pl.ds(pl.multiple_of(chip * cw, LANES), cw)]
            return outs[i].at[pl.ds(pl.multiple_of(chip * r + cc * h, 8), h), :]

        first = []
        for i in range(n):
            h = shards[i].shape[0] // 2
            for k, (cx, cy) in enumerate(chips):
                first.append(_remote(ins[i].at[pl.ds(c * h, h), :], window(i, me_chip, c), send_sems, recv_sems,
                                     6 * i + k, (cx, cy, c)))
        for cp in first:
            cp.start()
        passed = []
        for k, (cx, cy) in enumerate(chips):
            for i in range(n):
                landed = window(i, 2 * cx + cy, c)
                _remote(landed, landed, send_sems, recv_sems, 6 * i + k, sibling).wait_recv()
                fwd = _remote(landed, landed, send_sems, recv_sems, 6 * i + 3 + k, sibling)
                fwd.start()
                passed.append(fwd)
        for k, (cx, cy) in enumerate(chips):
            for i in range(n):
                other = window(i, 2 * cx + cy, 1 - c)
                _remote(other, other, send_sems, recv_sems, 6 * i + 3 + k, sibling).wait_recv()
        for cp in first + passed:
            cp.wait_send()

    return pl.pallas_call(
        body, name="gather_weights",
        out_shape=[jax.ShapeDtypeStruct(out_shape(s, m), s.dtype) for s, m in zip(shards, modes)],
        in_specs=[ANY] * n, out_specs=[ANY] * n, scratch_shapes=_dma_sems(6 * n),
    )(*shards)


def _pair_exchange(blocks):
    n = len(blocks)

    def body(*refs):
        ins, outs, (send_sems, recv_sems) = refs[:n], refs[n:2 * n], refs[2 * n:]
        x, y, c, _ = _place()
        cps = []
        for i in range(n):
            h = blocks[i].shape[1] // 2
            cps.append(_remote(ins[i].at[:, pl.ds((1 - c) * h, h), :], outs[i], send_sems, recv_sems, i,
                               (x, y, 1 - c)))
        for cp in cps:
            cp.start()
        for cp in cps:
            cp.wait()

    return pl.pallas_call(
        body, name="reduce_pair_exchange",
        out_shape=[jax.ShapeDtypeStruct((4, b.shape[1] // 2, b.shape[2]), b.dtype) for b in blocks],
        in_specs=[ANY] * n, out_specs=[ANY] * n, scratch_shapes=_dma_sems(n),
    )(*blocks)


def _chip_exchange(parts):
    n = len(parts)

    def body(*refs):
        ins, outs, (send_sems, recv_sems) = refs[:n], refs[n:2 * n], refs[2 * n:]
        x, y, c, chips = _place()
        me_chip = 2 * x + y
        sends = [_remote(ins[i].at[2 * cx + cy], outs[i].at[me_chip], send_sems, recv_sems, 3 * i + k, (cx, cy, c))
                 for i in range(n) for k, (cx, cy) in enumerate(chips)]
        for cp in sends:
            cp.start()
        for i in range(n):
            for k, (cx, cy) in enumerate(chips):
                slot = outs[i].at[2 * cx + cy]
                _remote(slot, slot, send_sems, recv_sems, 3 * i + k, (cx, cy, c)).wait_recv()
        for cp in sends:
            cp.wait_send()

    return pl.pallas_call(
        body, name="reduce_chip_exchange", out_shape=[jax.ShapeDtypeStruct(p.shape, p.dtype) for p in parts],
        in_specs=[ANY] * n, out_specs=[ANY] * n, scratch_shapes=_dma_sems(3 * n),
    )(*parts)


def _pair_join(tots):
    n = len(tots)

    def body(*refs):
        ins, outs, (send_sems, recv_sems) = refs[:n], refs[n:2 * n], refs[2 * n:]
        x, y, c, _ = _place()
        cps = [_remote(ins[i], outs[i], send_sems, recv_sems, i, (x, y, 1 - c)) for i in range(n)]
        for cp in cps:
            cp.start()
        for cp in cps:
            cp.wait()

    return pl.pallas_call(
        body, name="reduce_pair_join", out_shape=[jax.ShapeDtypeStruct(t.shape, t.dtype) for t in tots],
        in_specs=[ANY] * n, out_specs=[ANY] * n, scratch_shapes=_dma_sems(n),
    )(*tots)


def _add_tile(h, cw):
    cap = max(8, (512 * 1024) // max(cw, 1))
    return _pick(h, tuple(t for t in (1024, 512, 256, 128, 64, 32, 16, 8) if t <= cap))


def _pair_add(name, block, recv):
    n, r, cw = block.shape
    h = r // 2
    tr = _add_tile(h, cw)
    c = lax.axis_index("c").astype(jnp.int32).reshape((1,))

    def body(c_ref, a_ref, b_ref, o_ref):
        o_ref[...] = (a_ref[...] + b_ref[...]).astype(o_ref.dtype)

    return pl.pallas_call(
        body, name=name, out_shape=jax.ShapeDtypeStruct((n, h, cw), BF16),
        grid_spec=pltpu.PrefetchScalarGridSpec(
            num_scalar_prefetch=1, grid=(n, h // tr),
            in_specs=[pl.BlockSpec((1, tr, cw), lambda a, i, cr: (a, cr[0] * (h // tr) + i, 0)),
                      pl.BlockSpec((1, tr, cw), lambda a, i, cr: (a, i, 0))],
            out_specs=pl.BlockSpec((1, tr, cw), lambda a, i, cr: (a, i, 0))),
        compiler_params=_params(("parallel", "parallel")),
    )(c, block, recv)


def _chip_add(name, parts):
    n, h, cw = parts.shape
    tr = _add_tile(h, cw)

    def body(a_ref, o_ref):
        f = lambda i: a_ref[i].astype(F32)
        o_ref[...] = ((f(0) + f(1)) + f(2)) + f(3)

    return pl.pallas_call(
        body, name=name, out_shape=jax.ShapeDtypeStruct((h, cw), F32), grid=(h // tr,),
        in_specs=[pl.BlockSpec((n, tr, cw), lambda i: (0, i, 0))],
        out_specs=pl.BlockSpec((tr, cw), lambda i: (i, 0)),
        compiler_params=_params(("parallel",)),
    )(parts)


def _adamw(name, w, g, m, v):
    R, C = w.shape
    tr = _pick(R, (128, 64, 32, 16, 8))
    spec = pl.BlockSpec((tr, C), lambda i: (i, 0))

    def body(w_ref, g_ref, m_ref, v_ref, d_ref, mo_ref, vo_ref):
        gr = g_ref[...]
        mn = ADAM_B1 * m_ref[...] + (1.0 - ADAM_B1) * gr
        vn = ADAM_B2 * v_ref[...] + (1.0 - ADAM_B2) * jnp.square(gr)
        m_hat = mn / (1.0 - ADAM_B1 ** ADAM_STEP)
        v_hat = vn / (1.0 - ADAM_B2 ** ADAM_STEP)
        d_ref[...] = -ADAM_LR * (m_hat / (jnp.sqrt(v_hat) + ADAM_EPS) + ADAM_WD * w_ref[...])
        mo_ref[...] = mn
        vo_ref[...] = vn

    return pl.pallas_call(
        body, name=name, out_shape=[jax.ShapeDtypeStruct((R, C), F32)] * 3, grid=(R // tr,),
        in_specs=[spec] * 4, out_specs=[spec] * 3, compiler_params=_params(("parallel",)),
    )(w, g, m, v)


def _pack(parts, dtype, row_mult):
    flat = jnp.concatenate([p.reshape(-1).astype(dtype) for p in parts])
    unit = row_mult * PACK_W
    pad = (-flat.shape[0]) % unit
    if pad:
        flat = jnp.concatenate([flat, jnp.zeros((pad,), dtype)])
    return flat.reshape(-1, PACK_W)


def _unpack(flat, shapes):
    out, off = [], 0
    for s in shapes:
        n = int(np.prod(s))
        out.append(flat[off:off + n].reshape(s))
        off += n
    return out


SHARDED = ("w_in", "rwkv_w2", "rwkv_a2", "rwkv_g2", "w_branch_a", "w_branch_b", "w_o", "w_gate_up", "w_down",
           "meta_tokens")
SHARD_AXIS = {"w_in": 1, "rwkv_w2": 1, "rwkv_a2": 1, "rwkv_g2": 1, "w_branch_a": 1, "w_branch_b": 1, "w_o": 0,
              "w_gate_up": 1, "w_down": 0, "meta_tokens": 1}
GATHER_MODE = {"w_in": "stack", "rwkv_w2": "stack", "rwkv_a2": "stack", "rwkv_g2": "stack", "w_branch_a": "cols",
               "w_branch_b": "cols", "w_o": "rows", "w_gate_up": "cols", "w_down": "rows", "meta_tokens": "stack"}
SMALL = ("norm1_g", "rwkv_mu", "rwkv_w0", "rwkv_a0", "rwkv_k_k", "rwkv_k_a", "rwkv_r_k", "rwkv_gn_w",
         "rwkv_gn_b", "fox_q_norm_g", "fox_k_norm_g", "fox_f_bias", "norm2_g")
WEIGHTS = ("meta_tokens", "norm1_g", "w_in", "rwkv_mu", "rwkv_w0", "rwkv_w2", "rwkv_a0", "rwkv_a2", "rwkv_g2",
           "rwkv_k_k", "rwkv_k_a", "rwkv_r_k", "rwkv_gn_w", "rwkv_gn_b", "fox_q_norm_g", "fox_k_norm_g",
           "fox_f_bias", "w_branch_a", "w_branch_b", "w_o", "norm2_g", "w_gate_up", "w_down")


def _pad_rows(t, rows):
    return jnp.concatenate([t, jnp.zeros((rows - t.shape[0],) + t.shape[1:], t.dtype)], axis=0)


def _pad_cols(t, cols):
    return jnp.concatenate([t, jnp.zeros(t.shape[:-1] + (cols - t.shape[-1],), t.dtype)], axis=-1)


def _step(x, tgt, wts, mom1, mom2):
    seq, D = x.shape
    L = SEQ_ROW0 + seq
    RW = wts["rwkv_w0"].shape[-1]
    DL, AL, GL = wts["rwkv_w2"].shape[0], wts["rwkv_a2"].shape[0], wts["rwkv_g2"].shape[0]
    FW = wts["w_branch_b"].shape[0]
    FH = wts["fox_f_bias"].shape[-1]
    DFF = wts["w_down"].shape[0] * 4
    LORA = DL + AL + GL
    LW = -(-(LORA + FH) // 512) * 512
    assert RW == FW and (6 * RW) % D == 0 and (6 * RW + 2 * D) % LW == 0 and LORA % 8 == 0
    xj = lax.axis_index("x")
    yj = lax.axis_index("y")
    chip = 2 * xj + yj

    send = [wts[n] if n == "meta_tokens" else wts[n].astype(BF16) for n in SHARDED]
    modes = [GATHER_MODE[n] for n in SHARDED]
    gathered = _all_gather(send, modes)
    full, stacked = {}, {}
    for n, mode, shard, got in zip(SHARDED, modes, send, gathered):
        r, cw = shard.shape
        if mode == "stack":
            got = lax.dynamic_update_index_in_dim(got, shard, chip, 0)
            stacked[n] = got
            full[n] = jnp.concatenate([got[j] for j in range(4)], axis=1)
        elif mode == "cols":
            full[n] = lax.dynamic_update_slice(got, shard, (0, chip * cw))
        else:
            full[n] = lax.dynamic_update_slice(got, shard, (chip * r, 0))
    meta = full["meta_tokens"]
    w_in = full["w_in"]
    o = 0
    segs = {}
    for nm, wd in (("r", RW), ("k", RW), ("v", RW), ("wd", DL), ("ad", AL), ("gd", GL),
                   ("fq", FW), ("fk", FW), ("fv", FW), ("ff", FH), ("ga", D), ("gb", D)):
        segs[nm] = w_in[:, o:o + wd]
        o += wd
    lora_w = _pad_cols(jnp.concatenate([segs["wd"], segs["ad"], segs["gd"], segs["ff"]], axis=1), LW)
    w1 = jnp.concatenate([segs["r"], segs["k"], segs["v"], segs["fq"], segs["fk"], segs["fv"],
                          segs["ga"], segs["gb"], lora_w], axis=1)
    cb_f = 3
    cb_gate = (6 * RW) // D
    cb_lora = (6 * RW + 2 * D) // LW

    e_m, et_m = _head_mats(RW)
    ft_m = _fold_mat(RW)
    mu = wts["rwkv_mu"]
    mu_rkv = mu[:, :3 * RW]
    mu_l = _pad_cols(mu[:, 3 * RW:], LW)
    w2p = _pad_rows(full["rwkv_w2"].astype(F32), LW)
    a2p = _pad_rows(jnp.concatenate([jnp.zeros((DL, RW), F32), full["rwkv_a2"].astype(F32)], axis=0), LW)
    g2p = _pad_rows(jnp.concatenate([jnp.zeros((DL + AL, RW), F32), full["rwkv_g2"].astype(F32)], axis=0), LW)
    r_k = wts["rwkv_r_k"].reshape(1, RW)
    qg8 = jnp.broadcast_to(wts["fox_q_norm_g"], (8, HEAD))
    kg8 = jnp.broadcast_to(wts["fox_k_norm_g"], (8, HEAD))
    fb = _pad_cols(wts["fox_f_bias"], LANES)
    fmask = (jnp.arange(LANES) < FH).astype(F32).reshape(1, LANES)
    lmask = ((jnp.arange(LW) >= LORA) & (jnp.arange(LW) < LORA + FH)).astype(F32).reshape(1, LW)

    h0 = jnp.concatenate([jnp.zeros((PAD_ROWS, D), F32), meta, x], axis=0)
    n1 = wts["norm1_g"]

    (xn,), _ = _rowcall("rms1_fwd", lambda i, r, b: ([_rms_f(r[0], b[0])], []), L,
                        [(h0, D, 0, "row")], [n1], [(D, BF16)], [])
    proj = _matmul(xn, w1, "nn", F32, "proj_fwd")

    def shift_fn(i, r, b):
        rows = lax.broadcasted_iota(jnp.int32, (ROW_TILE, 1), 0)
        outs = []
        for z, halo, m_ in ((r[0], r[1], b[0]), (r[2], r[3], b[1])):
            first = jnp.where(i == 0, 0.0, halo[7:8, :])
            zp = jnp.where(rows == 0, first, pltpu.roll(z, 1, 0))
            outs.append(z + (zp - z) * m_)
        return outs, []

    rkv_w = 3 * RW
    (x_rkv, x_l), _ = _rowcall(
        "shift_fwd", shift_fn, L,
        [(proj, rkv_w, 0, "row"), (proj, rkv_w, 0, "prev"), (proj, LW, cb_lora, "row"), (proj, LW, cb_lora, "prev")],
        [mu_rkv, mu_l], [(rkv_w, F32), (LW, F32)], [])

    prep_p = [wts["rwkv_w0"], w2p, wts["rwkv_a0"], a2p, g2p, wts["rwkv_k_k"], wts["rwkv_k_a"], e_m, et_m]
    prep_rows = [(x_rkv, RW, 0, "row"), (x_rkv, RW, 1, "row"), (x_rkv, RW, 2, "row"), (x_l, LW, 0, "row")]
    (s_r, s_lw, s_k, s_v, s_a, s_b, gate_g), _ = _rowcall(
        "rwkv_prep_fwd", lambda i, r, b: (list(_prep_f(*r, *b)), []), L, prep_rows, prep_p,
        [(RW, F32)] * 7, [])
    y_scan, s_all = _scan_fwd(s_r, s_lw, s_k, s_v, s_a, s_b)
    post_p = [wts["rwkv_gn_w"], wts["rwkv_gn_b"], r_k, e_m, et_m]
    post_rows = [(y_scan, RW, 0, "row"), (s_r, RW, 0, "row"), (s_k, RW, 0, "row"), (s_v, RW, 0, "row"),
                 (gate_g, RW, 0, "row")]
    (y_a,), _ = _rowcall("rwkv_post_fwd", lambda i, r, b: ([_post_f(*r, *b)], []), L, post_rows, post_p,
                         [(RW, BF16)], [])

    fox_p = [qg8, kg8, fb, e_m, et_m, ft_m, fmask]

    def foxprep_fn(i, r, b):
        fl = _doth(r[2] * b[-1], b[-2])
        return list(_foxprep_f(r[0], r[1], fl, *b[:-2])), []

    sel = (np.arange(LW)[:, None] - LORA == np.arange(LANES)[None, :]).astype(np.float32)
    sel = jnp.asarray(sel)
    fox_rows = [(proj, FW, cb_f, "row"), (proj, FW, cb_f + 1, "row"), (proj, LW, cb_lora, "row")]
    (f_q, f_k, logf), _ = _rowcall("fox_prep_fwd", foxprep_fn, L, fox_rows, fox_p + [sel, lmask],
                                   [(FW, BF16), (FW, BF16), (LANES, F32)], [])
    ct = _cumsum_rows(logf)
    f_v = proj[:, (cb_f + 2) * FW:(cb_f + 3) * FW]
    y_b32, lse = _attn_fwd(f_q, f_k, f_v, ct)
    y_b = y_b32.astype(BF16)

    p_a = _matmul(y_a, full["w_branch_a"], "nn", F32, "branch_a_fwd")
    p_b = _matmul(y_b, full["w_branch_b"], "nn", F32, "branch_b_fwd")
    merge_rows = [(proj, D, cb_gate, "row"), (proj, D, cb_gate + 1, "row"), (p_a, D, 0, "row"), (p_b, D, 0, "row")]
    (merged,), _ = _rowcall("merge_fwd", lambda i, r, b: ([_merge_f(*r)], []), L, merge_rows, [], [(D, BF16)], [])
    h1 = _matmul(merged, full["w_o"], "nn", F32, "wo_fwd", add=h0)
    n2 = wts["norm2_g"]
    (xn2,), _ = _rowcall("rms2_fwd", lambda i, r, b: ([_rms_f(r[0], b[0])], []), L,
                         [(h1, D, 0, "row")], [n2], [(D, BF16)], [])
    gu = _matmul(xn2, full["w_gate_up"], "nn", F32, "gate_up_fwd")
    gu_rows = [(gu, DFF, 0, "row"), (gu, DFF, 1, "row")]
    (act,), _ = _rowcall("swiglu_fwd", lambda i, r, b: ([_swiglu_f(*r)], []), L, gu_rows, [], [(DFF, BF16)], [])
    h2 = _matmul(act, full["w_down"], "nn", F32, "down_fwd", add=h1)

    def loss_fn(i, r, b):
        err = jnp.where(i == 0, 0.0, r[0] - r[1])
        return [err * (1.0 / D)], [jnp.zeros((8, LANES), F32) + 0.5 / D * jnp.sum(err * err)]

    (dh2,), (loss_acc,) = _rowcall("loss", loss_fn, L, [(h2, D, 0, "row"), (tgt, D, 0, "lag")], [],
                                   [(D, F32)], [(8, LANES)])
    loss = lax.psum(loss_acc[0, 0], ("x", "y", "c"))

    dh2b = dh2.astype(BF16)
    g_w_down = _matmul(act, dh2b, "tn", F32, "down_dw")
    d_act = _matmul(dh2b, full["w_down"], "nt", F32, "down_dx")

    def swiglu_bwd(i, r, b):
        _, vjp = jax.vjp(_swiglu_f, r[0], r[1])
        return list(vjp(r[2])), []

    (d_gate, d_up), _ = _rowcall("swiglu_bwd", swiglu_bwd, L, gu_rows + [(d_act, DFF, 0, "row")], [],
                                 [(DFF, BF16), (DFF, BF16)], [])
    d_gu = jnp.concatenate([d_gate, d_up], axis=1)
    g_w_gu = _matmul(xn2, d_gu, "tn", F32, "gate_up_dw", col_blocks=4)
    d_xn2 = _matmul(d_gu, full["w_gate_up"], "nt", F32, "gate_up_dx")

    def rms_bwd(i, r, b):
        _, vjp = jax.vjp(_rms_f, r[0], b[0])
        dh, dg = vjp(r[1])
        return [dh + r[2]], [dg]

    (dh1,), (g_n2,) = _rowcall("rms2_bwd", rms_bwd, L,
                               [(h1, D, 0, "row"), (d_xn2, D, 0, "row"), (dh2, D, 0, "row")], [n2],
                               [(D, F32)], [(1, D)])
    dh1b = dh1.astype(BF16)
    g_w_o = _matmul(merged, dh1b, "tn", F32, "wo_dw")
    d_merged = _matmul(dh1b, full["w_o"], "nt", F32, "wo_dx")

    def merge_bwd(i, r, b):
        _, vjp = jax.vjp(_merge_f, *r[:4])
        return list(vjp(r[4])), []

    (d_za, d_zb, d_pa, d_pb), _ = _rowcall("merge_bwd", merge_bwd, L, merge_rows + [(d_merged, D, 0, "row")], [],
                                           [(D, BF16)] * 4, [])
    g_w_a = _matmul(y_a, d_pa, "tn", F32, "branch_a_dw", col_blocks=4)
    g_w_b = _matmul(y_b, d_pb, "tn", F32, "branch_b_dw", col_blocks=4)
    d_ya = _matmul(d_pa, full["w_branch_a"], "nt", F32, "branch_a_dx")
    d_yb = _matmul(d_pb, full["w_branch_b"], "nt", F32, "branch_b_dx")

    delta = _attn_bwd_delta(f_q, f_k, f_v, ct, lse, d_yb)
    d_fq = _attn_bwd_dq(f_q, f_k, f_v, ct, delta, lse, d_yb)
    d_fk, d_fv, dc_rows = _attn_bwd_dkv(f_q, f_k, f_v, ct, delta, lse, d_yb)
    dct = _pad_rows(dc_rows[:, :2, :].reshape(-1, L), LANES)
    d_logf = _rcumsum_cols(dct)

    def foxprep_bwd(i, r, b):
        def f(q, k, xl, qg, kg, fbias):
            return _foxprep_f(q, k, _doth(xl * b[-1], b[-2]), qg, kg, fbias, *b[3:7])
        _, vjp = jax.vjp(f, r[0], r[1], r[2], b[0], b[1], b[2])
        dq, dk, dxl, dqg, dkg, dfb = vjp((r[3], r[4], r[5]))
        return [dq, dk, dxl], [dqg, dkg, dfb]

    (d_zfq, d_zfk, d_zl_f), (g_qg8, g_kg8, g_fb) = _rowcall(
        "fox_prep_bwd", foxprep_bwd, L,
        fox_rows + [(d_fq, FW, 0, "row"), (d_fk, FW, 0, "row"), (d_logf, LANES, 0, "row")],
        fox_p + [sel, lmask], [(FW, BF16), (FW, BF16), (LW, F32)], [(8, HEAD), (8, HEAD), (1, LANES)])

    def post_bwd(i, r, b):
        _, vjp = jax.vjp(lambda *a: _post_f(*a, b[3], b[4]), *r[:5], b[0], b[1], b[2])
        g = vjp(r[5])
        return list(g[:5]), list(g[5:])

    (d_y, d_r1, d_k1, d_v1, d_g), (g_gn_w, g_gn_b, g_r_k) = _rowcall(
        "rwkv_post_bwd", post_bwd, L, post_rows + [(d_ya, RW, 0, "row")], post_p,
        [(RW, F32)] * 5, [(1, RW)] * 3)
    d_r2, d_lw, d_k2, d_v2, d_a, d_b = _scan_bwd(s_r, s_lw, s_k, s_v, s_a, s_b, s_all, d_y)

    def prep_bwd(i, r, b):
        _, vjp = jax.vjp(lambda *a: _prep_f(*a, b[7], b[8]), *r[:4], *b[:7])
        cts = (r[4] + r[10], r[5], r[6] + r[11], r[7] + r[12], r[8], r[9], r[13])
        g = vjp(cts)
        return list(g[:4]), list(g[4:])

    bwd_rows = prep_rows + [(d_r2, RW, 0, "row"), (d_lw, RW, 0, "row"), (d_k2, RW, 0, "row"), (d_v2, RW, 0, "row"),
                            (d_a, RW, 0, "row"), (d_b, RW, 0, "row"), (d_r1, RW, 0, "row"), (d_k1, RW, 0, "row"),
                            (d_v1, RW, 0, "row"), (d_g, RW, 0, "row")]
    (d_xr, d_xk, d_xv, d_xl), (g_w0, g_w2p, g_a0, g_a2p, g_g2p, g_kk, g_ka) = _rowcall(
        "rwkv_prep_bwd", prep_bwd, L, bwd_rows, prep_p, [(RW, F32)] * 3 + [(LW, F32)],
        [(1, RW), (LW, RW), (1, RW), (LW, RW), (LW, RW), (1, RW), (1, RW)])

    def shift_bwd(i, r, b):
        last = pl.num_programs(0) - 1
        rows = lax.broadcasted_iota(jnp.int32, (ROW_TILE, 1), 0)
        outs, sums = [], []
        groups = ((r[0], r[1], r[2], r[3], b[0], None), (r[4], r[5], r[6], r[7], b[1], r[8]))
        for d, dnext, z, zhalo, m_, extra in groups:
            nxt = jnp.where(i == last, 0.0, dnext[0:1, :])
            d_up = jnp.where(rows == ROW_TILE - 1, nxt, pltpu.roll(d, ROW_TILE - 1, 0))
            dz = d * (1.0 - m_) + d_up * m_
            if extra is not None:
                dz = dz + extra
            first = jnp.where(i == 0, 0.0, zhalo[7:8, :])
            zp = jnp.where(rows == 0, first, pltpu.roll(z, 1, 0))
            outs.append(dz)
            sums.append(_rowsum(d * (zp - z)))
        return outs, sums

    d_xrkv = jnp.concatenate([d_xr, d_xk, d_xv], axis=1)
    (d_zrkv, d_zl), (g_mu_rkv, g_mu_l) = _rowcall(
        "shift_bwd", shift_bwd, L,
        [(d_xrkv, rkv_w, 0, "row"), (d_xrkv, rkv_w, 0, "next"), (proj, rkv_w, 0, "row"), (proj, rkv_w, 0, "prev"),
         (d_xl, LW, 0, "row"), (d_xl, LW, 0, "next"), (proj, LW, cb_lora, "row"), (proj, LW, cb_lora, "prev"),
         (d_zl_f, LW, 0, "row")],
        [mu_rkv, mu_l], [(rkv_w, BF16), (LW, BF16)], [(1, rkv_w), (1, LW)])

    n_in = stacked["w_in"].shape[2] * 4
    cs = n_in // 4
    cp = -(-cs // LANES) * LANES
    d_ref = jnp.concatenate([d_zrkv, d_zl[:, :LORA], d_zfq, d_zfk, d_fv.astype(BF16), d_zl[:, LORA:LORA + FH],
                             d_za, d_zb], axis=1)
    d_blk = jnp.concatenate([_pad_cols(d_ref[:, j * cs:(j + 1) * cs], cp) for j in range(4)], axis=1)
    w_blk = jnp.concatenate([_pad_cols(stacked["w_in"][j], cp) for j in range(4)], axis=1)
    g_w_in = _matmul(xn, d_blk, "tn", F32, "proj_dw", col_blocks=4)
    d_xn = _matmul(d_blk, w_blk, "nt", F32, "proj_dx")
    (dh0,), (g_n1,) = _rowcall("rms1_bwd", rms_bwd, L,
                               [(h0, D, 0, "row"), (d_xn, D, 0, "row"), (dh1, D, 0, "row")], [n1],
                               [(D, F32)], [(1, D)])
    grad_x = dh0[SEQ_ROW0:]
    g_meta = dh0[PAD_ROWS:SEQ_ROW0]

    tiny = {"rwkv_w2": g_w2p[:DL], "rwkv_a2": g_a2p[DL:DL + AL], "rwkv_g2": g_g2p[DL + AL:LORA],
            "meta_tokens": g_meta}
    g_mu = jnp.concatenate([g_mu_rkv, g_mu_l[:, :LORA]], axis=1)
    gsmall = {
        "norm1_g": g_n1, "rwkv_mu": g_mu, "rwkv_w0": g_w0, "rwkv_a0": g_a0, "rwkv_k_k": g_kk, "rwkv_k_a": g_ka,
        "rwkv_r_k": g_r_k.reshape(wts["rwkv_r_k"].shape), "rwkv_gn_w": g_gn_w, "rwkv_gn_b": g_gn_b,
        "fox_q_norm_g": g_qg8[0:1], "fox_k_norm_g": g_kg8[0:1], "fox_f_bias": g_fb[:, :FH], "norm2_g": g_n2,
    }
    small_flat = jnp.concatenate([gsmall[n].reshape(-1) for n in SMALL])

    tiny_names = tuple(tiny)

    def tiny_block(j):
        parts = []
        for n in tiny_names:
            w = tiny[n].shape[1] // 4
            parts.append(tiny[n][:, j * w:(j + 1) * w])
        return _pack(parts + [small_flat], F32, 32)

    big_names = ("w_in", "w_gate_up", "w_branch_a", "w_branch_b", "w_o", "w_down")
    blocks = [g_w_in, g_w_gu, g_w_a, g_w_b, g_w_o.reshape(4, -1, D), g_w_down.reshape(4, -1, D),
              jnp.stack([tiny_block(j) for j in range(4)])]
    names = big_names + ("small",)
    cj = lax.axis_index("c")
    recv1 = _pair_exchange(blocks)
    parts = [_pair_add("reduce_pair_add_" + n, b, r) for n, b, r in zip(names, blocks, recv1)]
    recv2 = _chip_exchange(parts)
    tots = []
    for n, p, r in zip(names, parts, recv2):
        own = lax.dynamic_index_in_dim(p, chip, 0, keepdims=False)
        tots.append(_chip_add("reduce_chip_add_" + n, lax.dynamic_update_index_in_dim(r, own, chip, 0)))
    others = _pair_join(tots)
    red = [jnp.where(cj == 0, jnp.concatenate([t, o_], axis=0), jnp.concatenate([o_, t], axis=0))
           for t, o_ in zip(tots, others)]
    grads = {n: red[i] for i, n in enumerate(big_names)}
    grads["w_in"] = grads["w_in"][:, :cs]
    tiny_shapes = [wts[n].shape for n in tiny_names]
    got = _unpack(red[-1].reshape(-1), tiny_shapes + [small_flat.shape])
    for n, t in zip(tiny_names, got):
        grads[n] = t
    for n, t in zip(SMALL, _unpack(got[-1], [wts[n].shape for n in SMALL])):
        grads[n] = t

    delta, new_m, new_v = {}, {}, {}
    for n in SHARDED:
        delta[n], new_m[n], new_v[n] = _adamw("adamw_" + n, wts[n], grads[n], mom1[n], mom2[n])
    pk = lambda d: _pack([d[n] for n in SMALL], F32, 8)
    ds, ms, vs = _adamw("adamw_small", pk(wts), pk(grads), pk(mom1), pk(mom2))
    small_shapes = [wts[n].shape for n in SMALL]
    for dst, src in ((delta, ds), (new_m, ms), (new_v, vs)):
        for n, t in zip(SMALL, _unpack(src.reshape(-1), small_shapes)):
            dst[n] = t
    return loss, grad_x, grads, delta, new_m, new_v


def kernel(x, meta_tokens, norm1_g, w_in, rwkv_mu, rwkv_w0, rwkv_w2, rwkv_a0, rwkv_a2, rwkv_g2, rwkv_k_k, rwkv_k_a, rwkv_r_k, rwkv_gn_w, rwkv_gn_b, fox_q_norm_g, fox_k_norm_g, fox_f_bias, w_branch_a, w_branch_b, w_o, norm2_g, w_gate_up, w_down, loss_target, m_meta_tokens, m_norm1_g, m_w_in, m_rwkv_mu, m_rwkv_w0, m_rwkv_w2, m_rwkv_a0, m_rwkv_a2, m_rwkv_g2, m_rwkv_k_k, m_rwkv_k_a, m_rwkv_r_k, m_rwkv_gn_w, m_rwkv_gn_b, m_fox_q_norm_g, m_fox_k_norm_g, m_fox_f_bias, m_w_branch_a, m_w_branch_b, m_w_o, m_norm2_g, m_w_gate_up, m_w_down, v_meta_tokens, v_norm1_g, v_w_in, v_rwkv_mu, v_rwkv_w0, v_rwkv_w2, v_rwkv_a0, v_rwkv_a2, v_rwkv_g2, v_rwkv_k_k, v_rwkv_k_a, v_rwkv_r_k, v_rwkv_gn_w, v_rwkv_gn_b, v_fox_q_norm_g, v_fox_k_norm_g, v_fox_f_bias, v_w_branch_a, v_w_branch_b, v_w_o, v_norm2_g, v_w_gate_up, v_w_down):
    args = dict(locals())
    shapes = {n: args[n].shape for n in WEIGHTS}

    def drop_depth(t, n):
        if n == "meta_tokens":
            return t
        if n == "rwkv_r_k":
            return t.reshape(1, -1)
        return t.reshape(t.shape[1:]) if t.ndim == 3 else t

    wts = {n: drop_depth(args[n], n) for n in WEIGHTS}
    mom1 = {n: drop_depth(args["m_" + n], n) for n in WEIGHTS}
    mom2 = {n: drop_depth(args["v_" + n], n) for n in WEIGHTS}
    loss, grad_x, grads, delta, new_m, new_v = _step(x[0], loss_target[0], wts, mom1, mom2)
    outs = [loss, grad_x[None]]
    for d in (grads, delta, new_m, new_v):
        outs += [d[n].reshape(shapes[n]) for n in WEIGHTS]
    return tuple(outs)
```

```python
import functools

import jax
import jax.numpy as jnp
import numpy as np
from jax import lax
from jax.experimental import pallas as pl
from jax.experimental.pallas import tpu as pltpu

F32 = jnp.float32
BF16 = jnp.bfloat16
HI = lax.Precision.HIGHEST
MESH = pl.DeviceIdType.MESH
ANY = pl.BlockSpec(memory_space=pl.ANY)

N_META = 16
HEAD = 64
ROW_TILE = 128
PAD_ROWS = ROW_TILE - N_META
SEQ_ROW0 = ROW_TILE
CHUNK = 64
LANES = 128
PACK_W = 1024
RMS_EPS = 1e-6
GN_EPS = 64e-5
ATTN_SCALE = HEAD ** -0.5
NEG = -1e30
VMEM_LIMIT_V7X = 56 * 1024 * 1024

ADAM_LR = 0.001
ADAM_B1 = 0.9
ADAM_B2 = 0.999
ADAM_EPS = 1e-08
ADAM_WD = 0.01
ADAM_STEP = 10


def _params(sem=None):
    return pltpu.CompilerParams(dimension_semantics=sem, vmem_limit_bytes=VMEM_LIMIT_V7X)


def _pick(n, cands):
    for c in cands:
        if n % c == 0:
            return c
    return n


def _bf(t):
    return t.astype(BF16)


def _dotb(a, b):
    return jnp.dot(_bf(a), _bf(b), preferred_element_type=F32)


def _doth(a, b):
    return jnp.dot(a, b, precision=HI, preferred_element_type=F32)


_BIG = (2048, 1536, 1408, 1024, 768, 704, 512, 384, 256, 128)


def _matmul(a, b, mode, out_dtype, name, add=None, col_blocks=1):
    if mode == "nn":
        (M, R), (_, N) = a.shape, b.shape
        dims = (((1,), (0,)), ((), ()))
    elif mode == "nt":
        (M, R), (N, _) = a.shape, b.shape
        dims = (((1,), (1,)), ((), ()))
    else:
        (R, M), (_, N) = a.shape, b.shape
        dims = (((0,), (0,)), ((), ()))
    tm = _pick(M, (1408, 1024, 768, 512, 384, 256, 128))
    nb = N // col_blocks
    tn = _pick(nb, (1408, 1024, 896, 768, 704, 512, 384, 256, 128)) if mode == "tn" else _pick(nb, (512, 384, 256, 128))
    per = nb // tn
    tr = _pick(R, (1408, 1056, 768, 512, 384, 256, 128)) if mode == "tn" else _pick(R, _BIG)
    nr = R // tr

    if mode == "nn":
        a_spec = pl.BlockSpec((tm, tr), lambda i, j, r: (i, r))
        b_spec = pl.BlockSpec((tr, tn), lambda i, j, r: (r, j))
    elif mode == "nt":
        a_spec = pl.BlockSpec((tm, tr), lambda i, j, r: (i, r))
        b_spec = pl.BlockSpec((tn, tr), lambda i, j, r: (j, r))
    else:
        a_spec = pl.BlockSpec((tr, tm), lambda i, j, r: (r, i))
        b_spec = pl.BlockSpec((tr, tn), lambda i, j, r: (r, j))
    if col_blocks == 1:
        o_spec = pl.BlockSpec((tm, tn), lambda i, j, r: (i, j))
        o_shape = (M, N)
    else:
        o_spec = pl.BlockSpec((1, tm, tn), lambda i, j, r: (j // per, i, j % per))
        o_shape = (col_blocks, M, nb)
    has_add = add is not None

    def body(*refs):
        if has_add:
            a_ref, b_ref, add_ref, o_ref, acc = refs
        else:
            a_ref, b_ref, o_ref, acc = refs
        r = pl.program_id(2)

        @pl.when(r == 0)
        def _():
            acc[...] = jnp.zeros_like(acc)

        acc[...] += lax.dot_general(_bf(a_ref[...]), _bf(b_ref[...]), dims, preferred_element_type=F32)

        @pl.when(r == nr - 1)
        def _():
            res = acc[...]
            if has_add:
                res = res + add_ref[...]
            o_ref[...] = res.astype(o_ref.dtype).reshape(o_ref.shape)

    ins = [a, b] + ([add] if has_add else [])
    specs = [a_spec, b_spec] + ([o_spec] if has_add else [])
    return pl.pallas_call(
        body, name=name, out_shape=jax.ShapeDtypeStruct(o_shape, out_dtype),
        grid=(M // tm, N // tn, nr), in_specs=specs, out_specs=o_spec,
        scratch_shapes=[pltpu.VMEM((tm, tn), F32)],
        compiler_params=_params(("parallel", "parallel", "arbitrary")),
    )(*ins)


def _rowcall(name, fn, L, row_ins, bc_ins, row_outs, acc_outs, tm=ROW_TILE):
    nt = L // tm
    specs = []
    for arr, w, cb, kind in row_ins:
        if kind == "row":
            specs.append(pl.BlockSpec((tm, w), lambda i, cb=cb: (i, cb)))
        elif kind == "lag":
            specs.append(pl.BlockSpec((tm, w), lambda i, cb=cb: (jnp.maximum(i - 1, 0), cb)))
        elif kind == "prev":
            specs.append(pl.BlockSpec((8, w), lambda i, cb=cb: (jnp.maximum(i * (tm // 8) - 1, 0), cb)))
        else:
            specs.append(pl.BlockSpec((8, w), lambda i, cb=cb: (jnp.minimum((i + 1) * (tm // 8), L // 8 - 1), cb)))
    for arr in bc_ins:
        specs.append(pl.BlockSpec(arr.shape, lambda i, nd=arr.ndim: (0,) * nd))
    out_shapes = [jax.ShapeDtypeStruct((L, w), dt) for w, dt in row_outs]
    out_specs = [pl.BlockSpec((tm, w), lambda i: (i, 0)) for w, dt in row_outs]
    out_shapes += [jax.ShapeDtypeStruct(s, F32) for s in acc_outs]
    out_specs += [pl.BlockSpec(s, lambda i, nd=len(s): (0,) * nd) for s in acc_outs]
    n_row, n_bc, n_ro = len(row_ins), len(bc_ins), len(row_outs)

    def body(*refs):
        i = pl.program_id(0)
        vals = [r[...] for r in refs[: n_row + n_bc]]
        outs, sums = fn(i, vals[:n_row], vals[n_row:])
        o_refs = refs[n_row + n_bc:]
        for r, v in zip(o_refs[:n_ro], outs):
            r[...] = v.astype(r.dtype)

        @pl.when(i == 0)
        def _():
            for r in o_refs[n_ro:]:
                r[...] = jnp.zeros_like(r)

        for r, v in zip(o_refs[n_ro:], sums):
            r[...] += v

    res = pl.pallas_call(
        body, name=name, out_shape=out_shapes, grid=(nt,), in_specs=specs, out_specs=out_specs,
        compiler_params=_params(("arbitrary",)),
    )(*[a for a, _, _, _ in row_ins], *bc_ins)
    return list(res[:n_ro]), list(res[n_ro:])


def _rowsum(t):
    return jnp.sum(t, axis=0, keepdims=True)


def _head_mats(width):
    e = (np.arange(width)[:, None] // HEAD == np.arange(LANES)[None, :]).astype(np.float32)
    return jnp.asarray(e), jnp.asarray(e.T)


def _fold_mat(width):
    ft = (np.arange(HEAD)[:, None] == np.arange(width)[None, :] % HEAD).astype(np.float32)
    return jnp.asarray(ft)


def _rms_f(h, g):
    return (h * lax.rsqrt(jnp.mean(h * h, axis=-1, keepdims=True) + RMS_EPS)) * g


def _prep_f(xr, xk, xv, xl, w0, w2p, a0, a2p, g2p, k_k, k_a, e, et):
    w_log = -jax.nn.softplus(-(w0 + _dotb(jnp.tanh(xl), w2p))) - 0.5
    lw = -jnp.exp(w_log)
    a = jax.nn.sigmoid(a0 + _dotb(xl, a2p))
    g = _dotb(jax.nn.sigmoid(xl), g2p)
    kkr = xk * k_k
    inv = lax.rsqrt(jnp.maximum(_doth(kkr * kkr, e), 1e-24))
    kk = kkr * _doth(inv, et)
    kf = xk * (1.0 + (a - 1.0) * k_a)
    return xr, lw, kf, xv, -kk, kk * a, g


def _post_f(y, r, kf, v, g, gn_w, gn_b, r_k, e, et):
    mu = _doth(y, e) * (1.0 / HEAD)
    yc = y - _doth(mu, et)
    var = _doth(yc * yc, e) * (1.0 / HEAD)
    yn = yc * _doth(lax.rsqrt(var + GN_EPS), et) * gn_w + gn_b
    bonus = _doth(r * kf * r_k, e)
    return (yn + _doth(bonus, et) * v) * g


def _foxprep_f(q, k, fl, qg8, kg8, fb, e, et, ft, fmask):
    def norm(t, g8):
        ms = _doth(t * t, e) * (1.0 / HEAD)
        return t * _doth(lax.rsqrt(ms + RMS_EPS), et) * _doth(g8, ft)[0:1]
    logf = jax.nn.log_sigmoid(fl + fb) * fmask
    return norm(q, qg8), norm(k, kg8), logf


def _merge_f(za, zb, pa, pb):
    return jax.nn.sigmoid(za) * pa + jax.nn.sigmoid(zb) * pb


def _swiglu_f(gate, up):
    return jax.nn.silu(gate) * up


def _tri(n, strict):
    row = lax.broadcasted_iota(jnp.int32, (n, n), 0)
    col = lax.broadcasted_iota(jnp.int32, (n, n), 1)
    return (row > col) if strict else (row >= col)


def _split2(x):
    hi = x.astype(BF16)
    return hi, (x - hi.astype(F32)).astype(BF16)


def _mm3_raw(a, b, ca, cb):
    dn = (((ca,), (cb,)), ((), ()))
    ah, al = _split2(a)
    bh, bl = _split2(b)
    dot = lambda p, q: lax.dot_general(p, q, dn, preferred_element_type=F32)
    return dot(ah, bh) + (dot(al, bh) + dot(ah, bl))


@functools.partial(jax.custom_vjp, nondiff_argnums=(2, 3))
def _mm3(a, b, ca, cb):
    return _mm3_raw(a, b, ca, cb)


def _mm3_fwd(a, b, ca, cb):
    return _mm3_raw(a, b, ca, cb), (a, b)


def _mm3_bwd(ca, cb, res, ct):
    a, b = res
    da = _mm3_raw(ct, b, 1, 1 - cb) if ca == 1 else _mm3_raw(b, ct, 1 - cb, 1)
    db = _mm3_raw(a, ct, 1 - ca, 0) if cb == 0 else _mm3_raw(ct, a, 0, 1 - ca)
    return da, db


_mm3.defvjp(_mm3_fwd, _mm3_bwd)


def _mmx_raw(t, x, ct):
    x1 = x.astype(BF16)
    r1 = x - x1.astype(F32)
    x2 = r1.astype(BF16)
    x3 = (r1 - x2.astype(F32)).astype(BF16)
    tb = t.astype(BF16)
    dot = lambda q: lax.dot_general(tb, q, (((ct,), (0,)), ((), ())), preferred_element_type=F32)
    return dot(x1) + (dot(x2) + dot(x3))


@jax.custom_vjp
def _mmx(t, x):
    return _mmx_raw(t, x, 1)


def _mmx_fwd(t, x):
    return _mmx_raw(t, x, 1), t


def _mmx_bwd(t, ct):
    return jnp.zeros_like(t), _mmx_raw(t, ct, 0)


_mmx.defvjp(_mmx_fwd, _mmx_bwd)

SCAN_HEADS = 16


def _scan_step(r, lw, k, v, a, b, st):
    c = r.shape[0]
    nh = r.shape[1] // HEAD
    incl = _tri(c, False)
    row2 = lax.broadcasted_iota(jnp.int32, (2 * c, 2 * c), 0)
    col2 = lax.broadcasted_iota(jnp.int32, (2 * c, 2 * c), 1)
    t_row = jnp.where(row2 >= c, row2 - c, row2)
    t_col = jnp.where(col2 >= c, col2 - c, col2)
    mask2 = (t_row > t_col) | ((row2 >= c) & (t_row == t_col))
    right =lax.broadcasted_iota(jnp.int32, (c, 2 * c), 1) >= c
    eye = lax.broadcasted_iota(jnp.int32, (HEAD, HEAD), 0) == lax.broadcasted_iota(jnp.int32, (HEAD, HEAD), 1)
    cl = _mmx(incl.astype(F32), lw)
    last = cl[c - 1:c, :]
    rt = r * jnp.exp(cl)
    at = a * jnp.exp(cl - lw)
    pinv = jnp.exp(-cl)
    bt = b * pinv
    kt = k * pinv
    pend = jnp.exp(last - cl)
    bl = b * pend
    kl = k * pend
    pe_last = jnp.exp(last)
    hs = range(nh)
    sl = [slice(h * HEAD, (h + 1) * HEAD) for h in hs]
    ar = [jnp.concatenate([at[:, sl[h]], rt[:, sl[h]]], axis=0) for h in hs]
    bk = [jnp.concatenate([bt[:, sl[h]], kt[:, sl[h]]], axis=0) for h in hs]
    amat = [jnp.where(mask2, _mm3(ar[h], bk[h], 1, 1), 0.0) for h in hs]
    res = [_mm3(jnp.concatenate([ar[h], amat[h][:, c:]], axis=1),
                jnp.concatenate([st[h], v[:, sl[h]]], axis=0), 1, 0) for h in hs]
    z = [jnp.concatenate([amat[h][:c, :c], res[h][:c]], axis=1) for h in hs]
    for _ in range(max(1, int(np.ceil(np.log2(c))))):
        z = [_mm3(z[h][:, :c], z[h], 1, 0) + jnp.where(right, z[h], 0.0) for h in hs]
    u = [z[h][:, c:] for h in hs]
    ys = [res[h][c:] + _mm3(amat[h][c:, :c], u[h], 1, 0) for h in hs]
    s1s = [_mm3(jnp.concatenate([bl[:, sl[h]], kl[:, sl[h]], jnp.where(eye, pe_last[:, sl[h]], 0.0)], axis=0),
                jnp.concatenate([u[h], v[:, sl[h]], st[h]], axis=0), 0, 0) for h in hs]
    return tuple(ys), tuple(s1s)


def _scan_heads(W):
    return SCAN_HEADS if W % (SCAN_HEADS * HEAD) == 0 else 2


def _scan_fwd(r, lw, k, v, a, b):
    L, W = r.shape
    nh = _scan_heads(W)
    nc, ng = L // CHUNK, W // (nh * HEAD)
    spec = pl.BlockSpec((CHUNK, nh * HEAD), lambda p, c: (c, p))

    def body(r_ref, lw_ref, k_ref, v_ref, a_ref, b_ref, y_ref, s_ref, st):
        @pl.when(pl.program_id(1) == 0)
        def _():
            st[...] = jnp.zeros_like(st)

        s0 = st[...]
        ys, s1s = _scan_step(r_ref[...], lw_ref[...], k_ref[...], v_ref[...], a_ref[...], b_ref[...], s0)
        s_ref[0] = s0
        st[...] = jnp.stack(s1s)
        y_ref[...] = jnp.concatenate(ys, axis=1)

    return pl.pallas_call(
        body, name="wkv7_fwd",
        out_shape=[jax.ShapeDtypeStruct((L, W), F32), jax.ShapeDtypeStruct((nc, nh * ng, HEAD, HEAD), F32)],
        grid=(ng, nc), in_specs=[spec] * 6,
        out_specs=[spec, pl.BlockSpec((1, nh, HEAD, HEAD), lambda p, c: (c, p, 0, 0))],
        scratch_shapes=[pltpu.VMEM((nh, HEAD, HEAD), F32)],
        compiler_params=_params(("parallel", "arbitrary")),
    )(r, lw, k, v, a, b)


def _scan_bwd(r, lw, k, v, a, b, s_all, dy):
    L, W = r.shape
    nh = _scan_heads(W)
    nc, ng = L // CHUNK, W // (nh * HEAD)
    spec = pl.BlockSpec((CHUNK, nh * HEAD), lambda p, c: (nc - 1 - c, p))

    def body(r_ref, lw_ref, k_ref, v_ref, a_ref, b_ref, s_ref, dy_ref,
             dr_ref, dlw_ref, dk_ref, dv_ref, da_ref, db_ref, dst):
        @pl.when(pl.program_id(1) == 0)
        def _():
            dst[...] = jnp.zeros_like(dst)

        _, vjp = jax.vjp(_scan_step, r_ref[...], lw_ref[...], k_ref[...], v_ref[...], a_ref[...], b_ref[...],
                         s_ref[0])
        dys = tuple(dy_ref[:, h * HEAD:(h + 1) * HEAD] for h in range(nh))
        g = vjp((dys, tuple(dst[h] for h in range(nh))))
        for ref, val in zip((dr_ref, dlw_ref, dk_ref, dv_ref, da_ref, db_ref), g[:6]):
            ref[...] = val
        dst[...] = g[6]

    return pl.pallas_call(
        body, name="wkv7_bwd", out_shape=[jax.ShapeDtypeStruct((L, W), F32)] * 6,
        grid=(ng, nc),
        in_specs=[spec] * 6 + [pl.BlockSpec((1, nh, HEAD, HEAD), lambda p, c: (nc - 1 - c, p, 0, 0)), spec],
        out_specs=[spec] * 6,
        scratch_shapes=[pltpu.VMEM((nh, HEAD, HEAD), F32)],
        compiler_params=_params(("parallel", "arbitrary")),
    )(r, lw, k, v, a, b, s_all, dy)


def _cumsum_rows(logf):
    L = logf.shape[0]
    t = LANES

    def body(x_ref, o_ref, carry):
        @pl.when(pl.program_id(0) == 0)
        def _():
            carry[...] = jnp.zeros_like(carry)

        c = _doth(_tri(t, False).astype(F32), x_ref[...]) + carry[...]
        carry[...] = c[t - 1:t, :]
        o_ref[...] = c.T

    return pl.pallas_call(
        body, name="fox_cumsum", out_shape=jax.ShapeDtypeStruct((LANES, L), F32), grid=(L // t,),
        in_specs=[pl.BlockSpec((t, LANES), lambda i: (i, 0))],
        out_specs=pl.BlockSpec((LANES, t), lambda i: (0, i)),
        scratch_shapes=[pltpu.VMEM((1, LANES), F32)], compiler_params=_params(("arbitrary",)),
    )(logf)


def _rcumsum_cols(dct):
    L = dct.shape[1]
    t = LANES
    n = L // t

    def body(x_ref, o_ref, carry):
        @pl.when(pl.program_id(0) == 0)
        def _():
            carry[...] = jnp.zeros_like(carry)

        rc = _doth(x_ref[...], _tri(t, False).astype(F32)) + carry[...]
        carry[...] = rc[:, 0:1]
        o_ref[...] = rc.T

    return pl.pallas_call(
        body, name="fox_rcumsum", out_shape=jax.ShapeDtypeStruct((L, LANES), F32), grid=(n,),
        in_specs=[pl.BlockSpec((LANES, t), lambda i: (0, n - 1 - i))],
        out_specs=pl.BlockSpec((t, LANES), lambda i: (n - 1 - i, 0)),
        scratch_shapes=[pltpu.VMEM((LANES, 1), F32)], compiler_params=_params(("arbitrary",)),
    )(dct)


def _attn_tile(L):
    return _pick(L, (384, 256, 128))


def _attn_scores(q, k, ck, qi, kj, t):
    s = _dot_bnt(q, k) * ATTN_SCALE - ck
    qpos = qi * t + lax.broadcasted_iota(jnp.int32, (t, t), 0)
    kpos = kj * t + lax.broadcasted_iota(jnp.int32, (t, t), 1)
    mask = (kpos <= qpos) & (kpos >= PAD_ROWS)
    return jnp.where(mask, s, NEG), mask


def _dot_bnt(a, b):
    return lax.dot_general(_bf(a), _bf(b), (((1,), (1,)), ((), ())), preferred_element_type=F32)


def _dot_btn(a, b):
    return lax.dot_general(_bf(a), _bf(b), (((0,), (0,)), ((), ())), preferred_element_type=F32)


def _ck_rows(ct_ref, p):
    r0 = 2 * (p % 4)
    return ct_ref[pl.ds(r0, 1), :], ct_ref[pl.ds(r0 + 1, 1), :]


def _attn_fwd(q, k, v, ct):
    L, W = q.shape
    t = _attn_tile(L)
    nt, npair = L // t, W // LANES
    qspec = pl.BlockSpec((t, LANES), lambda p, i, j: (i, p))
    kspec = pl.BlockSpec((t, LANES), lambda p, i, j: (jnp.minimum(i, j), p))
    cspec = pl.BlockSpec((8, t), lambda p, i, j: (p // 4, jnp.minimum(i, j)))

    def body(q_ref, k_ref, v_ref, ct_ref, o_ref, lse_ref, m_s, l_s, acc):
        p, i, j = pl.program_id(0), pl.program_id(1), pl.program_id(2)

        @pl.when(j == 0)
        def _():
            m_s[...] = jnp.full_like(m_s, NEG)
            l_s[...] = jnp.zeros_like(l_s)
            acc[...] = jnp.zeros_like(acc)

        @pl.when(j <= i)
        def _():
            cks = _ck_rows(ct_ref, p)
            for hh in range(2):
                sl = slice(hh * HEAD, (hh + 1) * HEAD)
                s, _ = _attn_scores(q_ref[:, sl], k_ref[:, sl], cks[hh], i, j, t)
                m_old = m_s[hh]
                m_new = jnp.maximum(m_old, jnp.max(s, axis=-1, keepdims=True))
                alpha = jnp.exp(m_old - m_new)
                pr = jnp.exp(s - m_new)
                l_s[hh] = alpha * l_s[hh] + jnp.sum(pr, axis=-1, keepdims=True)
                p_hi, p_lo = _split2(pr)
                vh = _bf(v_ref[:, sl])
                acc[hh] = alpha * acc[hh] + (jnp.dot(p_hi, vh, preferred_element_type=F32)
                                            + jnp.dot(p_lo, vh, preferred_element_type=F32))
                m_s[hh] = m_new

        @pl.when(j == i)
        def _():
            lane = lax.broadcasted_iota(jnp.int32, (t, LANES), 1)
            lse = jnp.zeros((t, LANES), F32)
            for hh in range(2):
                o_ref[:, hh * HEAD:(hh + 1) * HEAD] = acc[hh] / l_s[hh]
                lse = jnp.where(lane == hh, m_s[hh] + jnp.log(l_s[hh]), lse)
            lse_ref[0] = lse

    return pl.pallas_call(
        body, name="fox_attn_fwd",
        out_shape=[jax.ShapeDtypeStruct((L, W), F32), jax.ShapeDtypeStruct((npair, L, LANES), F32)],
        grid=(npair, nt, nt), in_specs=[qspec, kspec, kspec, cspec],
        out_specs=[qspec, pl.BlockSpec((1, t, LANES), lambda p, i, j: (p, i, 0))],
        scratch_shapes=[pltpu.VMEM((2, t, 1), F32), pltpu.VMEM((2, t, 1), F32), pltpu.VMEM((2, t, HEAD), F32)],
        compiler_params=_params(("parallel", "parallel", "arbitrary")),
    )(q, k, v, ct)


def _attn_bwd(q, k, v, ct, o, lse, do):
    L, W = q.shape
    t = _attn_tile(L)
    nt, npair = L // t, W // LANES
    kspec = pl.BlockSpec((t, LANES), lambda p, j, i: (j, p))
    qspec = pl.BlockSpec((t, LANES), lambda p, j, i: (jnp.maximum(i, j), p))
    cspec = pl.BlockSpec((8, t), lambda p, j, i: (p // 4, j))
    lspec = pl.BlockSpec((1, t, LANES), lambda p, j, i: (p, jnp.maximum(i, j), 0))

    def body(q_ref, k_ref, v_ref, ct_ref, o_ref, lse_ref, do_ref, dk_ref, dv_ref, dc_ref, dq_ref,
             dk_s, dv_s, dc_s):
        p, j, i = pl.program_id(0), pl.program_id(1), pl.program_id(2)

        @pl.when(i == 0)
        def _():
            dk_s[...] = jnp.zeros_like(dk_s)
            dv_s[...] = jnp.zeros_like(dv_s)
            dc_s[...] = jnp.zeros_like(dc_s)

        def tile_dq():
            cks = _ck_rows(ct_ref, p)
            dqs = []
            for hh in range(2):
                sl = slice(hh * HEAD, (hh + 1) * HEAD)
                s, mask = _attn_scores(q_ref[:, sl], k_ref[:, sl], cks[hh], i, j, t)
                pr = jnp.where(mask, jnp.exp(s - lse_ref[0, :, hh:hh + 1]), 0.0)
                doh = _bf(do_ref[:, sl])
                delta = jnp.sum(doh.astype(F32) * o_ref[:, sl], axis=-1, keepdims=True)
                ds = pr * (_dot_bnt(doh, v_ref[:, sl]) - delta)
                dv_s[hh] += _dot_btn(pr, doh)
                dk_s[hh] += _dot_btn(ds, q_ref[:, sl])
                dc_s[hh] += -jnp.sum(ds, axis=0, keepdims=True)
                dqs.append(_dotb(ds, k_ref[:, sl]) * ATTN_SCALE)
            return jnp.concatenate(dqs, axis=1)

        rows = pl.ds(pl.multiple_of(i * t, t), t)

        @pl.when((i >= j) & (j == 0))
        def _():
            dq_ref[rows, :] = tile_dq()

        @pl.when((i >= j) & (j > 0))
        def _():
            dq_ref[rows, :] += tile_dq()

        @pl.when(i == nt - 1)
        def _():
            row = lax.broadcasted_iota(jnp.int32, (8, t), 0)
            dc = jnp.zeros((8, t), F32)
            for hh in range(2):
                dk_ref[:, hh * HEAD:(hh + 1) * HEAD] = dk_s[hh] * ATTN_SCALE
                dv_ref[:, hh * HEAD:(hh + 1) * HEAD] = dv_s[hh]
                dc = jnp.where(row == hh, dc_s[hh], dc)
            dc_ref[0] = dc

    return pl.pallas_call(
        body, name="fox_attn_bwd",
        out_shape=[jax.ShapeDtypeStruct((L, W), F32), jax.ShapeDtypeStruct((L, W), F32),
                   jax.ShapeDtypeStruct((npair, 8, L), F32), jax.ShapeDtypeStruct((L, W), F32)],
        grid=(npair, nt, nt), in_specs=[qspec, kspec, kspec, cspec, qspec, lspec, qspec],
        out_specs=[kspec, kspec, pl.BlockSpec((1, 8, t), lambda p, j, i: (p, 0, j)),
                   pl.BlockSpec((L, LANES), lambda p, j, i: (0, p))],
        scratch_shapes=[pltpu.VMEM((2, t, HEAD), F32), pltpu.VMEM((2, t, HEAD), F32), pltpu.VMEM((2, 1, t), F32)],
        compiler_params=_params(("parallel", "arbitrary", "arbitrary")),
    )(q, k, v, ct, o, lse, do)


def _place():
    x, y, c = lax.axis_index("x"), lax.axis_index("y"), lax.axis_index("c")
    chips = [(1 - x, y), (x, 1 - y), (1 - x, 1 - y)]
    return x, y, c, chips


def _remote(src, dst, send_sems, recv_sems, k, to):
    return pltpu.make_async_remote_copy(src_ref=src, dst_ref=dst, send_sem=send_sems.at[k],
                                        recv_sem=recv_sems.at[k], device_id=to, device_id_type=MESH)


def _dma_sems(n):
    return [pltpu.SemaphoreType.DMA((n,)), pltpu.SemaphoreType.DMA((n,))]


def _all_gather(shards, modes):
    n = len(shards)

    def out_shape(s, mode):
        r, c = s.shape
        return {"stack": (4, r, c), "cols": (r, 4 * c), "rows": (4 * r, c)}[mode]

    def body(*refs):
        ins, outs, (send_sems, recv_sems) = refs[:n], refs[n:2 * n], refs[2 * n:]
        x, y, c, chips = _place()
        me_chip = 2 * x + y
        sibling = (x, y, 1 - c)

        def window(i, chip, cc):
            r, cw = shards[i].shape
            h = r // 2
            if modes[i] == "stack":
                return outs[i].at[chip, pl.ds(cc * h, h), :]
            if modes[i] == "cols":
                return outs[i].at[pl.ds(cc * h, h), pl.ds(pl.multiple_of(chip * cw, LANES), cw)]
            return outs[i].at[pl.ds(pl.multiple_of(chip * r + cc * h, 8), h), :]

        first = []
        for i in range(n):
            h = shards[i].shape[0] // 2
            for k, (cx, cy) in enumerate(chips):
                first.append(_remote(ins[i].at[pl.ds(c * h, h), :], window(i, me_chip, c), send_sems, recv_sems,
                                     6 * i + k, (cx, cy, c)))
        for cp in first:
            cp.start()
        passed = []
        for k, (cx, cy) in enumerate(chips):
            for i in range(n):
                landed = window(i, 2 * cx + cy, c)
                _remote(landed, landed, send_sems, recv_sems, 6 * i + k, sibling).wait_recv()
                fwd = _remote(landed, landed, send_sems, recv_sems, 6 * i + 3 + k, sibling)
                fwd.start()
                passed.append(fwd)
        for k, (cx, cy) in enumerate(chips):
            for i in range(n):
                other = window(i, 2 * cx + cy, 1 - c)
                _remote(other, other, send_sems, recv_sems, 6 * i + 3 + k, sibling).wait_recv()
        for cp in first + passed:
            cp.wait_send()

    return pl.pallas_call(
        body, name="gather_weights",
        out_shape=[jax.ShapeDtypeStruct(out_shape(s, m), s.dtype) for s, m in zip(shards, modes)],
        in_specs=[ANY] * n, out_specs=[ANY] * n, scratch_shapes=_dma_sems(6 * n),
    )(*shards)


def _pair_exchange(blocks):
    n = len(blocks)

    def body(*refs):
        ins, outs, (send_sems, recv_sems) = refs[:n], refs[n:2 * n], refs[2 * n:]
        x, y, c, _ = _place()
        cps = []
        for i in range(n):
            h = blocks[i].shape[1] // 2
            cps.append(_remote(ins[i].at[:, pl.ds((1 - c) * h, h), :], outs[i], send_sems, recv_sems, i,
                               (x, y, 1 - c)))
        for cp in cps:
            cp.start()
        for cp in cps:
            cp.wait()

    return pl.pallas_call(
        body, name="reduce_pair_exchange",
        out_shape=[jax.ShapeDtypeStruct((4, b.shape[1] // 2, b.shape[2]), b.dtype) for b in blocks],
        in_specs=[ANY] * n, out_specs=[ANY] * n, scratch_shapes=_dma_sems(n),
    )(*blocks)


def _chip_exchange(parts):
    n = len(parts)

    def body(*refs):
        ins, outs, (send_sems, recv_sems) = refs[:n], refs[n:2 * n], refs[2 * n:]
        x, y, c, chips = _place()
        me_chip = 2 * x + y
        sends = [_remote(ins[i].at[2 * cx + cy], outs[i].at[me_chip], send_sems, recv_sems, 3 * i + k, (cx, cy, c))
                 for i in range(n) for k, (cx, cy) in enumerate(chips)]
        for cp in sends:
            cp.start()
        for i in range(n):
            for k, (cx, cy) in enumerate(chips):
                slot = outs[i].at[2 * cx + cy]
                _remote(slot, slot, send_sems, recv_sems, 3 * i + k, (cx, cy, c)).wait_recv()
        for cp in sends:
            cp.wait_send()

    return pl.pallas_call(
        body, name="reduce_chip_exchange", out_shape=[jax.ShapeDtypeStruct(p.shape, p.dtype) for p in parts],
        in_specs=[ANY] * n, out_specs=[ANY] * n, scratch_shapes=_dma_sems(3 * n),
    )(*parts)


def _pair_join(tots):
    n = len(tots)

    def body(*refs):
        ins, outs, (send_sems, recv_sems) = refs[:n], refs[n:2 * n], refs[2 * n:]
        x, y, c, _ = _place()
        cps = [_remote(ins[i], outs[i], send_sems, recv_sems, i, (x, y, 1 - c)) for i in range(n)]
        for cp in cps:
            cp.start()
        for cp in cps:
            cp.wait()

    return pl.pallas_call(
        body, name="reduce_pair_join", out_shape=[jax.ShapeDtypeStruct(t.shape, t.dtype) for t in tots],
        in_specs=[ANY] * n, out_specs=[ANY] * n, scratch_shapes=_dma_sems(n),
    )(*tots)


def _add_tile(h, cw):
    cap = max(8, (512 * 1024) // max(cw, 1))
    return _pick(h, tuple(t for t in (1024, 512, 256, 128, 64, 32, 16, 8) if t <= cap))


def _pair_add(name, block, recv):
    n, r, cw = block.shape
    h = r // 2
    tr = _add_tile(h, cw)
    c = lax.axis_index("c").astype(jnp.int32).reshape((1,))

    def body(c_ref, a_ref, b_ref, o_ref):
        o_ref[...] = (a_ref[...] + b_ref[...]).astype(o_ref.dtype)

    return pl.pallas_call(
        body, name=name, out_shape=jax.ShapeDtypeStruct((n, h, cw), BF16),
        grid_spec=pltpu.PrefetchScalarGridSpec(
            num_scalar_prefetch=1, grid=(n, h // tr),
            in_specs=[pl.BlockSpec((1, tr, cw), lambda a, i, cr: (a, cr[0] * (h // tr) + i, 0)),
                      pl.BlockSpec((1, tr, cw), lambda a, i, cr: (a, i, 0))],
            out_specs=pl.BlockSpec((1, tr, cw), lambda a, i, cr: (a, i, 0))),
        compiler_params=_params(("parallel", "parallel")),
    )(c, block, recv)


def _chip_add(name, parts):
    n, h, cw = parts.shape
    tr = _add_tile(h, cw)

    def body(a_ref, o_ref):
        f = lambda i: a_ref[i].astype(F32)
        o_ref[...] = ((f(0) + f(1)) + f(2)) + f(3)

    return pl.pallas_call(
        body, name=name, out_shape=jax.ShapeDtypeStruct((h, cw), F32), grid=(h // tr,),
        in_specs=[pl.BlockSpec((n, tr, cw), lambda i: (0, i, 0))],
        out_specs=pl.BlockSpec((tr, cw), lambda i: (i, 0)),
        compiler_params=_params(("parallel",)),
    )(parts)


def _adamw(name, w, g, m, v):
    R, C = w.shape
    tr = _pick(R, (128, 64, 32, 16, 8))
    spec = pl.BlockSpec((tr, C), lambda i: (i, 0))

    def body(w_ref, g_ref, m_ref, v_ref, d_ref, mo_ref, vo_ref):
        gr = g_ref[...]
        mn = ADAM_B1 * m_ref[...] + (1.0 - ADAM_B1) * gr
        vn = ADAM_B2 * v_ref[...] + (1.0 - ADAM_B2) * jnp.square(gr)
        m_hat = mn / (1.0 - ADAM_B1 ** ADAM_STEP)
        v_hat = vn / (1.0 - ADAM_B2 ** ADAM_STEP)
        d_ref[...] = -ADAM_LR * (m_hat / (jnp.sqrt(v_hat) + ADAM_EPS) + ADAM_WD * w_ref[...])
        mo_ref[...] = mn
        vo_ref[...] = vn

    return pl.pallas_call(
        body, name=name, out_shape=[jax.ShapeDtypeStruct((R, C), F32)] * 3, grid=(R // tr,),
        in_specs=[spec] * 4, out_specs=[spec] * 3, compiler_params=_params(("parallel",)),
    )(w, g, m, v)


def _pack(parts, dtype, row_mult):
    flat = jnp.concatenate([p.reshape(-1).astype(dtype) for p in parts])
    unit = row_mult * PACK_W
    pad = (-flat.shape[0]) % unit
    if pad:
        flat = jnp.concatenate([flat, jnp.zeros((pad,), dtype)])
    return flat.reshape(-1, PACK_W)


def _unpack(flat, shapes):
    out, off = [], 0
    for s in shapes:
        n = int(np.prod(s))
        out.append(flat[off:off + n].reshape(s))
        off += n
    return out


SHARDED = ("w_in", "rwkv_w2", "rwkv_a2", "rwkv_g2", "w_branch_a", "w_branch_b", "w_o", "w_gate_up", "w_down",
           "meta_tokens")
SHARD_AXIS = {"w_in": 1, "rwkv_w2": 1, "rwkv_a2": 1, "rwkv_g2": 1, "w_branch_a": 1, "w_branch_b": 1, "w_o": 0,
              "w_gate_up": 1, "w_down": 0, "meta_tokens": 1}
GATHER_MODE = {"w_in": "stack", "rwkv_w2": "stack", "rwkv_a2": "stack", "rwkv_g2": "stack", "w_branch_a": "cols",
               "w_branch_b": "cols", "w_o": "rows", "w_gate_up": "cols", "w_down": "rows", "meta_tokens": "stack"}
SMALL = ("norm1_g", "rwkv_mu", "rwkv_w0", "rwkv_a0", "rwkv_k_k", "rwkv_k_a", "rwkv_r_k", "rwkv_gn_w",
         "rwkv_gn_b", "fox_q_norm_g", "fox_k_norm_g", "fox_f_bias", "norm2_g")
WEIGHTS = ("meta_tokens", "norm1_g", "w_in", "rwkv_mu", "rwkv_w0", "rwkv_w2", "rwkv_a0", "rwkv_a2", "rwkv_g2",
           "rwkv_k_k", "rwkv_k_a", "rwkv_r_k", "rwkv_gn_w", "rwkv_gn_b", "fox_q_norm_g", "fox_k_norm_g",
           "fox_f_bias", "w_branch_a", "w_branch_b", "w_o", "norm2_g", "w_gate_up", "w_down")


def _pad_rows(t, rows):
    return jnp.concatenate([t, jnp.zeros((rows - t.shape[0],) + t.shape[1:], t.dtype)], axis=0)


def _pad_cols(t, cols):
    return jnp.concatenate([t, jnp.zeros(t.shape[:-1] + (cols - t.shape[-1],), t.dtype)], axis=-1)


def _step(x, tgt, wts, mom1, mom2):
    seq, D = x.shape
    L = SEQ_ROW0 + seq
    RW = wts["rwkv_w0"].shape[-1]
    DL, AL, GL = wts["rwkv_w2"].shape[0], wts["rwkv_a2"].shape[0], wts["rwkv_g2"].shape[0]
    FW = wts["w_branch_b"].shape[0]
    FH = wts["fox_f_bias"].shape[-1]
    DFF = wts["w_down"].shape[0] * 4
    LORA = DL + AL + GL
    LW = -(-(LORA + FH) // 512) * 512
    assert RW == FW and (6 * RW) % D == 0 and (6 * RW + 2 * D) % LW == 0 and LORA % 8 == 0
    xj = lax.axis_index("x")
    yj = lax.axis_index("y")
    chip = 2 * xj + yj

    send = [wts[n] if n == "meta_tokens" else wts[n].astype(BF16) for n in SHARDED]
    modes = [GATHER_MODE[n] for n in SHARDED]
    gathered = _all_gather(send, modes)
    full, stacked = {}, {}
    for n, mode, shard, got in zip(SHARDED, modes, send, gathered):
        r, cw = shard.shape
        if mode == "stack":
            got = lax.dynamic_update_index_in_dim(got, shard, chip, 0)
            stacked[n] = got
            full[n] = jnp.concatenate([got[j] for j in range(4)], axis=1)
        elif mode == "cols":
            full[n] = lax.dynamic_update_slice(got, shard, (0, chip * cw))
        else:
            full[n] = lax.dynamic_update_slice(got, shard, (chip * r, 0))
    meta = full["meta_tokens"]
    w_in = full["w_in"]
    o = 0
    segs = {}
    for nm, wd in (("r", RW), ("k", RW), ("v", RW), ("wd", DL), ("ad", AL), ("gd", GL),
                   ("fq", FW), ("fk", FW), ("fv", FW), ("ff", FH), ("ga", D), ("gb", D)):
        segs[nm] = w_in[:, o:o + wd]
        o += wd
    lora_w = _pad_cols(jnp.concatenate([segs["wd"], segs["ad"], segs["gd"], segs["ff"]], axis=1), LW)
    w1 = jnp.concatenate([segs["r"], segs["k"], segs["v"], segs["fq"], segs["fk"], segs["fv"],
                          segs["ga"], segs["gb"], lora_w], axis=1)
    cb_f = 3
    cb_gate = (6 * RW) // D
    cb_lora = (6 * RW + 2 * D) // LW

    e_m, et_m = _head_mats(RW)
    ft_m = _fold_mat(RW)
    mu = wts["rwkv_mu"]
    mu_rkv = mu[:, :3 * RW]
    mu_l = _pad_cols(mu[:, 3 * RW:], LW)
    w2p = _pad_rows(full["rwkv_w2"].astype(F32), LW)
    a2p = _pad_rows(jnp.concatenate([jnp.zeros((DL, RW), F32), full["rwkv_a2"].astype(F32)], axis=0), LW)
    g2p = _pad_rows(jnp.concatenate([jnp.zeros((DL + AL, RW), F32), full["rwkv_g2"].astype(F32)], axis=0), LW)
    r_k = wts["rwkv_r_k"].reshape(1, RW)
    qg8 = jnp.broadcast_to(wts["fox_q_norm_g"], (8, HEAD))
    kg8 = jnp.broadcast_to(wts["fox_k_norm_g"], (8, HEAD))
    fb = _pad_cols(wts["fox_f_bias"], LANES)
    fmask = (jnp.arange(LANES) < FH).astype(F32).reshape(1, LANES)
    lmask = ((jnp.arange(LW) >= LORA) & (jnp.arange(LW) < LORA + FH)).astype(F32).reshape(1, LW)

    h0 = jnp.concatenate([jnp.zeros((PAD_ROWS, D), F32), meta, x], axis=0)
    n1 = wts["norm1_g"]

    (xn,), _ = _rowcall("rms1_fwd", lambda i, r, b: ([_rms_f(r[0], b[0])], []), L,
                        [(h0, D, 0, "row")], [n1], [(D, BF16)], [])
    proj = _matmul(xn, w1, "nn", F32, "proj_fwd")

    def shift_fn(i, r, b):
        rows = lax.broadcasted_iota(jnp.int32, (ROW_TILE, 1), 0)
        outs = []
        for z, halo, m_ in ((r[0], r[1], b[0]), (r[2], r[3], b[1])):
            first = jnp.where(i == 0, 0.0, halo[7:8, :])
            zp = jnp.where(rows == 0, first, pltpu.roll(z, 1, 0))
            outs.append(z + (zp - z) * m_)
        return outs, []

    rkv_w = 3 * RW
    (x_rkv, x_l), _ = _rowcall(
        "shift_fwd", shift_fn, L,
        [(proj, rkv_w, 0, "row"), (proj, rkv_w, 0, "prev"), (proj, LW, cb_lora, "row"), (proj, LW, cb_lora, "prev")],
        [mu_rkv, mu_l], [(rkv_w, F32), (LW, F32)], [])

    prep_p = [wts["rwkv_w0"], w2p, wts["rwkv_a0"], a2p, g2p, wts["rwkv_k_k"], wts["rwkv_k_a"], e_m, et_m]
    prep_rows = [(x_rkv, RW, 0, "row"), (x_rkv, RW, 1, "row"), (x_rkv, RW, 2, "row"), (x_l, LW, 0, "row")]
    (s_r, s_lw, s_k, s_v, s_a, s_b, gate_g), _ = _rowcall(
        "rwkv_prep_fwd", lambda i, r, b: (list(_prep_f(*r, *b)), []), L, prep_rows, prep_p,
        [(RW, F32)] * 7, [])
    y_scan, s_all = _scan_fwd(s_r, s_lw, s_k, s_v, s_a, s_b)
    post_p = [wts["rwkv_gn_w"], wts["rwkv_gn_b"], r_k, e_m, et_m]
    post_rows = [(y_scan, RW, 0, "row"), (s_r, RW, 0, "row"), (s_k, RW, 0, "row"), (s_v, RW, 0, "row"),
                 (gate_g, RW, 0, "row")]
    (y_a,), _ = _rowcall("rwkv_post_fwd", lambda i, r, b: ([_post_f(*r, *b)], []), L, post_rows, post_p,
                         [(RW, BF16)], [])

    fox_p = [qg8, kg8, fb, e_m, et_m, ft_m, fmask]

    def foxprep_fn(i, r, b):
        fl = _doth(r[2] * b[-1], b[-2])
        return list(_foxprep_f(r[0], r[1], fl, *b[:-2])), []

    sel = (np.arange(LW)[:, None] - LORA == np.arange(LANES)[None, :]).astype(np.float32)
    sel = jnp.asarray(sel)
    fox_rows = [(proj, FW, cb_f, "row"), (proj, FW, cb_f + 1, "row"), (proj, LW, cb_lora, "row")]
    (f_q, f_k, logf), _ = _rowcall("fox_prep_fwd", foxprep_fn, L, fox_rows, fox_p + [sel, lmask],
                                   [(FW, BF16), (FW, BF16), (LANES, F32)], [])
    ct = _cumsum_rows(logf)
    f_v = proj[:, (cb_f + 2) * FW:(cb_f + 3) * FW]
    y_b32, lse = _attn_fwd(f_q, f_k, f_v, ct)
    y_b = y_b32.astype(BF16)

    p_a = _matmul(y_a, full["w_branch_a"], "nn", F32, "branch_a_fwd")
    p_b = _matmul(y_b, full["w_branch_b"], "nn", F32, "branch_b_fwd")
    merge_rows = [(proj, D, cb_gate, "row"), (proj, D, cb_gate + 1, "row"), (p_a, D, 0, "row"), (p_b, D, 0, "row")]
    (merged,), _ = _rowcall("merge_fwd", lambda i, r, b: ([_merge_f(*r)], []), L, merge_rows, [], [(D, BF16)], [])
    h1 = _matmul(merged, full["w_o"], "nn", F32, "wo_fwd", add=h0)
    n2 = wts["norm2_g"]
    (xn2,), _ = _rowcall("rms2_fwd", lambda i, r, b: ([_rms_f(r[0], b[0])], []), L,
                         [(h1, D, 0, "row")], [n2], [(D, BF16)], [])
    gu = _matmul(xn2, full["w_gate_up"], "nn", F32, "gate_up_fwd")
    gu_rows = [(gu, DFF, 0, "row"), (gu, DFF, 1, "row")]
    (act,), _ = _rowcall("swiglu_fwd", lambda i, r, b: ([_swiglu_f(*r)], []), L, gu_rows, [], [(DFF, BF16)], [])
    h2 = _matmul(act, full["w_down"], "nn", F32, "down_fwd", add=h1)

    def loss_fn(i, r, b):
        err = jnp.where(i == 0, 0.0, r[0] - r[1])
        return [err * (1.0 / D)], [jnp.zeros((8, LANES), F32) + 0.5 / D * jnp.sum(err * err)]

    (dh2,), (loss_acc,) = _rowcall("loss", loss_fn, L, [(h2, D, 0, "row"), (tgt, D, 0, "lag")], [],
                                   [(D, F32)], [(8, LANES)])
    loss = lax.psum(loss_acc[0, 0], ("x", "y", "c"))

    dh2b = dh2.astype(BF16)
    g_w_down = _matmul(act, dh2b, "tn", F32, "down_dw")
    d_act = _matmul(dh2b, full["w_down"], "nt", F32, "down_dx")

    def swiglu_bwd(i, r, b):
        _, vjp = jax.vjp(_swiglu_f, r[0], r[1])
        return list(vjp(r[2])), []

    (d_gate, d_up), _ = _rowcall("swiglu_bwd", swiglu_bwd, L, gu_rows + [(d_act, DFF, 0, "row")], [],
                                 [(DFF, BF16), (DFF, BF16)], [])
    d_gu = jnp.concatenate([d_gate, d_up], axis=1)
    g_w_gu = _matmul(xn2, d_gu, "tn", F32, "gate_up_dw", col_blocks=4)
    d_xn2 = _matmul(d_gu, full["w_gate_up"], "nt", F32, "gate_up_dx")

    def rms_bwd(i, r, b):
        _, vjp = jax.vjp(_rms_f, r[0], b[0])
        dh, dg = vjp(r[1])
        return [dh + r[2]], [dg]

    (dh1,), (g_n2,) = _rowcall("rms2_bwd", rms_bwd, L,
                               [(h1, D, 0, "row"), (d_xn2, D, 0, "row"), (dh2, D, 0, "row")], [n2],
                               [(D, F32)], [(1, D)])
    dh1b = dh1.astype(BF16)
    g_w_o = _matmul(merged, dh1b, "tn", F32, "wo_dw")
    d_merged = _matmul(dh1b, full["w_o"], "nt", F32, "wo_dx")

    def merge_bwd(i, r, b):
        _, vjp = jax.vjp(_merge_f, *r[:4])
        return list(vjp(r[4])), []

    (d_za, d_zb, d_pa, d_pb), _ = _rowcall("merge_bwd", merge_bwd, L, merge_rows + [(d_merged, D, 0, "row")], [],
                                           [(D, BF16)] * 4, [])
    g_w_a = _matmul(y_a, d_pa, "tn", F32, "branch_a_dw", col_blocks=4)
    g_w_b = _matmul(y_b, d_pb, "tn", F32, "branch_b_dw", col_blocks=4)
    d_ya = _matmul(d_pa, full["w_branch_a"], "nt", F32, "branch_a_dx")
    d_yb = _matmul(d_pb, full["w_branch_b"], "nt", F32, "branch_b_dx")

    d_fk, d_fv, dc_rows, d_fq = _attn_bwd(f_q, f_k, f_v, ct, y_b32, lse, d_yb)
    dct = _pad_rows(dc_rows[:, :2, :].reshape(-1, L), LANES)
    d_logf = _rcumsum_cols(dct)

    def foxprep_bwd(i, r, b):
        def f(q, k, xl, qg, kg, fbias):
            return _foxprep_f(q, k, _doth(xl * b[-1], b[-2]), qg, kg, fbias, *b[3:7])
        _, vjp = jax.vjp(f, r[0], r[1], r[2], b[0], b[1], b[2])
        dq, dk, dxl, dqg, dkg, dfb = vjp((r[3], r[4], r[5]))
        return [dq, dk, dxl], [dqg, dkg, dfb]

    (d_zfq, d_zfk, d_zl_f), (g_qg8, g_kg8, g_fb) = _rowcall(
        "fox_prep_bwd", foxprep_bwd, L,
        fox_rows + [(d_fq, FW, 0, "row"), (d_fk, FW, 0, "row"), (d_logf, LANES, 0, "row")],
        fox_p + [sel, lmask], [(FW, BF16), (FW, BF16), (LW, F32)], [(8, HEAD), (8, HEAD), (1, LANES)])

    def post_bwd(i, r, b):
        _, vjp = jax.vjp(lambda *a: _post_f(*a, b[3], b[4]), *r[:5], b[0], b[1], b[2])
        g = vjp(r[5])
        return list(g[:5]), list(g[5:])

    (d_y, d_r1, d_k1, d_v1, d_g), (g_gn_w, g_gn_b, g_r_k) = _rowcall(
        "rwkv_post_bwd", post_bwd, L, post_rows + [(d_ya, RW, 0, "row")], post_p,
        [(RW, F32)] * 5, [(1, RW)] * 3)
    d_r2, d_lw, d_k2, d_v2, d_a, d_b = _scan_bwd(s_r, s_lw, s_k, s_v, s_a, s_b, s_all, d_y)

    def prep_bwd(i, r, b):
        _, vjp = jax.vjp(lambda *a: _prep_f(*a, b[7], b[8]), *r[:4], *b[:7])
        cts = (r[4] + r[10], r[5], r[6] + r[11], r[7] + r[12], r[8], r[9], r[13])
        g = vjp(cts)
        return list(g[:4]), list(g[4:])

    bwd_rows = prep_rows + [(d_r2, RW, 0, "row"), (d_lw, RW, 0, "row"), (d_k2, RW, 0, "row"), (d_v2, RW, 0, "row"),
                            (d_a, RW, 0, "row"), (d_b, RW, 0, "row"), (d_r1, RW, 0, "row"), (d_k1, RW, 0, "row"),
                            (d_v1, RW, 0, "row"), (d_g, RW, 0, "row")]
    (d_xr, d_xk, d_xv, d_xl), (g_w0, g_w2p, g_a0, g_a2p, g_g2p, g_kk, g_ka) = _rowcall(
        "rwkv_prep_bwd", prep_bwd, L, bwd_rows, prep_p, [(RW, F32)] * 3 + [(LW, F32)],
        [(1, RW), (LW, RW), (1, RW), (LW, RW), (LW, RW), (1, RW), (1, RW)])

    def shift_bwd(i, r, b):
        last = pl.num_programs(0) - 1
        rows = lax.broadcasted_iota(jnp.int32, (ROW_TILE, 1), 0)
        outs, sums = [], []
        groups = ((r[0], r[1], r[2], r[3], b[0], None), (r[4], r[5], r[6], r[7], b[1], r[8]))
        for d, dnext, z, zhalo, m_, extra in groups:
            nxt = jnp.where(i == last, 0.0, dnext[0:1, :])
            d_up = jnp.where(rows == ROW_TILE - 1, nxt, pltpu.roll(d, ROW_TILE - 1, 0))
            dz = d * (1.0 - m_) + d_up * m_
            if extra is not None:
                dz = dz + extra
            first = jnp.where(i == 0, 0.0, zhalo[7:8, :])
            zp = jnp.where(rows == 0, first, pltpu.roll(z, 1, 0))
            outs.append(dz)
            sums.append(_rowsum(d * (zp - z)))
        return outs, sums

    d_xrkv = jnp.concatenate([d_xr, d_xk, d_xv], axis=1)
    (d_zrkv, d_zl), (g_mu_rkv, g_mu_l) = _rowcall(
        "shift_bwd", shift_bwd, L,
        [(d_xrkv, rkv_w, 0, "row"), (d_xrkv, rkv_w, 0, "next"), (proj, rkv_w, 0, "row"), (proj, rkv_w, 0, "prev"),
         (d_xl, LW, 0, "row"), (d_xl, LW, 0, "next"), (proj, LW, cb_lora, "row"), (proj, LW, cb_lora, "prev"),
         (d_zl_f, LW, 0, "row")],
        [mu_rkv, mu_l], [(rkv_w, BF16), (LW, BF16)], [(1, rkv_w), (1, LW)])

    n_in = stacked["w_in"].shape[2] * 4
    cs = n_in // 4
    cp = -(-cs // LANES) * LANES
    d_ref = jnp.concatenate([d_zrkv, d_zl[:, :LORA], d_zfq, d_zfk, d_fv.astype(BF16), d_zl[:, LORA:LORA + FH],
                             d_za, d_zb], axis=1)
    d_blk = jnp.concatenate([_pad_cols(d_ref[:, j * cs:(j + 1) * cs], cp) for j in range(4)], axis=1)
    w_blk = jnp.concatenate([_pad_cols(stacked["w_in"][j], cp) for j in range(4)], axis=1)
    g_w_in = _matmul(xn, d_blk, "tn", F32, "proj_dw", col_blocks=4)
    d_xn = _matmul(d_blk, w_blk, "nt", F32, "proj_dx")
    (dh0,), (g_n1,) = _rowcall("rms1_bwd", rms_bwd, L,
                               [(h0, D, 0, "row"), (d_xn, D, 0, "row"), (dh1, D, 0, "row")], [n1],
                               [(D, F32)], [(1, D)])
    grad_x = dh0[SEQ_ROW0:]
    g_meta = dh0[PAD_ROWS:SEQ_ROW0]

    tiny = {"rwkv_w2": g_w2p[:DL], "rwkv_a2": g_a2p[DL:DL + AL], "rwkv_g2": g_g2p[DL + AL:LORA],
            "meta_tokens": g_meta}
    g_mu = jnp.concatenate([g_mu_rkv, g_mu_l[:, :LORA]], axis=1)
    gsmall = {
        "norm1_g": g_n1, "rwkv_mu": g_mu, "rwkv_w0": g_w0, "rwkv_a0": g_a0, "rwkv_k_k": g_kk, "rwkv_k_a": g_ka,
        "rwkv_r_k": g_r_k.reshape(wts["rwkv_r_k"].shape), "rwkv_gn_w": g_gn_w, "rwkv_gn_b": g_gn_b,
        "fox_q_norm_g": g_qg8[0:1], "fox_k_norm_g": g_kg8[0:1], "fox_f_bias": g_fb[:, :FH], "norm2_g": g_n2,
    }
    small_flat = jnp.concatenate([gsmall[n].reshape(-1) for n in SMALL])

    tiny_names = tuple(tiny)

    def tiny_block(j):
        parts = []
        for n in tiny_names:
            w = tiny[n].shape[1] // 4
            parts.append(tiny[n][:, j * w:(j + 1) * w])
        return _pack(parts + [small_flat], F32, 32)

    big_names = ("w_in", "w_gate_up", "w_branch_a", "w_branch_b", "w_o", "w_down")
    blocks = [g_w_in, g_w_gu, g_w_a, g_w_b, g_w_o.reshape(4, -1, D), g_w_down.reshape(4, -1, D),
              jnp.stack([tiny_block(j) for j in range(4)])]
    names = big_names + ("small",)
    cj = lax.axis_index("c")
    recv1 = _pair_exchange(blocks)
    parts = [_pair_add("reduce_pair_add_" + n, b, r) for n, b, r in zip(names, blocks, recv1)]
    recv2 = _chip_exchange(parts)
    tots = []
    for n, p, r in zip(names, parts, recv2):
        own = lax.dynamic_index_in_dim(p, chip, 0, keepdims=False)
        tots.append(_chip_add("reduce_chip_add_" + n, lax.dynamic_update_index_in_dim(r, own, chip, 0)))
    others = _pair_join(tots)
    red = [jnp.where(cj == 0, jnp.concatenate([t, o_], axis=0), jnp.concatenate([o_, t], axis=0))
           for t, o_ in zip(tots, others)]
    grads = {n: red[i] for i, n in enumerate(big_names)}
    grads["w_in"] = grads["w_in"][:, :cs]
    tiny_shapes = [wts[n].shape for n in tiny_names]
    got = _unpack(red[-1].reshape(-1), tiny_shapes + [small_flat.shape])
    for n, t in zip(tiny_names, got):
        grads[n] = t
    for n, t in zip(SMALL, _unpack(got[-1], [wts[n].shape for n in SMALL])):
        grads[n] = t

    delta, new_m, new_v = {}, {}, {}
    for n in SHARDED:
        delta[n], new_m[n], new_v[n] = _adamw("adamw_" + n, wts[n], grads[n], mom1[n], mom2[n])
    pk = lambda d: _pack([d[n] for n in SMALL], F32, 8)
    ds, ms, vs = _adamw("adamw_small", pk(wts), pk(grads), pk(mom1), pk(mom2))
    small_shapes = [wts[n].shape for n in SMALL]
    for dst, src in ((delta, ds), (new_m, ms), (new_v, vs)):
        for n, t in zip(SMALL, _unpack(src.reshape(-1), small_shapes)):
            dst[n] = t
    return loss, grad_x, grads, delta, new_m, new_v


def kernel(x, meta_tokens, norm1_g, w_in, rwkv_mu, rwkv_w0, rwkv_w2, rwkv_a0, rwkv_a2, rwkv_g2, rwkv_k_k, rwkv_k_a, rwkv_r_k, rwkv_gn_w, rwkv_gn_b, fox_q_norm_g, fox_k_norm_g, fox_f_bias, w_branch_a, w_branch_b, w_o, norm2_g, w_gate_up, w_down, loss_target, m_meta_tokens, m_norm1_g, m_w_in, m_rwkv_mu, m_rwkv_w0, m_rwkv_w2, m_rwkv_a0, m_rwkv_a2, m_rwkv_g2, m_rwkv_k_k, m_rwkv_k_a, m_rwkv_r_k, m_rwkv_gn_w, m_rwkv_gn_b, m_fox_q_norm_g, m_fox_k_norm_g, m_fox_f_bias, m_w_branch_a, m_w_branch_b, m_w_o, m_norm2_g, m_w_gate_up, m_w_down, v_meta_tokens, v_norm1_g, v_w_in, v_rwkv_mu, v_rwkv_w0, v_rwkv_w2, v_rwkv_a0, v_rwkv_a2, v_rwkv_g2, v_rwkv_k_k, v_rwkv_k_a, v_rwkv_r_k, v_rwkv_gn_w, v_rwkv_gn_b, v_fox_q_norm_g, v_fox_k_norm_g, v_fox_f_bias, v_w_branch_a, v_w_branch_b, v_w_o, v_norm2_g, v_w_gate_up, v_w_down):
    args = dict(locals())
    shapes = {n: args[n].shape for n in WEIGHTS}

    def drop_depth(t, n):
        if n == "meta_tokens":
            return t
        if n == "rwkv_r_k":
            return t.reshape(1, -1)
        return t.reshape(t.shape[1:]) if t.ndim == 3 else t

    wts = {n: drop_depth(args[n], n) for n in WEIGHTS}
    mom1 = {n: drop_depth(args["m_" + n], n) for n in WEIGHTS}
    mom2 = {n: drop_depth(args["v_" + n], n) for n in WEIGHTS}
    loss, grad_x, grads, delta, new_m, new_v = _step(x[0], loss_target[0], wts, mom1, mom2)
    outs = [loss, grad_x[None]]
    for d in (grads, delta, new_m, new_v):
        outs += [d[n].reshape(shapes[n]) for n in WEIGHTS]
    return tuple(outs)
```

```python
import functools

import jax
import jax.numpy as jnp
import numpy as np
from jax import lax
from jax.experimental import pallas as pl
from jax.experimental.pallas import tpu as pltpu

F32 = jnp.float32
BF16 = jnp.bfloat16
MESH = pl.DeviceIdType.MESH
ANY = pl.BlockSpec(memory_space=pl.ANY)

N_META = 16
HEAD = 64
ROW_TILE = 128
PAD_ROWS = ROW_TILE - N_META
SEQ_ROW0 = ROW_TILE
CHUNK = 64
LANES = 128
PACK_W = 1024
RMS_EPS = 1e-6
GN_EPS = 64e-5
ATTN_SCALE = HEAD ** -0.5
NEG = -1e30
VMEM_LIMIT_V7X = 56 * 1024 * 1024

ADAM_LR = 0.001
ADAM_B1 = 0.9
ADAM_B2 = 0.999
ADAM_EPS = 1e-08
ADAM_WD = 0.01
ADAM_STEP = 10


def _params(sem=None):
    return pltpu.CompilerParams(dimension_semantics=sem, vmem_limit_bytes=VMEM_LIMIT_V7X)


def _pick(n, cands):
    for c in cands:
        if n % c == 0:
            return c
    return n


def _bf(t):
    return t.astype(BF16)


def _dotb(a, b):
    return jnp.dot(_bf(a), _bf(b), preferred_element_type=F32)


def _split3(x):
    x1 = x.astype(BF16)
    r1 = x - x1.astype(F32)
    x2 = r1.astype(BF16)
    return x1, x2, (r1 - x2.astype(F32)).astype(BF16)


def _mme_raw(x, e, ce):
    eb = e.astype(BF16)
    dot = lambda q: lax.dot_general(q, eb, (((1,), (ce,)), ((), ())), preferred_element_type=F32)
    x1, x2, x3 = _split3(x)
    return dot(x1) + (dot(x2) + dot(x3))


@jax.custom_vjp
def _doth(x, e):
    return _mme_raw(x, e, 0)


def _doth_fwd(x, e):
    return _mme_raw(x, e, 0), e


def _doth_bwd(e, ct):
    return _mme_raw(ct, e, 1), jnp.zeros_like(e)


_doth.defvjp(_doth_fwd, _doth_bwd)


_BIG = (2048, 1536, 1408, 1024, 768, 704, 512, 384, 256, 128)


def _matmul(a, b, mode, out_dtype, name, add=None, col_blocks=1):
    if mode == "nn":
        (M, R), (_, N) = a.shape, b.shape
        dims = (((1,), (0,)), ((), ()))
    elif mode == "nt":
        (M, R), (N, _) = a.shape, b.shape
        dims = (((1,), (1,)), ((), ()))
    else:
        (R, M), (_, N) = a.shape, b.shape
        dims = (((0,), (0,)), ((), ()))
    tm = _pick(M, (1408, 1024, 768, 512, 384, 256, 128))
    nb = N // col_blocks
    tn = _pick(nb, (1408, 1024, 896, 768, 704, 512, 384, 256, 128)) if mode == "tn" else _pick(nb, (512, 384, 256, 128))
    per = nb // tn
    tr = _pick(R, (1408, 1056, 768, 512, 384, 256, 128)) if mode == "tn" else _pick(R, _BIG)
    nr = R // tr

    if mode == "nn":
        a_spec = pl.BlockSpec((tm, tr), lambda i, j, r: (i, r))
        b_spec = pl.BlockSpec((tr, tn), lambda i, j, r: (r, j))
    elif mode == "nt":
        a_spec = pl.BlockSpec((tm, tr), lambda i, j, r: (i, r))
        b_spec = pl.BlockSpec((tn, tr), lambda i, j, r: (j, r))
    else:
        a_spec = pl.BlockSpec((tr, tm), lambda i, j, r: (r, i))
        b_spec = pl.BlockSpec((tr, tn), lambda i, j, r: (r, j))
    if col_blocks == 1:
        o_spec = pl.BlockSpec((tm, tn), lambda i, j, r: (i, j))
        o_shape = (M, N)
    else:
        o_spec = pl.BlockSpec((1, tm, tn), lambda i, j, r: (j // per, i, j % per))
        o_shape = (col_blocks, M, nb)
    has_add = add is not None

    def body(*refs):
        if has_add:
            a_ref, b_ref, add_ref, o_ref, acc = refs
        else:
            a_ref, b_ref, o_ref, acc = refs
        r = pl.program_id(2)

        @pl.when(r == 0)
        def _():
            acc[...] = jnp.zeros_like(acc)

        acc[...] += lax.dot_general(_bf(a_ref[...]), _bf(b_ref[...]), dims, preferred_element_type=F32)

        @pl.when(r == nr - 1)
        def _():
            res = acc[...]
            if has_add:
                res = res + add_ref[...]
            o_ref[...] = res.astype(o_ref.dtype).reshape(o_ref.shape)

    ins = [a, b] + ([add] if has_add else [])
    specs = [a_spec, b_spec] + ([o_spec] if has_add else [])
    return pl.pallas_call(
        body, name=name, out_shape=jax.ShapeDtypeStruct(o_shape, out_dtype),
        grid=(M // tm, N // tn, nr), in_specs=specs, out_specs=o_spec,
        scratch_shapes=[pltpu.VMEM((tm, tn), F32)],
        compiler_params=_params(("parallel", "parallel", "arbitrary")),
    )(*ins)


def _rowcall(name, fn, L, row_ins, bc_ins, row_outs, acc_outs, tm=ROW_TILE):
    nt = L // tm
    specs = []
    for arr, w, cb, kind in row_ins:
        if kind == "row":
            specs.append(pl.BlockSpec((tm, w), lambda i, cb=cb: (i, cb)))
        elif kind == "lag":
            specs.append(pl.BlockSpec((tm, w), lambda i, cb=cb: (jnp.maximum(i - 1, 0), cb)))
        elif kind == "prev":
            specs.append(pl.BlockSpec((8, w), lambda i, cb=cb: (jnp.maximum(i * (tm // 8) - 1, 0), cb)))
        else:
            specs.append(pl.BlockSpec((8, w), lambda i, cb=cb: (jnp.minimum((i + 1) * (tm // 8), L // 8 - 1), cb)))
    for arr in bc_ins:
        specs.append(pl.BlockSpec(arr.shape, lambda i, nd=arr.ndim: (0,) * nd))
    out_shapes = [jax.ShapeDtypeStruct((L, w), dt) for w, dt in row_outs]
    out_specs = [pl.BlockSpec((tm, w), lambda i: (i, 0)) for w, dt in row_outs]
    out_shapes += [jax.ShapeDtypeStruct(s, F32) for s in acc_outs]
    out_specs += [pl.BlockSpec(s, lambda i, nd=len(s): (0,) * nd) for s in acc_outs]
    n_row, n_bc, n_ro = len(row_ins), len(bc_ins), len(row_outs)

    def body(*refs):
        i = pl.program_id(0)
        vals = [r[...] for r in refs[: n_row + n_bc]]
        outs, sums = fn(i, vals[:n_row], vals[n_row:])
        o_refs = refs[n_row + n_bc:]
        for r, v in zip(o_refs[:n_ro], outs):
            r[...] = v.astype(r.dtype)

        @pl.when(i == 0)
        def _():
            for r in o_refs[n_ro:]:
                r[...] = jnp.zeros_like(r)

        for r, v in zip(o_refs[n_ro:], sums):
            r[...] += v

    res = pl.pallas_call(
        body, name=name, out_shape=out_shapes, grid=(nt,), in_specs=specs, out_specs=out_specs,
        compiler_params=_params(("arbitrary",)),
    )(*[a for a, _, _, _ in row_ins], *bc_ins)
    return list(res[:n_ro]), list(res[n_ro:])


def _rowsum(t):
    return jnp.sum(t, axis=0, keepdims=True)


def _head_mats(width):
    e = (np.arange(width)[:, None] // HEAD == np.arange(LANES)[None, :]).astype(np.float32)
    return jnp.asarray(e), jnp.asarray(e.T)


def _fold_mat(width):
    ft = (np.arange(HEAD)[:, None] == np.arange(width)[None, :] % HEAD).astype(np.float32)
    return jnp.asarray(ft)


def _rms_f(h, g):
    return (h * lax.rsqrt(jnp.mean(h * h, axis=-1, keepdims=True) + RMS_EPS)) * g


def _prep_f(xr, xk, xv, xl, w0, w2p, a0, a2p, g2p, k_k, k_a, e, et):
    w_log = -jax.nn.softplus(-(w0 + _dotb(jnp.tanh(xl), w2p))) - 0.5
    lw = -jnp.exp(w_log)
    a = jax.nn.sigmoid(a0 + _dotb(xl, a2p))
    g = _dotb(jax.nn.sigmoid(xl), g2p)
    kkr = xk * k_k
    inv = lax.rsqrt(jnp.maximum(_doth(kkr * kkr, e), 1e-24))
    kk = kkr * _doth(inv, et)
    kf = xk * (1.0 + (a - 1.0) * k_a)
    return xr, lw, kf, xv, -kk, kk * a, g


def _post_f(y, r, kf, v, g, gn_w, gn_b, r_k, e, et):
    mu = _doth(y, e) * (1.0 / HEAD)
    yc = y - _doth(mu, et)
    var = _doth(yc * yc, e) * (1.0 / HEAD)
    yn = yc * _doth(lax.rsqrt(var + GN_EPS), et) * gn_w + gn_b
    bonus = _doth(r * kf * r_k, e)
    return (yn + _doth(bonus, et) * v) * g


def _foxprep_f(q, k, fl, qg8, kg8, fb, e, et, ft, fmask):
    def norm(t, g8):
        ms = _doth(t * t, e) * (1.0 / HEAD)
        return t * _doth(lax.rsqrt(ms + RMS_EPS), et) * _doth(g8, ft)[0:1]
    logf = jax.nn.log_sigmoid(fl + fb) * fmask
    return norm(q, qg8), norm(k, kg8), logf


def _merge_f(za, zb, pa, pb):
    return jax.nn.sigmoid(za) * pa + jax.nn.sigmoid(zb) * pb


def _swiglu_f(gate, up):
    return jax.nn.silu(gate) * up


def _tri(n, strict):
    row = lax.broadcasted_iota(jnp.int32, (n, n), 0)
    col = lax.broadcasted_iota(jnp.int32, (n, n), 1)
    return (row > col) if strict else (row >= col)


def _split2(x):
    hi = x.astype(BF16)
    return hi, (x - hi.astype(F32)).astype(BF16)


def _mm3_raw(a, b, ca, cb):
    dn = (((ca,), (cb,)), ((), ()))
    ah, al = _split2(a)
    bh, bl = _split2(b)
    dot = lambda p, q: lax.dot_general(p, q, dn, preferred_element_type=F32)
    return dot(ah, bh) + (dot(al, bh) + dot(ah, bl))


@functools.partial(jax.custom_vjp, nondiff_argnums=(2, 3))
def _mm3(a, b, ca, cb):
    return _mm3_raw(a, b, ca, cb)


def _mm3_fwd(a, b, ca, cb):
    return _mm3_raw(a, b, ca, cb), (a, b)


def _mm3_bwd(ca, cb, res, ct):
    a, b = res
    da = _mm3_raw(ct, b, 1, 1 - cb) if ca == 1 else _mm3_raw(b, ct, 1 - cb, 1)
    db = _mm3_raw(a, ct, 1 - ca, 0) if cb == 0 else _mm3_raw(ct, a, 0, 1 - ca)
    return da, db


_mm3.defvjp(_mm3_fwd, _mm3_bwd)


def _mmx_raw(t, x, ct):
    x1 = x.astype(BF16)
    r1 = x - x1.astype(F32)
    x2 = r1.astype(BF16)
    x3 = (r1 - x2.astype(F32)).astype(BF16)
    tb = t.astype(BF16)
    dot = lambda q: lax.dot_general(tb, q, (((ct,), (0,)), ((), ())), preferred_element_type=F32)
    return dot(x1) + (dot(x2) + dot(x3))


@jax.custom_vjp
def _mmx(t, x):
    return _mmx_raw(t, x, 1)


def _mmx_fwd(t, x):
    return _mmx_raw(t, x, 1), t


def _mmx_bwd(t, ct):
    return jnp.zeros_like(t), _mmx_raw(t, ct, 0)


_mmx.defvjp(_mmx_fwd, _mmx_bwd)

SCAN_HEADS = 16


def _scan_step(r, lw, k, v, a, b, st):
    c = r.shape[0]
    nh = r.shape[1] // HEAD
    incl = _tri(c, False)
    row2 = lax.broadcasted_iota(jnp.int32, (2 * c, 2 * c), 0)
    col2 = lax.broadcasted_iota(jnp.int32, (2 * c, 2 * c), 1)
    t_row = jnp.where(row2 >= c, row2 - c, row2)
    t_col = jnp.where(col2 >= c, col2 - c, col2)
    mask2 = (t_row > t_col) | ((row2 >= c) & (t_row == t_col))
    right =lax.broadcasted_iota(jnp.int32, (c, 2 * c), 1) >= c
    eye = lax.broadcasted_iota(jnp.int32, (HEAD, HEAD), 0) == lax.broadcasted_iota(jnp.int32, (HEAD, HEAD), 1)
    cl = _mmx(incl.astype(F32), lw)
    last = cl[c - 1:c, :]
    rt = r * jnp.exp(cl)
    at = a * jnp.exp(cl - lw)
    pinv = jnp.exp(-cl)
    bt = b * pinv
    kt = k * pinv
    pend = jnp.exp(last - cl)
    bl = b * pend
    kl = k * pend
    pe_last = jnp.exp(last)
    hs = range(nh)
    sl = [slice(h * HEAD, (h + 1) * HEAD) for h in hs]
    ar = [jnp.concatenate([at[:, sl[h]], rt[:, sl[h]]], axis=0) for h in hs]
    bk = [jnp.concatenate([bt[:, sl[h]], kt[:, sl[h]]], axis=0) for h in hs]
    amat = [jnp.where(mask2, _mm3(ar[h], bk[h], 1, 1), 0.0) for h in hs]
    res = [_mm3(jnp.concatenate([ar[h], amat[h][:, c:]], axis=1),
                jnp.concatenate([st[h], v[:, sl[h]]], axis=0), 1, 0) for h in hs]
    z = [jnp.concatenate([amat[h][:c, :c], res[h][:c]], axis=1) for h in hs]
    for _ in range(max(1, int(np.ceil(np.log2(c))))):
        z = [_mm3(z[h][:, :c], z[h], 1, 0) + jnp.where(right, z[h], 0.0) for h in hs]
    u = [z[h][:, c:] for h in hs]
    ys = [res[h][c:] + _mm3(amat[h][c:, :c], u[h], 1, 0) for h in hs]
    s1s = [_mm3(jnp.concatenate([bl[:, sl[h]], kl[:, sl[h]], jnp.where(eye, pe_last[:, sl[h]], 0.0)], axis=0),
                jnp.concatenate([u[h], v[:, sl[h]], st[h]], axis=0), 0, 0) for h in hs]
    return tuple(ys), tuple(s1s)


def _scan_heads(W):
    return SCAN_HEADS if W % (SCAN_HEADS * HEAD) == 0 else 2


def _scan_fwd(r, lw, k, v, a, b):
    L, W = r.shape
    nh = _scan_heads(W)
    nc, ng = L // CHUNK, W // (nh * HEAD)
    spec = pl.BlockSpec((CHUNK, nh * HEAD), lambda p, c: (c, p))

    def body(r_ref, lw_ref, k_ref, v_ref, a_ref, b_ref, y_ref, s_ref, st):
        @pl.when(pl.program_id(1) == 0)
        def _():
            st[...] = jnp.zeros_like(st)

        s0 = st[...]
        ys, s1s = _scan_step(r_ref[...], lw_ref[...], k_ref[...], v_ref[...], a_ref[...], b_ref[...], s0)
        s_ref[0] = s0
        st[...] = jnp.stack(s1s)
        y_ref[...] = jnp.concatenate(ys, axis=1)

    return pl.pallas_call(
        body, name="wkv7_fwd",
        out_shape=[jax.ShapeDtypeStruct((L, W), F32), jax.ShapeDtypeStruct((nc, nh * ng, HEAD, HEAD), F32)],
        grid=(ng, nc), in_specs=[spec] * 6,
        out_specs=[spec, pl.BlockSpec((1, nh, HEAD, HEAD), lambda p, c: (c, p, 0, 0))],
        scratch_shapes=[pltpu.VMEM((nh, HEAD, HEAD), F32)],
        compiler_params=_params(("parallel", "arbitrary")),
    )(r, lw, k, v, a, b)


def _scan_bwd(r, lw, k, v, a, b, s_all, dy):
    L, W = r.shape
    nh = _scan_heads(W)
    nc, ng = L // CHUNK, W // (nh * HEAD)
    spec = pl.BlockSpec((CHUNK, nh * HEAD), lambda p, c: (nc - 1 - c, p))

    def body(r_ref, lw_ref, k_ref, v_ref, a_ref, b_ref, s_ref, dy_ref,
             dr_ref, dlw_ref, dk_ref, dv_ref, da_ref, db_ref, dst):
        @pl.when(pl.program_id(1) == 0)
        def _():
            dst[...] = jnp.zeros_like(dst)

        _, vjp = jax.vjp(_scan_step, r_ref[...], lw_ref[...], k_ref[...], v_ref[...], a_ref[...], b_ref[...],
                         s_ref[0])
        dys = tuple(dy_ref[:, h * HEAD:(h + 1) * HEAD] for h in range(nh))
        g = vjp((dys, tuple(dst[h] for h in range(nh))))
        for ref, val in zip((dr_ref, dlw_ref, dk_ref, dv_ref, da_ref, db_ref), g[:6]):
            ref[...] = val
        dst[...] = g[6]

    return pl.pallas_call(
        body, name="wkv7_bwd", out_shape=[jax.ShapeDtypeStruct((L, W), F32)] * 6,
        grid=(ng, nc),
        in_specs=[spec] * 6 + [pl.BlockSpec((1, nh, HEAD, HEAD), lambda p, c: (nc - 1 - c, p, 0, 0)), spec],
        out_specs=[spec] * 6,
        scratch_shapes=[pltpu.VMEM((nh, HEAD, HEAD), F32)],
        compiler_params=_params(("parallel", "arbitrary")),
    )(r, lw, k, v, a, b, s_all, dy)


def _cumsum_rows(logf):
    L = logf.shape[0]
    t = LANES

    def body(x_ref, o_ref, carry):
        @pl.when(pl.program_id(0) == 0)
        def _():
            carry[...] = jnp.zeros_like(carry)

        c = _mmx(_tri(t, False).astype(F32), x_ref[...]) + carry[...]
        carry[...] = c[t - 1:t, :]
        o_ref[...] = c.T

    return pl.pallas_call(
        body, name="fox_cumsum", out_shape=jax.ShapeDtypeStruct((LANES, L), F32), grid=(L // t,),
        in_specs=[pl.BlockSpec((t, LANES), lambda i: (i, 0))],
        out_specs=pl.BlockSpec((LANES, t), lambda i: (0, i)),
        scratch_shapes=[pltpu.VMEM((1, LANES), F32)], compiler_params=_params(("arbitrary",)),
    )(logf)


def _rcumsum_cols(dct):
    L = dct.shape[1]
    t = LANES
    n = L // t

    def body(x_ref, o_ref, carry):
        @pl.when(pl.program_id(0) == 0)
        def _():
            carry[...] = jnp.zeros_like(carry)

        rc = _doth(x_ref[...], _tri(t, False).astype(F32)) + carry[...]
        carry[...] = rc[:, 0:1]
        o_ref[...] = rc.T

    return pl.pallas_call(
        body, name="fox_rcumsum", out_shape=jax.ShapeDtypeStruct((L, LANES), F32), grid=(n,),
        in_specs=[pl.BlockSpec((LANES, t), lambda i: (0, n - 1 - i))],
        out_specs=pl.BlockSpec((t, LANES), lambda i: (n - 1 - i, 0)),
        scratch_shapes=[pltpu.VMEM((LANES, 1), F32)], compiler_params=_params(("arbitrary",)),
    )(dct)


def _attn_tile(L):
    return _pick(L, (384, 256, 128))


def _head_lanes(hh, shape):
    return lax.broadcasted_iota(jnp.int32, shape, len(shape) - 1) // HEAD == hh


def _own(hh, block, other=0):
    return jnp.where(_head_lanes(hh, block.shape), block, jnp.asarray(other, block.dtype))


def _attn_scores(q, k, ck, qi, kj, t):
    s = _dot_bnt(q, k) * ATTN_SCALE - ck
    qpos = qi * t + lax.broadcasted_iota(jnp.int32, (t, t), 0)
    kpos = kj * t + lax.broadcasted_iota(jnp.int32, (t, t), 1)
    mask = (kpos <= qpos) & (kpos >= PAD_ROWS)
    return jnp.where(mask, s, NEG), mask


def _dot_bnt(a, b):
    return lax.dot_general(_bf(a), _bf(b), (((1,), (1,)), ((), ())), preferred_element_type=F32)


def _dot_btn(a, b):
    return lax.dot_general(_bf(a), _bf(b), (((0,), (0,)), ((), ())), preferred_element_type=F32)


def _ck_rows(ct_ref, p):
    r0 = 2 * (p % 4)
    return ct_ref[pl.ds(r0, 1), :], ct_ref[pl.ds(r0 + 1, 1), :]


def _attn_fwd(q, k, v, ct):
    L, W = q.shape
    t = _attn_tile(L)
    nt, npair = L // t, W // LANES
    qspec = pl.BlockSpec((t, LANES), lambda p, i, j: (i, p))
    kspec = pl.BlockSpec((t, LANES), lambda p, i, j: (jnp.minimum(i, j), p))
    cspec = pl.BlockSpec((8, t), lambda p, i, j: (p // 4, jnp.minimum(i, j)))

    def body(q_ref, k_ref, v_ref, ct_ref, o_ref, lse_ref, m_s, acc):
        p, i, j = pl.program_id(0), pl.program_id(1), pl.program_id(2)

        @pl.when(j == 0)
        def _():
            m_s[...] = jnp.full_like(m_s, NEG)
            acc[...] = jnp.zeros_like(acc)

        @pl.when(j <= i)
        def _():
            cks = _ck_rows(ct_ref, p)
            qb, kb, vf = _bf(q_ref[...]), _bf(k_ref[...]), v_ref[...]
            ss = [_attn_scores(_own(hh, qb), kb, cks[hh], i, j, t)[0] for hh in range(2)]
            prs, alphas = [], []
            for hh in range(2):
                m_old = m_s[hh]
                m_new = jnp.maximum(m_old, jnp.max(ss[hh], axis=-1, keepdims=True))
                alphas.append(jnp.exp(m_old - m_new))
                prs.append(jnp.exp(ss[hh] - m_new))
                m_s[hh] = m_new
            pvs = []
            for hh in range(2):
                p_hi, p_lo = _split2(prs[hh])
                vx = _bf(_own(hh, vf, 1.0))
                pvs.append(jnp.dot(p_hi, vx, preferred_element_type=F32)
                           + jnp.dot(p_lo, vx, preferred_element_type=F32))
            for hh in range(2):
                acc[hh] = alphas[hh] * acc[hh] + pvs[hh]

        @pl.when(j == i)
        def _():
            lane = lax.broadcasted_iota(jnp.int32, (t, LANES), 1)
            lse = jnp.zeros((t, LANES), F32)
            out = jnp.zeros((t, LANES), F32)
            for hh in range(2):
                a = acc[hh]
                row_sum = pltpu.roll(a, HEAD, 1)
                out = jnp.where(_head_lanes(hh, a.shape), a / row_sum, out)
                l_col = a[:, HEAD:HEAD + 1] if hh == 0 else a[:, 0:1]
                lse = jnp.where(lane == hh, m_s[hh] + jnp.log(l_col), lse)
            o_ref[...] = out
            lse_ref[0] = lse

    return pl.pallas_call(
        body, name="fox_attn_fwd",
        out_shape=[jax.ShapeDtypeStruct((L, W), F32), jax.ShapeDtypeStruct((npair, L, LANES), F32)],
        grid=(npair, nt, nt), in_specs=[qspec, kspec, kspec, cspec],
        out_specs=[qspec, pl.BlockSpec((1, t, LANES), lambda p, i, j: (p, i, 0))],
        scratch_shapes=[pltpu.VMEM((2, t, 1), F32), pltpu.VMEM((2, t, LANES), F32)],
        compiler_params=_params(("parallel", "parallel", "arbitrary")),
    )(q, k, v, ct)


def _attn_bwd(q, k, v, ct, o, lse, do):
    L, W = q.shape
    t = _attn_tile(L)
    nt, npair = L // t, W // LANES
    kspec = pl.BlockSpec((t, LANES), lambda p, j, i: (j, p))
    qspec = pl.BlockSpec((t, LANES), lambda p, j, i: (jnp.maximum(i, j), p))
    cspec = pl.BlockSpec((8, t), lambda p, j, i: (p // 4, j))
    lspec = pl.BlockSpec((1, t, LANES), lambda p, j, i: (p, jnp.maximum(i, j), 0))

    def body(q_ref, k_ref, v_ref, ct_ref, o_ref, lse_ref, do_ref, dk_ref, dv_ref, dc_ref, dq_ref,
             dk_s, dv_s, dc_s):
        p, j, i = pl.program_id(0), pl.program_id(1), pl.program_id(2)

        @pl.when(i == 0)
        def _():
            dk_s[...] = jnp.zeros_like(dk_s)
            dv_s[...] = jnp.zeros_like(dv_s)
            dc_s[...] = jnp.zeros_like(dc_s)

        def tile_dq():
            cks = _ck_rows(ct_ref, p)
            qb, kb, vb, dob = _bf(q_ref[...]), _bf(k_ref[...]), _bf(v_ref[...]), _bf(do_ref[...])
            of = o_ref[...]
            hs = range(2)
            qm = [_own(hh, qb) for hh in hs]
            dom = [_own(hh, dob) for hh in hs]
            sm = [_attn_scores(qm[hh], kb, cks[hh], i, j, t) for hh in hs]
            dps = [_dot_bnt(dom[hh], vb) for hh in hs]
            prs = [jnp.where(sm[hh][1], jnp.exp(sm[hh][0] - lse_ref[0, :, hh:hh + 1]), 0.0) for hh in hs]
            dss = [prs[hh] * (dps[hh] - jnp.sum(dom[hh].astype(F32) * of, axis=-1, keepdims=True)) for hh in hs]
            dv_s[...] += _dot_btn(prs[0], dom[0]) + _dot_btn(prs[1], dom[1])
            dk_s[...] += _dot_btn(dss[0], qm[0]) + _dot_btn(dss[1], qm[1])
            for hh in hs:
                dc_s[hh] += -jnp.sum(dss[hh], axis=0, keepdims=True)
            return (_dotb(dss[0], _own(0, kb)) + _dotb(dss[1], _own(1, kb))) * ATTN_SCALE

        rows = pl.ds(pl.multiple_of(i * t, t), t)

        @pl.when((i >= j) & (j == 0))
        def _():
            dq_ref[rows, :] = tile_dq()

        @pl.when((i >= j) & (j > 0))
        def _():
            dq_ref[rows, :] += tile_dq()

        @pl.when(i == nt - 1)
        def _():
            row = lax.broadcasted_iota(jnp.int32, (8, t), 0)
            dc = jnp.zeros((8, t), F32)
            for hh in range(2):
                dc = jnp.where(row == hh, dc_s[hh], dc)
            dk_ref[...] = dk_s[...] * ATTN_SCALE
            dv_ref[...] = dv_s[...]
            dc_ref[0] = dc

    return pl.pallas_call(
        body, name="fox_attn_bwd",
        out_shape=[jax.ShapeDtypeStruct((L, W), F32), jax.ShapeDtypeStruct((L, W), F32),
                   jax.ShapeDtypeStruct((npair, 8, L), F32), jax.ShapeDtypeStruct((L, W), F32)],
        grid=(npair, nt, nt), in_specs=[qspec, kspec, kspec, cspec, qspec, lspec, qspec],
        out_specs=[kspec, kspec, pl.BlockSpec((1, 8, t), lambda p, j, i: (p, 0, j)),
                   pl.BlockSpec((L, LANES), lambda p, j, i: (0, p))],
        scratch_shapes=[pltpu.VMEM((t, LANES), F32), pltpu.VMEM((t, LANES), F32), pltpu.VMEM((2, 1, t), F32)],
        compiler_params=_params(("parallel", "arbitrary", "arbitrary")),
    )(q, k, v, ct, o, lse, do)


def _place():
    x, y, c = lax.axis_index("x"), lax.axis_index("y"), lax.axis_index("c")
    chips = [(1 - x, y), (x, 1 - y), (1 - x, 1 - y)]
    return x, y, c, chips


def _remote(src, dst, send_sems, recv_sems, k, to):
    return pltpu.make_async_remote_copy(src_ref=src, dst_ref=dst, send_sem=send_sems.at[k],
                                        recv_sem=recv_sems.at[k], device_id=to, device_id_type=MESH)


def _dma_sems(n):
    return [pltpu.SemaphoreType.DMA((n,)), pltpu.SemaphoreType.DMA((n,))]


def _all_gather(shards, modes):
    n = len(shards)

    def out_shape(s, mode):
        r, c = s.shape
        return {"stack": (4, r, c), "cols": (r, 4 * c), "rows": (4 * r, c)}[mode]

    def body(*refs):
        ins, outs, (send_sems, recv_sems) = refs[:n], refs[n:2 * n], refs[2 * n:]
        x, y, c, chips = _place()
        me_chip = 2 * x + y
        sibling = (x, y, 1 - c)

        def window(i, chip, cc):
            r, cw = shards[i].shape
            h = r // 2
            if modes[i] == "stack":
                return outs[i].at[chip, pl.ds(cc * h, h), :]
            if modes[i] == "cols":
                return outs[i].at[pl.ds(cc * h, h), pl.ds(pl.multiple_of(chip * cw, LANES), cw)]
            return outs[i].at[pl.ds(pl.multiple_of(chip * r + cc * h, 8), h), :]

        first = []
        for i in range(n):
            h = shards[i].shape[0] // 2
            for k, (cx, cy) in enumerate(chips):
                first.append(_remote(ins[i].at[pl.ds(c * h, h), :], window(i, me_chip, c), send_sems, recv_sems,
                                     6 * i + k, (cx, cy, c)))
        for cp in first:
            cp.start()
        passed = []
        for k, (cx, cy) in enumerate(chips):
            for i in range(n):
                landed = window(i, 2 * cx + cy, c)
                _remote(landed, landed, send_sems, recv_sems, 6 * i + k, sibling).wait_recv()
                fwd = _remote(landed, landed, send_sems, recv_sems, 6 * i + 3 + k, sibling)
                fwd.start()
                passed.append(fwd)
        for k, (cx, cy) in enumerate(chips):
            for i in range(n):
                other = window(i, 2 * cx + cy, 1 - c)
                _remote(other, other, send_sems, recv_sems, 6 * i + 3 + k, sibling).wait_recv()
        for cp in first + passed:
            cp.wait_send()

    return pl.pallas_call(
        body, name="gather_weights",
        out_shape=[jax.ShapeDtypeStruct(out_shape(s, m), s.dtype) for s, m in zip(shards, modes)],
        in_specs=[ANY] * n, out_specs=[ANY] * n, scratch_shapes=_dma_sems(6 * n),
    )(*shards)


def _pair_exchange(blocks):
    n = len(blocks)

    def body(*refs):
        ins, outs, (send_sems, recv_sems) = refs[:n], refs[n:2 * n], refs[2 * n:]
        x, y, c, _ = _place()
        cps = []
        for i in range(n):
            h = blocks[i].shape[1] // 2
            cps.append(_remote(ins[i].at[:, pl.ds((1 - c) * h, h), :], outs[i], send_sems, recv_sems, i,
                               (x, y, 1 - c)))
        for cp in cps:
            cp.start()
        for cp in cps:
            cp.wait()

    return pl.pallas_call(
        body, name="reduce_pair_exchange",
        out_shape=[jax.ShapeDtypeStruct((4, b.shape[1] // 2, b.shape[2]), b.dtype) for b in blocks],
        in_specs=[ANY] * n, out_specs=[ANY] * n, scratch_shapes=_dma_sems(n),
    )(*blocks)


def _chip_exchange(parts):
    n = len(parts)

    def body(*refs):
        ins, outs, (send_sems, recv_sems) = refs[:n], refs[n:2 * n], refs[2 * n:]
        x, y, c, chips = _place()
        me_chip = 2 * x + y
        sends = [_remote(ins[i].at[2 * cx + cy], outs[i].at[me_chip], send_sems, recv_sems, 3 * i + k, (cx, cy, c))
                 for i in range(n) for k, (cx, cy) in enumerate(chips)]
        for cp in sends:
            cp.start()
        for i in range(n):
            for k, (cx, cy) in enumerate(chips):
                slot = outs[i].at[2 * cx + cy]
                _remote(slot, slot, send_sems, recv_sems, 3 * i + k, (cx, cy, c)).wait_recv()
        for cp in sends:
            cp.wait_send()

    return pl.pallas_call(
        body, name="reduce_chip_exchange", out_shape=[jax.ShapeDtypeStruct(p.shape, p.dtype) for p in parts],
        in_specs=[ANY] * n, out_specs=[ANY] * n, scratch_shapes=_dma_sems(3 * n),
    )(*parts)


def _pair_join(tots):
    n = len(tots)

    def body(*refs):
        ins, outs, (send_sems, recv_sems) = refs[:n], refs[n:2 * n], refs[2 * n:]
        x, y, c, _ = _place()
        cps = [_remote(ins[i], outs[i], send_sems, recv_sems, i, (x, y, 1 - c)) for i in range(n)]
        for cp in cps:
            cp.start()
        for cp in cps:
            cp.wait()

    return pl.pallas_call(
        body, name="reduce_pair_join", out_shape=[jax.ShapeDtypeStruct(t.shape, t.dtype) for t in tots],
        in_specs=[ANY] * n, out_specs=[ANY] * n, scratch_shapes=_dma_sems(n),
    )(*tots)


def _add_tile(h, cw):
    cap = max(8, (512 * 1024) // max(cw, 1))
    return _pick(h, tuple(t for t in (1024, 512, 256, 128, 64, 32, 16, 8) if t <= cap))


def _pair_add(name, block, recv):
    n, r, cw = block.shape
    h = r // 2
    tr = _add_tile(h, cw)
    c = lax.axis_index("c").astype(jnp.int32).reshape((1,))

    def body(c_ref, a_ref, b_ref, o_ref):
        o_ref[...] = (a_ref[...] + b_ref[...]).astype(o_ref.dtype)

    return pl.pallas_call(
        body, name=name, out_shape=jax.ShapeDtypeStruct((n, h, cw), BF16),
        grid_spec=pltpu.PrefetchScalarGridSpec(
            num_scalar_prefetch=1, grid=(n, h // tr),
            in_specs=[pl.BlockSpec((1, tr, cw), lambda a, i, cr: (a, cr[0] * (h // tr) + i, 0)),
                      pl.BlockSpec((1, tr, cw), lambda a, i, cr: (a, i, 0))],
            out_specs=pl.BlockSpec((1, tr, cw), lambda a, i, cr: (a, i, 0))),
        compiler_params=_params(("parallel", "parallel")),
    )(c, block, recv)


def _chip_add(name, parts):
    n, h, cw = parts.shape
    tr = _add_tile(h, cw)

    def body(a_ref, o_ref):
        f = lambda i: a_ref[i].astype(F32)
        o_ref[...] = ((f(0) + f(1)) + f(2)) + f(3)

    return pl.pallas_call(
        body, name=name, out_shape=jax.ShapeDtypeStruct((h, cw), F32), grid=(h // tr,),
        in_specs=[pl.BlockSpec((n, tr, cw), lambda i: (0, i, 0))],
        out_specs=pl.BlockSpec((tr, cw), lambda i: (i, 0)),
        compiler_params=_params(("parallel",)),
    )(parts)


def _adamw(name, w, g, m, v):
    R, C = w.shape
    tr = _pick(R, (128, 64, 32, 16, 8))
    spec = pl.BlockSpec((tr, C), lambda i: (i, 0))

    def body(w_ref, g_ref, m_ref, v_ref, d_ref, mo_ref, vo_ref):
        gr = g_ref[...]
        mn = ADAM_B1 * m_ref[...] + (1.0 - ADAM_B1) * gr
        vn = ADAM_B2 * v_ref[...] + (1.0 - ADAM_B2) * jnp.square(gr)
        m_hat = mn / (1.0 - ADAM_B1 ** ADAM_STEP)
        v_hat = vn / (1.0 - ADAM_B2 ** ADAM_STEP)
        d_ref[...] = -ADAM_LR * (m_hat / (jnp.sqrt(v_hat) + ADAM_EPS) + ADAM_WD * w_ref[...])
        mo_ref[...] = mn
        vo_ref[...] = vn

    return pl.pallas_call(
        body, name=name, out_shape=[jax.ShapeDtypeStruct((R, C), F32)] * 3, grid=(R // tr,),
        in_specs=[spec] * 4, out_specs=[spec] * 3, compiler_params=_params(("parallel",)),
    )(w, g, m, v)


def _pack(parts, dtype, row_mult):
    flat = jnp.concatenate([p.reshape(-1).astype(dtype) for p in parts])
    unit = row_mult * PACK_W
    pad = (-flat.shape[0]) % unit
    if pad:
        flat = jnp.concatenate([flat, jnp.zeros((pad,), dtype)])
    return flat.reshape(-1, PACK_W)


def _unpack(flat, shapes):
    out, off = [], 0
    for s in shapes:
        n = int(np.prod(s))
        out.append(flat[off:off + n].reshape(s))
        off += n
    return out


SHARDED = ("w_in", "rwkv_w2", "rwkv_a2", "rwkv_g2", "w_branch_a", "w_branch_b", "w_o", "w_gate_up", "w_down",
           "meta_tokens")
SHARD_AXIS = {"w_in": 1, "rwkv_w2": 1, "rwkv_a2": 1, "rwkv_g2": 1, "w_branch_a": 1, "w_branch_b": 1, "w_o": 0,
              "w_gate_up": 1, "w_down": 0, "meta_tokens": 1}
GATHER_MODE = {"w_in": "stack", "rwkv_w2": "stack", "rwkv_a2": "stack", "rwkv_g2": "stack", "w_branch_a": "cols",
               "w_branch_b": "cols", "w_o": "rows", "w_gate_up": "cols", "w_down": "rows", "meta_tokens": "stack"}
SMALL = ("norm1_g", "rwkv_mu", "rwkv_w0", "rwkv_a0", "rwkv_k_k", "rwkv_k_a", "rwkv_r_k", "rwkv_gn_w",
         "rwkv_gn_b", "fox_q_norm_g", "fox_k_norm_g", "fox_f_bias", "norm2_g")
WEIGHTS = ("meta_tokens", "norm1_g", "w_in", "rwkv_mu", "rwkv_w0", "rwkv_w2", "rwkv_a0", "rwkv_a2", "rwkv_g2",
           "rwkv_k_k", "rwkv_k_a", "rwkv_r_k", "rwkv_gn_w", "rwkv_gn_b", "fox_q_norm_g", "fox_k_norm_g",
           "fox_f_bias", "w_branch_a", "w_branch_b", "w_o", "norm2_g", "w_gate_up", "w_down")


def _pad_rows(t, rows):
    return jnp.concatenate([t, jnp.zeros((rows - t.shape[0],) + t.shape[1:], t.dtype)], axis=0)


def _pad_cols(t, cols):
    return jnp.concatenate([t, jnp.zeros(t.shape[:-1] + (cols - t.shape[-1],), t.dtype)], axis=-1)


def _step(x, tgt, wts, mom1, mom2):
    seq, D = x.shape
    L = SEQ_ROW0 + seq
    RW = wts["rwkv_w0"].shape[-1]
    DL, AL, GL = wts["rwkv_w2"].shape[0], wts["rwkv_a2"].shape[0], wts["rwkv_g2"].shape[0]
    FW = wts["w_branch_b"].shape[0]
    FH = wts["fox_f_bias"].shape[-1]
    DFF = wts["w_down"].shape[0] * 4
    LORA = DL + AL + GL
    LW = -(-(LORA + FH) // 512) * 512
    assert RW == FW and (6 * RW) % D == 0 and (6 * RW + 2 * D) % LW == 0 and LORA % 8 == 0
    xj = lax.axis_index("x")
    yj = lax.axis_index("y")
    chip = 2 * xj + yj

    send = [wts[n] if n == "meta_tokens" else wts[n].astype(BF16) for n in SHARDED]
    modes = [GATHER_MODE[n] for n in SHARDED]
    gathered = _all_gather(send, modes)
    full, stacked = {}, {}
    for n, mode, shard, got in zip(SHARDED, modes, send, gathered):
        r, cw = shard.shape
        if mode == "stack":
            got = lax.dynamic_update_index_in_dim(got, shard, chip, 0)
            stacked[n] = got
            full[n] = jnp.concatenate([got[j] for j in range(4)], axis=1)
        elif mode == "cols":
            full[n] = lax.dynamic_update_slice(got, shard, (0, chip * cw))
        else:
            full[n] = lax.dynamic_update_slice(got, shard, (chip * r, 0))
    meta = full["meta_tokens"]
    w_in = full["w_in"]
    o = 0
    segs = {}
    for nm, wd in (("r", RW), ("k", RW), ("v", RW), ("wd", DL), ("ad", AL), ("gd", GL),
                   ("fq", FW), ("fk", FW), ("fv", FW), ("ff", FH), ("ga", D), ("gb", D)):
        segs[nm] = w_in[:, o:o + wd]
        o += wd
    lora_w = _pad_cols(jnp.concatenate([segs["wd"], segs["ad"], segs["gd"], segs["ff"]], axis=1), LW)
    w1 = jnp.concatenate([segs["r"], segs["k"], segs["v"], segs["fq"], segs["fk"], segs["fv"],
                          segs["ga"], segs["gb"], lora_w], axis=1)
    cb_f = 3
    cb_gate = (6 * RW) // D
    cb_lora = (6 * RW + 2 * D) // LW

    e_m, et_m = _head_mats(RW)
    ft_m = _fold_mat(RW)
    mu = wts["rwkv_mu"]
    mu_rkv = mu[:, :3 * RW]
    mu_l = _pad_cols(mu[:, 3 * RW:], LW)
    w2p = _pad_rows(full["rwkv_w2"].astype(F32), LW)
    a2p = _pad_rows(jnp.concatenate([jnp.zeros((DL, RW), F32), full["rwkv_a2"].astype(F32)], axis=0), LW)
    g2p = _pad_rows(jnp.concatenate([jnp.zeros((DL + AL, RW), F32), full["rwkv_g2"].astype(F32)], axis=0), LW)
    r_k = wts["rwkv_r_k"].reshape(1, RW)
    qg8 = jnp.broadcast_to(wts["fox_q_norm_g"], (8, HEAD))
    kg8 = jnp.broadcast_to(wts["fox_k_norm_g"], (8, HEAD))
    fb = _pad_cols(wts["fox_f_bias"], LANES)
    fmask = (jnp.arange(LANES) < FH).astype(F32).reshape(1, LANES)
    lmask = ((jnp.arange(LW) >= LORA) & (jnp.arange(LW) < LORA + FH)).astype(F32).reshape(1, LW)

    h0 = jnp.concatenate([jnp.zeros((PAD_ROWS, D), F32), meta, x], axis=0)
    n1 = wts["norm1_g"]

    (xn,), _ = _rowcall("rms1_fwd", lambda i, r, b: ([_rms_f(r[0], b[0])], []), L,
                        [(h0, D, 0, "row")], [n1], [(D, BF16)], [])
    proj = _matmul(xn, w1, "nn", F32, "proj_fwd")

    def shift_fn(i, r, b):
        rows = lax.broadcasted_iota(jnp.int32, (ROW_TILE, 1), 0)
        outs = []
        for z, halo, m_ in ((r[0], r[1], b[0]), (r[2], r[3], b[1])):
            first = jnp.where(i == 0, 0.0, halo[7:8, :])
            zp = jnp.where(rows == 0, first, pltpu.roll(z, 1, 0))
            outs.append(z + (zp - z) * m_)
        return outs, []

    rkv_w = 3 * RW
    (x_rkv, x_l), _ = _rowcall(
        "shift_fwd", shift_fn, L,
        [(proj, rkv_w, 0, "row"), (proj, rkv_w, 0, "prev"), (proj, LW, cb_lora, "row"), (proj, LW, cb_lora, "prev")],
        [mu_rkv, mu_l], [(rkv_w, F32), (LW, F32)], [])

    prep_p = [wts["rwkv_w0"], w2p, wts["rwkv_a0"], a2p, g2p, wts["rwkv_k_k"], wts["rwkv_k_a"], e_m, et_m]
    prep_rows = [(x_rkv, RW, 0, "row"), (x_rkv, RW, 1, "row"), (x_rkv, RW, 2, "row"), (x_l, LW, 0, "row")]
    (s_r, s_lw, s_k, s_v, s_a, s_b, gate_g), _ = _rowcall(
        "rwkv_prep_fwd", lambda i, r, b: (list(_prep_f(*r, *b)), []), L, prep_rows, prep_p,
        [(RW, F32)] * 7, [])
    y_scan, s_all = _scan_fwd(s_r, s_lw, s_k, s_v, s_a, s_b)
    post_p = [wts["rwkv_gn_w"], wts["rwkv_gn_b"], r_k, e_m, et_m]
    post_rows = [(y_scan, RW, 0, "row"), (s_r, RW, 0, "row"), (s_k, RW, 0, "row"), (s_v, RW, 0, "row"),
                 (gate_g, RW, 0, "row")]
    (y_a,), _ = _rowcall("rwkv_post_fwd", lambda i, r, b: ([_post_f(*r, *b)], []), L, post_rows, post_p,
                         [(RW, BF16)], [])

    fox_p = [qg8, kg8, fb, e_m, et_m, ft_m, fmask]

    def foxprep_fn(i, r, b):
        fl = _doth(r[2] * b[-1], b[-2])
        return list(_foxprep_f(r[0], r[1], fl, *b[:-2])), []

    sel = (np.arange(LW)[:, None] - LORA == np.arange(LANES)[None, :]).astype(np.float32)
    sel = jnp.asarray(sel)
    fox_rows = [(proj, FW, cb_f, "row"), (proj, FW, cb_f + 1, "row"), (proj, LW, cb_lora, "row")]
    (f_q, f_k, logf), _ = _rowcall("fox_prep_fwd", foxprep_fn, L, fox_rows, fox_p + [sel, lmask],
                                   [(FW, BF16), (FW, BF16), (LANES, F32)], [])
    ct = _cumsum_rows(logf)
    f_v = proj[:, (cb_f + 2) * FW:(cb_f + 3) * FW]
    y_b32, lse = _attn_fwd(f_q, f_k, f_v, ct)
    y_b = y_b32.astype(BF16)

    p_a = _matmul(y_a, full["w_branch_a"], "nn", F32, "branch_a_fwd")
    p_b = _matmul(y_b, full["w_branch_b"], "nn", F32, "branch_b_fwd")
    merge_rows = [(proj, D, cb_gate, "row"), (proj, D, cb_gate + 1, "row"), (p_a, D, 0, "row"), (p_b, D, 0, "row")]
    (merged,), _ = _rowcall("merge_fwd", lambda i, r, b: ([_merge_f(*r)], []), L, merge_rows, [], [(D, BF16)], [])
    h1 = _matmul(merged, full["w_o"], "nn", F32, "wo_fwd", add=h0)
    n2 = wts["norm2_g"]
    (xn2,), _ = _rowcall("rms2_fwd", lambda i, r, b: ([_rms_f(r[0], b[0])], []), L,
                         [(h1, D, 0, "row")], [n2], [(D, BF16)], [])
    gu = _matmul(xn2, full["w_gate_up"], "nn", F32, "gate_up_fwd")
    gu_rows = [(gu, DFF, 0, "row"), (gu, DFF, 1, "row")]
    (act,), _ = _rowcall("swiglu_fwd", lambda i, r, b: ([_swiglu_f(*r)], []), L, gu_rows, [], [(DFF, BF16)], [])
    h2 = _matmul(act, full["w_down"], "nn", F32, "down_fwd", add=h1)

    def loss_fn(i, r, b):
        err = jnp.where(i == 0, 0.0, r[0] - r[1])
        return [err * (1.0 / D)], [jnp.zeros((8, LANES), F32) + 0.5 / D * jnp.sum(err * err)]

    (dh2,), (loss_acc,) = _rowcall("loss", loss_fn, L, [(h2, D, 0, "row"), (tgt, D, 0, "lag")], [],
                                   [(D, F32)], [(8, LANES)])
    loss = lax.psum(loss_acc[0, 0], ("x", "y", "c"))

    dh2b = dh2.astype(BF16)
    g_w_down = _matmul(act, dh2b, "tn", F32, "down_dw")
    d_act = _matmul(dh2b, full["w_down"], "nt", F32, "down_dx")

    def swiglu_bwd(i, r, b):
        _, vjp = jax.vjp(_swiglu_f, r[0], r[1])
        return list(vjp(r[2])), []

    (d_gate, d_up), _ = _rowcall("swiglu_bwd", swiglu_bwd, L, gu_rows + [(d_act, DFF, 0, "row")], [],
                                 [(DFF, BF16), (DFF, BF16)], [])
    d_gu = jnp.concatenate([d_gate, d_up], axis=1)
    g_w_gu = _matmul(xn2, d_gu, "tn", F32, "gate_up_dw", col_blocks=4)
    d_xn2 = _matmul(d_gu, full["w_gate_up"], "nt", F32, "gate_up_dx")

    def rms_bwd(i, r, b):
        _, vjp = jax.vjp(_rms_f, r[0], b[0])
        dh, dg = vjp(r[1])
        return [dh + r[2]], [dg]

    (dh1,), (g_n2,) = _rowcall("rms2_bwd", rms_bwd, L,
                               [(h1, D, 0, "row"), (d_xn2, D, 0, "row"), (dh2, D, 0, "row")], [n2],
                               [(D, F32)], [(1, D)])
    dh1b = dh1.astype(BF16)
    g_w_o = _matmul(merged, dh1b, "tn", F32, "wo_dw")
    d_merged = _matmul(dh1b, full["w_o"], "nt", F32, "wo_dx")

    def merge_bwd(i, r, b):
        _, vjp = jax.vjp(_merge_f, *r[:4])
        return list(vjp(r[4])), []

    (d_za, d_zb, d_pa, d_pb), _ = _rowcall("merge_bwd", merge_bwd, L, merge_rows + [(d_merged, D, 0, "row")], [],
                                           [(D, BF16)] * 4, [])
    g_w_a = _matmul(y_a, d_pa, "tn", F32, "branch_a_dw", col_blocks=4)
    g_w_b = _matmul(y_b, d_pb, "tn", F32, "branch_b_dw", col_blocks=4)
    d_ya = _matmul(d_pa, full["w_branch_a"], "nt", F32, "branch_a_dx")
    d_yb = _matmul(d_pb, full["w_branch_b"], "nt", F32, "branch_b_dx")

    d_fk, d_fv, dc_rows, d_fq = _attn_bwd(f_q, f_k, f_v, ct, y_b32, lse, d_yb)
    dct = _pad_rows(dc_rows[:, :2, :].reshape(-1, L), LANES)
    d_logf = _rcumsum_cols(dct)

    def foxprep_bwd(i, r, b):
        def f(q, k, xl, qg, kg, fbias):
            return _foxprep_f(q, k, _doth(xl * b[-1], b[-2]), qg, kg, fbias, *b[3:7])
        _, vjp = jax.vjp(f, r[0], r[1], r[2], b[0], b[1], b[2])
        dq, dk, dxl, dqg, dkg, dfb = vjp((r[3], r[4], r[5]))
        return [dq, dk, dxl], [dqg, dkg, dfb]

    (d_zfq, d_zfk, d_zl_f), (g_qg8, g_kg8, g_fb) = _rowcall(
        "fox_prep_bwd", foxprep_bwd, L,
        fox_rows + [(d_fq, FW, 0, "row"), (d_fk, FW, 0, "row"), (d_logf, LANES, 0, "row")],
        fox_p + [sel, lmask], [(FW, BF16), (FW, BF16), (LW, F32)], [(8, HEAD), (8, HEAD), (1, LANES)])

    def post_bwd(i, r, b):
        _, vjp = jax.vjp(lambda *a: _post_f(*a, b[3], b[4]), *r[:5], b[0], b[1], b[2])
        g = vjp(r[5])
        return list(g[:5]), list(g[5:])

    (d_y, d_r1, d_k1, d_v1, d_g), (g_gn_w, g_gn_b, g_r_k) = _rowcall(
        "rwkv_post_bwd", post_bwd, L, post_rows + [(d_ya, RW, 0, "row")], post_p,
        [(RW, F32)] * 5, [(1, RW)] * 3)
    d_r2, d_lw, d_k2, d_v2, d_a, d_b = _scan_bwd(s_r, s_lw, s_k, s_v, s_a, s_b, s_all, d_y)

    def prep_bwd(i, r, b):
        _, vjp = jax.vjp(lambda *a: _prep_f(*a, b[7], b[8]), *r[:4], *b[:7])
        cts = (r[4] + r[10], r[5], r[6] + r[11], r[7] + r[12], r[8], r[9], r[13])
        g = vjp(cts)
        return list(g[:4]), list(g[4:])

    bwd_rows = prep_rows + [(d_r2, RW, 0, "row"), (d_lw, RW, 0, "row"), (d_k2, RW, 0, "row"), (d_v2, RW, 0, "row"),
                            (d_a, RW, 0, "row"), (d_b, RW, 0, "row"), (d_r1, RW, 0, "row"), (d_k1, RW, 0, "row"),
                            (d_v1, RW, 0, "row"), (d_g, RW, 0, "row")]
    (d_xr, d_xk, d_xv, d_xl), (g_w0, g_w2p, g_a0, g_a2p, g_g2p, g_kk, g_ka) = _rowcall(
        "rwkv_prep_bwd", prep_bwd, L, bwd_rows, prep_p, [(RW, F32)] * 3 + [(LW, F32)],
        [(1, RW), (LW, RW), (1, RW), (LW, RW), (LW, RW), (1, RW), (1, RW)])

    def shift_bwd(i, r, b):
        last = pl.num_programs(0) - 1
        rows = lax.broadcasted_iota(jnp.int32, (ROW_TILE, 1), 0)
        outs, sums = [], []
        groups = ((r[0], r[1], r[2], r[3], b[0], None), (r[4], r[5], r[6], r[7], b[1], r[8]))
        for d, dnext, z, zhalo, m_, extra in groups:
            nxt = jnp.where(i == last, 0.0, dnext[0:1, :])
            d_up = jnp.where(rows == ROW_TILE - 1, nxt, pltpu.roll(d, ROW_TILE - 1, 0))
            dz = d * (1.0 - m_) + d_up * m_
            if extra is not None:
                dz = dz + extra
            first = jnp.where(i == 0, 0.0, zhalo[7:8, :])
            zp = jnp.where(rows == 0, first, pltpu.roll(z, 1, 0))
            outs.append(dz)
            sums.append(_rowsum(d * (zp - z)))
        return outs, sums

    d_xrkv = jnp.concatenate([d_xr, d_xk, d_xv], axis=1)
    (d_zrkv, d_zl), (g_mu_rkv, g_mu_l) = _rowcall(
        "shift_bwd", shift_bwd, L,
        [(d_xrkv, rkv_w, 0, "row"), (d_xrkv, rkv_w, 0, "next"), (proj, rkv_w, 0, "row"), (proj, rkv_w, 0, "prev"),
         (d_xl, LW, 0, "row"), (d_xl, LW, 0, "next"), (proj, LW, cb_lora, "row"), (proj, LW, cb_lora, "prev"),
         (d_zl_f, LW, 0, "row")],
        [mu_rkv, mu_l], [(rkv_w, BF16), (LW, BF16)], [(1, rkv_w), (1, LW)])

    n_in = stacked["w_in"].shape[2] * 4
    cs = n_in // 4
    cp = -(-cs // LANES) * LANES
    d_ref = jnp.concatenate([d_zrkv, d_zl[:, :LORA], d_zfq, d_zfk, d_fv.astype(BF16), d_zl[:, LORA:LORA + FH],
                             d_za, d_zb], axis=1)
    d_blk = jnp.concatenate([_pad_cols(d_ref[:, j * cs:(j + 1) * cs], cp) for j in range(4)], axis=1)
    w_blk = jnp.concatenate([_pad_cols(stacked["w_in"][j], cp) for j in range(4)], axis=1)
    g_w_in = _matmul(xn, d_blk, "tn", F32, "proj_dw", col_blocks=4)
    d_xn = _matmul(d_blk, w_blk, "nt", F32, "proj_dx")
    (dh0,), (g_n1,) = _rowcall("rms1_bwd", rms_bwd, L,
                               [(h0, D, 0, "row"), (d_xn, D, 0, "row"), (dh1, D, 0, "row")], [n1],
                               [(D, F32)], [(1, D)])
    grad_x = dh0[SEQ_ROW0:]
    g_meta = dh0[PAD_ROWS:SEQ_ROW0]

    tiny = {"rwkv_w2": g_w2p[:DL], "rwkv_a2": g_a2p[DL:DL + AL], "rwkv_g2": g_g2p[DL + AL:LORA],
            "meta_tokens": g_meta}
    g_mu = jnp.concatenate([g_mu_rkv, g_mu_l[:, :LORA]], axis=1)
    gsmall = {
        "norm1_g": g_n1, "rwkv_mu": g_mu, "rwkv_w0": g_w0, "rwkv_a0": g_a0, "rwkv_k_k": g_kk, "rwkv_k_a": g_ka,
        "rwkv_r_k": g_r_k.reshape(wts["rwkv_r_k"].shape), "rwkv_gn_w": g_gn_w, "rwkv_gn_b": g_gn_b,
        "fox_q_norm_g": g_qg8[0:1], "fox_k_norm_g": g_kg8[0:1], "fox_f_bias": g_fb[:, :FH], "norm2_g": g_n2,
    }
    small_flat = jnp.concatenate([gsmall[n].reshape(-1) for n in SMALL])

    tiny_names = tuple(tiny)

    def tiny_block(j):
        parts = []
        for n in tiny_names:
            w = tiny[n].shape[1] // 4
            parts.append(tiny[n][:, j * w:(j + 1) * w])
        return _pack(parts + [small_flat], F32, 32)

    big_names = ("w_in", "w_gate_up", "w_branch_a", "w_branch_b", "w_o", "w_down")
    blocks = [g_w_in, g_w_gu, g_w_a, g_w_b, g_w_o.reshape(4, -1, D), g_w_down.reshape(4, -1, D),
              jnp.stack([tiny_block(j) for j in range(4)])]
    names = big_names + ("small",)
    cj = lax.axis_index("c")
    recv1 = _pair_exchange(blocks)
    parts = [_pair_add("reduce_pair_add_" + n, b, r) for n, b, r in zip(names, blocks, recv1)]
    recv2 = _chip_exchange(parts)
    tots = []
    for n, p, r in zip(names, parts, recv2):
        own = lax.dynamic_index_in_dim(p, chip, 0, keepdims=False)
        tots.append(_chip_add("reduce_chip_add_" + n, lax.dynamic_update_index_in_dim(r, own, chip, 0)))
    others = _pair_join(tots)
    red = [jnp.where(cj == 0, jnp.concatenate([t, o_], axis=0), jnp.concatenate([o_, t], axis=0))
           for t, o_ in zip(tots, others)]
    grads = {n: red[i] for i, n in enumerate(big_names)}
    grads["w_in"] = grads["w_in"][:, :cs]
    tiny_shapes = [wts[n].shape for n in tiny_names]
    got = _unpack(red[-1].reshape(-1), tiny_shapes + [small_flat.shape])
    for n, t in zip(tiny_names, got):
        grads[n] = t
    for n, t in zip(SMALL, _unpack(got[-1], [wts[n].shape for n in SMALL])):
        grads[n] = t

    delta, new_m, new_v = {}, {}, {}
    for n in SHARDED:
        delta[n], new_m[n], new_v[n] = _adamw("adamw_" + n, wts[n], grads[n], mom1[n], mom2[n])
    pk = lambda d: _pack([d[n] for n in SMALL], F32, 8)
    ds, ms, vs = _adamw("adamw_small", pk(wts), pk(grads), pk(mom1), pk(mom2))
    small_shapes = [wts[n].shape for n in SMALL]
    for dst, src in ((delta, ds), (new_m, ms), (new_v, vs)):
        for n, t in zip(SMALL, _unpack(src.reshape(-1), small_shapes)):
            dst[n] = t
    return loss, grad_x, grads, delta, new_m, new_v


def kernel(x, meta_tokens, norm1_g, w_in, rwkv_mu, rwkv_w0, rwkv_w2, rwkv_a0, rwkv_a2, rwkv_g2, rwkv_k_k, rwkv_k_a, rwkv_r_k, rwkv_gn_w, rwkv_gn_b, fox_q_norm_g, fox_k_norm_g, fox_f_bias, w_branch_a, w_branch_b, w_o, norm2_g, w_gate_up, w_down, loss_target, m_meta_tokens, m_norm1_g, m_w_in, m_rwkv_mu, m_rwkv_w0, m_rwkv_w2, m_rwkv_a0, m_rwkv_a2, m_rwkv_g2, m_rwkv_k_k, m_rwkv_k_a, m_rwkv_r_k, m_rwkv_gn_w, m_rwkv_gn_b, m_fox_q_norm_g, m_fox_k_norm_g, m_fox_f_bias, m_w_branch_a, m_w_branch_b, m_w_o, m_norm2_g, m_w_gate_up, m_w_down, v_meta_tokens, v_norm1_g, v_w_in, v_rwkv_mu, v_rwkv_w0, v_rwkv_w2, v_rwkv_a0, v_rwkv_a2, v_rwkv_g2, v_rwkv_k_k, v_rwkv_k_a, v_rwkv_r_k, v_rwkv_gn_w, v_rwkv_gn_b, v_fox_q_norm_g, v_fox_k_norm_g, v_fox_f_bias, v_w_branch_a, v_w_branch_b, v_w_o, v_norm2_g, v_w_gate_up, v_w_down):
    args = dict(locals())
    shapes = {n: args[n].shape for n in WEIGHTS}

    def drop_depth(t, n):
        if n == "meta_tokens":
            return t
        if n == "rwkv_r_k":
            return t.reshape(1, -1)
        return t.reshape(t.shape[1:]) if t.ndim == 3 else t

    wts = {n: drop_depth(args[n], n) for n in WEIGHTS}
    mom1 = {n: drop_depth(args["m_" + n], n) for n in WEIGHTS}
    mom2 = {n: drop_depth(args["v_" + n], n) for n in WEIGHTS}
    loss, grad_x, grads, delta, new_m, new_v = _step(x[0], loss_target[0], wts, mom1, mom2)
    outs = [loss, grad_x[None]]
    for d in (grads, delta, new_m, new_v):
        outs += [d[n].reshape(shapes[n]) for n in WEIGHTS]
    return tuple(outs)
```

```python
import functools

import jax
import jax.numpy as jnp
import numpy as np
from jax import lax
from jax.experimental import pallas as pl
from jax.experimental.pallas import tpu as pltpu

F32 = jnp.float32
BF16 = jnp.bfloat16
MESH = pl.DeviceIdType.MESH
ANY = pl.BlockSpec(memory_space=pl.ANY)

N_META = 16
HEAD = 64
ROW_TILE = 128
PAD_ROWS = ROW_TILE - N_META
SEQ_ROW0 = ROW_TILE
CHUNK = 64
LANES = 128
PACK_W = 1024
RMS_EPS = 1e-6
GN_EPS = 64e-5
ATTN_SCALE = HEAD ** -0.5
NEG = -1e30
VMEM_LIMIT_V7X = 56 * 1024 * 1024

ADAM_LR = 0.001
ADAM_B1 = 0.9
ADAM_B2 = 0.999
ADAM_EPS = 1e-08
ADAM_WD = 0.01
ADAM_STEP = 10


def _params(sem=None):
    return pltpu.CompilerParams(dimension_semantics=sem, vmem_limit_bytes=VMEM_LIMIT_V7X)


def _pick(n, cands):
    for c in cands:
        if n % c == 0:
            return c
    return n


def _bf(t):
    return t.astype(BF16)


def _dotb(a, b):
    return jnp.dot(_bf(a), _bf(b), preferred_element_type=F32)


def _split3(x):
    x1 = x.astype(BF16)
    r1 = x - x1.astype(F32)
    x2 = r1.astype(BF16)
    return x1, x2, (r1 - x2.astype(F32)).astype(BF16)


def _mme_raw(x, e, ce):
    eb = e.astype(BF16)
    dot = lambda q: lax.dot_general(q, eb, (((1,), (ce,)), ((), ())), preferred_element_type=F32)
    x1, x2, x3 = _split3(x)
    return dot(x1) + (dot(x2) + dot(x3))


@jax.custom_vjp
def _doth(x, e):
    return _mme_raw(x, e, 0)


def _doth_fwd(x, e):
    return _mme_raw(x, e, 0), e


def _doth_bwd(e, ct):
    return _mme_raw(ct, e, 1), jnp.zeros_like(e)


_doth.defvjp(_doth_fwd, _doth_bwd)


_BIG = (2048, 1536, 1408, 1024, 768, 704, 512, 384, 256, 128)


def _matmul(a, b, mode, out_dtype, name, add=None, col_blocks=1):
    if mode == "nn":
        (M, R), (_, N) = a.shape, b.shape
        dims = (((1,), (0,)), ((), ()))
    elif mode == "nt":
        (M, R), (N, _) = a.shape, b.shape
        dims = (((1,), (1,)), ((), ()))
    else:
        (R, M), (_, N) = a.shape, b.shape
        dims = (((0,), (0,)), ((), ()))
    tm = _pick(M, (1408, 1024, 768, 512, 384, 256, 128))
    nb = N // col_blocks
    tn = _pick(nb, (1408, 1024, 896, 768, 704, 512, 384, 256, 128)) if mode == "tn" else _pick(nb, (512, 384, 256, 128))
    per = nb // tn
    tr = _pick(R, (1408, 1056, 768, 512, 384, 256, 128)) if mode == "tn" else _pick(R, _BIG)
    nr = R // tr

    if mode == "nn":
        a_spec = pl.BlockSpec((tm, tr), lambda i, j, r: (i, r))
        b_spec = pl.BlockSpec((tr, tn), lambda i, j, r: (r, j))
    elif mode == "nt":
        a_spec = pl.BlockSpec((tm, tr), lambda i, j, r: (i, r))
        b_spec = pl.BlockSpec((tn, tr), lambda i, j, r: (j, r))
    else:
        a_spec = pl.BlockSpec((tr, tm), lambda i, j, r: (r, i))
        b_spec = pl.BlockSpec((tr, tn), lambda i, j, r: (r, j))
    if col_blocks == 1:
        o_spec = pl.BlockSpec((tm, tn), lambda i, j, r: (i, j))
        o_shape = (M, N)
    else:
        o_spec = pl.BlockSpec((1, tm, tn), lambda i, j, r: (j // per, i, j % per))
        o_shape = (col_blocks, M, nb)
    has_add = add is not None

    def body(*refs):
        if has_add:
            a_ref, b_ref, add_ref, o_ref, acc = refs
        else:
            a_ref, b_ref, o_ref, acc = refs
        r = pl.program_id(2)

        @pl.when(r == 0)
        def _():
            acc[...] = jnp.zeros_like(acc)

        acc[...] += lax.dot_general(_bf(a_ref[...]), _bf(b_ref[...]), dims, preferred_element_type=F32)

        @pl.when(r == nr - 1)
        def _():
            res = acc[...]
            if has_add:
                res = res + add_ref[...]
            o_ref[...] = res.astype(o_ref.dtype).reshape(o_ref.shape)

    ins = [a, b] + ([add] if has_add else [])
    specs = [a_spec, b_spec] + ([o_spec] if has_add else [])
    return pl.pallas_call(
        body, name=name, out_shape=jax.ShapeDtypeStruct(o_shape, out_dtype),
        grid=(M // tm, N // tn, nr), in_specs=specs, out_specs=o_spec,
        scratch_shapes=[pltpu.VMEM((tm, tn), F32)],
        compiler_params=_params(("parallel", "parallel", "arbitrary")),
    )(*ins)


def _rowcall(name, fn, L, row_ins, bc_ins, row_outs, acc_outs, tm=ROW_TILE):
    nt = L // tm
    specs = []
    for arr, w, cb, kind in row_ins:
        if kind == "row":
            specs.append(pl.BlockSpec((tm, w), lambda i, cb=cb: (i, cb)))
        elif kind == "lag":
            specs.append(pl.BlockSpec((tm, w), lambda i, cb=cb: (jnp.maximum(i - 1, 0), cb)))
        elif kind == "prev":
            specs.append(pl.BlockSpec((8, w), lambda i, cb=cb: (jnp.maximum(i * (tm // 8) - 1, 0), cb)))
        else:
            specs.append(pl.BlockSpec((8, w), lambda i, cb=cb: (jnp.minimum((i + 1) * (tm // 8), L // 8 - 1), cb)))
    for arr in bc_ins:
        specs.append(pl.BlockSpec(arr.shape, lambda i, nd=arr.ndim: (0,) * nd))
    out_shapes = [jax.ShapeDtypeStruct((L, w), dt) for w, dt in row_outs]
    out_specs = [pl.BlockSpec((tm, w), lambda i: (i, 0)) for w, dt in row_outs]
    out_shapes += [jax.ShapeDtypeStruct(s, F32) for s in acc_outs]
    out_specs += [pl.BlockSpec(s, lambda i, nd=len(s): (0,) * nd) for s in acc_outs]
    n_row, n_bc, n_ro = len(row_ins), len(bc_ins), len(row_outs)

    def body(*refs):
        i = pl.program_id(0)
        vals = [r[...] for r in refs[: n_row + n_bc]]
        outs, sums = fn(i, vals[:n_row], vals[n_row:])
        o_refs = refs[n_row + n_bc:]
        for r, v in zip(o_refs[:n_ro], outs):
            r[...] = v.astype(r.dtype)

        @pl.when(i == 0)
        def _():
            for r in o_refs[n_ro:]:
                r[...] = jnp.zeros_like(r)

        for r, v in zip(o_refs[n_ro:], sums):
            r[...] += v

    res = pl.pallas_call(
        body, name=name, out_shape=out_shapes, grid=(nt,), in_specs=specs, out_specs=out_specs,
        compiler_params=_params(("arbitrary",)),
    )(*[a for a, _, _, _ in row_ins], *bc_ins)
    return list(res[:n_ro]), list(res[n_ro:])


def _rowsum(t):
    return jnp.sum(t, axis=0, keepdims=True)


def _head_mats(width):
    e = (np.arange(width)[:, None] // HEAD == np.arange(LANES)[None, :]).astype(np.float32)
    return jnp.asarray(e), jnp.asarray(e.T)


def _fold_mat(width):
    ft = (np.arange(HEAD)[:, None] == np.arange(width)[None, :] % HEAD).astype(np.float32)
    return jnp.asarray(ft)


def _rms_f(h, g):
    return (h * lax.rsqrt(jnp.mean(h * h, axis=-1, keepdims=True) + RMS_EPS)) * g


def _prep_f(xr, xk, xv, xl, w0, w2p, a0, a2p, g2p, k_k, k_a, e, et):
    w_log = -jax.nn.softplus(-(w0 + _dotb(jnp.tanh(xl), w2p))) - 0.5
    lw = -jnp.exp(w_log)
    a = jax.nn.sigmoid(a0 + _dotb(xl, a2p))
    g = _dotb(jax.nn.sigmoid(xl), g2p)
    kkr = xk * k_k
    inv = lax.rsqrt(jnp.maximum(_doth(kkr * kkr, e), 1e-24))
    kk = kkr * _doth(inv, et)
    kf = xk * (1.0 + (a - 1.0) * k_a)
    return xr, lw, kf, xv, -kk, kk * a, g


def _post_f(y, r, kf, v, g, gn_w, gn_b, r_k, e, et):
    mu = _doth(y, e) * (1.0 / HEAD)
    yc = y - _doth(mu, et)
    var = _doth(yc * yc, e) * (1.0 / HEAD)
    yn = yc * _doth(lax.rsqrt(var + GN_EPS), et) * gn_w + gn_b
    bonus = _doth(r * kf * r_k, e)
    return (yn + _doth(bonus, et) * v) * g


def _foxprep_f(q, k, fl, qg8, kg8, fb, e, et, ft, fmask):
    def norm(t, g8):
        ms = _doth(t * t, e) * (1.0 / HEAD)
        return t * _doth(lax.rsqrt(ms + RMS_EPS), et) * _doth(g8, ft)[0:1]
    logf = jax.nn.log_sigmoid(fl + fb) * fmask
    return norm(q, qg8), norm(k, kg8), logf


def _merge_f(za, zb, pa, pb):
    return jax.nn.sigmoid(za) * pa + jax.nn.sigmoid(zb) * pb


def _swiglu_f(gate, up):
    return jax.nn.silu(gate) * up


def _tri(n, strict):
    row = lax.broadcasted_iota(jnp.int32, (n, n), 0)
    col = lax.broadcasted_iota(jnp.int32, (n, n), 1)
    return (row > col) if strict else (row >= col)


def _split2(x):
    hi = x.astype(BF16)
    return hi, (x - hi.astype(F32)).astype(BF16)


def _mm3_raw(a, b, ca, cb):
    dn = (((ca,), (cb,)), ((), ()))
    ah, al = _split2(a)
    bh, bl = _split2(b)
    dot = lambda p, q: lax.dot_general(p, q, dn, preferred_element_type=F32)
    return dot(ah, bh) + (dot(al, bh) + dot(ah, bl))


@functools.partial(jax.custom_vjp, nondiff_argnums=(2, 3))
def _mm3(a, b, ca, cb):
    return _mm3_raw(a, b, ca, cb)


def _mm3_fwd(a, b, ca, cb):
    return _mm3_raw(a, b, ca, cb), (a, b)


def _mm3_bwd(ca, cb, res, ct):
    a, b = res
    da = _mm3_raw(ct, b, 1, 1 - cb) if ca == 1 else _mm3_raw(b, ct, 1 - cb, 1)
    db = _mm3_raw(a, ct, 1 - ca, 0) if cb == 0 else _mm3_raw(ct, a, 0, 1 - ca)
    return da, db


_mm3.defvjp(_mm3_fwd, _mm3_bwd)


def _mmx_raw(t, x, ct):
    x1 = x.astype(BF16)
    r1 = x - x1.astype(F32)
    x2 = r1.astype(BF16)
    x3 = (r1 - x2.astype(F32)).astype(BF16)
    tb = t.astype(BF16)
    dot = lambda q: lax.dot_general(tb, q, (((ct,), (0,)), ((), ())), preferred_element_type=F32)
    return dot(x1) + (dot(x2) + dot(x3))


@jax.custom_vjp
def _mmx(t, x):
    return _mmx_raw(t, x, 1)


def _mmx_fwd(t, x):
    return _mmx_raw(t, x, 1), t


def _mmx_bwd(t, ct):
    return jnp.zeros_like(t), _mmx_raw(t, ct, 0)


_mmx.defvjp(_mmx_fwd, _mmx_bwd)

SCAN_HEADS = 16


def _scan_step(r, lw, k, v, a, b, st):
    c = r.shape[0]
    nh = r.shape[1] // HEAD
    incl = _tri(c, False)
    row2 = lax.broadcasted_iota(jnp.int32, (2 * c, 2 * c), 0)
    col2 = lax.broadcasted_iota(jnp.int32, (2 * c, 2 * c), 1)
    t_row = jnp.where(row2 >= c, row2 - c, row2)
    t_col = jnp.where(col2 >= c, col2 - c, col2)
    mask2 = (t_row > t_col) | ((row2 >= c) & (t_row == t_col))
    right =lax.broadcasted_iota(jnp.int32, (c, 2 * c), 1) >= c
    eye = lax.broadcasted_iota(jnp.int32, (HEAD, HEAD), 0) == lax.broadcasted_iota(jnp.int32, (HEAD, HEAD), 1)
    cl = _mmx(incl.astype(F32), lw)
    last = cl[c - 1:c, :]
    rt = r * jnp.exp(cl)
    at = a * jnp.exp(cl - lw)
    pinv = jnp.exp(-cl)
    bt = b * pinv
    kt = k * pinv
    pend = jnp.exp(last - cl)
    bl = b * pend
    kl = k * pend
    pe_last = jnp.exp(last)
    hs = range(nh)
    sl = [slice(h * HEAD, (h + 1) * HEAD) for h in hs]
    ar = [jnp.concatenate([at[:, sl[h]], rt[:, sl[h]]], axis=0) for h in hs]
    bk = [jnp.concatenate([bt[:, sl[h]], kt[:, sl[h]]], axis=0) for h in hs]
    amat = [jnp.where(mask2, _mm3(ar[h], bk[h], 1, 1), 0.0) for h in hs]
    res = [_mm3(jnp.concatenate([ar[h], amat[h][:, c:]], axis=1),
                jnp.concatenate([st[h], v[:, sl[h]]], axis=0), 1, 0) for h in hs]
    z = [jnp.concatenate([amat[h][:c, :c], res[h][:c]], axis=1) for h in hs]
    for _ in range(max(1, int(np.ceil(np.log2(c))))):
        z = [_mm3(z[h][:, :c], z[h], 1, 0) + jnp.where(right, z[h], 0.0) for h in hs]
    u = [z[h][:, c:] for h in hs]
    ys = [res[h][c:] + _mm3(amat[h][c:, :c], u[h], 1, 0) for h in hs]
    s1s = [_mm3(jnp.concatenate([bl[:, sl[h]], kl[:, sl[h]], jnp.where(eye, pe_last[:, sl[h]], 0.0)], axis=0),
                jnp.concatenate([u[h], v[:, sl[h]], st[h]], axis=0), 0, 0) for h in hs]
    return tuple(ys), tuple(s1s)


def _scan_heads(W):
    return SCAN_HEADS if W % (SCAN_HEADS * HEAD) == 0 else 2


def _scan_fwd(r, lw, k, v, a, b):
    L, W = r.shape
    nh = _scan_heads(W)
    nc, ng = L // CHUNK, W // (nh * HEAD)
    spec = pl.BlockSpec((CHUNK, nh * HEAD), lambda p, c: (c, p))

    def body(r_ref, lw_ref, k_ref, v_ref, a_ref, b_ref, y_ref, s_ref, st):
        @pl.when(pl.program_id(1) == 0)
        def _():
            st[...] = jnp.zeros_like(st)

        s0 = st[...]
        ys, s1s = _scan_step(r_ref[...], lw_ref[...], k_ref[...], v_ref[...], a_ref[...], b_ref[...], s0)
        s_ref[0] = s0
        st[...] = jnp.stack(s1s)
        y_ref[...] = jnp.concatenate(ys, axis=1)

    return pl.pallas_call(
        body, name="wkv7_fwd",
        out_shape=[jax.ShapeDtypeStruct((L, W), F32), jax.ShapeDtypeStruct((nc, nh * ng, HEAD, HEAD), F32)],
        grid=(ng, nc), in_specs=[spec] * 6,
        out_specs=[spec, pl.BlockSpec((1, nh, HEAD, HEAD), lambda p, c: (c, p, 0, 0))],
        scratch_shapes=[pltpu.VMEM((nh, HEAD, HEAD), F32)],
        compiler_params=_params(("parallel", "arbitrary")),
    )(r, lw, k, v, a, b)


def _scan_bwd(r, lw, k, v, a, b, s_all, dy):
    L, W = r.shape
    nh = _scan_heads(W)
    nc, ng = L // CHUNK, W // (nh * HEAD)
    spec = pl.BlockSpec((CHUNK, nh * HEAD), lambda p, c: (nc - 1 - c, p))

    def body(r_ref, lw_ref, k_ref, v_ref, a_ref, b_ref, s_ref, dy_ref,
             dr_ref, dlw_ref, dk_ref, dv_ref, da_ref, db_ref, dst):
        @pl.when(pl.program_id(1) == 0)
        def _():
            dst[...] = jnp.zeros_like(dst)

        _, vjp = jax.vjp(_scan_step, r_ref[...], lw_ref[...], k_ref[...], v_ref[...], a_ref[...], b_ref[...],
                         s_ref[0])
        dys = tuple(dy_ref[:, h * HEAD:(h + 1) * HEAD] for h in range(nh))
        g = vjp((dys, tuple(dst[h] for h in range(nh))))
        for ref, val in zip((dr_ref, dlw_ref, dk_ref, dv_ref, da_ref, db_ref), g[:6]):
            ref[...] = val
        dst[...] = g[6]

    return pl.pallas_call(
        body, name="wkv7_bwd", out_shape=[jax.ShapeDtypeStruct((L, W), F32)] * 6,
        grid=(ng, nc),
        in_specs=[spec] * 6 + [pl.BlockSpec((1, nh, HEAD, HEAD), lambda p, c: (nc - 1 - c, p, 0, 0)), spec],
        out_specs=[spec] * 6,
        scratch_shapes=[pltpu.VMEM((nh, HEAD, HEAD), F32)],
        compiler_params=_params(("parallel", "arbitrary")),
    )(r, lw, k, v, a, b, s_all, dy)


def _cumsum_rows(logf):
    L = logf.shape[0]
    t = LANES

    def body(x_ref, o_ref, carry):
        @pl.when(pl.program_id(0) == 0)
        def _():
            carry[...] = jnp.zeros_like(carry)

        c = _mmx(_tri(t, False).astype(F32), x_ref[...]) + carry[...]
        carry[...] = c[t - 1:t, :]
        o_ref[...] = c.T

    return pl.pallas_call(
        body, name="fox_cumsum", out_shape=jax.ShapeDtypeStruct((LANES, L), F32), grid=(L // t,),
        in_specs=[pl.BlockSpec((t, LANES), lambda i: (i, 0))],
        out_specs=pl.BlockSpec((LANES, t), lambda i: (0, i)),
        scratch_shapes=[pltpu.VMEM((1, LANES), F32)], compiler_params=_params(("arbitrary",)),
    )(logf)


def _rcumsum_cols(dct):
    L = dct.shape[1]
    t = LANES
    n = L // t

    def body(x_ref, o_ref, carry):
        @pl.when(pl.program_id(0) == 0)
        def _():
            carry[...] = jnp.zeros_like(carry)

        rc = _doth(x_ref[...], _tri(t, False).astype(F32)) + carry[...]
        carry[...] = rc[:, 0:1]
        o_ref[...] = rc.T

    return pl.pallas_call(
        body, name="fox_rcumsum", out_shape=jax.ShapeDtypeStruct((L, LANES), F32), grid=(n,),
        in_specs=[pl.BlockSpec((LANES, t), lambda i: (0, n - 1 - i))],
        out_specs=pl.BlockSpec((t, LANES), lambda i: (n - 1 - i, 0)),
        scratch_shapes=[pltpu.VMEM((LANES, 1), F32)], compiler_params=_params(("arbitrary",)),
    )(dct)


def _attn_tile(L):
    return _pick(L, (384, 256, 128))


def _head_lanes(hh, shape):
    return lax.broadcasted_iota(jnp.int32, shape, len(shape) - 1) // HEAD == hh


def _own(hh, block, other=0):
    return jnp.where(_head_lanes(hh, block.shape), block, jnp.asarray(other, block.dtype))


def _attn_scores(q, k, ck, qi, kj, t):
    s = _dot_bnt(q, k) * ATTN_SCALE - ck
    qpos = qi * t + lax.broadcasted_iota(jnp.int32, (t, t), 0)
    kpos = kj * t + lax.broadcasted_iota(jnp.int32, (t, t), 1)
    mask = (kpos <= qpos) & (kpos >= PAD_ROWS)
    return jnp.where(mask, s, NEG), mask


def _dot_bnt(a, b):
    return lax.dot_general(_bf(a), _bf(b), (((1,), (1,)), ((), ())), preferred_element_type=F32)


def _dot_btn(a, b):
    return lax.dot_general(_bf(a), _bf(b), (((0,), (0,)), ((), ())), preferred_element_type=F32)


def _ck_rows(ct_ref, p):
    r0 = 2 * (p % 4)
    return ct_ref[pl.ds(r0, 1), :], ct_ref[pl.ds(r0 + 1, 1), :]


def _carried(plan, refs, n_in, n_out, n_scratch):
    ci = len(plan["arrays"]) if plan else 0
    co = len(plan["out_shapes"]) if plan else 0
    a = n_in + ci
    b = a + n_out + co
    return (refs[:n_in], refs[n_in:a], refs[a:a + n_out], refs[a + n_out:b], refs[b:b + n_scratch],
            refs[b + n_scratch:])


def _attn_fwd(q, k, v, ct, plan=None):
    L, W = q.shape
    t = _attn_tile(L)
    nt, npair = L // t, W // LANES
    qspec = pl.BlockSpec((t, LANES), lambda p, i, j: (i, p))
    kspec = pl.BlockSpec((t, LANES), lambda p, i, j: (jnp.minimum(i, j), p))
    cspec = pl.BlockSpec((8, t), lambda p, i, j: (p // 4, jnp.minimum(i, j)))

    def body(*refs):
        (q_ref, k_ref, v_ref, ct_ref), p_in, (o_ref, lse_ref), p_out, (m_s, acc), sems = _carried(plan, refs, 4, 2, 2)
        p, i, j = pl.program_id(0), pl.program_id(1), pl.program_id(2)
        if plan:
            @pl.when((p == 0) & (i == 0) & (j == 0))
            def _():
                plan["start"](p_in, p_out, *sems)

            if plan["mid"] is not None:
                @pl.when((p == npair // 2) & (i == 0) & (j == 0))
                def _():
                    plan["mid"](p_in, p_out, *sems)

        @pl.when(j == 0)
        def _():
            m_s[...] = jnp.full_like(m_s, NEG)
            acc[...] = jnp.zeros_like(acc)

        @pl.when(j <= i)
        def _():
            cks = _ck_rows(ct_ref, p)
            qb, kb, vf = _bf(q_ref[...]), _bf(k_ref[...]), v_ref[...]
            ss = [_attn_scores(_own(hh, qb), kb, cks[hh], i, j, t)[0] for hh in range(2)]
            prs, alphas = [], []
            for hh in range(2):
                m_old = m_s[hh]
                m_new = jnp.maximum(m_old, jnp.max(ss[hh], axis=-1, keepdims=True))
                alphas.append(jnp.exp(m_old - m_new))
                prs.append(jnp.exp(ss[hh] - m_new))
                m_s[hh] = m_new
            pvs = []
            for hh in range(2):
                p_hi, p_lo = _split2(prs[hh])
                vx = _bf(_own(hh, vf, 1.0))
                pvs.append(jnp.dot(p_hi, vx, preferred_element_type=F32)
                           + jnp.dot(p_lo, vx, preferred_element_type=F32))
            for hh in range(2):
                acc[hh] = alphas[hh] * acc[hh] + pvs[hh]

        @pl.when(j == i)
        def _():
            lane = lax.broadcasted_iota(jnp.int32, (t, LANES), 1)
            lse = jnp.zeros((t, LANES), F32)
            out = jnp.zeros((t, LANES), F32)
            for hh in range(2):
                a = acc[hh]
                row_sum = pltpu.roll(a, HEAD, 1)
                out = jnp.where(_head_lanes(hh, a.shape), a / row_sum, out)
                l_col = a[:, HEAD:HEAD + 1] if hh == 0 else a[:, 0:1]
                lse = jnp.where(lane == hh, m_s[hh] + jnp.log(l_col), lse)
            o_ref[...] = out
            lse_ref[0] = lse

        if plan:
            @pl.when((p == npair - 1) & (i == nt - 1) & (j == nt - 1))
            def _():
                plan["wait"](p_in, p_out, *sems)

    extra = plan["arrays"] if plan else []
    extra_out = plan["out_shapes"] if plan else []
    res = pl.pallas_call(
        body, name="fox_attn_fwd",
        out_shape=[jax.ShapeDtypeStruct((L, W), F32), jax.ShapeDtypeStruct((npair, L, LANES), F32)] + extra_out,
        grid=(npair, nt, nt), in_specs=[qspec, kspec, kspec, cspec] + [ANY] * len(extra),
        out_specs=[qspec, pl.BlockSpec((1, t, LANES), lambda p, i, j: (p, i, 0))] + [ANY] * len(extra_out),
        scratch_shapes=[pltpu.VMEM((2, t, 1), F32), pltpu.VMEM((2, t, LANES), F32)]
        + (_dma_sems(plan["nsem"]) if plan else []),
        compiler_params=_params(("arbitrary",) * 3 if plan else ("parallel", "parallel", "arbitrary")),
    )(q, k, v, ct, *extra)
    return res[0], res[1], list(res[2:])


def _attn_bwd(q, k, v, ct, o, lse, do, plan=None):
    L, W = q.shape
    t = _attn_tile(L)
    nt, npair = L // t, W // LANES
    kspec = pl.BlockSpec((t, LANES), lambda p, j, i: (j, p))
    qspec = pl.BlockSpec((t, LANES), lambda p, j, i: (jnp.maximum(i, j), p))
    cspec = pl.BlockSpec((8, t), lambda p, j, i: (p // 4, j))
    lspec = pl.BlockSpec((1, t, LANES), lambda p, j, i: (p, jnp.maximum(i, j), 0))

    def body(*refs):
        ((q_ref, k_ref, v_ref, ct_ref, o_ref, lse_ref, do_ref), p_in, (dk_ref, dv_ref, dc_ref, dq_ref), p_out,
         (dk_s, dv_s, dc_s), sems) = _carried(plan, refs, 7, 4, 3)
        p, j, i = pl.program_id(0), pl.program_id(1), pl.program_id(2)
        if plan:
            @pl.when((p == 0) & (i == 0) & (j == 0))
            def _():
                plan["start"](p_in, p_out, *sems)

        @pl.when(i == 0)
        def _():
            dk_s[...] = jnp.zeros_like(dk_s)
            dv_s[...] = jnp.zeros_like(dv_s)
            dc_s[...] = jnp.zeros_like(dc_s)

        def tile_dq():
            cks = _ck_rows(ct_ref, p)
            qb, kb, vb, dob = _bf(q_ref[...]), _bf(k_ref[...]), _bf(v_ref[...]), _bf(do_ref[...])
            of = o_ref[...]
            hs = range(2)
            qm = [_own(hh, qb) for hh in hs]
            dom = [_own(hh, dob) for hh in hs]
            sm = [_attn_scores(qm[hh], kb, cks[hh], i, j, t) for hh in hs]
            dps = [_dot_bnt(dom[hh], vb) for hh in hs]
            prs = [jnp.where(sm[hh][1], jnp.exp(sm[hh][0] - lse_ref[0, :, hh:hh + 1]), 0.0) for hh in hs]
            dss = [prs[hh] * (dps[hh] - jnp.sum(dom[hh].astype(F32) * of, axis=-1, keepdims=True)) for hh in hs]
            dv_s[...] += _dot_btn(prs[0], dom[0]) + _dot_btn(prs[1], dom[1])
            dk_s[...] += _dot_btn(dss[0], qm[0]) + _dot_btn(dss[1], qm[1])
            for hh in hs:
                dc_s[hh] += -jnp.sum(dss[hh], axis=0, keepdims=True)
            return (_dotb(dss[0], _own(0, kb)) + _dotb(dss[1], _own(1, kb))) * ATTN_SCALE

        rows = pl.ds(pl.multiple_of(i * t, t), t)

        @pl.when((i >= j) & (j == 0))
        def _():
            dq_ref[rows, :] = tile_dq()

        @pl.when((i >= j) & (j > 0))
        def _():
            dq_ref[rows, :] += tile_dq()

        @pl.when(i == nt - 1)
        def _():
            row = lax.broadcasted_iota(jnp.int32, (8, t), 0)
            dc = jnp.zeros((8, t), F32)
            for hh in range(2):
                dc = jnp.where(row == hh, dc_s[hh], dc)
            dk_ref[...] = dk_s[...] * ATTN_SCALE
            dv_ref[...] = dv_s[...]
            dc_ref[0] = dc

        if plan:
            @pl.when((p == npair - 1) & (i == nt - 1) & (j == nt - 1))
            def _():
                plan["wait"](p_in, p_out, *sems)

    extra = plan["arrays"] if plan else []
    extra_out = plan["out_shapes"] if plan else []
    res = pl.pallas_call(
        body, name="fox_attn_bwd",
        out_shape=[jax.ShapeDtypeStruct((L, W), F32), jax.ShapeDtypeStruct((L, W), F32),
                   jax.ShapeDtypeStruct((npair, 8, L), F32), jax.ShapeDtypeStruct((L, W), F32)] + extra_out,
        grid=(npair, nt, nt), in_specs=[qspec, kspec, kspec, cspec, qspec, lspec, qspec] + [ANY] * len(extra),
        out_specs=[kspec, kspec, pl.BlockSpec((1, 8, t), lambda p, j, i: (p, 0, j)),
                   pl.BlockSpec((L, LANES), lambda p, j, i: (0, p))] + [ANY] * len(extra_out),
        scratch_shapes=[pltpu.VMEM((t, LANES), F32), pltpu.VMEM((t, LANES), F32), pltpu.VMEM((2, 1, t), F32)]
        + (_dma_sems(plan["nsem"]) if plan else []),
        compiler_params=_params(("arbitrary",) * 3 if plan else ("parallel", "arbitrary", "arbitrary")),
    )(q, k, v, ct, o, lse, do, *extra)
    return res[0], res[1], res[2], res[3], list(res[4:])


def _place():
    x, y, c = lax.axis_index("x"), lax.axis_index("y"), lax.axis_index("c")
    chips = [(1 - x, y), (x, 1 - y), (1 - x, 1 - y)]
    return x, y, c, chips


def _remote(src, dst, send_sems, recv_sems, k, to):
    return pltpu.make_async_remote_copy(src_ref=src, dst_ref=dst, send_sem=send_sems.at[k],
                                        recv_sem=recv_sems.at[k], device_id=to, device_id_type=MESH)


def _dma_sems(n):
    return [pltpu.SemaphoreType.DMA((n,)), pltpu.SemaphoreType.DMA((n,))]


def _all_gather(shards, modes):
    return _run_exchange("gather_weights", _gather_plan(shards, modes))


def _gather_plan(shards, modes):
    n = len(shards)

    def out_shape(s, mode):
        r, c = s.shape
        return {"stack": (4, r, c), "cols": (r, 4 * c), "rows": (4 * r, c)}[mode]

    def window(outs, i, chip, cc):
        r, cw = shards[i].shape
        h = r // 2
        if modes[i] == "stack":
            return outs[i].at[chip, pl.ds(cc * h, h), :]
        if modes[i] == "cols":
            return outs[i].at[pl.ds(cc * h, h), pl.ds(pl.multiple_of(chip * cw, LANES), cw)]
        return outs[i].at[pl.ds(pl.multiple_of(chip * r + cc * h, 8), h), :]

    def first(ins, outs, ss, rs):
        x, y, c, chips = _place()
        cps = []
        for i in range(n):
            h = shards[i].shape[0] // 2
            for k, (cx, cy) in enumerate(chips):
                cps.append(_remote(ins[i].at[pl.ds(c * h, h), :], window(outs, i, 2 * x + y, c), ss, rs,
                                   6 * i + k, (cx, cy, c)))
        return cps

    def passed(outs, ss, rs):
        x, y, c, chips = _place()
        cps = []
        for k, (cx, cy) in enumerate(chips):
            for i in range(n):
                landed = window(outs, i, 2 * cx + cy, c)
                cps.append((_remote(landed, landed, ss, rs, 6 * i + k, (x, y, 1 - c)),
                            _remote(landed, landed, ss, rs, 6 * i + 3 + k, (x, y, 1 - c))))
        return cps

    def start(ins, outs, ss, rs):
        for cp in first(ins, outs, ss, rs):
            cp.start()

    def mid(ins, outs, ss, rs):
        for arrival, fwd in passed(outs, ss, rs):
            arrival.wait_recv()
            fwd.start()

    def wait(ins, outs, ss, rs):
        x, y, c, chips = _place()
        for k, (cx, cy) in enumerate(chips):
            for i in range(n):
                other = window(outs, i, 2 * cx + cy, 1 - c)
                _remote(other, other, ss, rs, 6 * i + 3 + k, (x, y, 1 - c)).wait_recv()
        for cp in first(ins, outs, ss, rs) + [fwd for _, fwd in passed(outs, ss, rs)]:
            cp.wait_send()

    return dict(arrays=list(shards), nsem=6 * n, start=start, mid=mid, wait=wait,
                out_shapes=[jax.ShapeDtypeStruct(out_shape(s, m), s.dtype) for s, m in zip(shards, modes)])


def _run_exchange(name, plan):
    n = len(plan["arrays"])

    def body(*refs):
        ins, outs, (ss, rs) = refs[:n], refs[n:n + len(plan["out_shapes"])], refs[n + len(plan["out_shapes"]):]
        plan["start"](ins, outs, ss, rs)
        if plan["mid"] is not None:
            plan["mid"](ins, outs, ss, rs)
        plan["wait"](ins, outs, ss, rs)

    return pl.pallas_call(
        body, name=name, out_shape=plan["out_shapes"], in_specs=[ANY] * n,
        out_specs=[ANY] * len(plan["out_shapes"]), scratch_shapes=_dma_sems(plan["nsem"]),
    )(*plan["arrays"])


def _pair_exchange(name, blocks):
    n = len(blocks)

    def body(*refs):
        ins, outs, (send_sems, recv_sems) = refs[:n], refs[n:2 * n], refs[2 * n:]
        x, y, c, _ = _place()
        cps = []
        for i in range(n):
            h = blocks[i].shape[1] // 2
            cps.append(_remote(ins[i].at[:, pl.ds((1 - c) * h, h), :], outs[i], send_sems, recv_sems, i,
                               (x, y, 1 - c)))
        for cp in cps:
            cp.start()
        for cp in cps:
            cp.wait()

    return pl.pallas_call(
        body, name=name,
        out_shape=[jax.ShapeDtypeStruct((4, b.shape[1] // 2, b.shape[2]), b.dtype) for b in blocks],
        in_specs=[ANY] * n, out_specs=[ANY] * n, scratch_shapes=_dma_sems(n),
    )(*blocks)


def _chip_exchange(parts):
    return _run_exchange("reduce_chip_exchange", _chip_exchange_plan(parts))


def _chip_exchange_plan(parts):
    n = len(parts)

    def sends(ins, outs, ss, rs):
        x, y, c, chips = _place()
        return [_remote(ins[i].at[2 * cx + cy], outs[i].at[2 * x + y], ss, rs, 3 * i + k, (cx, cy, c))
                for i in range(n) for k, (cx, cy) in enumerate(chips)]

    def start(ins, outs, ss, rs):
        for cp in sends(ins, outs, ss, rs):
            cp.start()

    def wait(ins, outs, ss, rs):
        x, y, c, chips = _place()
        for i in range(n):
            for k, (cx, cy) in enumerate(chips):
                slot = outs[i].at[2 * cx + cy]
                _remote(slot, slot, ss, rs, 3 * i + k, (cx, cy, c)).wait_recv()
        for cp in sends(ins, outs, ss, rs):
            cp.wait_send()

    return dict(arrays=list(parts), nsem=3 * n, start=start, mid=None, wait=wait,
                out_shapes=[jax.ShapeDtypeStruct(p.shape, p.dtype) for p in parts])


def _pair_join(tots):
    n = len(tots)

    def body(*refs):
        ins, outs, (send_sems, recv_sems) = refs[:n], refs[n:2 * n], refs[2 * n:]
        x, y, c, _ = _place()
        cps = [_remote(ins[i], outs[i], send_sems, recv_sems, i, (x, y, 1 - c)) for i in range(n)]
        for cp in cps:
            cp.start()
        for cp in cps:
            cp.wait()

    return pl.pallas_call(
        body, name="reduce_pair_join", out_shape=[jax.ShapeDtypeStruct(t.shape, t.dtype) for t in tots],
        in_specs=[ANY] * n, out_specs=[ANY] * n, scratch_shapes=_dma_sems(n),
    )(*tots)


def _add_tile(h, cw):
    cap = max(8, (512 * 1024) // max(cw, 1))
    return _pick(h, tuple(t for t in (1024, 512, 256, 128, 64, 32, 16, 8) if t <= cap))


def _pair_add(name, block, recv):
    n, r, cw = block.shape
    h = r // 2
    tr = _add_tile(h, cw)
    c = lax.axis_index("c").astype(jnp.int32).reshape((1,))

    def body(c_ref, a_ref, b_ref, o_ref):
        o_ref[...] = (a_ref[...] + b_ref[...]).astype(o_ref.dtype)

    return pl.pallas_call(
        body, name=name, out_shape=jax.ShapeDtypeStruct((n, h, cw), BF16),
        grid_spec=pltpu.PrefetchScalarGridSpec(
            num_scalar_prefetch=1, grid=(n, h // tr),
            in_specs=[pl.BlockSpec((1, tr, cw), lambda a, i, cr: (a, cr[0] * (h // tr) + i, 0)),
                      pl.BlockSpec((1, tr, cw), lambda a, i, cr: (a, i, 0))],
            out_specs=pl.BlockSpec((1, tr, cw), lambda a, i, cr: (a, i, 0))),
        compiler_params=_params(("parallel", "parallel")),
    )(c, block, recv)


def _chip_add(name, parts):
    n, h, cw = parts.shape
    tr = _add_tile(h, cw)

    def body(a_ref, o_ref):
        f = lambda i: a_ref[i].astype(F32)
        o_ref[...] = ((f(0) + f(1)) + f(2)) + f(3)

    return pl.pallas_call(
        body, name=name, out_shape=jax.ShapeDtypeStruct((h, cw), F32), grid=(h // tr,),
        in_specs=[pl.BlockSpec((n, tr, cw), lambda i: (0, i, 0))],
        out_specs=pl.BlockSpec((tr, cw), lambda i: (i, 0)),
        compiler_params=_params(("parallel",)),
    )(parts)


def _adamw(name, w, g, m, v):
    R, C = w.shape
    tr = _pick(R, (128, 64, 32, 16, 8))
    spec = pl.BlockSpec((tr, C), lambda i: (i, 0))

    def body(w_ref, g_ref, m_ref, v_ref, d_ref, mo_ref, vo_ref):
        gr = g_ref[...]
        mn = ADAM_B1 * m_ref[...] + (1.0 - ADAM_B1) * gr
        vn = ADAM_B2 * v_ref[...] + (1.0 - ADAM_B2) * jnp.square(gr)
        m_hat = mn / (1.0 - ADAM_B1 ** ADAM_STEP)
        v_hat = vn / (1.0 - ADAM_B2 ** ADAM_STEP)
        d_ref[...] = -ADAM_LR * (m_hat / (jnp.sqrt(v_hat) + ADAM_EPS) + ADAM_WD * w_ref[...])
        mo_ref[...] = mn
        vo_ref[...] = vn

    return pl.pallas_call(
        body, name=name, out_shape=[jax.ShapeDtypeStruct((R, C), F32)] * 3, grid=(R // tr,),
        in_specs=[spec] * 4, out_specs=[spec] * 3, compiler_params=_params(("parallel",)),
    )(w, g, m, v)


def _pack(parts, dtype, row_mult):
    flat = jnp.concatenate([p.reshape(-1).astype(dtype) for p in parts])
    unit = row_mult * PACK_W
    pad = (-flat.shape[0]) % unit
    if pad:
        flat = jnp.concatenate([flat, jnp.zeros((pad,), dtype)])
    return flat.reshape(-1, PACK_W)


def _unpack(flat, shapes):
    out, off = [], 0
    for s in shapes:
        n = int(np.prod(s))
        out.append(flat[off:off + n].reshape(s))
        off += n
    return out


SHARDED = ("w_in", "rwkv_w2", "rwkv_a2", "rwkv_g2", "w_branch_a", "w_branch_b", "w_o", "w_gate_up", "w_down",
           "meta_tokens")
SHARD_AXIS = {"w_in": 1, "rwkv_w2": 1, "rwkv_a2": 1, "rwkv_g2": 1, "w_branch_a": 1, "w_branch_b": 1, "w_o": 0,
              "w_gate_up": 1, "w_down": 0, "meta_tokens": 1}
GATHER_MODE = {"w_in": "stack", "rwkv_w2": "stack", "rwkv_a2": "stack", "rwkv_g2": "stack", "w_branch_a": "cols",
               "w_branch_b": "cols", "w_o": "rows", "w_gate_up": "cols", "w_down": "rows", "meta_tokens": "stack"}
LATE_GATHER = ("w_gate_up", "w_down")
SMALL = ("norm1_g", "rwkv_mu", "rwkv_w0", "rwkv_a0", "rwkv_k_k", "rwkv_k_a", "rwkv_r_k", "rwkv_gn_w",
         "rwkv_gn_b", "fox_q_norm_g", "fox_k_norm_g", "fox_f_bias", "norm2_g")
WEIGHTS = ("meta_tokens", "norm1_g", "w_in", "rwkv_mu", "rwkv_w0", "rwkv_w2", "rwkv_a0", "rwkv_a2", "rwkv_g2",
           "rwkv_k_k", "rwkv_k_a", "rwkv_r_k", "rwkv_gn_w", "rwkv_gn_b", "fox_q_norm_g", "fox_k_norm_g",
           "fox_f_bias", "w_branch_a", "w_branch_b", "w_o", "norm2_g", "w_gate_up", "w_down")


def _pad_rows(t, rows):
    return jnp.concatenate([t, jnp.zeros((rows - t.shape[0],) + t.shape[1:], t.dtype)], axis=0)


def _pad_cols(t, cols):
    return jnp.concatenate([t, jnp.zeros(t.shape[:-1] + (cols - t.shape[-1],), t.dtype)], axis=-1)


def _step(x, tgt, wts, mom1, mom2):
    seq, D = x.shape
    L = SEQ_ROW0 + seq
    RW = wts["rwkv_w0"].shape[-1]
    DL, AL, GL = wts["rwkv_w2"].shape[0], wts["rwkv_a2"].shape[0], wts["rwkv_g2"].shape[0]
    FW = wts["w_branch_b"].shape[0]
    FH = wts["fox_f_bias"].shape[-1]
    DFF = wts["w_down"].shape[0] * 4
    LORA = DL + AL + GL
    LW = -(-(LORA + FH) // 512) * 512
    assert RW == FW and (6 * RW) % D == 0 and (6 * RW + 2 * D) % LW == 0 and LORA % 8 == 0
    xj = lax.axis_index("x")
    yj = lax.axis_index("y")
    chip = 2 * xj + yj

    send = {n: wts[n] if n == "meta_tokens" else wts[n].astype(BF16) for n in SHARDED}
    early = tuple(n for n in SHARDED if n not in LATE_GATHER)
    full, stacked = {}, {}

    def place_own(n, got):
        shard, mode = send[n], GATHER_MODE[n]
        r, cw = shard.shape
        if mode == "stack":
            got = lax.dynamic_update_index_in_dim(got, shard, chip, 0)
            stacked[n] = got
            full[n] = jnp.concatenate([got[j] for j in range(4)], axis=1)
        elif mode == "cols":
            full[n] = lax.dynamic_update_slice(got, shard, (0, chip * cw))
        else:
            full[n] = lax.dynamic_update_slice(got, shard, (chip * r, 0))

    for n, got in zip(early, _all_gather([send[n] for n in early], [GATHER_MODE[n] for n in early])):
        place_own(n, got)
    late_plan = _gather_plan([send[n] for n in LATE_GATHER], [GATHER_MODE[n] for n in LATE_GATHER])
    meta = full["meta_tokens"]
    w_in = full["w_in"]
    o = 0
    segs = {}
    for nm, wd in (("r", RW), ("k", RW), ("v", RW), ("wd", DL), ("ad", AL), ("gd", GL),
                   ("fq", FW), ("fk", FW), ("fv", FW), ("ff", FH), ("ga", D), ("gb", D)):
        segs[nm] = w_in[:, o:o + wd]
        o += wd
    lora_w = _pad_cols(jnp.concatenate([segs["wd"], segs["ad"], segs["gd"], segs["ff"]], axis=1), LW)
    w1 = jnp.concatenate([segs["r"], segs["k"], segs["v"], segs["fq"], segs["fk"], segs["fv"],
                          segs["ga"], segs["gb"], lora_w], axis=1)
    cb_f = 3
    cb_gate = (6 * RW) // D
    cb_lora = (6 * RW + 2 * D) // LW

    e_m, et_m = _head_mats(RW)
    ft_m = _fold_mat(RW)
    mu = wts["rwkv_mu"]
    mu_rkv = mu[:, :3 * RW]
    mu_l = _pad_cols(mu[:, 3 * RW:], LW)
    w2p = _pad_rows(full["rwkv_w2"].astype(F32), LW)
    a2p = _pad_rows(jnp.concatenate([jnp.zeros((DL, RW), F32), full["rwkv_a2"].astype(F32)], axis=0), LW)
    g2p = _pad_rows(jnp.concatenate([jnp.zeros((DL + AL, RW), F32), full["rwkv_g2"].astype(F32)], axis=0), LW)
    r_k = wts["rwkv_r_k"].reshape(1, RW)
    qg8 = jnp.broadcast_to(wts["fox_q_norm_g"], (8, HEAD))
    kg8 = jnp.broadcast_to(wts["fox_k_norm_g"], (8, HEAD))
    fb = _pad_cols(wts["fox_f_bias"], LANES)
    fmask = (jnp.arange(LANES) < FH).astype(F32).reshape(1, LANES)
    lmask = ((jnp.arange(LW) >= LORA) & (jnp.arange(LW) < LORA + FH)).astype(F32).reshape(1, LW)

    h0 = jnp.concatenate([jnp.zeros((PAD_ROWS, D), F32), meta, x], axis=0)
    n1 = wts["norm1_g"]

    (xn,), _ = _rowcall("rms1_fwd", lambda i, r, b: ([_rms_f(r[0], b[0])], []), L,
                        [(h0, D, 0, "row")], [n1], [(D, BF16)], [])
    proj = _matmul(xn, w1, "nn", F32, "proj_fwd")

    def shift_fn(i, r, b):
        rows = lax.broadcasted_iota(jnp.int32, (ROW_TILE, 1), 0)
        outs = []
        for z, halo, m_ in ((r[0], r[1], b[0]), (r[2], r[3], b[1])):
            first = jnp.where(i == 0, 0.0, halo[7:8, :])
            zp = jnp.where(rows == 0, first, pltpu.roll(z, 1, 0))
            outs.append(z + (zp - z) * m_)
        return outs, []

    rkv_w = 3 * RW
    (x_rkv, x_l), _ = _rowcall(
        "shift_fwd", shift_fn, L,
        [(proj, rkv_w, 0, "row"), (proj, rkv_w, 0, "prev"), (proj, LW, cb_lora, "row"), (proj, LW, cb_lora, "prev")],
        [mu_rkv, mu_l], [(rkv_w, F32), (LW, F32)], [])

    prep_p = [wts["rwkv_w0"], w2p, wts["rwkv_a0"], a2p, g2p, wts["rwkv_k_k"], wts["rwkv_k_a"], e_m, et_m]
    prep_rows = [(x_rkv, RW, 0, "row"), (x_rkv, RW, 1, "row"), (x_rkv, RW, 2, "row"), (x_l, LW, 0, "row")]
    (s_r, s_lw, s_k, s_v, s_a, s_b, gate_g), _ = _rowcall(
        "rwkv_prep_fwd", lambda i, r, b: (list(_prep_f(*r, *b)), []), L, prep_rows, prep_p,
        [(RW, F32)] * 7, [])
    y_scan, s_all = _scan_fwd(s_r, s_lw, s_k, s_v, s_a, s_b)
    post_p = [wts["rwkv_gn_w"], wts["rwkv_gn_b"], r_k, e_m, et_m]
    post_rows = [(y_scan, RW, 0, "row"), (s_r, RW, 0, "row"), (s_k, RW, 0, "row"), (s_v, RW, 0, "row"),
                 (gate_g, RW, 0, "row")]
    (y_a,), _ = _rowcall("rwkv_post_fwd", lambda i, r, b: ([_post_f(*r, *b)], []), L, post_rows, post_p,
                         [(RW, BF16)], [])

    fox_p = [qg8, kg8, fb, e_m, et_m, ft_m, fmask]

    def foxprep_fn(i, r, b):
        fl = _doth(r[2] * b[-1], b[-2])
        return list(_foxprep_f(r[0], r[1], fl, *b[:-2])), []

    sel = (np.arange(LW)[:, None] - LORA == np.arange(LANES)[None, :]).astype(np.float32)
    sel = jnp.asarray(sel)
    fox_rows = [(proj, FW, cb_f, "row"), (proj, FW, cb_f + 1, "row"), (proj, LW, cb_lora, "row")]
    (f_q, f_k, logf), _ = _rowcall("fox_prep_fwd", foxprep_fn, L, fox_rows, fox_p + [sel, lmask],
                                   [(FW, BF16), (FW, BF16), (LANES, F32)], [])
    ct = _cumsum_rows(logf)
    f_v = proj[:, (cb_f + 2) * FW:(cb_f + 3) * FW]
    y_b32, lse, got_late = _attn_fwd(f_q, f_k, f_v, ct, plan=late_plan)
    for n, got in zip(LATE_GATHER, got_late):
        place_own(n, got)
    y_b = y_b32.astype(BF16)

    p_a = _matmul(y_a, full["w_branch_a"], "nn", F32, "branch_a_fwd")
    p_b = _matmul(y_b, full["w_branch_b"], "nn", F32, "branch_b_fwd")
    merge_rows = [(proj, D, cb_gate, "row"), (proj, D, cb_gate + 1, "row"), (p_a, D, 0, "row"), (p_b, D, 0, "row")]
    (merged,), _ = _rowcall("merge_fwd", lambda i, r, b: ([_merge_f(*r)], []), L, merge_rows, [], [(D, BF16)], [])
    h1 = _matmul(merged, full["w_o"], "nn", F32, "wo_fwd", add=h0)
    n2 = wts["norm2_g"]
    (xn2,), _ = _rowcall("rms2_fwd", lambda i, r, b: ([_rms_f(r[0], b[0])], []), L,
                         [(h1, D, 0, "row")], [n2], [(D, BF16)], [])
    gu = _matmul(xn2, full["w_gate_up"], "nn", F32, "gate_up_fwd")
    gu_rows = [(gu, DFF, 0, "row"), (gu, DFF, 1, "row")]
    (act,), _ = _rowcall("swiglu_fwd", lambda i, r, b: ([_swiglu_f(*r)], []), L, gu_rows, [], [(DFF, BF16)], [])
    h2 = _matmul(act, full["w_down"], "nn", F32, "down_fwd", add=h1)

    def loss_fn(i, r, b):
        err = jnp.where(i == 0, 0.0, r[0] - r[1])
        return [err * (1.0 / D)], [jnp.zeros((8, LANES), F32) + 0.5 / D * jnp.sum(err * err)]

    (dh2,), (loss_acc,) = _rowcall("loss", loss_fn, L, [(h2, D, 0, "row"), (tgt, D, 0, "lag")], [],
                                   [(D, F32)], [(8, LANES)])
    loss = lax.psum(loss_acc[0, 0], ("x", "y", "c"))

    dh2b = dh2.astype(BF16)
    g_w_down = _matmul(act, dh2b, "tn", F32, "down_dw")
    d_act = _matmul(dh2b, full["w_down"], "nt", F32, "down_dx")

    def swiglu_bwd(i, r, b):
        _, vjp = jax.vjp(_swiglu_f, r[0], r[1])
        return list(vjp(r[2])), []

    (d_gate, d_up), _ = _rowcall("swiglu_bwd", swiglu_bwd, L, gu_rows + [(d_act, DFF, 0, "row")], [],
                                 [(DFF, BF16), (DFF, BF16)], [])
    d_gu = jnp.concatenate([d_gate, d_up], axis=1)
    g_w_gu = _matmul(xn2, d_gu, "tn", F32, "gate_up_dw", col_blocks=4)
    d_xn2 = _matmul(d_gu, full["w_gate_up"], "nt", F32, "gate_up_dx")

    def rms_bwd(i, r, b):
        _, vjp = jax.vjp(_rms_f, r[0], b[0])
        dh, dg = vjp(r[1])
        return [dh + r[2]], [dg]

    (dh1,), (g_n2,) = _rowcall("rms2_bwd", rms_bwd, L,
                               [(h1, D, 0, "row"), (d_xn2, D, 0, "row"), (dh2, D, 0, "row")], [n2],
                               [(D, F32)], [(1, D)])
    dh1b = dh1.astype(BF16)
    g_w_o = _matmul(merged, dh1b, "tn", F32, "wo_dw")
    d_merged = _matmul(dh1b, full["w_o"], "nt", F32, "wo_dx")

    def merge_bwd(i, r, b):
        _, vjp = jax.vjp(_merge_f, *r[:4])
        return list(vjp(r[4])), []

    (d_za, d_zb, d_pa, d_pb), _ = _rowcall("merge_bwd", merge_bwd, L, merge_rows + [(d_merged, D, 0, "row")], [],
                                           [(D, BF16)] * 4, [])
    g_w_a = _matmul(y_a, d_pa, "tn", F32, "branch_a_dw", col_blocks=4)
    g_w_b = _matmul(y_b, d_pb, "tn", F32, "branch_b_dw", col_blocks=4)
    d_ya = _matmul(d_pa, full["w_branch_a"], "nt", F32, "branch_a_dx")
    d_yb = _matmul(d_pb, full["w_branch_b"], "nt", F32, "branch_b_dx")

    cj = lax.axis_index("c")
    names1 = ("w_gate_up", "w_branch_a", "w_branch_b", "w_o", "w_down")
    blocks1 = [g_w_gu, g_w_a, g_w_b, g_w_o.reshape(4, -1, D), g_w_down.reshape(4, -1, D)]
    parts1 = [_pair_add("reduce_pair_add_" + n, b, r)
              for n, b, r in zip(names1, blocks1, _pair_exchange("reduce_pair_exchange_1", blocks1))]
    d_fk, d_fv, dc_rows, d_fq, recv1 = _attn_bwd(f_q, f_k, f_v, ct, y_b32, lse, d_yb,
                                                 plan=_chip_exchange_plan(parts1))
    dct = _pad_rows(dc_rows[:, :2, :].reshape(-1, L), LANES)
    d_logf = _rcumsum_cols(dct)

    def foxprep_bwd(i, r, b):
        def f(q, k, xl, qg, kg, fbias):
            return _foxprep_f(q, k, _doth(xl * b[-1], b[-2]), qg, kg, fbias, *b[3:7])
        _, vjp = jax.vjp(f, r[0], r[1], r[2], b[0], b[1], b[2])
        dq, dk, dxl, dqg, dkg, dfb = vjp((r[3], r[4], r[5]))
        return [dq, dk, dxl], [dqg, dkg, dfb]

    (d_zfq, d_zfk, d_zl_f), (g_qg8, g_kg8, g_fb) = _rowcall(
        "fox_prep_bwd", foxprep_bwd, L,
        fox_rows + [(d_fq, FW, 0, "row"), (d_fk, FW, 0, "row"), (d_logf, LANES, 0, "row")],
        fox_p + [sel, lmask], [(FW, BF16), (FW, BF16), (LW, F32)], [(8, HEAD), (8, HEAD), (1, LANES)])

    def post_bwd(i, r, b):
        _, vjp = jax.vjp(lambda *a: _post_f(*a, b[3], b[4]), *r[:5], b[0], b[1], b[2])
        g = vjp(r[5])
        return list(g[:5]), list(g[5:])

    (d_y, d_r1, d_k1, d_v1, d_g), (g_gn_w, g_gn_b, g_r_k) = _rowcall(
        "rwkv_post_bwd", post_bwd, L, post_rows + [(d_ya, RW, 0, "row")], post_p,
        [(RW, F32)] * 5, [(1, RW)] * 3)
    d_r2, d_lw, d_k2, d_v2, d_a, d_b = _scan_bwd(s_r, s_lw, s_k, s_v, s_a, s_b, s_all, d_y)

    def prep_bwd(i, r, b):
        _, vjp = jax.vjp(lambda *a: _prep_f(*a, b[7], b[8]), *r[:4], *b[:7])
        cts = (r[4] + r[10], r[5], r[6] + r[11], r[7] + r[12], r[8], r[9], r[13])
        g = vjp(cts)
        return list(g[:4]), list(g[4:])

    bwd_rows = prep_rows + [(d_r2, RW, 0, "row"), (d_lw, RW, 0, "row"), (d_k2, RW, 0, "row"), (d_v2, RW, 0, "row"),
                            (d_a, RW, 0, "row"), (d_b, RW, 0, "row"), (d_r1, RW, 0, "row"), (d_k1, RW, 0, "row"),
                            (d_v1, RW, 0, "row"), (d_g, RW, 0, "row")]
    (d_xr, d_xk, d_xv, d_xl), (g_w0, g_w2p, g_a0, g_a2p, g_g2p, g_kk, g_ka) = _rowcall(
        "rwkv_prep_bwd", prep_bwd, L, bwd_rows, prep_p, [(RW, F32)] * 3 + [(LW, F32)],
        [(1, RW), (LW, RW), (1, RW), (LW, RW), (LW, RW), (1, RW), (1, RW)])

    def shift_bwd(i, r, b):
        last = pl.num_programs(0) - 1
        rows = lax.broadcasted_iota(jnp.int32, (ROW_TILE, 1), 0)
        outs, sums = [], []
        groups = ((r[0], r[1], r[2], r[3], b[0], None), (r[4], r[5], r[6], r[7], b[1], r[8]))
        for d, dnext, z, zhalo, m_, extra in groups:
            nxt = jnp.where(i == last, 0.0, dnext[0:1, :])
            d_up = jnp.where(rows == ROW_TILE - 1, nxt, pltpu.roll(d, ROW_TILE - 1, 0))
            dz = d * (1.0 - m_) + d_up * m_
            if extra is not None:
                dz = dz + extra
            first = jnp.where(i == 0, 0.0, zhalo[7:8, :])
            zp = jnp.where(rows == 0, first, pltpu.roll(z, 1, 0))
            outs.append(dz)
            sums.append(_rowsum(d * (zp - z)))
        return outs, sums

    d_xrkv = jnp.concatenate([d_xr, d_xk, d_xv], axis=1)
    (d_zrkv, d_zl), (g_mu_rkv, g_mu_l) = _rowcall(
        "shift_bwd", shift_bwd, L,
        [(d_xrkv, rkv_w, 0, "row"), (d_xrkv, rkv_w, 0, "next"), (proj, rkv_w, 0, "row"), (proj, rkv_w, 0, "prev"),
         (d_xl, LW, 0, "row"), (d_xl, LW, 0, "next"), (proj, LW, cb_lora, "row"), (proj, LW, cb_lora, "prev"),
         (d_zl_f, LW, 0, "row")],
        [mu_rkv, mu_l], [(rkv_w, BF16), (LW, BF16)], [(1, rkv_w), (1, LW)])

    n_in = stacked["w_in"].shape[2] * 4
    cs = n_in // 4
    cp = -(-cs // LANES) * LANES
    d_ref = jnp.concatenate([d_zrkv, d_zl[:, :LORA], d_zfq, d_zfk, d_fv.astype(BF16), d_zl[:, LORA:LORA + FH],
                             d_za, d_zb], axis=1)
    d_blk = jnp.concatenate([_pad_cols(d_ref[:, j * cs:(j + 1) * cs], cp) for j in range(4)], axis=1)
    w_blk = jnp.concatenate([_pad_cols(stacked["w_in"][j], cp) for j in range(4)], axis=1)
    g_w_in = _matmul(xn, d_blk, "tn", F32, "proj_dw", col_blocks=4)
    d_xn = _matmul(d_blk, w_blk, "nt", F32, "proj_dx")
    (dh0,), (g_n1,) = _rowcall("rms1_bwd", rms_bwd, L,
                               [(h0, D, 0, "row"), (d_xn, D, 0, "row"), (dh1, D, 0, "row")], [n1],
                               [(D, F32)], [(1, D)])
    grad_x = dh0[SEQ_ROW0:]
    g_meta = dh0[PAD_ROWS:SEQ_ROW0]

    tiny = {"rwkv_w2": g_w2p[:DL], "rwkv_a2": g_a2p[DL:DL + AL], "rwkv_g2": g_g2p[DL + AL:LORA],
            "meta_tokens": g_meta}
    g_mu = jnp.concatenate([g_mu_rkv, g_mu_l[:, :LORA]], axis=1)
    gsmall = {
        "norm1_g": g_n1, "rwkv_mu": g_mu, "rwkv_w0": g_w0, "rwkv_a0": g_a0, "rwkv_k_k": g_kk, "rwkv_k_a": g_ka,
        "rwkv_r_k": g_r_k.reshape(wts["rwkv_r_k"].shape), "rwkv_gn_w": g_gn_w, "rwkv_gn_b": g_gn_b,
        "fox_q_norm_g": g_qg8[0:1], "fox_k_norm_g": g_kg8[0:1], "fox_f_bias": g_fb[:, :FH], "norm2_g": g_n2,
    }
    small_flat = jnp.concatenate([gsmall[n].reshape(-1) for n in SMALL])

    tiny_names = tuple(tiny)

    def tiny_block(j):
        parts = []
        for n in tiny_names:
            w = tiny[n].shape[1] // 4
            parts.append(tiny[n][:, j * w:(j + 1) * w])
        return _pack(parts + [small_flat], F32, 32)

    names2 = ("w_in", "small")
    blocks2 = [g_w_in, jnp.stack([tiny_block(j) for j in range(4)])]
    parts2 = [_pair_add("reduce_pair_add_" + n, b, r)
              for n, b, r in zip(names2, blocks2, _pair_exchange("reduce_pair_exchange_2", blocks2))]
    recv2 = _chip_exchange(parts2)
    names = names1 + names2
    tots = []
    for n, p, r in zip(names, parts1 + parts2, list(recv1) + list(recv2)):
        own = lax.dynamic_index_in_dim(p, chip, 0, keepdims=False)
        tots.append(_chip_add("reduce_chip_add_" + n, lax.dynamic_update_index_in_dim(r, own, chip, 0)))
    others = _pair_join(tots)
    red = [jnp.where(cj == 0, jnp.concatenate([t, o_], axis=0), jnp.concatenate([o_, t], axis=0))
           for t, o_ in zip(tots, others)]
    grads = {n: red[i] for i, n in enumerate(names[:-1])}
    grads["w_in"] = grads["w_in"][:, :cs]
    tiny_shapes = [wts[n].shape for n in tiny_names]
    got = _unpack(red[-1].reshape(-1), tiny_shapes + [small_flat.shape])
    for n, t in zip(tiny_names, got):
        grads[n] = t
    for n, t in zip(SMALL, _unpack(got[-1], [wts[n].shape for n in SMALL])):
        grads[n] = t

    delta, new_m, new_v = {}, {}, {}
    for n in SHARDED:
        delta[n], new_m[n], new_v[n] = _adamw("adamw_" + n, wts[n], grads[n], mom1[n], mom2[n])
    pk = lambda d: _pack([d[n] for n in SMALL], F32, 8)
    ds, ms, vs = _adamw("adamw_small", pk(wts), pk(grads), pk(mom1), pk(mom2))
    small_shapes = [wts[n].shape for n in SMALL]
    for dst, src in ((delta, ds), (new_m, ms), (new_v, vs)):
        for n, t in zip(SMALL, _unpack(src.reshape(-1), small_shapes)):
            dst[n] = t
    return loss, grad_x, grads, delta, new_m, new_v


def kernel(x, meta_tokens, norm1_g, w_in, rwkv_mu, rwkv_w0, rwkv_w2, rwkv_a0, rwkv_a2, rwkv_g2, rwkv_k_k, rwkv_k_a, rwkv_r_k, rwkv_gn_w, rwkv_gn_b, fox_q_norm_g, fox_k_norm_g, fox_f_bias, w_branch_a, w_branch_b, w_o, norm2_g, w_gate_up, w_down, loss_target, m_meta_tokens, m_norm1_g, m_w_in, m_rwkv_mu, m_rwkv_w0, m_rwkv_w2, m_rwkv_a0, m_rwkv_a2, m_rwkv_g2, m_rwkv_k_k, m_rwkv_k_a, m_rwkv_r_k, m_rwkv_gn_w, m_rwkv_gn_b, m_fox_q_norm_g, m_fox_k_norm_g, m_fox_f_bias, m_w_branch_a, m_w_branch_b, m_w_o, m_norm2_g, m_w_gate_up, m_w_down, v_meta_tokens, v_norm1_g, v_w_in, v_rwkv_mu, v_rwkv_w0, v_rwkv_w2, v_rwkv_a0, v_rwkv_a2, v_rwkv_g2, v_rwkv_k_k, v_rwkv_k_a, v_rwkv_r_k, v_rwkv_gn_w, v_rwkv_gn_b, v_fox_q_norm_g, v_fox_k_norm_g, v_fox_f_bias, v_w_branch_a, v_w_branch_b, v_w_o, v_norm2_g, v_w_gate_up, v_w_down):
    args = dict(locals())
    shapes = {n: args[n].shape for n in WEIGHTS}

    def drop_depth(t, n):
        if n == "meta_tokens":
            return t
        if n == "rwkv_r_k":
            return t.reshape(1, -1)
        return t.reshape(t.shape[1:]) if t.ndim == 3 else t

    wts = {n: drop_depth(args[n], n) for n in WEIGHTS}
    mom1 = {n: drop_depth(args["m_" + n], n) for n in WEIGHTS}
    mom2 = {n: drop_depth(args["v_" + n], n) for n in WEIGHTS}
    loss, grad_x, grads, delta, new_m, new_v = _step(x[0], loss_target[0], wts, mom1, mom2)
    outs = [loss, grad_x[None]]
    for d in (grads, delta, new_m, new_v):
        outs += [d[n].reshape(shapes[n]) for n in WEIGHTS]
    return tuple(outs)
```

```python
import functools

import jax
import jax.numpy as jnp
import numpy as np
from jax import lax
from jax.experimental import pallas as pl
from jax.experimental.pallas import tpu as pltpu

F32 = jnp.float32
BF16 = jnp.bfloat16
MESH = pl.DeviceIdType.MESH
ANY = pl.BlockSpec(memory_space=pl.ANY)

N_META = 16
HEAD = 64
ROW_TILE = 128
PAD_ROWS = ROW_TILE - N_META
SEQ_ROW0 = ROW_TILE
CHUNK = 64
LANES = 128
PACK_W = 1024
RMS_EPS = 1e-6
GN_EPS = 64e-5
ATTN_SCALE = HEAD ** -0.5
NEG = -1e30
VMEM_LIMIT_V7X = 56 * 1024 * 1024

ADAM_LR = 0.001
ADAM_B1 = 0.9
ADAM_B2 = 0.999
ADAM_EPS = 1e-08
ADAM_WD = 0.01
ADAM_STEP = 10


def _params(sem=None):
    return pltpu.CompilerParams(dimension_semantics=sem, vmem_limit_bytes=VMEM_LIMIT_V7X)


def _pick(n, cands):
    for c in cands:
        if n % c == 0:
            return c
    return n


def _bf(t):
    return t.astype(BF16)


def _dotb(a, b):
    return jnp.dot(_bf(a), _bf(b), preferred_element_type=F32)


def _split3(x):
    x1 = x.astype(BF16)
    r1 = x - x1.astype(F32)
    x2 = r1.astype(BF16)
    return x1, x2, (r1 - x2.astype(F32)).astype(BF16)


def _mme_raw(x, e, ce):
    eb = e.astype(BF16)
    dot = lambda q: lax.dot_general(q, eb, (((1,), (ce,)), ((), ())), preferred_element_type=F32)
    x1, x2, x3 = _split3(x)
    return dot(x1) + (dot(x2) + dot(x3))


@jax.custom_vjp
def _doth(x, e):
    return _mme_raw(x, e, 0)


def _doth_fwd(x, e):
    return _mme_raw(x, e, 0), e


def _doth_bwd(e, ct):
    return _mme_raw(ct, e, 1), jnp.zeros_like(e)


_doth.defvjp(_doth_fwd, _doth_bwd)


_BIG = (2048, 1536, 1408, 1024, 768, 704, 512, 384, 256, 128)


def _matmul(a, b, mode, out_dtype, name, add=None, col_blocks=1, plan=None):
    if mode == "nn":
        (M, R), (_, N) = a.shape, b.shape
        dims = (((1,), (0,)), ((), ()))
    elif mode == "nt":
        (M, R), (N, _) = a.shape, b.shape
        dims = (((1,), (1,)), ((), ()))
    else:
        (R, M), (_, N) = a.shape, b.shape
        dims = (((0,), (0,)), ((), ()))
    tm = _pick(M, (1408, 1024, 768, 512, 384, 256, 128))
    nb = N // col_blocks
    tn = _pick(nb, (1408, 1024, 896, 768, 704, 512, 384, 256, 128)) if mode == "tn" else _pick(nb, (512, 384, 256, 128))
    per = nb // tn
    tr = _pick(R, (1408, 1056, 768, 512, 384, 256, 128)) if mode == "tn" else _pick(R, _BIG)
    nr = R // tr

    if mode == "nn":
        a_spec = pl.BlockSpec((tm, tr), lambda i, j, r: (i, r))
        b_spec = pl.BlockSpec((tr, tn), lambda i, j, r: (r, j))
    elif mode == "nt":
        a_spec = pl.BlockSpec((tm, tr), lambda i, j, r: (i, r))
        b_spec = pl.BlockSpec((tn, tr), lambda i, j, r: (j, r))
    else:
        a_spec = pl.BlockSpec((tr, tm), lambda i, j, r: (r, i))
        b_spec = pl.BlockSpec((tr, tn), lambda i, j, r: (r, j))
    if col_blocks == 1:
        o_spec = pl.BlockSpec((tm, tn), lambda i, j, r: (i, j))
        o_shape = (M, N)
    else:
        o_spec = pl.BlockSpec((1, tm, tn), lambda i, j, r: (j // per, i, j % per))
        o_shape = (col_blocks, M, nb)
    has_add = add is not None

    grid = (M // tm, N // tn, nr)

    def body(*refs):
        own_in, p_in, (o_ref,), p_out, (acc,), sems = _carried(plan, refs, 3 if has_add else 2, 1, 1)
        a_ref, b_ref = own_in[:2]
        i, j, r = pl.program_id(0), pl.program_id(1), pl.program_id(2)
        if plan:
            @pl.when((i == 0) & (j == 0) & (r == 0))
            def _():
                plan["start"](p_in, p_out, *sems)

        @pl.when(r == 0)
        def _():
            acc[...] = jnp.zeros_like(acc)

        acc[...] += lax.dot_general(_bf(a_ref[...]), _bf(b_ref[...]), dims, preferred_element_type=F32)

        @pl.when(r == nr - 1)
        def _():
            res = acc[...]
            if has_add:
                res = res + own_in[2][...]
            o_ref[...] = res.astype(o_ref.dtype).reshape(o_ref.shape)

        if plan:
            @pl.when((i == grid[0] - 1) & (j == grid[1] - 1) & (r == nr - 1))
            def _():
                plan["wait"](p_in, p_out, *sems)

    ins = [a, b] + ([add] if has_add else [])
    specs = [a_spec, b_spec] + ([o_spec] if has_add else [])
    extra = plan["arrays"] if plan else []
    extra_out = plan["out_shapes"] if plan else []
    res = pl.pallas_call(
        body, name=name, out_shape=[jax.ShapeDtypeStruct(o_shape, out_dtype)] + extra_out,
        grid=grid, in_specs=specs + [ANY] * len(extra), out_specs=[o_spec] + [ANY] * len(extra_out),
        scratch_shapes=[pltpu.VMEM((tm, tn), F32)] + (_dma_sems(plan["nsem"]) if plan else []),
        compiler_params=_params(("arbitrary",) * 3 if plan else ("parallel", "parallel", "arbitrary")),
    )(*ins, *extra)
    return (res[0], list(res[1:])) if plan else res[0]


def _rowcall(name, fn, L, row_ins, bc_ins, row_outs, acc_outs, tm=ROW_TILE):
    nt = L // tm
    specs = []
    for arr, w, cb, kind in row_ins:
        if kind == "row":
            specs.append(pl.BlockSpec((tm, w), lambda i, cb=cb: (i, cb)))
        elif kind == "lag":
            specs.append(pl.BlockSpec((tm, w), lambda i, cb=cb: (jnp.maximum(i - 1, 0), cb)))
        elif kind == "prev":
            specs.append(pl.BlockSpec((8, w), lambda i, cb=cb: (jnp.maximum(i * (tm // 8) - 1, 0), cb)))
        else:
            specs.append(pl.BlockSpec((8, w), lambda i, cb=cb: (jnp.minimum((i + 1) * (tm // 8), L // 8 - 1), cb)))
    for arr in bc_ins:
        specs.append(pl.BlockSpec(arr.shape, lambda i, nd=arr.ndim: (0,) * nd))
    out_shapes = [jax.ShapeDtypeStruct((L, w), dt) for w, dt in row_outs]
    out_specs = [pl.BlockSpec((tm, w), lambda i: (i, 0)) for w, dt in row_outs]
    out_shapes += [jax.ShapeDtypeStruct(s, F32) for s in acc_outs]
    out_specs += [pl.BlockSpec(s, lambda i, nd=len(s): (0,) * nd) for s in acc_outs]
    n_row, n_bc, n_ro = len(row_ins), len(bc_ins), len(row_outs)

    def body(*refs):
        i = pl.program_id(0)
        vals = [r[...] for r in refs[: n_row + n_bc]]
        outs, sums = fn(i, vals[:n_row], vals[n_row:])
        o_refs = refs[n_row + n_bc:]
        for r, v in zip(o_refs[:n_ro], outs):
            r[...] = v.astype(r.dtype)

        @pl.when(i == 0)
        def _():
            for r in o_refs[n_ro:]:
                r[...] = jnp.zeros_like(r)

        for r, v in zip(o_refs[n_ro:], sums):
            r[...] += v

    res = pl.pallas_call(
        body, name=name, out_shape=out_shapes, grid=(nt,), in_specs=specs, out_specs=out_specs,
        compiler_params=_params(("arbitrary",)),
    )(*[a for a, _, _, _ in row_ins], *bc_ins)
    return list(res[:n_ro]), list(res[n_ro:])


def _rowsum(t):
    return jnp.sum(t, axis=0, keepdims=True)


def _head_mats(width):
    e = (np.arange(width)[:, None] // HEAD == np.arange(LANES)[None, :]).astype(np.float32)
    return jnp.asarray(e), jnp.asarray(e.T)


def _fold_mat(width):
    ft = (np.arange(HEAD)[:, None] == np.arange(width)[None, :] % HEAD).astype(np.float32)
    return jnp.asarray(ft)


def _rms_f(h, g):
    return (h * lax.rsqrt(jnp.mean(h * h, axis=-1, keepdims=True) + RMS_EPS)) * g


def _prep_f(xr, xk, xv, xl, w0, w2p, a0, a2p, g2p, k_k, k_a, e, et):
    w_log = -jax.nn.softplus(-(w0 + _dotb(jnp.tanh(xl), w2p))) - 0.5
    lw = -jnp.exp(w_log)
    a = jax.nn.sigmoid(a0 + _dotb(xl, a2p))
    g = _dotb(jax.nn.sigmoid(xl), g2p)
    kkr = xk * k_k
    inv = lax.rsqrt(jnp.maximum(_doth(kkr * kkr, e), 1e-24))
    kk = kkr * _doth(inv, et)
    kf = xk * (1.0 + (a - 1.0) * k_a)
    return xr, lw, kf, xv, -kk, kk * a, g


def _post_f(y, r, kf, v, g, gn_w, gn_b, r_k, e, et):
    mu = _doth(y, e) * (1.0 / HEAD)
    yc = y - _doth(mu, et)
    var = _doth(yc * yc, e) * (1.0 / HEAD)
    yn = yc * _doth(lax.rsqrt(var + GN_EPS), et) * gn_w + gn_b
    bonus = _doth(r * kf * r_k, e)
    return (yn + _doth(bonus, et) * v) * g


def _foxprep_f(q, k, fl, qg8, kg8, fb, e, et, ft, fmask):
    def norm(t, g8):
        ms = _doth(t * t, e) * (1.0 / HEAD)
        return t * _doth(lax.rsqrt(ms + RMS_EPS), et) * _doth(g8, ft)[0:1]
    logf = jax.nn.log_sigmoid(fl + fb) * fmask
    return norm(q, qg8), norm(k, kg8), logf


def _merge_f(za, zb, pa, pb):
    return jax.nn.sigmoid(za) * pa + jax.nn.sigmoid(zb) * pb


def _swiglu_f(gate, up):
    return jax.nn.silu(gate) * up


def _tri(n, strict):
    row = lax.broadcasted_iota(jnp.int32, (n, n), 0)
    col = lax.broadcasted_iota(jnp.int32, (n, n), 1)
    return (row > col) if strict else (row >= col)


def _split2(x):
    hi = x.astype(BF16)
    return hi, (x - hi.astype(F32)).astype(BF16)


def _mm3_raw(a, b, ca, cb):
    dn = (((ca,), (cb,)), ((), ()))
    ah, al = _split2(a)
    bh, bl = _split2(b)
    dot = lambda p, q: lax.dot_general(p, q, dn, preferred_element_type=F32)
    return dot(ah, bh) + (dot(al, bh) + dot(ah, bl))


@functools.partial(jax.custom_vjp, nondiff_argnums=(2, 3))
def _mm3(a, b, ca, cb):
    return _mm3_raw(a, b, ca, cb)


def _mm3_fwd(a, b, ca, cb):
    return _mm3_raw(a, b, ca, cb), (a, b)


def _mm3_bwd(ca, cb, res, ct):
    a, b = res
    da = _mm3_raw(ct, b, 1, 1 - cb) if ca == 1 else _mm3_raw(b, ct, 1 - cb, 1)
    db = _mm3_raw(a, ct, 1 - ca, 0) if cb == 0 else _mm3_raw(ct, a, 0, 1 - ca)
    return da, db


_mm3.defvjp(_mm3_fwd, _mm3_bwd)


def _mmx_raw(t, x, ct):
    x1 = x.astype(BF16)
    r1 = x - x1.astype(F32)
    x2 = r1.astype(BF16)
    x3 = (r1 - x2.astype(F32)).astype(BF16)
    tb = t.astype(BF16)
    dot = lambda q: lax.dot_general(tb, q, (((ct,), (0,)), ((), ())), preferred_element_type=F32)
    return dot(x1) + (dot(x2) + dot(x3))


@jax.custom_vjp
def _mmx(t, x):
    return _mmx_raw(t, x, 1)


def _mmx_fwd(t, x):
    return _mmx_raw(t, x, 1), t


def _mmx_bwd(t, ct):
    return jnp.zeros_like(t), _mmx_raw(t, ct, 0)


_mmx.defvjp(_mmx_fwd, _mmx_bwd)

SCAN_HEADS = 16


def _scan_step(r, lw, k, v, a, b, st):
    c = r.shape[0]
    nh = r.shape[1] // HEAD
    incl = _tri(c, False)
    row2 = lax.broadcasted_iota(jnp.int32, (2 * c, 2 * c), 0)
    col2 = lax.broadcasted_iota(jnp.int32, (2 * c, 2 * c), 1)
    t_row = jnp.where(row2 >= c, row2 - c, row2)
    t_col = jnp.where(col2 >= c, col2 - c, col2)
    mask2 = (t_row > t_col) | ((row2 >= c) & (t_row == t_col))
    right =lax.broadcasted_iota(jnp.int32, (c, 2 * c), 1) >= c
    eye = lax.broadcasted_iota(jnp.int32, (HEAD, HEAD), 0) == lax.broadcasted_iota(jnp.int32, (HEAD, HEAD), 1)
    cl = _mmx(incl.astype(F32), lw)
    last = cl[c - 1:c, :]
    rt = r * jnp.exp(cl)
    at = a * jnp.exp(cl - lw)
    pinv = jnp.exp(-cl)
    bt = b * pinv
    kt = k * pinv
    pend = jnp.exp(last - cl)
    bl = b * pend
    kl = k * pend
    pe_last = jnp.exp(last)
    hs = range(nh)
    sl = [slice(h * HEAD, (h + 1) * HEAD) for h in hs]
    ar = [jnp.concatenate([at[:, sl[h]], rt[:, sl[h]]], axis=0) for h in hs]
    bk = [jnp.concatenate([bt[:, sl[h]], kt[:, sl[h]]], axis=0) for h in hs]
    amat = [jnp.where(mask2, _mm3(ar[h], bk[h], 1, 1), 0.0) for h in hs]
    res = [_mm3(jnp.concatenate([ar[h], amat[h][:, c:]], axis=1),
                jnp.concatenate([st[h], v[:, sl[h]]], axis=0), 1, 0) for h in hs]
    z = [jnp.concatenate([amat[h][:c, :c], res[h][:c]], axis=1) for h in hs]
    for _ in range(max(1, int(np.ceil(np.log2(c))))):
        z = [_mm3(z[h][:, :c], z[h], 1, 0) + jnp.where(right, z[h], 0.0) for h in hs]
    u = [z[h][:, c:] for h in hs]
    ys = [res[h][c:] + _mm3(amat[h][c:, :c], u[h], 1, 0) for h in hs]
    s1s = [_mm3(jnp.concatenate([bl[:, sl[h]], kl[:, sl[h]], jnp.where(eye, pe_last[:, sl[h]], 0.0)], axis=0),
                jnp.concatenate([u[h], v[:, sl[h]], st[h]], axis=0), 0, 0) for h in hs]
    return tuple(ys), tuple(s1s)


def _scan_heads(W):
    return SCAN_HEADS if W % (SCAN_HEADS * HEAD) == 0 else 2


def _scan_fwd(r, lw, k, v, a, b):
    L, W = r.shape
    nh = _scan_heads(W)
    nc, ng = L // CHUNK, W // (nh * HEAD)
    spec = pl.BlockSpec((CHUNK, nh * HEAD), lambda p, c: (c, p))

    def body(r_ref, lw_ref, k_ref, v_ref, a_ref, b_ref, y_ref, s_ref, st):
        @pl.when(pl.program_id(1) == 0)
        def _():
            st[...] = jnp.zeros_like(st)

        s0 = st[...]
        ys, s1s = _scan_step(r_ref[...], lw_ref[...], k_ref[...], v_ref[...], a_ref[...], b_ref[...], s0)
        s_ref[0] = s0
        st[...] = jnp.stack(s1s)
        y_ref[...] = jnp.concatenate(ys, axis=1)

    return pl.pallas_call(
        body, name="wkv7_fwd",
        out_shape=[jax.ShapeDtypeStruct((L, W), F32), jax.ShapeDtypeStruct((nc, nh * ng, HEAD, HEAD), F32)],
        grid=(ng, nc), in_specs=[spec] * 6,
        out_specs=[spec, pl.BlockSpec((1, nh, HEAD, HEAD), lambda p, c: (c, p, 0, 0))],
        scratch_shapes=[pltpu.VMEM((nh, HEAD, HEAD), F32)],
        compiler_params=_params(("parallel", "arbitrary")),
    )(r, lw, k, v, a, b)


def _scan_bwd(r, lw, k, v, a, b, s_all, dy):
    L, W = r.shape
    nh = _scan_heads(W)
    nc, ng = L // CHUNK, W // (nh * HEAD)
    spec = pl.BlockSpec((CHUNK, nh * HEAD), lambda p, c: (nc - 1 - c, p))

    def body(r_ref, lw_ref, k_ref, v_ref, a_ref, b_ref, s_ref, dy_ref,
             dr_ref, dlw_ref, dk_ref, dv_ref, da_ref, db_ref, dst):
        @pl.when(pl.program_id(1) == 0)
        def _():
            dst[...] = jnp.zeros_like(dst)

        _, vjp = jax.vjp(_scan_step, r_ref[...], lw_ref[...], k_ref[...], v_ref[...], a_ref[...], b_ref[...],
                         s_ref[0])
        dys = tuple(dy_ref[:, h * HEAD:(h + 1) * HEAD] for h in range(nh))
        g = vjp((dys, tuple(dst[h] for h in range(nh))))
        for ref, val in zip((dr_ref, dlw_ref, dk_ref, dv_ref, da_ref, db_ref), g[:6]):
            ref[...] = val
        dst[...] = g[6]

    return pl.pallas_call(
        body, name="wkv7_bwd", out_shape=[jax.ShapeDtypeStruct((L, W), F32)] * 6,
        grid=(ng, nc),
        in_specs=[spec] * 6 + [pl.BlockSpec((1, nh, HEAD, HEAD), lambda p, c: (nc - 1 - c, p, 0, 0)), spec],
        out_specs=[spec] * 6,
        scratch_shapes=[pltpu.VMEM((nh, HEAD, HEAD), F32)],
        compiler_params=_params(("parallel", "arbitrary")),
    )(r, lw, k, v, a, b, s_all, dy)


def _cumsum_rows(logf):
    L = logf.shape[0]
    t = LANES

    def body(x_ref, o_ref, carry):
        @pl.when(pl.program_id(0) == 0)
        def _():
            carry[...] = jnp.zeros_like(carry)

        c = _mmx(_tri(t, False).astype(F32), x_ref[...]) + carry[...]
        carry[...] = c[t - 1:t, :]
        o_ref[...] = c.T

    return pl.pallas_call(
        body, name="fox_cumsum", out_shape=jax.ShapeDtypeStruct((LANES, L), F32), grid=(L // t,),
        in_specs=[pl.BlockSpec((t, LANES), lambda i: (i, 0))],
        out_specs=pl.BlockSpec((LANES, t), lambda i: (0, i)),
        scratch_shapes=[pltpu.VMEM((1, LANES), F32)], compiler_params=_params(("arbitrary",)),
    )(logf)


def _rcumsum_cols(dct):
    L = dct.shape[1]
    t = LANES
    n = L // t

    def body(x_ref, o_ref, carry):
        @pl.when(pl.program_id(0) == 0)
        def _():
            carry[...] = jnp.zeros_like(carry)

        rc = _doth(x_ref[...], _tri(t, False).astype(F32)) + carry[...]
        carry[...] = rc[:, 0:1]
        o_ref[...] = rc.T

    return pl.pallas_call(
        body, name="fox_rcumsum", out_shape=jax.ShapeDtypeStruct((L, LANES), F32), grid=(n,),
        in_specs=[pl.BlockSpec((LANES, t), lambda i: (0, n - 1 - i))],
        out_specs=pl.BlockSpec((t, LANES), lambda i: (n - 1 - i, 0)),
        scratch_shapes=[pltpu.VMEM((LANES, 1), F32)], compiler_params=_params(("arbitrary",)),
    )(dct)


def _attn_tile(L):
    return _pick(L, (384, 256, 128))


def _head_lanes(hh, shape):
    return lax.broadcasted_iota(jnp.int32, shape, len(shape) - 1) // HEAD == hh


def _own(hh, block, other=0):
    return jnp.where(_head_lanes(hh, block.shape), block, jnp.asarray(other, block.dtype))


def _attn_scores(q, k, ck, qi, kj, t):
    s = _dot_bnt(q, k) * ATTN_SCALE - ck
    qpos = qi * t + lax.broadcasted_iota(jnp.int32, (t, t), 0)
    kpos = kj * t + lax.broadcasted_iota(jnp.int32, (t, t), 1)
    mask = (kpos <= qpos) & (kpos >= PAD_ROWS)
    return jnp.where(mask, s, NEG), mask


def _dot_bnt(a, b):
    return lax.dot_general(_bf(a), _bf(b), (((1,), (1,)), ((), ())), preferred_element_type=F32)


def _dot_btn(a, b):
    return lax.dot_general(_bf(a), _bf(b), (((0,), (0,)), ((), ())), preferred_element_type=F32)


def _ck_rows(ct_ref, p):
    r0 = 2 * (p % 4)
    return ct_ref[pl.ds(r0, 1), :], ct_ref[pl.ds(r0 + 1, 1), :]


def _carried(plan, refs, n_in, n_out, n_scratch):
    ci = len(plan["arrays"]) if plan else 0
    co = len(plan["out_shapes"]) if plan else 0
    a = n_in + ci
    b = a + n_out + co
    return (refs[:n_in], refs[n_in:a], refs[a:a + n_out], refs[a + n_out:b], refs[b:b + n_scratch],
            refs[b + n_scratch:])


def _attn_fwd(q, k, v, ct, plan=None):
    L, W = q.shape
    t = _attn_tile(L)
    nt, npair = L // t, W // LANES
    qspec = pl.BlockSpec((t, LANES), lambda p, i, j: (i, p))
    kspec = pl.BlockSpec((t, LANES), lambda p, i, j: (jnp.minimum(i, j), p))
    cspec = pl.BlockSpec((8, t), lambda p, i, j: (p // 4, jnp.minimum(i, j)))

    def body(*refs):
        (q_ref, k_ref, v_ref, ct_ref), p_in, (o_ref, lse_ref), p_out, (m_s, acc), sems = _carried(plan, refs, 4, 2, 2)
        p, i, j = pl.program_id(0), pl.program_id(1), pl.program_id(2)
        if plan:
            @pl.when((p == 0) & (i == 0) & (j == 0))
            def _():
                plan["start"](p_in, p_out, *sems)

            if plan["mid"] is not None:
                @pl.when((p == npair // 2) & (i == 0) & (j == 0))
                def _():
                    plan["mid"](p_in, p_out, *sems)

        @pl.when(j == 0)
        def _():
            m_s[...] = jnp.full_like(m_s, NEG)
            acc[...] = jnp.zeros_like(acc)

        @pl.when(j <= i)
        def _():
            cks = _ck_rows(ct_ref, p)
            qb, kb, vf = _bf(q_ref[...]), _bf(k_ref[...]), v_ref[...]
            ss = [_attn_scores(_own(hh, qb), kb, cks[hh], i, j, t)[0] for hh in range(2)]
            prs, alphas = [], []
            for hh in range(2):
                m_old = m_s[hh]
                m_new = jnp.maximum(m_old, jnp.max(ss[hh], axis=-1, keepdims=True))
                alphas.append(jnp.exp(m_old - m_new))
                prs.append(jnp.exp(ss[hh] - m_new))
                m_s[hh] = m_new
            pvs = []
            for hh in range(2):
                p_hi, p_lo = _split2(prs[hh])
                vx = _bf(_own(hh, vf, 1.0))
                pvs.append(jnp.dot(p_hi, vx, preferred_element_type=F32)
                           + jnp.dot(p_lo, vx, preferred_element_type=F32))
            for hh in range(2):
                acc[hh] = alphas[hh] * acc[hh] + pvs[hh]

        @pl.when(j == i)
        def _():
            lane = lax.broadcasted_iota(jnp.int32, (t, LANES), 1)
            lse = jnp.zeros((t, LANES), F32)
            out = jnp.zeros((t, LANES), F32)
            for hh in range(2):
                a = acc[hh]
                row_sum = pltpu.roll(a, HEAD, 1)
                out = jnp.where(_head_lanes(hh, a.shape), a / row_sum, out)
                l_col = a[:, HEAD:HEAD + 1] if hh == 0 else a[:, 0:1]
                lse = jnp.where(lane == hh, m_s[hh] + jnp.log(l_col), lse)
            o_ref[...] = out
            lse_ref[0] = lse

        if plan:
            @pl.when((p == npair - 1) & (i == nt - 1) & (j == nt - 1))
            def _():
                plan["wait"](p_in, p_out, *sems)

    extra = plan["arrays"] if plan else []
    extra_out = plan["out_shapes"] if plan else []
    res = pl.pallas_call(
        body, name="fox_attn_fwd",
        out_shape=[jax.ShapeDtypeStruct((L, W), F32), jax.ShapeDtypeStruct((npair, L, LANES), F32)] + extra_out,
        grid=(npair, nt, nt), in_specs=[qspec, kspec, kspec, cspec] + [ANY] * len(extra),
        out_specs=[qspec, pl.BlockSpec((1, t, LANES), lambda p, i, j: (p, i, 0))] + [ANY] * len(extra_out),
        scratch_shapes=[pltpu.VMEM((2, t, 1), F32), pltpu.VMEM((2, t, LANES), F32)]
        + (_dma_sems(plan["nsem"]) if plan else []),
        compiler_params=_params(("arbitrary",) * 3 if plan else ("parallel", "parallel", "arbitrary")),
    )(q, k, v, ct, *extra)
    return res[0], res[1], list(res[2:])


def _attn_bwd(q, k, v, ct, o, lse, do, plan=None):
    L, W = q.shape
    t = _attn_tile(L)
    nt, npair = L // t, W // LANES
    kspec = pl.BlockSpec((t, LANES), lambda p, j, i: (j, p))
    qspec = pl.BlockSpec((t, LANES), lambda p, j, i: (jnp.maximum(i, j), p))
    cspec = pl.BlockSpec((8, t), lambda p, j, i: (p // 4, j))
    lspec = pl.BlockSpec((1, t, LANES), lambda p, j, i: (p, jnp.maximum(i, j), 0))

    def body(*refs):
        ((q_ref, k_ref, v_ref, ct_ref, o_ref, lse_ref, do_ref), p_in, (dk_ref, dv_ref, dc_ref, dq_ref), p_out,
         (dk_s, dv_s, dc_s), sems) = _carried(plan, refs, 7, 4, 3)
        p, j, i = pl.program_id(0), pl.program_id(1), pl.program_id(2)
        if plan:
            @pl.when((p == 0) & (i == 0) & (j == 0))
            def _():
                plan["start"](p_in, p_out, *sems)

        @pl.when(i == 0)
        def _():
            dk_s[...] = jnp.zeros_like(dk_s)
            dv_s[...] = jnp.zeros_like(dv_s)
            dc_s[...] = jnp.zeros_like(dc_s)

        def tile_dq():
            cks = _ck_rows(ct_ref, p)
            qb, kb, vb, dob = _bf(q_ref[...]), _bf(k_ref[...]), _bf(v_ref[...]), _bf(do_ref[...])
            of = o_ref[...]
            hs = range(2)
            qm = [_own(hh, qb) for hh in hs]
            dom = [_own(hh, dob) for hh in hs]
            sm = [_attn_scores(qm[hh], kb, cks[hh], i, j, t) for hh in hs]
            dps = [_dot_bnt(dom[hh], vb) for hh in hs]
            prs = [jnp.where(sm[hh][1], jnp.exp(sm[hh][0] - lse_ref[0, :, hh:hh + 1]), 0.0) for hh in hs]
            dss = [prs[hh] * (dps[hh] - jnp.sum(dom[hh].astype(F32) * of, axis=-1, keepdims=True)) for hh in hs]
            dv_s[...] += _dot_btn(prs[0], dom[0]) + _dot_btn(prs[1], dom[1])
            dk_s[...] += _dot_btn(dss[0], qm[0]) + _dot_btn(dss[1], qm[1])
            for hh in hs:
                dc_s[hh] += -jnp.sum(dss[hh], axis=0, keepdims=True)
            return (_dotb(dss[0], _own(0, kb)) + _dotb(dss[1], _own(1, kb))) * ATTN_SCALE

        rows = pl.ds(pl.multiple_of(i * t, t), t)

        @pl.when((i >= j) & (j == 0))
        def _():
            dq_ref[rows, :] = tile_dq()

        @pl.when((i >= j) & (j > 0))
        def _():
            dq_ref[rows, :] += tile_dq()

        @pl.when(i == nt - 1)
        def _():
            row = lax.broadcasted_iota(jnp.int32, (8, t), 0)
            dc = jnp.zeros((8, t), F32)
            for hh in range(2):
                dc = jnp.where(row == hh, dc_s[hh], dc)
            dk_ref[...] = dk_s[...] * ATTN_SCALE
            dv_ref[...] = dv_s[...]
            dc_ref[0] = dc

        if plan:
            @pl.when((p == npair - 1) & (i == nt - 1) & (j == nt - 1))
            def _():
                plan["wait"](p_in, p_out, *sems)

    extra = plan["arrays"] if plan else []
    extra_out = plan["out_shapes"] if plan else []
    res = pl.pallas_call(
        body, name="fox_attn_bwd",
        out_shape=[jax.ShapeDtypeStruct((L, W), F32), jax.ShapeDtypeStruct((L, W), F32),
                   jax.ShapeDtypeStruct((npair, 8, L), F32), jax.ShapeDtypeStruct((L, W), F32)] + extra_out,
        grid=(npair, nt, nt), in_specs=[qspec, kspec, kspec, cspec, qspec, lspec, qspec] + [ANY] * len(extra),
        out_specs=[kspec, kspec, pl.BlockSpec((1, 8, t), lambda p, j, i: (p, 0, j)),
                   pl.BlockSpec((L, LANES), lambda p, j, i: (0, p))] + [ANY] * len(extra_out),
        scratch_shapes=[pltpu.VMEM((t, LANES), F32), pltpu.VMEM((t, LANES), F32), pltpu.VMEM((2, 1, t), F32)]
        + (_dma_sems(plan["nsem"]) if plan else []),
        compiler_params=_params(("arbitrary",) * 3 if plan else ("parallel", "arbitrary", "arbitrary")),
    )(q, k, v, ct, o, lse, do, *extra)
    return res[0], res[1], res[2], res[3], list(res[4:])


def _place():
    x, y, c = lax.axis_index("x"), lax.axis_index("y"), lax.axis_index("c")
    chips = [(1 - x, y), (x, 1 - y), (1 - x, 1 - y)]
    return x, y, c, chips


def _remote(src, dst, send_sems, recv_sems, k, to):
    return pltpu.make_async_remote_copy(src_ref=src, dst_ref=dst, send_sem=send_sems.at[k],
                                        recv_sem=recv_sems.at[k], device_id=to, device_id_type=MESH)


def _dma_sems(n):
    return [pltpu.SemaphoreType.DMA((n,)), pltpu.SemaphoreType.DMA((n,))]


def _all_gather(shards, modes):
    return _run_exchange("gather_weights", _gather_plan(shards, modes))


def _gather_plan(shards, modes):
    n = len(shards)

    def out_shape(s, mode):
        r, c = s.shape
        return {"stack": (4, r, c), "cols": (r, 4 * c), "rows": (4 * r, c)}[mode]

    def window(outs, i, chip, cc):
        r, cw = shards[i].shape
        h = r // 2
        if modes[i] == "stack":
            return outs[i].at[chip, pl.ds(cc * h, h), :]
        if modes[i] == "cols":
            return outs[i].at[pl.ds(cc * h, h), pl.ds(pl.multiple_of(chip * cw, LANES), cw)]
        return outs[i].at[pl.ds(pl.multiple_of(chip * r + cc * h, 8), h), :]

    def first(ins, outs, ss, rs):
        x, y, c, chips = _place()
        cps = []
        for i in range(n):
            h = shards[i].shape[0] // 2
            for k, (cx, cy) in enumerate(chips):
                cps.append(_remote(ins[i].at[pl.ds(c * h, h), :], window(outs, i, 2 * x + y, c), ss, rs,
                                   6 * i + k, (cx, cy, c)))
        return cps

    def passed(outs, ss, rs):
        x, y, c, chips = _place()
        cps = []
        for k, (cx, cy) in enumerate(chips):
            for i in range(n):
                landed = window(outs, i, 2 * cx + cy, c)
                cps.append((_remote(landed, landed, ss, rs, 6 * i + k, (x, y, 1 - c)),
                            _remote(landed, landed, ss, rs, 6 * i + 3 + k, (x, y, 1 - c))))
        return cps

    def start(ins, outs, ss, rs):
        for cp in first(ins, outs, ss, rs):
            cp.start()

    def mid(ins, outs, ss, rs):
        for arrival, fwd in passed(outs, ss, rs):
            arrival.wait_recv()
            fwd.start()

    def wait(ins, outs, ss, rs):
        x, y, c, chips = _place()
        for k, (cx, cy) in enumerate(chips):
            for i in range(n):
                other = window(outs, i, 2 * cx + cy, 1 - c)
                _remote(other, other, ss, rs, 6 * i + 3 + k, (x, y, 1 - c)).wait_recv()
        for cp in first(ins, outs, ss, rs) + [fwd for _, fwd in passed(outs, ss, rs)]:
            cp.wait_send()

    return dict(arrays=list(shards), nsem=6 * n, start=start, mid=mid, wait=wait,
                out_shapes=[jax.ShapeDtypeStruct(out_shape(s, m), s.dtype) for s, m in zip(shards, modes)])


def _run_exchange(name, plan):
    n = len(plan["arrays"])

    def body(*refs):
        ins, outs, (ss, rs) = refs[:n], refs[n:n + len(plan["out_shapes"])], refs[n + len(plan["out_shapes"]):]
        plan["start"](ins, outs, ss, rs)
        if plan["mid"] is not None:
            plan["mid"](ins, outs, ss, rs)
        plan["wait"](ins, outs, ss, rs)

    return pl.pallas_call(
        body, name=name, out_shape=plan["out_shapes"], in_specs=[ANY] * n,
        out_specs=[ANY] * len(plan["out_shapes"]), scratch_shapes=_dma_sems(plan["nsem"]),
    )(*plan["arrays"])


def _pair_exchange(name, blocks):
    n = len(blocks)

    def body(*refs):
        ins, outs, (send_sems, recv_sems) = refs[:n], refs[n:2 * n], refs[2 * n:]
        x, y, c, _ = _place()
        cps = []
        for i in range(n):
            h = blocks[i].shape[1] // 2
            cps.append(_remote(ins[i].at[:, pl.ds((1 - c) * h, h), :], outs[i], send_sems, recv_sems, i,
                               (x, y, 1 - c)))
        for cp in cps:
            cp.start()
        for cp in cps:
            cp.wait()

    return pl.pallas_call(
        body, name=name,
        out_shape=[jax.ShapeDtypeStruct((4, b.shape[1] // 2, b.shape[2]), b.dtype) for b in blocks],
        in_specs=[ANY] * n, out_specs=[ANY] * n, scratch_shapes=_dma_sems(n),
    )(*blocks)


def _chip_exchange(parts):
    return _run_exchange("reduce_chip_exchange", _chip_exchange_plan(parts))


def _chip_exchange_plan(parts):
    n = len(parts)

    def sends(ins, outs, ss, rs):
        x, y, c, chips = _place()
        return [_remote(ins[i].at[2 * cx + cy], outs[i].at[2 * x + y], ss, rs, 3 * i + k, (cx, cy, c))
                for i in range(n) for k, (cx, cy) in enumerate(chips)]

    def start(ins, outs, ss, rs):
        for cp in sends(ins, outs, ss, rs):
            cp.start()

    def wait(ins, outs, ss, rs):
        x, y, c, chips = _place()
        for i in range(n):
            for k, (cx, cy) in enumerate(chips):
                slot = outs[i].at[2 * cx + cy]
                _remote(slot, slot, ss, rs, 3 * i + k, (cx, cy, c)).wait_recv()
        for cp in sends(ins, outs, ss, rs):
            cp.wait_send()

    return dict(arrays=list(parts), nsem=3 * n, start=start, mid=None, wait=wait,
                out_shapes=[jax.ShapeDtypeStruct(p.shape, p.dtype) for p in parts])


def _pair_join(tots):
    n = len(tots)

    def body(*refs):
        ins, outs, (send_sems, recv_sems) = refs[:n], refs[n:2 * n], refs[2 * n:]
        x, y, c, _ = _place()
        cps = [_remote(ins[i], outs[i], send_sems, recv_sems, i, (x, y, 1 - c)) for i in range(n)]
        for cp in cps:
            cp.start()
        for cp in cps:
            cp.wait()

    return pl.pallas_call(
        body, name="reduce_pair_join", out_shape=[jax.ShapeDtypeStruct(t.shape, t.dtype) for t in tots],
        in_specs=[ANY] * n, out_specs=[ANY] * n, scratch_shapes=_dma_sems(n),
    )(*tots)


def _add_tile(h, cw):
    cap = max(8, (512 * 1024) // max(cw, 1))
    return _pick(h, tuple(t for t in (1024, 512, 256, 128, 64, 32, 16, 8) if t <= cap))


def _pair_add(name, block, recv):
    n, r, cw = block.shape
    h = r // 2
    tr = _add_tile(h, cw)
    c = lax.axis_index("c").astype(jnp.int32).reshape((1,))

    def body(c_ref, a_ref, b_ref, o_ref):
        o_ref[...] = (a_ref[...] + b_ref[...]).astype(o_ref.dtype)

    return pl.pallas_call(
        body, name=name, out_shape=jax.ShapeDtypeStruct((n, h, cw), BF16),
        grid_spec=pltpu.PrefetchScalarGridSpec(
            num_scalar_prefetch=1, grid=(n, h // tr),
            in_specs=[pl.BlockSpec((1, tr, cw), lambda a, i, cr: (a, cr[0] * (h // tr) + i, 0)),
                      pl.BlockSpec((1, tr, cw), lambda a, i, cr: (a, i, 0))],
            out_specs=pl.BlockSpec((1, tr, cw), lambda a, i, cr: (a, i, 0))),
        compiler_params=_params(("parallel", "parallel")),
    )(c, block, recv)


def _chip_add(name, parts):
    n, h, cw = parts.shape
    tr = _add_tile(h, cw)

    def body(a_ref, o_ref):
        f = lambda i: a_ref[i].astype(F32)
        o_ref[...] = ((f(0) + f(1)) + f(2)) + f(3)

    return pl.pallas_call(
        body, name=name, out_shape=jax.ShapeDtypeStruct((h, cw), F32), grid=(h // tr,),
        in_specs=[pl.BlockSpec((n, tr, cw), lambda i: (0, i, 0))],
        out_specs=pl.BlockSpec((tr, cw), lambda i: (i, 0)),
        compiler_params=_params(("parallel",)),
    )(parts)


def _adamw(name, w, g, m, v):
    R, C = w.shape
    tr = _pick(R, (128, 64, 32, 16, 8))
    spec = pl.BlockSpec((tr, C), lambda i: (i, 0))

    def body(w_ref, g_ref, m_ref, v_ref, d_ref, mo_ref, vo_ref):
        gr = g_ref[...]
        mn = ADAM_B1 * m_ref[...] + (1.0 - ADAM_B1) * gr
        vn = ADAM_B2 * v_ref[...] + (1.0 - ADAM_B2) * jnp.square(gr)
        m_hat = mn / (1.0 - ADAM_B1 ** ADAM_STEP)
        v_hat = vn / (1.0 - ADAM_B2 ** ADAM_STEP)
        d_ref[...] = -ADAM_LR * (m_hat / (jnp.sqrt(v_hat) + ADAM_EPS) + ADAM_WD * w_ref[...])
        mo_ref[...] = mn
        vo_ref[...] = vn

    return pl.pallas_call(
        body, name=name, out_shape=[jax.ShapeDtypeStruct((R, C), F32)] * 3, grid=(R // tr,),
        in_specs=[spec] * 4, out_specs=[spec] * 3, compiler_params=_params(("parallel",)),
    )(w, g, m, v)


def _pack(parts, dtype, row_mult):
    flat = jnp.concatenate([p.reshape(-1).astype(dtype) for p in parts])
    unit = row_mult * PACK_W
    pad = (-flat.shape[0]) % unit
    if pad:
        flat = jnp.concatenate([flat, jnp.zeros((pad,), dtype)])
    return flat.reshape(-1, PACK_W)


def _unpack(flat, shapes):
    out, off = [], 0
    for s in shapes:
        n = int(np.prod(s))
        out.append(flat[off:off + n].reshape(s))
        off += n
    return out


SHARDED = ("w_in", "rwkv_w2", "rwkv_a2", "rwkv_g2", "w_branch_a", "w_branch_b", "w_o", "w_gate_up", "w_down",
           "meta_tokens")
SHARD_AXIS = {"w_in": 1, "rwkv_w2": 1, "rwkv_a2": 1, "rwkv_g2": 1, "w_branch_a": 1, "w_branch_b": 1, "w_o": 0,
              "w_gate_up": 1, "w_down": 0, "meta_tokens": 1}
GATHER_MODE = {"w_in": "stack", "rwkv_w2": "stack", "rwkv_a2": "stack", "rwkv_g2": "stack", "w_branch_a": "cols",
               "w_branch_b": "cols", "w_o": "rows", "w_gate_up": "cols", "w_down": "rows", "meta_tokens": "stack"}
LATE_GATHER = ("w_branch_a", "w_branch_b", "w_o", "w_gate_up", "w_down")
SMALL = ("norm1_g", "rwkv_mu", "rwkv_w0", "rwkv_a0", "rwkv_k_k", "rwkv_k_a", "rwkv_r_k", "rwkv_gn_w",
         "rwkv_gn_b", "fox_q_norm_g", "fox_k_norm_g", "fox_f_bias", "norm2_g")
WEIGHTS = ("meta_tokens", "norm1_g", "w_in", "rwkv_mu", "rwkv_w0", "rwkv_w2", "rwkv_a0", "rwkv_a2", "rwkv_g2",
           "rwkv_k_k", "rwkv_k_a", "rwkv_r_k", "rwkv_gn_w", "rwkv_gn_b", "fox_q_norm_g", "fox_k_norm_g",
           "fox_f_bias", "w_branch_a", "w_branch_b", "w_o", "norm2_g", "w_gate_up", "w_down")


def _pad_rows(t, rows):
    return jnp.concatenate([t, jnp.zeros((rows - t.shape[0],) + t.shape[1:], t.dtype)], axis=0)


def _pad_cols(t, cols):
    return jnp.concatenate([t, jnp.zeros(t.shape[:-1] + (cols - t.shape[-1],), t.dtype)], axis=-1)


def _step(x, tgt, wts, mom1, mom2):
    seq, D = x.shape
    L = SEQ_ROW0 + seq
    RW = wts["rwkv_w0"].shape[-1]
    DL, AL, GL = wts["rwkv_w2"].shape[0], wts["rwkv_a2"].shape[0], wts["rwkv_g2"].shape[0]
    FW = wts["w_branch_b"].shape[0]
    FH = wts["fox_f_bias"].shape[-1]
    DFF = wts["w_down"].shape[0] * 4
    LORA = DL + AL + GL
    LW = -(-(LORA + FH) // 512) * 512
    assert RW == FW and (6 * RW) % D == 0 and (6 * RW + 2 * D) % LW == 0 and LORA % 8 == 0
    xj = lax.axis_index("x")
    yj = lax.axis_index("y")
    chip = 2 * xj + yj

    send = {n: wts[n] if n == "meta_tokens" else wts[n].astype(BF16) for n in SHARDED}
    early = tuple(n for n in SHARDED if n not in LATE_GATHER)
    full, stacked = {}, {}

    def place_own(n, got):
        shard, mode = send[n], GATHER_MODE[n]
        r, cw = shard.shape
        if mode == "stack":
            got = lax.dynamic_update_index_in_dim(got, shard, chip, 0)
            stacked[n] = got
            full[n] = jnp.concatenate([got[j] for j in range(4)], axis=1)
        elif mode == "cols":
            full[n] = lax.dynamic_update_slice(got, shard, (0, chip * cw))
        else:
            full[n] = lax.dynamic_update_slice(got, shard, (chip * r, 0))

    for n, got in zip(early, _all_gather([send[n] for n in early], [GATHER_MODE[n] for n in early])):
        place_own(n, got)
    late_plan = _gather_plan([send[n] for n in LATE_GATHER], [GATHER_MODE[n] for n in LATE_GATHER])
    meta = full["meta_tokens"]
    w_in = full["w_in"]
    o = 0
    segs = {}
    for nm, wd in (("r", RW), ("k", RW), ("v", RW), ("wd", DL), ("ad", AL), ("gd", GL),
                   ("fq", FW), ("fk", FW), ("fv", FW), ("ff", FH), ("ga", D), ("gb", D)):
        segs[nm] = w_in[:, o:o + wd]
        o += wd
    lora_w = _pad_cols(jnp.concatenate([segs["wd"], segs["ad"], segs["gd"], segs["ff"]], axis=1), LW)
    w1 = jnp.concatenate([segs["r"], segs["k"], segs["v"], segs["fq"], segs["fk"], segs["fv"],
                          segs["ga"], segs["gb"], lora_w], axis=1)
    cb_f = 3
    cb_gate = (6 * RW) // D
    cb_lora = (6 * RW + 2 * D) // LW

    e_m, et_m = _head_mats(RW)
    ft_m = _fold_mat(RW)
    mu = wts["rwkv_mu"]
    mu_rkv = mu[:, :3 * RW]
    mu_l = _pad_cols(mu[:, 3 * RW:], LW)
    w2p = _pad_rows(full["rwkv_w2"].astype(F32), LW)
    a2p = _pad_rows(jnp.concatenate([jnp.zeros((DL, RW), F32), full["rwkv_a2"].astype(F32)], axis=0), LW)
    g2p = _pad_rows(jnp.concatenate([jnp.zeros((DL + AL, RW), F32), full["rwkv_g2"].astype(F32)], axis=0), LW)
    r_k = wts["rwkv_r_k"].reshape(1, RW)
    qg8 = jnp.broadcast_to(wts["fox_q_norm_g"], (8, HEAD))
    kg8 = jnp.broadcast_to(wts["fox_k_norm_g"], (8, HEAD))
    fb = _pad_cols(wts["fox_f_bias"], LANES)
    fmask = (jnp.arange(LANES) < FH).astype(F32).reshape(1, LANES)
    lmask = ((jnp.arange(LW) >= LORA) & (jnp.arange(LW) < LORA + FH)).astype(F32).reshape(1, LW)

    h0 = jnp.concatenate([jnp.zeros((PAD_ROWS, D), F32), meta, x], axis=0)
    n1 = wts["norm1_g"]

    (xn,), _ = _rowcall("rms1_fwd", lambda i, r, b: ([_rms_f(r[0], b[0])], []), L,
                        [(h0, D, 0, "row")], [n1], [(D, BF16)], [])
    proj = _matmul(xn, w1, "nn", F32, "proj_fwd")

    def shift_fn(i, r, b):
        rows = lax.broadcasted_iota(jnp.int32, (ROW_TILE, 1), 0)
        outs = []
        for z, halo, m_ in ((r[0], r[1], b[0]), (r[2], r[3], b[1])):
            first = jnp.where(i == 0, 0.0, halo[7:8, :])
            zp = jnp.where(rows == 0, first, pltpu.roll(z, 1, 0))
            outs.append(z + (zp - z) * m_)
        return outs, []

    rkv_w = 3 * RW
    (x_rkv, x_l), _ = _rowcall(
        "shift_fwd", shift_fn, L,
        [(proj, rkv_w, 0, "row"), (proj, rkv_w, 0, "prev"), (proj, LW, cb_lora, "row"), (proj, LW, cb_lora, "prev")],
        [mu_rkv, mu_l], [(rkv_w, F32), (LW, F32)], [])

    prep_p = [wts["rwkv_w0"], w2p, wts["rwkv_a0"], a2p, g2p, wts["rwkv_k_k"], wts["rwkv_k_a"], e_m, et_m]
    prep_rows = [(x_rkv, RW, 0, "row"), (x_rkv, RW, 1, "row"), (x_rkv, RW, 2, "row"), (x_l, LW, 0, "row")]
    (s_r, s_lw, s_k, s_v, s_a, s_b, gate_g), _ = _rowcall(
        "rwkv_prep_fwd", lambda i, r, b: (list(_prep_f(*r, *b)), []), L, prep_rows, prep_p,
        [(RW, F32)] * 7, [])
    y_scan, s_all = _scan_fwd(s_r, s_lw, s_k, s_v, s_a, s_b)
    post_p = [wts["rwkv_gn_w"], wts["rwkv_gn_b"], r_k, e_m, et_m]
    post_rows = [(y_scan, RW, 0, "row"), (s_r, RW, 0, "row"), (s_k, RW, 0, "row"), (s_v, RW, 0, "row"),
                 (gate_g, RW, 0, "row")]
    (y_a,), _ = _rowcall("rwkv_post_fwd", lambda i, r, b: ([_post_f(*r, *b)], []), L, post_rows, post_p,
                         [(RW, BF16)], [])

    fox_p = [qg8, kg8, fb, e_m, et_m, ft_m, fmask]

    def foxprep_fn(i, r, b):
        fl = _doth(r[2] * b[-1], b[-2])
        return list(_foxprep_f(r[0], r[1], fl, *b[:-2])), []

    sel = (np.arange(LW)[:, None] - LORA == np.arange(LANES)[None, :]).astype(np.float32)
    sel = jnp.asarray(sel)
    fox_rows = [(proj, FW, cb_f, "row"), (proj, FW, cb_f + 1, "row"), (proj, LW, cb_lora, "row")]
    (f_q, f_k, logf), _ = _rowcall("fox_prep_fwd", foxprep_fn, L, fox_rows, fox_p + [sel, lmask],
                                   [(FW, BF16), (FW, BF16), (LANES, F32)], [])
    ct = _cumsum_rows(logf)
    f_v = proj[:, (cb_f + 2) * FW:(cb_f + 3) * FW]
    y_b32, lse, got_late = _attn_fwd(f_q, f_k, f_v, ct, plan=late_plan)
    for n, got in zip(LATE_GATHER, got_late):
        place_own(n, got)
    y_b = y_b32.astype(BF16)

    p_a = _matmul(y_a, full["w_branch_a"], "nn", F32, "branch_a_fwd")
    p_b = _matmul(y_b, full["w_branch_b"], "nn", F32, "branch_b_fwd")
    merge_rows = [(proj, D, cb_gate, "row"), (proj, D, cb_gate + 1, "row"), (p_a, D, 0, "row"), (p_b, D, 0, "row")]
    (merged,), _ = _rowcall("merge_fwd", lambda i, r, b: ([_merge_f(*r)], []), L, merge_rows, [], [(D, BF16)], [])
    h1 = _matmul(merged, full["w_o"], "nn", F32, "wo_fwd", add=h0)
    n2 = wts["norm2_g"]
    (xn2,), _ = _rowcall("rms2_fwd", lambda i, r, b: ([_rms_f(r[0], b[0])], []), L,
                         [(h1, D, 0, "row")], [n2], [(D, BF16)], [])
    gu = _matmul(xn2, full["w_gate_up"], "nn", F32, "gate_up_fwd")
    gu_rows = [(gu, DFF, 0, "row"), (gu, DFF, 1, "row")]
    (act,), _ = _rowcall("swiglu_fwd", lambda i, r, b: ([_swiglu_f(*r)], []), L, gu_rows, [], [(DFF, BF16)], [])
    h2 = _matmul(act, full["w_down"], "nn", F32, "down_fwd", add=h1)

    def loss_fn(i, r, b):
        err = jnp.where(i == 0, 0.0, r[0] - r[1])
        return [err * (1.0 / D)], [jnp.zeros((8, LANES), F32) + 0.5 / D * jnp.sum(err * err)]

    (dh2,), (loss_acc,) = _rowcall("loss", loss_fn, L, [(h2, D, 0, "row"), (tgt, D, 0, "lag")], [],
                                   [(D, F32)], [(8, LANES)])
    loss = lax.psum(loss_acc[0, 0], ("x", "y", "c"))

    dh2b = dh2.astype(BF16)
    g_w_down = _matmul(act, dh2b, "tn", F32, "down_dw")
    d_act = _matmul(dh2b, full["w_down"], "nt", F32, "down_dx")

    def swiglu_bwd(i, r, b):
        _, vjp = jax.vjp(_swiglu_f, r[0], r[1])
        return list(vjp(r[2])), []

    (d_gate, d_up), _ = _rowcall("swiglu_bwd", swiglu_bwd, L, gu_rows + [(d_act, DFF, 0, "row")], [],
                                 [(DFF, BF16), (DFF, BF16)], [])
    d_gu = jnp.concatenate([d_gate, d_up], axis=1)
    g_w_gu = _matmul(xn2, d_gu, "tn", F32, "gate_up_dw", col_blocks=4)
    d_xn2 = _matmul(d_gu, full["w_gate_up"], "nt", F32, "gate_up_dx")

    def rms_bwd(i, r, b):
        _, vjp = jax.vjp(_rms_f, r[0], b[0])
        dh, dg = vjp(r[1])
        return [dh + r[2]], [dg]

    (dh1,), (g_n2,) = _rowcall("rms2_bwd", rms_bwd, L,
                               [(h1, D, 0, "row"), (d_xn2, D, 0, "row"), (dh2, D, 0, "row")], [n2],
                               [(D, F32)], [(1, D)])
    dh1b = dh1.astype(BF16)
    g_w_o = _matmul(merged, dh1b, "tn", F32, "wo_dw")
    d_merged = _matmul(dh1b, full["w_o"], "nt", F32, "wo_dx")

    def merge_bwd(i, r, b):
        _, vjp = jax.vjp(_merge_f, *r[:4])
        return list(vjp(r[4])), []

    (d_za, d_zb, d_pa, d_pb), _ = _rowcall("merge_bwd", merge_bwd, L, merge_rows + [(d_merged, D, 0, "row")], [],
                                           [(D, BF16)] * 4, [])
    g_w_a = _matmul(y_a, d_pa, "tn", F32, "branch_a_dw", col_blocks=4)
    g_w_b = _matmul(y_b, d_pb, "tn", F32, "branch_b_dw", col_blocks=4)
    d_ya = _matmul(d_pa, full["w_branch_a"], "nt", F32, "branch_a_dx")
    d_yb = _matmul(d_pb, full["w_branch_b"], "nt", F32, "branch_b_dx")

    cj = lax.axis_index("c")
    names1 = ("w_gate_up", "w_branch_a", "w_branch_b", "w_o", "w_down")
    blocks1 = [g_w_gu, g_w_a, g_w_b, g_w_o.reshape(4, -1, D), g_w_down.reshape(4, -1, D)]
    parts1 = [_pair_add("reduce_pair_add_" + n, b, r)
              for n, b, r in zip(names1, blocks1, _pair_exchange("reduce_pair_exchange_1", blocks1))]
    d_fk, d_fv, dc_rows, d_fq, recv1 = _attn_bwd(f_q, f_k, f_v, ct, y_b32, lse, d_yb,
                                                 plan=_chip_exchange_plan(parts1))
    dct = _pad_rows(dc_rows[:, :2, :].reshape(-1, L), LANES)
    d_logf = _rcumsum_cols(dct)

    def foxprep_bwd(i, r, b):
        def f(q, k, xl, qg, kg, fbias):
            return _foxprep_f(q, k, _doth(xl * b[-1], b[-2]), qg, kg, fbias, *b[3:7])
        _, vjp = jax.vjp(f, r[0], r[1], r[2], b[0], b[1], b[2])
        dq, dk, dxl, dqg, dkg, dfb = vjp((r[3], r[4], r[5]))
        return [dq, dk, dxl], [dqg, dkg, dfb]

    (d_zfq, d_zfk, d_zl_f), (g_qg8, g_kg8, g_fb) = _rowcall(
        "fox_prep_bwd", foxprep_bwd, L,
        fox_rows + [(d_fq, FW, 0, "row"), (d_fk, FW, 0, "row"), (d_logf, LANES, 0, "row")],
        fox_p + [sel, lmask], [(FW, BF16), (FW, BF16), (LW, F32)], [(8, HEAD), (8, HEAD), (1, LANES)])

    def post_bwd(i, r, b):
        _, vjp = jax.vjp(lambda *a: _post_f(*a, b[3], b[4]), *r[:5], b[0], b[1], b[2])
        g = vjp(r[5])
        return list(g[:5]), list(g[5:])

    (d_y, d_r1, d_k1, d_v1, d_g), (g_gn_w, g_gn_b, g_r_k) = _rowcall(
        "rwkv_post_bwd", post_bwd, L, post_rows + [(d_ya, RW, 0, "row")], post_p,
        [(RW, F32)] * 5, [(1, RW)] * 3)
    d_r2, d_lw, d_k2, d_v2, d_a, d_b = _scan_bwd(s_r, s_lw, s_k, s_v, s_a, s_b, s_all, d_y)

    def prep_bwd(i, r, b):
        _, vjp = jax.vjp(lambda *a: _prep_f(*a, b[7], b[8]), *r[:4], *b[:7])
        cts = (r[4] + r[10], r[5], r[6] + r[11], r[7] + r[12], r[8], r[9], r[13])
        g = vjp(cts)
        return list(g[:4]), list(g[4:])

    bwd_rows = prep_rows + [(d_r2, RW, 0, "row"), (d_lw, RW, 0, "row"), (d_k2, RW, 0, "row"), (d_v2, RW, 0, "row"),
                            (d_a, RW, 0, "row"), (d_b, RW, 0, "row"), (d_r1, RW, 0, "row"), (d_k1, RW, 0, "row"),
                            (d_v1, RW, 0, "row"), (d_g, RW, 0, "row")]
    (d_xr, d_xk, d_xv, d_xl), (g_w0, g_w2p, g_a0, g_a2p, g_g2p, g_kk, g_ka) = _rowcall(
        "rwkv_prep_bwd", prep_bwd, L, bwd_rows, prep_p, [(RW, F32)] * 3 + [(LW, F32)],
        [(1, RW), (LW, RW), (1, RW), (LW, RW), (LW, RW), (1, RW), (1, RW)])

    def shift_bwd(i, r, b):
        last = pl.num_programs(0) - 1
        rows = lax.broadcasted_iota(jnp.int32, (ROW_TILE, 1), 0)
        outs, sums = [], []
        groups = ((r[0], r[1], r[2], r[3], b[0], None), (r[4], r[5], r[6], r[7], b[1], r[8]))
        for d, dnext, z, zhalo, m_, extra in groups:
            nxt = jnp.where(i == last, 0.0, dnext[0:1, :])
            d_up = jnp.where(rows == ROW_TILE - 1, nxt, pltpu.roll(d, ROW_TILE - 1, 0))
            dz = d * (1.0 - m_) + d_up * m_
            if extra is not None:
                dz = dz + extra
            first = jnp.where(i == 0, 0.0, zhalo[7:8, :])
            zp = jnp.where(rows == 0, first, pltpu.roll(z, 1, 0))
            outs.append(dz)
            sums.append(_rowsum(d * (zp - z)))
        return outs, sums

    d_xrkv = jnp.concatenate([d_xr, d_xk, d_xv], axis=1)
    (d_zrkv, d_zl), (g_mu_rkv, g_mu_l) = _rowcall(
        "shift_bwd", shift_bwd, L,
        [(d_xrkv, rkv_w, 0, "row"), (d_xrkv, rkv_w, 0, "next"), (proj, rkv_w, 0, "row"), (proj, rkv_w, 0, "prev"),
         (d_xl, LW, 0, "row"), (d_xl, LW, 0, "next"), (proj, LW, cb_lora, "row"), (proj, LW, cb_lora, "prev"),
         (d_zl_f, LW, 0, "row")],
        [mu_rkv, mu_l], [(rkv_w, BF16), (LW, BF16)], [(1, rkv_w), (1, LW)])

    n_in = stacked["w_in"].shape[2] * 4
    cs = n_in // 4
    cp = -(-cs // LANES) * LANES
    d_ref = jnp.concatenate([d_zrkv, d_zl[:, :LORA], d_zfq, d_zfk, d_fv.astype(BF16), d_zl[:, LORA:LORA + FH],
                             d_za, d_zb], axis=1)
    d_blk = jnp.concatenate([_pad_cols(d_ref[:, j * cs:(j + 1) * cs], cp) for j in range(4)], axis=1)
    w_blk = jnp.concatenate([_pad_cols(stacked["w_in"][j], cp) for j in range(4)], axis=1)
    g_w_in = _matmul(xn, d_blk, "tn", F32, "proj_dw", col_blocks=4)
    part_w_in = _pair_add("reduce_pair_add_w_in", g_w_in, _pair_exchange("reduce_pair_exchange_2", [g_w_in])[0])
    d_xn, recv_w_in = _matmul(d_blk, w_blk, "nt", F32, "proj_dx", plan=_chip_exchange_plan([part_w_in]))
    (dh0,), (g_n1,) = _rowcall("rms1_bwd", rms_bwd, L,
                               [(h0, D, 0, "row"), (d_xn, D, 0, "row"), (dh1, D, 0, "row")], [n1],
                               [(D, F32)], [(1, D)])
    grad_x = dh0[SEQ_ROW0:]
    g_meta = dh0[PAD_ROWS:SEQ_ROW0]

    tiny = {"rwkv_w2": g_w2p[:DL], "rwkv_a2": g_a2p[DL:DL + AL], "rwkv_g2": g_g2p[DL + AL:LORA],
            "meta_tokens": g_meta}
    g_mu = jnp.concatenate([g_mu_rkv, g_mu_l[:, :LORA]], axis=1)
    gsmall = {
        "norm1_g": g_n1, "rwkv_mu": g_mu, "rwkv_w0": g_w0, "rwkv_a0": g_a0, "rwkv_k_k": g_kk, "rwkv_k_a": g_ka,
        "rwkv_r_k": g_r_k.reshape(wts["rwkv_r_k"].shape), "rwkv_gn_w": g_gn_w, "rwkv_gn_b": g_gn_b,
        "fox_q_norm_g": g_qg8[0:1], "fox_k_norm_g": g_kg8[0:1], "fox_f_bias": g_fb[:, :FH], "norm2_g": g_n2,
    }
    small_flat = jnp.concatenate([gsmall[n].reshape(-1) for n in SMALL])

    tiny_names = tuple(tiny)

    def tiny_block(j):
        parts = []
        for n in tiny_names:
            w = tiny[n].shape[1] // 4
            parts.append(tiny[n][:, j * w:(j + 1) * w])
        return _pack(parts + [small_flat], F32, 32)

    small_blocks = jnp.stack([tiny_block(j) for j in range(4)])
    part_small = _pair_add("reduce_pair_add_small", small_blocks,
                           _pair_exchange("reduce_pair_exchange_3", [small_blocks])[0])
    recv_small = _chip_exchange([part_small])
    names = names1 + ("w_in", "small")
    tots = []
    for n, p, r in zip(names, parts1 + [part_w_in, part_small], list(recv1) + list(recv_w_in) + list(recv_small)):
        own = lax.dynamic_index_in_dim(p, chip, 0, keepdims=False)
        tots.append(_chip_add("reduce_chip_add_" + n, lax.dynamic_update_index_in_dim(r, own, chip, 0)))
    others = _pair_join(tots)
    red = [jnp.where(cj == 0, jnp.concatenate([t, o_], axis=0), jnp.concatenate([o_, t], axis=0))
           for t, o_ in zip(tots, others)]
    grads = {n: red[i] for i, n in enumerate(names[:-1])}
    grads["w_in"] = grads["w_in"][:, :cs]
    tiny_shapes = [wts[n].shape for n in tiny_names]
    got = _unpack(red[-1].reshape(-1), tiny_shapes + [small_flat.shape])
    for n, t in zip(tiny_names, got):
        grads[n] = t
    for n, t in zip(SMALL, _unpack(got[-1], [wts[n].shape for n in SMALL])):
        grads[n] = t

    delta, new_m, new_v = {}, {}, {}
    for n in SHARDED:
        delta[n], new_m[n], new_v[n] = _adamw("adamw_" + n, wts[n], grads[n], mom1[n], mom2[n])
    pk = lambda d: _pack([d[n] for n in SMALL], F32, 8)
    ds, ms, vs = _adamw("adamw_small", pk(wts), pk(grads), pk(mom1), pk(mom2))
    small_shapes = [wts[n].shape for n in SMALL]
    for dst, src in ((delta, ds), (new_m, ms), (new_v, vs)):
        for n, t in zip(SMALL, _unpack(src.reshape(-1), small_shapes)):
            dst[n] = t
    return loss, grad_x, grads, delta, new_m, new_v


def kernel(x, meta_tokens, norm1_g, w_in, rwkv_mu, rwkv_w0, rwkv_w2, rwkv_a0, rwkv_a2, rwkv_g2, rwkv_k_k, rwkv_k_a, rwkv_r_k, rwkv_gn_w, rwkv_gn_b, fox_q_norm_g, fox_k_norm_g, fox_f_bias, w_branch_a, w_branch_b, w_o, norm2_g, w_gate_up, w_down, loss_target, m_meta_tokens, m_norm1_g, m_w_in, m_rwkv_mu, m_rwkv_w0, m_rwkv_w2, m_rwkv_a0, m_rwkv_a2, m_rwkv_g2, m_rwkv_k_k, m_rwkv_k_a, m_rwkv_r_k, m_rwkv_gn_w, m_rwkv_gn_b, m_fox_q_norm_g, m_fox_k_norm_g, m_fox_f_bias, m_w_branch_a, m_w_branch_b, m_w_o, m_norm2_g, m_w_gate_up, m_w_down, v_meta_tokens, v_norm1_g, v_w_in, v_rwkv_mu, v_rwkv_w0, v_rwkv_w2, v_rwkv_a0, v_rwkv_a2, v_rwkv_g2, v_rwkv_k_k, v_rwkv_k_a, v_rwkv_r_k, v_rwkv_gn_w, v_rwkv_gn_b, v_fox_q_norm_g, v_fox_k_norm_g, v_fox_f_bias, v_w_branch_a, v_w_branch_b, v_w_o, v_norm2_g, v_w_gate_up, v_w_down):
    args = dict(locals())
    shapes = {n: args[n].shape for n in WEIGHTS}

    def drop_depth(t, n):
        if n == "meta_tokens":
            return t
        if n == "rwkv_r_k":
            return t.reshape(1, -1)
        return t.reshape(t.shape[1:]) if t.ndim == 3 else t

    wts = {n: drop_depth(args[n], n) for n in WEIGHTS}
    mom1 = {n: drop_depth(args["m_" + n], n) for n in WEIGHTS}
    mom2 = {n: drop_depth(args["v_" + n], n) for n in WEIGHTS}
    loss, grad_x, grads, delta, new_m, new_v = _step(x[0], loss_target[0], wts, mom1, mom2)
    outs = [loss, grad_x[None]]
    for d in (grads, delta, new_m, new_v):
        outs += [d[n].reshape(shapes[n]) for n in WEIGHTS]
    return tuple(outs)
```

```python
import functools

import jax
import jax.numpy as jnp
import numpy as np
from jax import lax
from jax.experimental import pallas as pl
from jax.experimental.pallas import tpu as pltpu

F32 = jnp.float32
BF16 = jnp.bfloat16
MESH = pl.DeviceIdType.MESH
ANY = pl.BlockSpec(memory_space=pl.ANY)

N_META = 16
HEAD = 64
ROW_TILE = 128
PAD_ROWS = ROW_TILE - N_META
SEQ_ROW0 = ROW_TILE
CHUNK = 64
LANES = 128
PACK_W = 1024
RMS_EPS = 1e-6
GN_EPS = 64e-5
ATTN_SCALE = HEAD ** -0.5
NEG = -1e30
VMEM_LIMIT_V7X = 56 * 1024 * 1024

ADAM_LR = 0.001
ADAM_B1 = 0.9
ADAM_B2 = 0.999
ADAM_EPS = 1e-08
ADAM_WD = 0.01
ADAM_STEP = 10


def _params(sem=None):
    return pltpu.CompilerParams(dimension_semantics=sem, vmem_limit_bytes=VMEM_LIMIT_V7X)


def _pick(n, cands):
    for c in cands:
        if n % c == 0:
            return c
    return n


def _bf(t):
    return t.astype(BF16)


def _dotb(a, b):
    return jnp.dot(_bf(a), _bf(b), preferred_element_type=F32)


def _split3(x):
    x1 = x.astype(BF16)
    r1 = x - x1.astype(F32)
    x2 = r1.astype(BF16)
    return x1, x2, (r1 - x2.astype(F32)).astype(BF16)


def _mme_raw(x, e, ce):
    eb = e.astype(BF16)
    dot = lambda q: lax.dot_general(q, eb, (((1,), (ce,)), ((), ())), preferred_element_type=F32)
    x1, x2, x3 = _split3(x)
    return dot(x1) + (dot(x2) + dot(x3))


@jax.custom_vjp
def _doth(x, e):
    return _mme_raw(x, e, 0)


def _doth_fwd(x, e):
    return _mme_raw(x, e, 0), e


def _doth_bwd(e, ct):
    return _mme_raw(ct, e, 1), jnp.zeros_like(e)


_doth.defvjp(_doth_fwd, _doth_bwd)


_BIG = (2048, 1536, 1408, 1024, 768, 704, 512, 384, 256, 128)


def _matmul(a, b, mode, out_dtype, name, add=None, col_blocks=1, plan=None):
    if mode == "nn":
        (M, R), (_, N) = a.shape, b.shape
        dims = (((1,), (0,)), ((), ()))
    elif mode == "nt":
        (M, R), (N, _) = a.shape, b.shape
        dims = (((1,), (1,)), ((), ()))
    else:
        (R, M), (_, N) = a.shape, b.shape
        dims = (((0,), (0,)), ((), ()))
    tm = _pick(M, (1408, 1024, 768, 512, 384, 256, 128))
    nb = N // col_blocks
    tn = _pick(nb, (1408, 1024, 896, 768, 704, 512, 384, 256, 128)) if mode == "tn" else _pick(nb, (512, 384, 256, 128))
    per = nb // tn
    tr = _pick(R, (1408, 1056, 768, 512, 384, 256, 128)) if mode == "tn" else _pick(R, _BIG)
    nr = R // tr

    if mode == "nn":
        a_spec = pl.BlockSpec((tm, tr), lambda i, j, r: (i, r))
        b_spec = pl.BlockSpec((tr, tn), lambda i, j, r: (r, j))
    elif mode == "nt":
        a_spec = pl.BlockSpec((tm, tr), lambda i, j, r: (i, r))
        b_spec = pl.BlockSpec((tn, tr), lambda i, j, r: (j, r))
    else:
        a_spec = pl.BlockSpec((tr, tm), lambda i, j, r: (r, i))
        b_spec = pl.BlockSpec((tr, tn), lambda i, j, r: (r, j))
    if col_blocks == 1:
        o_spec = pl.BlockSpec((tm, tn), lambda i, j, r: (i, j))
        o_shape = (M, N)
    else:
        o_spec = pl.BlockSpec((1, tm, tn), lambda i, j, r: (j // per, i, j % per))
        o_shape = (col_blocks, M, nb)
    has_add = add is not None

    grid = (M // tm, N // tn, nr)

    def body(*refs):
        own_in, p_in, (o_ref,), p_out, (acc,), sems = _carried(plan, refs, 3 if has_add else 2, 1, 1)
        a_ref, b_ref = own_in[:2]
        i, j, r = pl.program_id(0), pl.program_id(1), pl.program_id(2)
        if plan:
            @pl.when((i == 0) & (j == 0) & (r == 0))
            def _():
                plan["start"](p_in, p_out, *sems)

        @pl.when(r == 0)
        def _():
            acc[...] = jnp.zeros_like(acc)

        acc[...] += lax.dot_general(_bf(a_ref[...]), _bf(b_ref[...]), dims, preferred_element_type=F32)

        @pl.when(r == nr - 1)
        def _():
            res = acc[...]
            if has_add:
                res = res + own_in[2][...]
            o_ref[...] = res.astype(o_ref.dtype).reshape(o_ref.shape)

        if plan:
            @pl.when((i == grid[0] - 1) & (j == grid[1] - 1) & (r == nr - 1))
            def _():
                plan["wait"](p_in, p_out, *sems)

    ins = [a, b] + ([add] if has_add else [])
    specs = [a_spec, b_spec] + ([o_spec] if has_add else [])
    extra = plan["arrays"] if plan else []
    extra_out = plan["out_shapes"] if plan else []
    res = pl.pallas_call(
        body, name=name, out_shape=[jax.ShapeDtypeStruct(o_shape, out_dtype)] + extra_out,
        grid=grid, in_specs=specs + [ANY] * len(extra), out_specs=[o_spec] + [ANY] * len(extra_out),
        scratch_shapes=[pltpu.VMEM((tm, tn), F32)] + (_dma_sems(plan["nsem"]) if plan else []),
        compiler_params=_params(("arbitrary",) * 3 if plan else ("parallel", "parallel", "arbitrary")),
    )(*ins, *extra)
    return (res[0], list(res[1:])) if plan else res[0]


def _rowcall(name, fn, L, row_ins, bc_ins, row_outs, acc_outs, tm=ROW_TILE):
    nt = L // tm
    specs = []
    for arr, w, cb, kind in row_ins:
        if kind == "row":
            specs.append(pl.BlockSpec((tm, w), lambda i, cb=cb: (i, cb)))
        elif kind == "lag":
            specs.append(pl.BlockSpec((tm, w), lambda i, cb=cb: (jnp.maximum(i - 1, 0), cb)))
        elif kind == "prev":
            specs.append(pl.BlockSpec((8, w), lambda i, cb=cb: (jnp.maximum(i * (tm // 8) - 1, 0), cb)))
        else:
            specs.append(pl.BlockSpec((8, w), lambda i, cb=cb: (jnp.minimum((i + 1) * (tm // 8), L // 8 - 1), cb)))
    for arr in bc_ins:
        specs.append(pl.BlockSpec(arr.shape, lambda i, nd=arr.ndim: (0,) * nd))
    out_shapes = [jax.ShapeDtypeStruct((L, w), dt) for w, dt in row_outs]
    out_specs = [pl.BlockSpec((tm, w), lambda i: (i, 0)) for w, dt in row_outs]
    out_shapes += [jax.ShapeDtypeStruct(s, F32) for s in acc_outs]
    out_specs += [pl.BlockSpec(s, lambda i, nd=len(s): (0,) * nd) for s in acc_outs]
    n_row, n_bc, n_ro = len(row_ins), len(bc_ins), len(row_outs)

    def body(*refs):
        i = pl.program_id(0)
        vals = [r[...] for r in refs[: n_row + n_bc]]
        outs, sums = fn(i, vals[:n_row], vals[n_row:])
        o_refs = refs[n_row + n_bc:]
        for r, v in zip(o_refs[:n_ro], outs):
            r[...] = v.astype(r.dtype)

        @pl.when(i == 0)
        def _():
            for r in o_refs[n_ro:]:
                r[...] = jnp.zeros_like(r)

        for r, v in zip(o_refs[n_ro:], sums):
            r[...] += v

    res = pl.pallas_call(
        body, name=name, out_shape=out_shapes, grid=(nt,), in_specs=specs, out_specs=out_specs,
        compiler_params=_params(("arbitrary",)),
    )(*[a for a, _, _, _ in row_ins], *bc_ins)
    return list(res[:n_ro]), list(res[n_ro:])


def _rowsum(t):
    return jnp.sum(t, axis=0, keepdims=True)


def _head_mats(width):
    e = (np.arange(width)[:, None] // HEAD == np.arange(LANES)[None, :]).astype(np.float32)
    return jnp.asarray(e), jnp.asarray(e.T)


def _fold_mat(width):
    ft = (np.arange(HEAD)[:, None] == np.arange(width)[None, :] % HEAD).astype(np.float32)
    return jnp.asarray(ft)


def _rms_f(h, g):
    return (h * lax.rsqrt(jnp.mean(h * h, axis=-1, keepdims=True) + RMS_EPS)) * g


def _prep_f(xr, xk, xv, xl, w0, w2p, a0, a2p, g2p, k_k, k_a, e, et):
    w_log = -jax.nn.softplus(-(w0 + _dotb(jnp.tanh(xl), w2p))) - 0.5
    lw = -jnp.exp(w_log)
    a = jax.nn.sigmoid(a0 + _dotb(xl, a2p))
    g = _dotb(jax.nn.sigmoid(xl), g2p)
    kkr = xk * k_k
    inv = lax.rsqrt(jnp.maximum(_doth(kkr * kkr, e), 1e-24))
    kk = kkr * _doth(inv, et)
    kf = xk * (1.0 + (a - 1.0) * k_a)
    return xr, lw, kf, xv, -kk, kk * a, g


def _post_f(y, r, kf, v, g, gn_w, gn_b, r_k, e, et):
    mu = _doth(y, e) * (1.0 / HEAD)
    yc = y - _doth(mu, et)
    var = _doth(yc * yc, e) * (1.0 / HEAD)
    yn = yc * _doth(lax.rsqrt(var + GN_EPS), et) * gn_w + gn_b
    bonus = _doth(r * kf * r_k, e)
    return (yn + _doth(bonus, et) * v) * g


def _foxprep_f(q, k, fl, qg8, kg8, fb, e, et, ft, fmask):
    def norm(t, g8):
        ms = _doth(t * t, e) * (1.0 / HEAD)
        return t * _doth(lax.rsqrt(ms + RMS_EPS), et) * _doth(g8, ft)[0:1]
    logf = jax.nn.log_sigmoid(fl + fb) * fmask
    return norm(q, qg8), norm(k, kg8), logf


def _merge_f(za, zb, pa, pb):
    return jax.nn.sigmoid(za) * pa + jax.nn.sigmoid(zb) * pb


def _swiglu_f(gate, up):
    return jax.nn.silu(gate) * up


def _tri(n, strict):
    row = lax.broadcasted_iota(jnp.int32, (n, n), 0)
    col = lax.broadcasted_iota(jnp.int32, (n, n), 1)
    return (row > col) if strict else (row >= col)


def _split2(x):
    hi = x.astype(BF16)
    return hi, (x - hi.astype(F32)).astype(BF16)


def _mm3_raw(a, b, ca, cb):
    dn = (((ca,), (cb,)), ((), ()))
    ah, al = _split2(a)
    bh, bl = _split2(b)
    dot = lambda p, q: lax.dot_general(p, q, dn, preferred_element_type=F32)
    return dot(ah, bh) + (dot(al, bh) + dot(ah, bl))


@functools.partial(jax.custom_vjp, nondiff_argnums=(2, 3))
def _mm3(a, b, ca, cb):
    return _mm3_raw(a, b, ca, cb)


def _mm3_fwd(a, b, ca, cb):
    return _mm3_raw(a, b, ca, cb), (a, b)


def _mm3_bwd(ca, cb, res, ct):
    a, b = res
    da = _mm3_raw(ct, b, 1, 1 - cb) if ca == 1 else _mm3_raw(b, ct, 1 - cb, 1)
    db = _mm3_raw(a, ct, 1 - ca, 0) if cb == 0 else _mm3_raw(ct, a, 0, 1 - ca)
    return da, db


_mm3.defvjp(_mm3_fwd, _mm3_bwd)


def _mmx_raw(t, x, ct):
    x1 = x.astype(BF16)
    r1 = x - x1.astype(F32)
    x2 = r1.astype(BF16)
    x3 = (r1 - x2.astype(F32)).astype(BF16)
    tb = t.astype(BF16)
    dot = lambda q: lax.dot_general(tb, q, (((ct,), (0,)), ((), ())), preferred_element_type=F32)
    return dot(x1) + (dot(x2) + dot(x3))


@jax.custom_vjp
def _mmx(t, x):
    return _mmx_raw(t, x, 1)


def _mmx_fwd(t, x):
    return _mmx_raw(t, x, 1), t


def _mmx_bwd(t, ct):
    return jnp.zeros_like(t), _mmx_raw(t, ct, 0)


_mmx.defvjp(_mmx_fwd, _mmx_bwd)

SCAN_HEADS = 16


def _scan_step(r, lw, k, v, a, b, st):
    c = r.shape[0]
    nh = r.shape[1] // HEAD
    incl = _tri(c, False)
    row2 = lax.broadcasted_iota(jnp.int32, (2 * c, 2 * c), 0)
    col2 = lax.broadcasted_iota(jnp.int32, (2 * c, 2 * c), 1)
    t_row = jnp.where(row2 >= c, row2 - c, row2)
    t_col = jnp.where(col2 >= c, col2 - c, col2)
    mask2 = (t_row > t_col) | ((row2 >= c) & (t_row == t_col))
    right =lax.broadcasted_iota(jnp.int32, (c, 2 * c), 1) >= c
    eye = lax.broadcasted_iota(jnp.int32, (HEAD, HEAD), 0) == lax.broadcasted_iota(jnp.int32, (HEAD, HEAD), 1)
    cl = _mmx(incl.astype(F32), lw)
    last = cl[c - 1:c, :]
    rt = r * jnp.exp(cl)
    at = a * jnp.exp(cl - lw)
    pinv = jnp.exp(-cl)
    bt = b * pinv
    kt = k * pinv
    pend = jnp.exp(last - cl)
    bl = b * pend
    kl = k * pend
    pe_last = jnp.exp(last)
    hs = range(nh)
    sl = [slice(h * HEAD, (h + 1) * HEAD) for h in hs]
    ar = [jnp.concatenate([at[:, sl[h]], rt[:, sl[h]]], axis=0) for h in hs]
    bk = [jnp.concatenate([bt[:, sl[h]], kt[:, sl[h]]], axis=0) for h in hs]
    amat = [jnp.where(mask2, _mm3(ar[h], bk[h], 1, 1), 0.0) for h in hs]
    res = [_mm3(jnp.concatenate([ar[h], amat[h][:, c:]], axis=1),
                jnp.concatenate([st[h], v[:, sl[h]]], axis=0), 1, 0) for h in hs]
    z = [jnp.concatenate([amat[h][:c, :c], res[h][:c]], axis=1) for h in hs]
    for _ in range(max(1, int(np.ceil(np.log2(c))))):
        z = [_mm3(z[h][:, :c], z[h], 1, 0) + jnp.where(right, z[h], 0.0) for h in hs]
    u = [z[h][:, c:] for h in hs]
    ys = [res[h][c:] + _mm3(amat[h][c:, :c], u[h], 1, 0) for h in hs]
    s1s = [_mm3(jnp.concatenate([bl[:, sl[h]], kl[:, sl[h]], jnp.where(eye, pe_last[:, sl[h]], 0.0)], axis=0),
                jnp.concatenate([u[h], v[:, sl[h]], st[h]], axis=0), 0, 0) for h in hs]
    return tuple(ys), tuple(s1s)


def _scan_heads(W):
    return SCAN_HEADS if W % (SCAN_HEADS * HEAD) == 0 else 2


def _scan_fwd(r, lw, k, v, a, b):
    L, W = r.shape
    nh = _scan_heads(W)
    nc, ng = L // CHUNK, W // (nh * HEAD)
    spec = pl.BlockSpec((CHUNK, nh * HEAD), lambda p, c: (c, p))

    def body(r_ref, lw_ref, k_ref, v_ref, a_ref, b_ref, y_ref, s_ref, st):
        @pl.when(pl.program_id(1) == 0)
        def _():
            st[...] = jnp.zeros_like(st)

        s0 = st[...]
        ys, s1s = _scan_step(r_ref[...], lw_ref[...], k_ref[...], v_ref[...], a_ref[...], b_ref[...], s0)
        s_ref[0] = s0
        st[...] = jnp.stack(s1s)
        y_ref[...] = jnp.concatenate(ys, axis=1)

    return pl.pallas_call(
        body, name="wkv7_fwd",
        out_shape=[jax.ShapeDtypeStruct((L, W), F32), jax.ShapeDtypeStruct((nc, nh * ng, HEAD, HEAD), F32)],
        grid=(ng, nc), in_specs=[spec] * 6,
        out_specs=[spec, pl.BlockSpec((1, nh, HEAD, HEAD), lambda p, c: (c, p, 0, 0))],
        scratch_shapes=[pltpu.VMEM((nh, HEAD, HEAD), F32)],
        compiler_params=_params(("parallel", "arbitrary")),
    )(r, lw, k, v, a, b)


def _scan_bwd(r, lw, k, v, a, b, s_all, dy):
    L, W = r.shape
    nh = _scan_heads(W)
    nc, ng = L // CHUNK, W // (nh * HEAD)
    spec = pl.BlockSpec((CHUNK, nh * HEAD), lambda p, c: (nc - 1 - c, p))

    def body(r_ref, lw_ref, k_ref, v_ref, a_ref, b_ref, s_ref, dy_ref,
             dr_ref, dlw_ref, dk_ref, dv_ref, da_ref, db_ref, dst):
        @pl.when(pl.program_id(1) == 0)
        def _():
            dst[...] = jnp.zeros_like(dst)

        _, vjp = jax.vjp(_scan_step, r_ref[...], lw_ref[...], k_ref[...], v_ref[...], a_ref[...], b_ref[...],
                         s_ref[0])
        dys = tuple(dy_ref[:, h * HEAD:(h + 1) * HEAD] for h in range(nh))
        g = vjp((dys, tuple(dst[h] for h in range(nh))))
        for ref, val in zip((dr_ref, dlw_ref, dk_ref, dv_ref, da_ref, db_ref), g[:6]):
            ref[...] = val
        dst[...] = g[6]

    return pl.pallas_call(
        body, name="wkv7_bwd", out_shape=[jax.ShapeDtypeStruct((L, W), F32)] * 6,
        grid=(ng, nc),
        in_specs=[spec] * 6 + [pl.BlockSpec((1, nh, HEAD, HEAD), lambda p, c: (nc - 1 - c, p, 0, 0)), spec],
        out_specs=[spec] * 6,
        scratch_shapes=[pltpu.VMEM((nh, HEAD, HEAD), F32)],
        compiler_params=_params(("parallel", "arbitrary")),
    )(r, lw, k, v, a, b, s_all, dy)


def _cumsum_rows(logf):
    L = logf.shape[0]
    t = LANES

    def body(x_ref, o_ref, carry):
        @pl.when(pl.program_id(0) == 0)
        def _():
            carry[...] = jnp.zeros_like(carry)

        c = _mmx(_tri(t, False).astype(F32), x_ref[...]) + carry[...]
        carry[...] = c[t - 1:t, :]
        o_ref[...] = c.T

    return pl.pallas_call(
        body, name="fox_cumsum", out_shape=jax.ShapeDtypeStruct((LANES, L), F32), grid=(L // t,),
        in_specs=[pl.BlockSpec((t, LANES), lambda i: (i, 0))],
        out_specs=pl.BlockSpec((LANES, t), lambda i: (0, i)),
        scratch_shapes=[pltpu.VMEM((1, LANES), F32)], compiler_params=_params(("arbitrary",)),
    )(logf)


def _rcumsum_cols(dct):
    L = dct.shape[1]
    t = LANES
    n = L // t

    def body(x_ref, o_ref, carry):
        @pl.when(pl.program_id(0) == 0)
        def _():
            carry[...] = jnp.zeros_like(carry)

        rc = _doth(x_ref[...], _tri(t, False).astype(F32)) + carry[...]
        carry[...] = rc[:, 0:1]
        o_ref[...] = rc.T

    return pl.pallas_call(
        body, name="fox_rcumsum", out_shape=jax.ShapeDtypeStruct((L, LANES), F32), grid=(n,),
        in_specs=[pl.BlockSpec((LANES, t), lambda i: (0, n - 1 - i))],
        out_specs=pl.BlockSpec((t, LANES), lambda i: (n - 1 - i, 0)),
        scratch_shapes=[pltpu.VMEM((LANES, 1), F32)], compiler_params=_params(("arbitrary",)),
    )(dct)


def _attn_tile(L):
    return _pick(L, (384, 256, 128))


def _head_lanes(hh, shape):
    return lax.broadcasted_iota(jnp.int32, shape, len(shape) - 1) // HEAD == hh


def _own(hh, block, other=0):
    return jnp.where(_head_lanes(hh, block.shape), block, jnp.asarray(other, block.dtype))


def _attn_scores(q, k, ck, qi, kj, t):
    s = _dot_bnt(q, k) * ATTN_SCALE - ck
    qpos = qi * t + lax.broadcasted_iota(jnp.int32, (t, t), 0)
    kpos = kj * t + lax.broadcasted_iota(jnp.int32, (t, t), 1)
    mask = (kpos <= qpos) & (kpos >= PAD_ROWS)
    return jnp.where(mask, s, NEG), mask


def _dot_bnt(a, b):
    return lax.dot_general(_bf(a), _bf(b), (((1,), (1,)), ((), ())), preferred_element_type=F32)


def _dot_btn(a, b):
    return lax.dot_general(_bf(a), _bf(b), (((0,), (0,)), ((), ())), preferred_element_type=F32)


ATTN_HEADS = 2


def _attn_group(W):
    return ATTN_HEADS if W % (ATTN_HEADS * HEAD) == 0 else 2


def _ck_rows(ct_ref, p, g):
    r0 = (g * p) % 8
    return [ct_ref[pl.ds(r0 + hh, 1), :] for hh in range(g)]


def _carried(plan, refs, n_in, n_out, n_scratch):
    ci = len(plan["arrays"]) if plan else 0
    co = len(plan["out_shapes"]) if plan else 0
    a = n_in + ci
    b = a + n_out + co
    return (refs[:n_in], refs[n_in:a], refs[a:a + n_out], refs[a + n_out:b], refs[b:b + n_scratch],
            refs[b + n_scratch:])


def _attn_fwd(q, k, v, ct, plan=None):
    L, W = q.shape
    t = _attn_tile(L)
    g = _attn_group(W)
    gw = g * HEAD
    nt, npair = L // t, W // gw
    pairs = [(i, j) for i in range(nt) for j in range(i + 1)]
    it = jnp.asarray([i for i, _ in pairs], jnp.int32)
    jt = jnp.asarray([j for _, j in pairs], jnp.int32)
    ns = len(pairs)
    qspec = pl.BlockSpec((t, gw), lambda p, s, it, jt: (it[s], p))
    kspec = pl.BlockSpec((t, gw), lambda p, s, it, jt: (jt[s], p))
    cspec = pl.BlockSpec((8, t), lambda p, s, it, jt: (g * p // 8, jt[s]))

    def body(it_ref, jt_ref, *refs):
        (q_ref, k_ref, v_ref, ct_ref), p_in, (o_ref, lse_ref), p_out, (m_s, acc), sems = _carried(plan, refs, 4, 2, 2)
        p, s = pl.program_id(0), pl.program_id(1)
        i, j = it_ref[s], jt_ref[s]
        if plan:
            @pl.when((p == 0) & (s == 0))
            def _():
                plan["start"](p_in, p_out, *sems)

            if plan["mid"] is not None:
                @pl.when((p == npair // 2) & (s == 0))
                def _():
                    plan["mid"](p_in, p_out, *sems)

        @pl.when(j == 0)
        def _():
            m_s[...] = jnp.full_like(m_s, NEG)
            acc[...] = jnp.zeros_like(acc)

        def accumulate():
            cks = _ck_rows(ct_ref, p, g)
            qb, kb, vf = _bf(q_ref[...]), _bf(k_ref[...]), v_ref[...]
            ss = [_attn_scores(_own(hh, qb), kb, cks[hh], i, j, t)[0] for hh in range(g)]
            prs, alphas = [], []
            for hh in range(g):
                m_old = m_s[hh]
                m_new = jnp.maximum(m_old, jnp.max(ss[hh], axis=-1, keepdims=True))
                alphas.append(jnp.exp(m_old - m_new))
                prs.append(jnp.exp(ss[hh] - m_new))
                m_s[hh] = m_new
            pvs = []
            for hh in range(g):
                p_hi, p_lo = _split2(prs[hh])
                vx = _bf(_own(hh, vf, 1.0))
                pvs.append(jnp.dot(p_hi, vx, preferred_element_type=F32)
                           + jnp.dot(p_lo, vx, preferred_element_type=F32))
            for hh in range(g):
                acc[hh] = alphas[hh] * acc[hh] + pvs[hh]

        accumulate()

        @pl.when(j == i)
        def _():
            lane = lax.broadcasted_iota(jnp.int32, (t, LANES), 1)
            lse = jnp.zeros((t, LANES), F32)
            out = jnp.zeros((t, gw), F32)
            for hh in range(g):
                a = acc[hh]
                row_sum = pltpu.roll(a, HEAD, 1)
                out = jnp.where(_head_lanes(hh, a.shape), a / row_sum, out)
                nb = ((hh + 1) % g) * HEAD
                lse = jnp.where(lane == hh, m_s[hh] + jnp.log(a[:, nb:nb + 1]), lse)
            o_ref[...] = out
            lse_ref[0] = lse

        if plan:
            @pl.when((p == npair - 1) & (s == ns - 1))
            def _():
                plan["wait"](p_in, p_out, *sems)

    extra = plan["arrays"] if plan else []
    extra_out = plan["out_shapes"] if plan else []
    res = pl.pallas_call(
        body, name="fox_attn_fwd",
        out_shape=[jax.ShapeDtypeStruct((L, W), F32), jax.ShapeDtypeStruct((npair, L, LANES), F32)] + extra_out,
        grid_spec=pltpu.PrefetchScalarGridSpec(
            num_scalar_prefetch=2, grid=(npair, ns),
            in_specs=[qspec, kspec, kspec, cspec] + [ANY] * len(extra),
            out_specs=[qspec, pl.BlockSpec((1, t, LANES), lambda p, s, it, jt: (p, it[s], 0))]
            + [ANY] * len(extra_out),
            scratch_shapes=[pltpu.VMEM((g, t, 1), F32), pltpu.VMEM((g, t, gw), F32)]
            + (_dma_sems(plan["nsem"]) if plan else [])),
        compiler_params=_params(("arbitrary",) * 2 if plan else ("parallel", "arbitrary")),
    )(it, jt, q, k, v, ct, *extra)
    return res[0], res[1], list(res[2:])


def _attn_bwd(q, k, v, ct, o, lse, do, plan=None):
    L, W = q.shape
    t = _attn_tile(L)
    g = _attn_group(W)
    gw = g * HEAD
    nt, npair = L // t, W // gw
    pairs = [(i, j) for j in range(nt) for i in range(j, nt)]
    it = jnp.asarray([i for i, _ in pairs], jnp.int32)
    jt = jnp.asarray([j for _, j in pairs], jnp.int32)
    ns = len(pairs)
    kspec = pl.BlockSpec((t, gw), lambda p, s, it, jt: (jt[s], p))
    qspec = pl.BlockSpec((t, gw), lambda p, s, it, jt: (it[s], p))
    cspec = pl.BlockSpec((8, t), lambda p, s, it, jt: (g * p // 8, jt[s]))
    lspec = pl.BlockSpec((1, t, LANES), lambda p, s, it, jt: (p, it[s], 0))

    def body(it_ref, jt_ref, *refs):
        ((q_ref, k_ref, v_ref, ct_ref, o_ref, lse_ref, do_ref), p_in, (dk_ref, dv_ref, dc_ref, dq_ref), p_out,
         (dk_s, dv_s, dc_s), sems) = _carried(plan, refs, 7, 4, 3)
        p, s = pl.program_id(0), pl.program_id(1)
        i, j = it_ref[s], jt_ref[s]
        if plan:
            @pl.when((p == 0) & (s == 0))
            def _():
                plan["start"](p_in, p_out, *sems)

        @pl.when(i == j)
        def _():
            dk_s[...] = jnp.zeros_like(dk_s)
            dv_s[...] = jnp.zeros_like(dv_s)
            dc_s[...] = jnp.zeros_like(dc_s)

        def tile_dq():
            cks = _ck_rows(ct_ref, p, g)
            qb, kb, vb, dob = _bf(q_ref[...]), _bf(k_ref[...]), _bf(v_ref[...]), _bf(do_ref[...])
            of = o_ref[...]
            hs = range(g)
            qm = [_own(hh, qb) for hh in hs]
            dom = [_own(hh, dob) for hh in hs]
            sm = [_attn_scores(qm[hh], kb, cks[hh], i, j, t) for hh in hs]
            dps = [_dot_bnt(dom[hh], vb) for hh in hs]
            prs = [jnp.where(sm[hh][1], jnp.exp(sm[hh][0] - lse_ref[0, :, hh:hh + 1]), 0.0) for hh in hs]
            dss = [prs[hh] * (dps[hh] - jnp.sum(dom[hh].astype(F32) * of, axis=-1, keepdims=True)) for hh in hs]
            dv_s[...] += sum(_dot_btn(prs[hh], dom[hh]) for hh in hs)
            dk_s[...] += sum(_dot_btn(dss[hh], qm[hh]) for hh in hs)
            for hh in hs:
                dc_s[hh] += -jnp.sum(dss[hh], axis=0, keepdims=True)
            return sum(_dotb(dss[hh], _own(hh, kb)) for hh in hs) * ATTN_SCALE

        rows = pl.ds(pl.multiple_of(i * t, t), t)

        @pl.when(j == 0)
        def _():
            dq_ref[rows, :] = tile_dq()

        @pl.when(j > 0)
        def _():
            dq_ref[rows, :] += tile_dq()

        @pl.when(i == nt - 1)
        def _():
            row = lax.broadcasted_iota(jnp.int32, (8, t), 0)
            dc = jnp.zeros((8, t), F32)
            for hh in range(g):
                dc = jnp.where(row == hh, dc_s[hh], dc)
            dk_ref[...] = dk_s[...] * ATTN_SCALE
            dv_ref[...] = dv_s[...]
            dc_ref[0] = dc

        if plan:
            @pl.when((p == npair - 1) & (s == ns - 1))
            def _():
                plan["wait"](p_in, p_out, *sems)

    extra = plan["arrays"] if plan else []
    extra_out = plan["out_shapes"] if plan else []
    res = pl.pallas_call(
        body, name="fox_attn_bwd",
        out_shape=[jax.ShapeDtypeStruct((L, W), F32), jax.ShapeDtypeStruct((L, W), F32),
                   jax.ShapeDtypeStruct((npair, 8, L), F32), jax.ShapeDtypeStruct((L, W), F32)] + extra_out,
        grid_spec=pltpu.PrefetchScalarGridSpec(
            num_scalar_prefetch=2, grid=(npair, ns),
            in_specs=[qspec, kspec, kspec, cspec, qspec, lspec, qspec] + [ANY] * len(extra),
            out_specs=[kspec, kspec, pl.BlockSpec((1, 8, t), lambda p, s, it, jt: (p, 0, jt[s])),
                       pl.BlockSpec((L, gw), lambda p, s, it, jt: (0, p))] + [ANY] * len(extra_out),
            scratch_shapes=[pltpu.VMEM((t, gw), F32), pltpu.VMEM((t, gw), F32), pltpu.VMEM((g, 1, t), F32)]
            + (_dma_sems(plan["nsem"]) if plan else [])),
        compiler_params=_params(("arbitrary",) * 2 if plan else ("parallel", "arbitrary")),
    )(it, jt, q, k, v, ct, o, lse, do, *extra)
    return res[0], res[1], res[2], res[3], list(res[4:])


def _place():
    x, y, c = lax.axis_index("x"), lax.axis_index("y"), lax.axis_index("c")
    chips = [(1 - x, y), (x, 1 - y), (1 - x, 1 - y)]
    return x, y, c, chips


def _remote(src, dst, send_sems, recv_sems, k, to):
    return pltpu.make_async_remote_copy(src_ref=src, dst_ref=dst, send_sem=send_sems.at[k],
                                        recv_sem=recv_sems.at[k], device_id=to, device_id_type=MESH)


def _dma_sems(n):
    return [pltpu.SemaphoreType.DMA((n,)), pltpu.SemaphoreType.DMA((n,))]


def _all_gather(shards, modes):
    return _run_exchange("gather_weights", _gather_plan(shards, modes))


def _gather_plan(shards, modes):
    n = len(shards)

    def out_shape(s, mode):
        r, c = s.shape
        return {"stack": (4, r, c), "cols": (r, 4 * c), "rows": (4 * r, c)}[mode]

    def window(outs, i, chip, cc):
        r, cw = shards[i].shape
        h = r // 2
        if modes[i] == "stack":
            return outs[i].at[chip, pl.ds(cc * h, h), :]
        if modes[i] == "cols":
            return outs[i].at[pl.ds(cc * h, h), pl.ds(pl.multiple_of(chip * cw, LANES), cw)]
        return outs[i].at[pl.ds(pl.multiple_of(chip * r + cc * h, 8), h), :]

    def first(ins, outs, ss, rs):
        x, y, c, chips = _place()
        cps = []
        for i in range(n):
            h = shards[i].shape[0] // 2
            for k, (cx, cy) in enumerate(chips):
                cps.append(_remote(ins[i].at[pl.ds(c * h, h), :], window(outs, i, 2 * x + y, c), ss, rs,
                                   6 * i + k, (cx, cy, c)))
        return cps

    def passed(outs, ss, rs):
        x, y, c, chips = _place()
        cps = []
        for k, (cx, cy) in enumerate(chips):
            for i in range(n):
                landed = window(outs, i, 2 * cx + cy, c)
                cps.append((_remote(landed, landed, ss, rs, 6 * i + k, (x, y, 1 - c)),
                            _remote(landed, landed, ss, rs, 6 * i + 3 + k, (x, y, 1 - c))))
        return cps

    def start(ins, outs, ss, rs):
        for cp in first(ins, outs, ss, rs):
            cp.start()

    def mid(ins, outs, ss, rs):
        for arrival, fwd in passed(outs, ss, rs):
            arrival.wait_recv()
            fwd.start()

    def wait(ins, outs, ss, rs):
        x, y, c, chips = _place()
        for k, (cx, cy) in enumerate(chips):
            for i in range(n):
                other = window(outs, i, 2 * cx + cy, 1 - c)
                _remote(other, other, ss, rs, 6 * i + 3 + k, (x, y, 1 - c)).wait_recv()
        for cp in first(ins, outs, ss, rs) + [fwd for _, fwd in passed(outs, ss, rs)]:
            cp.wait_send()

    return dict(arrays=list(shards), nsem=6 * n, start=start, mid=mid, wait=wait,
                out_shapes=[jax.ShapeDtypeStruct(out_shape(s, m), s.dtype) for s, m in zip(shards, modes)])


def _run_exchange(name, plan):
    n = len(plan["arrays"])

    def body(*refs):
        ins, outs, (ss, rs) = refs[:n], refs[n:n + len(plan["out_shapes"])], refs[n + len(plan["out_shapes"]):]
        plan["start"](ins, outs, ss, rs)
        if plan["mid"] is not None:
            plan["mid"](ins, outs, ss, rs)
        plan["wait"](ins, outs, ss, rs)

    return pl.pallas_call(
        body, name=name, out_shape=plan["out_shapes"], in_specs=[ANY] * n,
        out_specs=[ANY] * len(plan["out_shapes"]), scratch_shapes=_dma_sems(plan["nsem"]),
    )(*plan["arrays"])


def _pair_exchange(name, blocks):
    return _run_exchange(name, _pair_exchange_plan(blocks))


def _pair_exchange_plan(blocks):
    n = len(blocks)

    def copies(ins, outs, ss, rs):
        x, y, c, _ = _place()
        cps = []
        for i in range(n):
            h = blocks[i].shape[1] // 2
            cps.append(_remote(ins[i].at[:, pl.ds((1 - c) * h, h), :], outs[i], ss, rs, i, (x, y, 1 - c)))
        return cps

    def start(ins, outs, ss, rs):
        for cp in copies(ins, outs, ss, rs):
            cp.start()

    def wait(ins, outs, ss, rs):
        for cp in copies(ins, outs, ss, rs):
            cp.wait()

    return dict(arrays=list(blocks), nsem=n, start=start, mid=None, wait=wait,
                out_shapes=[jax.ShapeDtypeStruct((4, b.shape[1] // 2, b.shape[2]), b.dtype) for b in blocks])


def _chip_exchange(parts):
    return _run_exchange("reduce_chip_exchange", _chip_exchange_plan(parts))


def _chip_exchange_plan(parts):
    n = len(parts)

    def sends(ins, outs, ss, rs):
        x, y, c, chips = _place()
        return [_remote(ins[i].at[2 * cx + cy], outs[i].at[2 * x + y], ss, rs, 3 * i + k, (cx, cy, c))
                for i in range(n) for k, (cx, cy) in enumerate(chips)]

    def start(ins, outs, ss, rs):
        for cp in sends(ins, outs, ss, rs):
            cp.start()

    def wait(ins, outs, ss, rs):
        x, y, c, chips = _place()
        for i in range(n):
            for k, (cx, cy) in enumerate(chips):
                slot = outs[i].at[2 * cx + cy]
                _remote(slot, slot, ss, rs, 3 * i + k, (cx, cy, c)).wait_recv()
        for cp in sends(ins, outs, ss, rs):
            cp.wait_send()

    return dict(arrays=list(parts), nsem=3 * n, start=start, mid=None, wait=wait,
                out_shapes=[jax.ShapeDtypeStruct(p.shape, p.dtype) for p in parts])


def _pair_join(tots):
    n = len(tots)

    def body(*refs):
        ins, outs, (send_sems, recv_sems) = refs[:n], refs[n:2 * n], refs[2 * n:]
        x, y, c, _ = _place()
        cps = [_remote(ins[i], outs[i], send_sems, recv_sems, i, (x, y, 1 - c)) for i in range(n)]
        for cp in cps:
            cp.start()
        for cp in cps:
            cp.wait()

    return pl.pallas_call(
        body, name="reduce_pair_join", out_shape=[jax.ShapeDtypeStruct(t.shape, t.dtype) for t in tots],
        in_specs=[ANY] * n, out_specs=[ANY] * n, scratch_shapes=_dma_sems(n),
    )(*tots)


def _add_tile(h, cw):
    cap = max(8, (512 * 1024) // max(cw, 1))
    return _pick(h, tuple(t for t in (1024, 512, 256, 128, 64, 32, 16, 8) if t <= cap))


def _pair_add(name, block, recv):
    n, r, cw = block.shape
    h = r // 2
    tr = _add_tile(h, cw)
    c = lax.axis_index("c").astype(jnp.int32).reshape((1,))

    def body(c_ref, a_ref, b_ref, o_ref):
        o_ref[...] = (a_ref[...] + b_ref[...]).astype(o_ref.dtype)

    return pl.pallas_call(
        body, name=name, out_shape=jax.ShapeDtypeStruct((n, h, cw), BF16),
        grid_spec=pltpu.PrefetchScalarGridSpec(
            num_scalar_prefetch=1, grid=(n, h // tr),
            in_specs=[pl.BlockSpec((1, tr, cw), lambda a, i, cr: (a, cr[0] * (h // tr) + i, 0)),
                      pl.BlockSpec((1, tr, cw), lambda a, i, cr: (a, i, 0))],
            out_specs=pl.BlockSpec((1, tr, cw), lambda a, i, cr: (a, i, 0))),
        compiler_params=_params(("parallel", "parallel")),
    )(c, block, recv)


def _chip_add(name, parts):
    n, h, cw = parts.shape
    tr = _add_tile(h, cw)

    def body(a_ref, o_ref):
        f = lambda i: a_ref[i].astype(F32)
        o_ref[...] = ((f(0) + f(1)) + f(2)) + f(3)

    return pl.pallas_call(
        body, name=name, out_shape=jax.ShapeDtypeStruct((h, cw), F32), grid=(h // tr,),
        in_specs=[pl.BlockSpec((n, tr, cw), lambda i: (0, i, 0))],
        out_specs=pl.BlockSpec((tr, cw), lambda i: (i, 0)),
        compiler_params=_params(("parallel",)),
    )(parts)


def _adamw(name, w, g, m, v):
    R, C = w.shape
    tr = _pick(R, (128, 64, 32, 16, 8))
    spec = pl.BlockSpec((tr, C), lambda i: (i, 0))

    def body(w_ref, g_ref, m_ref, v_ref, d_ref, mo_ref, vo_ref):
        gr = g_ref[...]
        mn = ADAM_B1 * m_ref[...] + (1.0 - ADAM_B1) * gr
        vn = ADAM_B2 * v_ref[...] + (1.0 - ADAM_B2) * jnp.square(gr)
        m_hat = mn / (1.0 - ADAM_B1 ** ADAM_STEP)
        v_hat = vn / (1.0 - ADAM_B2 ** ADAM_STEP)
        d_ref[...] = -ADAM_LR * (m_hat / (jnp.sqrt(v_hat) + ADAM_EPS) + ADAM_WD * w_ref[...])
        mo_ref[...] = mn
        vo_ref[...] = vn

    return pl.pallas_call(
        body, name=name, out_shape=[jax.ShapeDtypeStruct((R, C), F32)] * 3, grid=(R // tr,),
        in_specs=[spec] * 4, out_specs=[spec] * 3, compiler_params=_params(("parallel",)),
    )(w, g, m, v)


def _pack(parts, dtype, row_mult):
    flat = jnp.concatenate([p.reshape(-1).astype(dtype) for p in parts])
    unit = row_mult * PACK_W
    pad = (-flat.shape[0]) % unit
    if pad:
        flat = jnp.concatenate([flat, jnp.zeros((pad,), dtype)])
    return flat.reshape(-1, PACK_W)


def _unpack(flat, shapes):
    out, off = [], 0
    for s in shapes:
        n = int(np.prod(s))
        out.append(flat[off:off + n].reshape(s))
        off += n
    return out


SHARDED = ("w_in", "rwkv_w2", "rwkv_a2", "rwkv_g2", "w_branch_a", "w_branch_b", "w_o", "w_gate_up", "w_down",
           "meta_tokens")
SHARD_AXIS = {"w_in": 1, "rwkv_w2": 1, "rwkv_a2": 1, "rwkv_g2": 1, "w_branch_a": 1, "w_branch_b": 1, "w_o": 0,
              "w_gate_up": 1, "w_down": 0, "meta_tokens": 1}
GATHER_MODE = {"w_in": "stack", "rwkv_w2": "stack", "rwkv_a2": "stack", "rwkv_g2": "stack", "w_branch_a": "cols",
               "w_branch_b": "cols", "w_o": "rows", "w_gate_up": "cols", "w_down": "rows", "meta_tokens": "stack"}
LATE_GATHER = ("w_branch_a", "w_branch_b", "w_o", "w_gate_up", "w_down")
SMALL = ("norm1_g", "rwkv_mu", "rwkv_w0", "rwkv_a0", "rwkv_k_k", "rwkv_k_a", "rwkv_r_k", "rwkv_gn_w",
         "rwkv_gn_b", "fox_q_norm_g", "fox_k_norm_g", "fox_f_bias", "norm2_g")
WEIGHTS = ("meta_tokens", "norm1_g", "w_in", "rwkv_mu", "rwkv_w0", "rwkv_w2", "rwkv_a0", "rwkv_a2", "rwkv_g2",
           "rwkv_k_k", "rwkv_k_a", "rwkv_r_k", "rwkv_gn_w", "rwkv_gn_b", "fox_q_norm_g", "fox_k_norm_g",
           "fox_f_bias", "w_branch_a", "w_branch_b", "w_o", "norm2_g", "w_gate_up", "w_down")


def _pad_rows(t, rows):
    return jnp.concatenate([t, jnp.zeros((rows - t.shape[0],) + t.shape[1:], t.dtype)], axis=0)


def _pad_cols(t, cols):
    return jnp.concatenate([t, jnp.zeros(t.shape[:-1] + (cols - t.shape[-1],), t.dtype)], axis=-1)


def _step(x, tgt, wts, mom1, mom2):
    seq, D = x.shape
    L = SEQ_ROW0 + seq
    RW = wts["rwkv_w0"].shape[-1]
    DL, AL, GL = wts["rwkv_w2"].shape[0], wts["rwkv_a2"].shape[0], wts["rwkv_g2"].shape[0]
    FW = wts["w_branch_b"].shape[0]
    FH = wts["fox_f_bias"].shape[-1]
    DFF = wts["w_down"].shape[0] * 4
    LORA = DL + AL + GL
    LW = -(-(LORA + FH) // 512) * 512
    assert RW == FW and (6 * RW) % D == 0 and (6 * RW + 2 * D) % LW == 0 and LORA % 8 == 0
    xj = lax.axis_index("x")
    yj = lax.axis_index("y")
    chip = 2 * xj + yj

    send = {n: wts[n] if n == "meta_tokens" else wts[n].astype(BF16) for n in SHARDED}
    early = tuple(n for n in SHARDED if n not in LATE_GATHER)
    full, stacked = {}, {}

    def place_own(n, got):
        shard, mode = send[n], GATHER_MODE[n]
        r, cw = shard.shape
        if mode == "stack":
            got = lax.dynamic_update_index_in_dim(got, shard, chip, 0)
            stacked[n] = got
            full[n] = jnp.concatenate([got[j] for j in range(4)], axis=1)
        elif mode == "cols":
            full[n] = lax.dynamic_update_slice(got, shard, (0, chip * cw))
        else:
            full[n] = lax.dynamic_update_slice(got, shard, (chip * r, 0))

    for n, got in zip(early, _all_gather([send[n] for n in early], [GATHER_MODE[n] for n in early])):
        place_own(n, got)
    late_plan = _gather_plan([send[n] for n in LATE_GATHER], [GATHER_MODE[n] for n in LATE_GATHER])
    meta = full["meta_tokens"]
    w_in = full["w_in"]
    o = 0
    segs = {}
    for nm, wd in (("r", RW), ("k", RW), ("v", RW), ("wd", DL), ("ad", AL), ("gd", GL),
                   ("fq", FW), ("fk", FW), ("fv", FW), ("ff", FH), ("ga", D), ("gb", D)):
        segs[nm] = w_in[:, o:o + wd]
        o += wd
    lora_w = _pad_cols(jnp.concatenate([segs["wd"], segs["ad"], segs["gd"], segs["ff"]], axis=1), LW)
    w1 = jnp.concatenate([segs["r"], segs["k"], segs["v"], segs["fq"], segs["fk"], segs["fv"],
                          segs["ga"], segs["gb"], lora_w], axis=1)
    cb_f = 3
    cb_gate = (6 * RW) // D
    cb_lora = (6 * RW + 2 * D) // LW

    e_m, et_m = _head_mats(RW)
    ft_m = _fold_mat(RW)
    mu = wts["rwkv_mu"]
    mu_rkv = mu[:, :3 * RW]
    mu_l = _pad_cols(mu[:, 3 * RW:], LW)
    w2p = _pad_rows(full["rwkv_w2"].astype(F32), LW)
    a2p = _pad_rows(jnp.concatenate([jnp.zeros((DL, RW), F32), full["rwkv_a2"].astype(F32)], axis=0), LW)
    g2p = _pad_rows(jnp.concatenate([jnp.zeros((DL + AL, RW), F32), full["rwkv_g2"].astype(F32)], axis=0), LW)
    r_k = wts["rwkv_r_k"].reshape(1, RW)
    qg8 = jnp.broadcast_to(wts["fox_q_norm_g"], (8, HEAD))
    kg8 = jnp.broadcast_to(wts["fox_k_norm_g"], (8, HEAD))
    fb = _pad_cols(wts["fox_f_bias"], LANES)
    fmask = (jnp.arange(LANES) < FH).astype(F32).reshape(1, LANES)
    lmask = ((jnp.arange(LW) >= LORA) & (jnp.arange(LW) < LORA + FH)).astype(F32).reshape(1, LW)

    h0 = jnp.concatenate([jnp.zeros((PAD_ROWS, D), F32), meta, x], axis=0)
    n1 = wts["norm1_g"]

    (xn,), _ = _rowcall("rms1_fwd", lambda i, r, b: ([_rms_f(r[0], b[0])], []), L,
                        [(h0, D, 0, "row")], [n1], [(D, BF16)], [])
    proj = _matmul(xn, w1, "nn", F32, "proj_fwd")

    def shift_fn(i, r, b):
        rows = lax.broadcasted_iota(jnp.int32, (ROW_TILE, 1), 0)
        outs = []
        for z, halo, m_ in ((r[0], r[1], b[0]), (r[2], r[3], b[1])):
            first = jnp.where(i == 0, 0.0, halo[7:8, :])
            zp = jnp.where(rows == 0, first, pltpu.roll(z, 1, 0))
            outs.append(z + (zp - z) * m_)
        return outs, []

    rkv_w = 3 * RW
    (x_rkv, x_l), _ = _rowcall(
        "shift_fwd", shift_fn, L,
        [(proj, rkv_w, 0, "row"), (proj, rkv_w, 0, "prev"), (proj, LW, cb_lora, "row"), (proj, LW, cb_lora, "prev")],
        [mu_rkv, mu_l], [(rkv_w, F32), (LW, F32)], [])

    prep_p = [wts["rwkv_w0"], w2p, wts["rwkv_a0"], a2p, g2p, wts["rwkv_k_k"], wts["rwkv_k_a"], e_m, et_m]
    prep_rows = [(x_rkv, RW, 0, "row"), (x_rkv, RW, 1, "row"), (x_rkv, RW, 2, "row"), (x_l, LW, 0, "row")]
    (s_r, s_lw, s_k, s_v, s_a, s_b, gate_g), _ = _rowcall(
        "rwkv_prep_fwd", lambda i, r, b: (list(_prep_f(*r, *b)), []), L, prep_rows, prep_p,
        [(RW, F32)] * 7, [])
    y_scan, s_all = _scan_fwd(s_r, s_lw, s_k, s_v, s_a, s_b)
    post_p = [wts["rwkv_gn_w"], wts["rwkv_gn_b"], r_k, e_m, et_m]
    post_rows = [(y_scan, RW, 0, "row"), (s_r, RW, 0, "row"), (s_k, RW, 0, "row"), (s_v, RW, 0, "row"),
                 (gate_g, RW, 0, "row")]
    (y_a,), _ = _rowcall("rwkv_post_fwd", lambda i, r, b: ([_post_f(*r, *b)], []), L, post_rows, post_p,
                         [(RW, BF16)], [])

    fox_p = [qg8, kg8, fb, e_m, et_m, ft_m, fmask]

    def foxprep_fn(i, r, b):
        fl = _doth(r[2] * b[-1], b[-2])
        return list(_foxprep_f(r[0], r[1], fl, *b[:-2])), []

    sel = (np.arange(LW)[:, None] - LORA == np.arange(LANES)[None, :]).astype(np.float32)
    sel = jnp.asarray(sel)
    fox_rows = [(proj, FW, cb_f, "row"), (proj, FW, cb_f + 1, "row"), (proj, LW, cb_lora, "row")]
    (f_q, f_k, logf), _ = _rowcall("fox_prep_fwd", foxprep_fn, L, fox_rows, fox_p + [sel, lmask],
                                   [(FW, BF16), (FW, BF16), (LANES, F32)], [])
    ct = _cumsum_rows(logf)
    f_v = proj[:, (cb_f + 2) * FW:(cb_f + 3) * FW]
    y_b32, lse, got_late = _attn_fwd(f_q, f_k, f_v, ct, plan=late_plan)
    for n, got in zip(LATE_GATHER, got_late):
        place_own(n, got)
    y_b = y_b32.astype(BF16)

    p_a = _matmul(y_a, full["w_branch_a"], "nn", F32, "branch_a_fwd")
    p_b = _matmul(y_b, full["w_branch_b"], "nn", F32, "branch_b_fwd")
    merge_rows = [(proj, D, cb_gate, "row"), (proj, D, cb_gate + 1, "row"), (p_a, D, 0, "row"), (p_b, D, 0, "row")]
    (merged,), _ = _rowcall("merge_fwd", lambda i, r, b: ([_merge_f(*r)], []), L, merge_rows, [], [(D, BF16)], [])
    h1 = _matmul(merged, full["w_o"], "nn", F32, "wo_fwd", add=h0)
    n2 = wts["norm2_g"]
    (xn2,), _ = _rowcall("rms2_fwd", lambda i, r, b: ([_rms_f(r[0], b[0])], []), L,
                         [(h1, D, 0, "row")], [n2], [(D, BF16)], [])
    gu = _matmul(xn2, full["w_gate_up"], "nn", F32, "gate_up_fwd")
    gu_rows = [(gu, DFF, 0, "row"), (gu, DFF, 1, "row")]
    (act,), _ = _rowcall("swiglu_fwd", lambda i, r, b: ([_swiglu_f(*r)], []), L, gu_rows, [], [(DFF, BF16)], [])
    h2 = _matmul(act, full["w_down"], "nn", F32, "down_fwd", add=h1)

    def loss_fn(i, r, b):
        err = jnp.where(i == 0, 0.0, r[0] - r[1])
        return [err * (1.0 / D)], [jnp.zeros((8, LANES), F32) + 0.5 / D * jnp.sum(err * err)]

    (dh2,), (loss_acc,) = _rowcall("loss", loss_fn, L, [(h2, D, 0, "row"), (tgt, D, 0, "lag")], [],
                                   [(D, F32)], [(8, LANES)])
    loss = lax.psum(loss_acc[0, 0], ("x", "y", "c"))

    dh2b = dh2.astype(BF16)
    g_w_down = _matmul(act, dh2b, "tn", F32, "down_dw")
    d_act = _matmul(dh2b, full["w_down"], "nt", F32, "down_dx")

    def swiglu_bwd(i, r, b):
        _, vjp = jax.vjp(_swiglu_f, r[0], r[1])
        return list(vjp(r[2])), []

    (d_gate, d_up), _ = _rowcall("swiglu_bwd", swiglu_bwd, L, gu_rows + [(d_act, DFF, 0, "row")], [],
                                 [(DFF, BF16), (DFF, BF16)], [])
    d_gu = jnp.concatenate([d_gate, d_up], axis=1)
    g_w_gu = _matmul(xn2, d_gu, "tn", F32, "gate_up_dw", col_blocks=4)
    ffn_blocks = [g_w_gu, g_w_down.reshape(4, -1, D)]
    d_xn2, ffn_recv = _matmul(d_gu, full["w_gate_up"], "nt", F32, "gate_up_dx", plan=_pair_exchange_plan(ffn_blocks))

    def rms_bwd(i, r, b):
        _, vjp = jax.vjp(_rms_f, r[0], b[0])
        dh, dg = vjp(r[1])
        return [dh + r[2]], [dg]

    (dh1,), (g_n2,) = _rowcall("rms2_bwd", rms_bwd, L,
                               [(h1, D, 0, "row"), (d_xn2, D, 0, "row"), (dh2, D, 0, "row")], [n2],
                               [(D, F32)], [(1, D)])
    dh1b = dh1.astype(BF16)
    g_w_o = _matmul(merged, dh1b, "tn", F32, "wo_dw")
    d_merged = _matmul(dh1b, full["w_o"], "nt", F32, "wo_dx")

    def merge_bwd(i, r, b):
        _, vjp = jax.vjp(_merge_f, *r[:4])
        return list(vjp(r[4])), []

    (d_za, d_zb, d_pa, d_pb), _ = _rowcall("merge_bwd", merge_bwd, L, merge_rows + [(d_merged, D, 0, "row")], [],
                                           [(D, BF16)] * 4, [])
    g_w_a = _matmul(y_a, d_pa, "tn", F32, "branch_a_dw", col_blocks=4)
    g_w_b = _matmul(y_b, d_pb, "tn", F32, "branch_b_dw", col_blocks=4)
    d_ya = _matmul(d_pa, full["w_branch_a"], "nt", F32, "branch_a_dx")
    d_yb = _matmul(d_pb, full["w_branch_b"], "nt", F32, "branch_b_dx")

    cj = lax.axis_index("c")
    names1 = ("w_gate_up", "w_down", "w_branch_a", "w_branch_b", "w_o")
    blocks1 = [g_w_a, g_w_b, g_w_o.reshape(4, -1, D)]
    parts1 = [_pair_add("reduce_pair_add_" + n, b, r)
              for n, b, r in zip(names1, ffn_blocks + blocks1,
                                 list(ffn_recv) + list(_pair_exchange("reduce_pair_exchange_1", blocks1)))]
    d_fk, d_fv, dc_rows, d_fq, recv1 = _attn_bwd(f_q, f_k, f_v, ct, y_b32, lse, d_yb,
                                                 plan=_chip_exchange_plan(parts1))
    dct = _pad_rows(dc_rows[:, :_attn_group(FW), :].reshape(-1, L), LANES)
    d_logf = _rcumsum_cols(dct)

    def foxprep_bwd(i, r, b):
        def f(q, k, xl, qg, kg, fbias):
            return _foxprep_f(q, k, _doth(xl * b[-1], b[-2]), qg, kg, fbias, *b[3:7])
        _, vjp = jax.vjp(f, r[0], r[1], r[2], b[0], b[1], b[2])
        dq, dk, dxl, dqg, dkg, dfb = vjp((r[3], r[4], r[5]))
        return [dq, dk, dxl], [dqg, dkg, dfb]

    (d_zfq, d_zfk, d_zl_f), (g_qg8, g_kg8, g_fb) = _rowcall(
        "fox_prep_bwd", foxprep_bwd, L,
        fox_rows + [(d_fq, FW, 0, "row"), (d_fk, FW, 0, "row"), (d_logf, LANES, 0, "row")],
        fox_p + [sel, lmask], [(FW, BF16), (FW, BF16), (LW, F32)], [(8, HEAD), (8, HEAD), (1, LANES)])

    def post_bwd(i, r, b):
        _, vjp = jax.vjp(lambda *a: _post_f(*a, b[3], b[4]), *r[:5], b[0], b[1], b[2])
        g = vjp(r[5])
        return list(g[:5]), list(g[5:])

    (d_y, d_r1, d_k1, d_v1, d_g), (g_gn_w, g_gn_b, g_r_k) = _rowcall(
        "rwkv_post_bwd", post_bwd, L, post_rows + [(d_ya, RW, 0, "row")], post_p,
        [(RW, F32)] * 5, [(1, RW)] * 3)
    d_r2, d_lw, d_k2, d_v2, d_a, d_b = _scan_bwd(s_r, s_lw, s_k, s_v, s_a, s_b, s_all, d_y)

    def prep_bwd(i, r, b):
        _, vjp = jax.vjp(lambda *a: _prep_f(*a, b[7], b[8]), *r[:4], *b[:7])
        cts = (r[4] + r[10], r[5], r[6] + r[11], r[7] + r[12], r[8], r[9], r[13])
        g = vjp(cts)
        return list(g[:4]), list(g[4:])

    bwd_rows = prep_rows + [(d_r2, RW, 0, "row"), (d_lw, RW, 0, "row"), (d_k2, RW, 0, "row"), (d_v2, RW, 0, "row"),
                            (d_a, RW, 0, "row"), (d_b, RW, 0, "row"), (d_r1, RW, 0, "row"), (d_k1, RW, 0, "row"),
                            (d_v1, RW, 0, "row"), (d_g, RW, 0, "row")]
    (d_xr, d_xk, d_xv, d_xl), (g_w0, g_w2p, g_a0, g_a2p, g_g2p, g_kk, g_ka) = _rowcall(
        "rwkv_prep_bwd", prep_bwd, L, bwd_rows, prep_p, [(RW, F32)] * 3 + [(LW, F32)],
        [(1, RW), (LW, RW), (1, RW), (LW, RW), (LW, RW), (1, RW), (1, RW)])

    def shift_bwd(i, r, b):
        last = pl.num_programs(0) - 1
        rows = lax.broadcasted_iota(jnp.int32, (ROW_TILE, 1), 0)
        outs, sums = [], []
        groups = ((r[0], r[1], r[2], r[3], b[0], None), (r[4], r[5], r[6], r[7], b[1], r[8]))
        for d, dnext, z, zhalo, m_, extra in groups:
            nxt = jnp.where(i == last, 0.0, dnext[0:1, :])
            d_up = jnp.where(rows == ROW_TILE - 1, nxt, pltpu.roll(d, ROW_TILE - 1, 0))
            dz = d * (1.0 - m_) + d_up * m_
            if extra is not None:
                dz = dz + extra
            first = jnp.where(i == 0, 0.0, zhalo[7:8, :])
            zp = jnp.where(rows == 0, first, pltpu.roll(z, 1, 0))
            outs.append(dz)
            sums.append(_rowsum(d * (zp - z)))
        return outs, sums

    d_xrkv = jnp.concatenate([d_xr, d_xk, d_xv], axis=1)
    (d_zrkv, d_zl), (g_mu_rkv, g_mu_l) = _rowcall(
        "shift_bwd", shift_bwd, L,
        [(d_xrkv, rkv_w, 0, "row"), (d_xrkv, rkv_w, 0, "next"), (proj, rkv_w, 0, "row"), (proj, rkv_w, 0, "prev"),
         (d_xl, LW, 0, "row"), (d_xl, LW, 0, "next"), (proj, LW, cb_lora, "row"), (proj, LW, cb_lora, "prev"),
         (d_zl_f, LW, 0, "row")],
        [mu_rkv, mu_l], [(rkv_w, BF16), (LW, BF16)], [(1, rkv_w), (1, LW)])

    n_in = stacked["w_in"].shape[2] * 4
    cs = n_in // 4
    cp = -(-cs // LANES) * LANES
    d_ref = jnp.concatenate([d_zrkv, d_zl[:, :LORA], d_zfq, d_zfk, d_fv.astype(BF16), d_zl[:, LORA:LORA + FH],
                             d_za, d_zb], axis=1)
    d_blk = jnp.concatenate([_pad_cols(d_ref[:, j * cs:(j + 1) * cs], cp) for j in range(4)], axis=1)
    w_blk = jnp.concatenate([_pad_cols(stacked["w_in"][j], cp) for j in range(4)], axis=1)
    g_w_in = _matmul(xn, d_blk, "tn", F32, "proj_dw", col_blocks=4)
    part_w_in = _pair_add("reduce_pair_add_w_in", g_w_in, _pair_exchange("reduce_pair_exchange_2", [g_w_in])[0])
    d_xn, recv_w_in = _matmul(d_blk, w_blk, "nt", F32, "proj_dx", plan=_chip_exchange_plan([part_w_in]))
    (dh0,), (g_n1,) = _rowcall("rms1_bwd", rms_bwd, L,
                               [(h0, D, 0, "row"), (d_xn, D, 0, "row"), (dh1, D, 0, "row")], [n1],
                               [(D, F32)], [(1, D)])
    grad_x = dh0[SEQ_ROW0:]
    g_meta = dh0[PAD_ROWS:SEQ_ROW0]

    tiny = {"rwkv_w2": g_w2p[:DL], "rwkv_a2": g_a2p[DL:DL + AL], "rwkv_g2": g_g2p[DL + AL:LORA],
            "meta_tokens": g_meta}
    g_mu = jnp.concatenate([g_mu_rkv, g_mu_l[:, :LORA]], axis=1)
    gsmall = {
        "norm1_g": g_n1, "rwkv_mu": g_mu, "rwkv_w0": g_w0, "rwkv_a0": g_a0, "rwkv_k_k": g_kk, "rwkv_k_a": g_ka,
        "rwkv_r_k": g_r_k.reshape(wts["rwkv_r_k"].shape), "rwkv_gn_w": g_gn_w, "rwkv_gn_b": g_gn_b,
        "fox_q_norm_g": g_qg8[0:1], "fox_k_norm_g": g_kg8[0:1], "fox_f_bias": g_fb[:, :FH], "norm2_g": g_n2,
    }
    small_flat = jnp.concatenate([gsmall[n].reshape(-1) for n in SMALL])

    tiny_names = tuple(tiny)

    def tiny_block(j):
        parts = []
        for n in tiny_names:
            w = tiny[n].shape[1] // 4
            parts.append(tiny[n][:, j * w:(j + 1) * w])
        return _pack(parts + [small_flat], F32, 32)

    small_blocks = jnp.stack([tiny_block(j) for j in range(4)])
    part_small = _pair_add("reduce_pair_add_small", small_blocks,
                           _pair_exchange("reduce_pair_exchange_3", [small_blocks])[0])
    recv_small = _chip_exchange([part_small])
    names = names1 + ("w_in", "small")
    tots = []
    for n, p, r in zip(names, parts1 + [part_w_in, part_small], list(recv1) + list(recv_w_in) + list(recv_small)):
        own = lax.dynamic_index_in_dim(p, chip, 0, keepdims=False)
        tots.append(_chip_add("reduce_chip_add_" + n, lax.dynamic_update_index_in_dim(r, own, chip, 0)))
    others = _pair_join(tots)
    red = [jnp.where(cj == 0, jnp.concatenate([t, o_], axis=0), jnp.concatenate([o_, t], axis=0))
           for t, o_ in zip(tots, others)]
    grads = {n: red[i] for i, n in enumerate(names[:-1])}
    grads["w_in"] = grads["w_in"][:, :cs]
    tiny_shapes = [wts[n].shape for n in tiny_names]
    got = _unpack(red[-1].reshape(-1), tiny_shapes + [small_flat.shape])
    for n, t in zip(tiny_names, got):
        grads[n] = t
    for n, t in zip(SMALL, _unpack(got[-1], [wts[n].shape for n in SMALL])):
        grads[n] = t

    delta, new_m, new_v = {}, {}, {}
    for n in SHARDED:
        delta[n], new_m[n], new_v[n] = _adamw("adamw_" + n, wts[n], grads[n], mom1[n], mom2[n])
    pk = lambda d: _pack([d[n] for n in SMALL], F32, 8)
    ds, ms, vs = _adamw("adamw_small", pk(wts), pk(grads), pk(mom1), pk(mom2))
    small_shapes = [wts[n].shape for n in SMALL]
    for dst, src in ((delta, ds), (new_m, ms), (new_v, vs)):
        for n, t in zip(SMALL, _unpack(src.reshape(-1), small_shapes)):
            dst[n] = t
    return loss, grad_x, grads, delta, new_m, new_v


def kernel(x, meta_tokens, norm1_g, w_in, rwkv_mu, rwkv_w0, rwkv_w2, rwkv_a0, rwkv_a2, rwkv_g2, rwkv_k_k, rwkv_k_a, rwkv_r_k, rwkv_gn_w, rwkv_gn_b, fox_q_norm_g, fox_k_norm_g, fox_f_bias, w_branch_a, w_branch_b, w_o, norm2_g, w_gate_up, w_down, loss_target, m_meta_tokens, m_norm1_g, m_w_in, m_rwkv_mu, m_rwkv_w0, m_rwkv_w2, m_rwkv_a0, m_rwkv_a2, m_rwkv_g2, m_rwkv_k_k, m_rwkv_k_a, m_rwkv_r_k, m_rwkv_gn_w, m_rwkv_gn_b, m_fox_q_norm_g, m_fox_k_norm_g, m_fox_f_bias, m_w_branch_a, m_w_branch_b, m_w_o, m_norm2_g, m_w_gate_up, m_w_down, v_meta_tokens, v_norm1_g, v_w_in, v_rwkv_mu, v_rwkv_w0, v_rwkv_w2, v_rwkv_a0, v_rwkv_a2, v_rwkv_g2, v_rwkv_k_k, v_rwkv_k_a, v_rwkv_r_k, v_rwkv_gn_w, v_rwkv_gn_b, v_fox_q_norm_g, v_fox_k_norm_g, v_fox_f_bias, v_w_branch_a, v_w_branch_b, v_w_o, v_norm2_g, v_w_gate_up, v_w_down):
    args = dict(locals())
    shapes = {n: args[n].shape for n in WEIGHTS}

    def drop_depth(t, n):
        if n == "meta_tokens":
            return t
        if n == "rwkv_r_k":
            return t.reshape(1, -1)
        return t.reshape(t.shape[1:]) if t.ndim == 3 else t

    wts = {n: drop_depth(args[n], n) for n in WEIGHTS}
    mom1 = {n: drop_depth(args["m_" + n], n) for n in WEIGHTS}
    mom2 = {n: drop_depth(args["v_" + n], n) for n in WEIGHTS}
    loss, grad_x, grads, delta, new_m, new_v = _step(x[0], loss_target[0], wts, mom1, mom2)
    outs = [loss, grad_x[None]]
    for d in (grads, delta, new_m, new_v):
        outs += [d[n].reshape(shapes[n]) for n in WEIGHTS]
    return tuple(outs)
```

```python
import functools

import jax
import jax.numpy as jnp
import numpy as np
from jax import lax
from jax.experimental import pallas as pl
from jax.experimental.pallas import tpu as pltpu

F32 = jnp.float32
BF16 = jnp.bfloat16
MESH = pl.DeviceIdType.MESH
ANY = pl.BlockSpec(memory_space=pl.ANY)

N_META = 16
HEAD = 64
ROW_TILE = 128
PAD_ROWS = ROW_TILE - N_META
SEQ_ROW0 = ROW_TILE
CHUNK = 64
LANES = 128
PACK_W = 1024
RMS_EPS = 1e-6
GN_EPS = 64e-5
ATTN_SCALE = HEAD ** -0.5
NEG = -1e30
VMEM_LIMIT_V7X = 56 * 1024 * 1024

ADAM_LR = 0.001
ADAM_B1 = 0.9
ADAM_B2 = 0.999
ADAM_EPS = 1e-08
ADAM_WD = 0.01
ADAM_STEP = 10


def _params(sem=None):
    return pltpu.CompilerParams(dimension_semantics=sem, vmem_limit_bytes=VMEM_LIMIT_V7X)


def _pick(n, cands):
    for c in cands:
        if n % c == 0:
            return c
    return n


def _bf(t):
    return t.astype(BF16)


def _dotb(a, b):
    return jnp.dot(_bf(a), _bf(b), preferred_element_type=F32)


def _split3(x):
    x1 = x.astype(BF16)
    r1 = x - x1.astype(F32)
    x2 = r1.astype(BF16)
    return x1, x2, (r1 - x2.astype(F32)).astype(BF16)


def _mme_raw(x, e, ce):
    eb = e.astype(BF16)
    dot = lambda q: lax.dot_general(q, eb, (((1,), (ce,)), ((), ())), preferred_element_type=F32)
    x1, x2, x3 = _split3(x)
    return dot(x1) + (dot(x2) + dot(x3))


@jax.custom_vjp
def _doth(x, e):
    return _mme_raw(x, e, 0)


def _doth_fwd(x, e):
    return _mme_raw(x, e, 0), e


def _doth_bwd(e, ct):
    return _mme_raw(ct, e, 1), jnp.zeros_like(e)


_doth.defvjp(_doth_fwd, _doth_bwd)


_BIG = (2048, 1536, 1408, 1024, 768, 704, 512, 384, 256, 128)


def _matmul(a, b, mode, out_dtype, name, add=None, col_blocks=1, plan=None):
    if mode == "nn":
        (M, R), (_, N) = a.shape, b.shape
        dims = (((1,), (0,)), ((), ()))
    elif mode == "nt":
        (M, R), (N, _) = a.shape, b.shape
        dims = (((1,), (1,)), ((), ()))
    else:
        (R, M), (_, N) = a.shape, b.shape
        dims = (((0,), (0,)), ((), ()))
    tm = _pick(M, (1408, 1024, 768, 512, 384, 256, 128))
    nb = N // col_blocks
    tn = _pick(nb, (1408, 1024, 896, 768, 704, 512, 384, 256, 128)) if mode == "tn" else _pick(nb, (512, 384, 256, 128))
    per = nb // tn
    tr = _pick(R, (1408, 1056, 768, 512, 384, 256, 128)) if mode == "tn" else _pick(R, _BIG)
    nr = R // tr

    if mode == "nn":
        a_spec = pl.BlockSpec((tm, tr), lambda i, j, r: (i, r))
        b_spec = pl.BlockSpec((tr, tn), lambda i, j, r: (r, j))
    elif mode == "nt":
        a_spec = pl.BlockSpec((tm, tr), lambda i, j, r: (i, r))
        b_spec = pl.BlockSpec((tn, tr), lambda i, j, r: (j, r))
    else:
        a_spec = pl.BlockSpec((tr, tm), lambda i, j, r: (r, i))
        b_spec = pl.BlockSpec((tr, tn), lambda i, j, r: (r, j))
    if col_blocks == 1:
        o_spec = pl.BlockSpec((tm, tn), lambda i, j, r: (i, j))
        o_shape = (M, N)
    else:
        o_spec = pl.BlockSpec((1, tm, tn), lambda i, j, r: (j // per, i, j % per))
        o_shape = (col_blocks, M, nb)
    has_add = add is not None

    grid = (M // tm, N // tn, nr)

    def body(*refs):
        own_in, p_in, (o_ref,), p_out, (acc,), sems = _carried(plan, refs, 3 if has_add else 2, 1, 1)
        a_ref, b_ref = own_in[:2]
        i, j, r = pl.program_id(0), pl.program_id(1), pl.program_id(2)
        if plan:
            @pl.when((i == 0) & (j == 0) & (r == 0))
            def _():
                plan["start"](p_in, p_out, *sems)

        @pl.when(r == 0)
        def _():
            acc[...] = jnp.zeros_like(acc)

        acc[...] += lax.dot_general(_bf(a_ref[...]), _bf(b_ref[...]), dims, preferred_element_type=F32)

        @pl.when(r == nr - 1)
        def _():
            res = acc[...]
            if has_add:
                res = res + own_in[2][...]
            o_ref[...] = res.astype(o_ref.dtype).reshape(o_ref.shape)

        if plan:
            @pl.when((i == grid[0] - 1) & (j == grid[1] - 1) & (r == nr - 1))
            def _():
                plan["wait"](p_in, p_out, *sems)

    ins = [a, b] + ([add] if has_add else [])
    specs = [a_spec, b_spec] + ([o_spec] if has_add else [])
    extra = plan["arrays"] if plan else []
    extra_out = plan["out_shapes"] if plan else []
    res = pl.pallas_call(
        body, name=name, out_shape=[jax.ShapeDtypeStruct(o_shape, out_dtype)] + extra_out,
        grid=grid, in_specs=specs + [ANY] * len(extra), out_specs=[o_spec] + [ANY] * len(extra_out),
        scratch_shapes=[pltpu.VMEM((tm, tn), F32)] + (_dma_sems(plan["nsem"]) if plan else []),
        compiler_params=_params(("arbitrary",) * 3 if plan else ("parallel", "parallel", "arbitrary")),
    )(*ins, *extra)
    return (res[0], list(res[1:])) if plan else res[0]


def _rowcall(name, fn, L, row_ins, bc_ins, row_outs, acc_outs, tm=ROW_TILE):
    nt = L // tm
    specs = []
    for arr, w, cb, kind in row_ins:
        if kind == "row":
            specs.append(pl.BlockSpec((tm, w), lambda i, cb=cb: (i, cb)))
        elif kind == "lag":
            specs.append(pl.BlockSpec((tm, w), lambda i, cb=cb: (jnp.maximum(i - 1, 0), cb)))
        elif kind == "prev":
            specs.append(pl.BlockSpec((8, w), lambda i, cb=cb: (jnp.maximum(i * (tm // 8) - 1, 0), cb)))
        else:
            specs.append(pl.BlockSpec((8, w), lambda i, cb=cb: (jnp.minimum((i + 1) * (tm // 8), L // 8 - 1), cb)))
    for arr in bc_ins:
        specs.append(pl.BlockSpec(arr.shape, lambda i, nd=arr.ndim: (0,) * nd))
    out_shapes = [jax.ShapeDtypeStruct((L, w), dt) for w, dt in row_outs]
    out_specs = [pl.BlockSpec((tm, w), lambda i: (i, 0)) for w, dt in row_outs]
    out_shapes += [jax.ShapeDtypeStruct(s, F32) for s in acc_outs]
    out_specs += [pl.BlockSpec(s, lambda i, nd=len(s): (0,) * nd) for s in acc_outs]
    n_row, n_bc, n_ro = len(row_ins), len(bc_ins), len(row_outs)

    def body(*refs):
        i = pl.program_id(0)
        vals = [r[...] for r in refs[: n_row + n_bc]]
        outs, sums = fn(i, vals[:n_row], vals[n_row:])
        o_refs = refs[n_row + n_bc:]
        for r, v in zip(o_refs[:n_ro], outs):
            r[...] = v.astype(r.dtype)

        @pl.when(i == 0)
        def _():
            for r in o_refs[n_ro:]:
                r[...] = jnp.zeros_like(r)

        for r, v in zip(o_refs[n_ro:], sums):
            r[...] += v

    res = pl.pallas_call(
        body, name=name, out_shape=out_shapes, grid=(nt,), in_specs=specs, out_specs=out_specs,
        compiler_params=_params(("arbitrary",)),
    )(*[a for a, _, _, _ in row_ins], *bc_ins)
    return list(res[:n_ro]), list(res[n_ro:])


def _rowsum(t):
    return jnp.sum(t, axis=0, keepdims=True)


def _head_mats(width):
    e = (np.arange(width)[:, None] // HEAD == np.arange(LANES)[None, :]).astype(np.float32)
    return jnp.asarray(e), jnp.asarray(e.T)


def _fold_mat(width):
    ft = (np.arange(HEAD)[:, None] == np.arange(width)[None, :] % HEAD).astype(np.float32)
    return jnp.asarray(ft)


def _rms_f(h, g):
    return (h * lax.rsqrt(jnp.mean(h * h, axis=-1, keepdims=True) + RMS_EPS)) * g


def _prep_f(xr, xk, xv, xl, w0, w2p, a0, a2p, g2p, k_k, k_a, e, et):
    w_log = -jax.nn.softplus(-(w0 + _dotb(jnp.tanh(xl), w2p))) - 0.5
    lw = -jnp.exp(w_log)
    a = jax.nn.sigmoid(a0 + _dotb(xl, a2p))
    g = _dotb(jax.nn.sigmoid(xl), g2p)
    kkr = xk * k_k
    inv = lax.rsqrt(jnp.maximum(_doth(kkr * kkr, e), 1e-24))
    kk = kkr * _doth(inv, et)
    kf = xk * (1.0 + (a - 1.0) * k_a)
    return xr, lw, kf, xv, -kk, kk * a, g


def _post_f(y, r, kf, v, g, gn_w, gn_b, r_k, e, et):
    mu = _doth(y, e) * (1.0 / HEAD)
    yc = y - _doth(mu, et)
    var = _doth(yc * yc, e) * (1.0 / HEAD)
    yn = yc * _doth(lax.rsqrt(var + GN_EPS), et) * gn_w + gn_b
    bonus = _doth(r * kf * r_k, e)
    return (yn + _doth(bonus, et) * v) * g


def _foxprep_f(q, k, fl, qg8, kg8, fb, e, et, ft, fmask):
    def norm(t, g8):
        ms = _doth(t * t, e) * (1.0 / HEAD)
        return t * _doth(lax.rsqrt(ms + RMS_EPS), et) * _doth(g8, ft)[0:1]
    logf = jax.nn.log_sigmoid(fl + fb) * fmask
    return norm(q, qg8), norm(k, kg8), logf


def _merge_f(za, zb, pa, pb):
    return jax.nn.sigmoid(za) * pa + jax.nn.sigmoid(zb) * pb


def _swiglu_f(gate, up):
    return jax.nn.silu(gate) * up


def _tri(n, strict):
    row = lax.broadcasted_iota(jnp.int32, (n, n), 0)
    col = lax.broadcasted_iota(jnp.int32, (n, n), 1)
    return (row > col) if strict else (row >= col)


def _split2(x):
    hi = x.astype(BF16)
    return hi, (x - hi.astype(F32)).astype(BF16)


def _mm3_raw(a, b, ca, cb):
    dn = (((ca,), (cb,)), ((), ()))
    ah, al = _split2(a)
    bh, bl = _split2(b)
    dot = lambda p, q: lax.dot_general(p, q, dn, preferred_element_type=F32)
    return dot(ah, bh) + (dot(al, bh) + dot(ah, bl))


@functools.partial(jax.custom_vjp, nondiff_argnums=(2, 3))
def _mm3(a, b, ca, cb):
    return _mm3_raw(a, b, ca, cb)


def _mm3_fwd(a, b, ca, cb):
    return _mm3_raw(a, b, ca, cb), (a, b)


def _mm3_bwd(ca, cb, res, ct):
    a, b = res
    da = _mm3_raw(ct, b, 1, 1 - cb) if ca == 1 else _mm3_raw(b, ct, 1 - cb, 1)
    db = _mm3_raw(a, ct, 1 - ca, 0) if cb == 0 else _mm3_raw(ct, a, 0, 1 - ca)
    return da, db


_mm3.defvjp(_mm3_fwd, _mm3_bwd)


def _mmx_raw(t, x, ct):
    x1 = x.astype(BF16)
    r1 = x - x1.astype(F32)
    x2 = r1.astype(BF16)
    x3 = (r1 - x2.astype(F32)).astype(BF16)
    tb = t.astype(BF16)
    dot = lambda q: lax.dot_general(tb, q, (((ct,), (0,)), ((), ())), preferred_element_type=F32)
    return dot(x1) + (dot(x2) + dot(x3))


@jax.custom_vjp
def _mmx(t, x):
    return _mmx_raw(t, x, 1)


def _mmx_fwd(t, x):
    return _mmx_raw(t, x, 1), t


def _mmx_bwd(t, ct):
    return jnp.zeros_like(t), _mmx_raw(t, ct, 0)


_mmx.defvjp(_mmx_fwd, _mmx_bwd)

SCAN_HEADS = 16


def _scan_step(r, lw, k, v, a, b, st):
    c = r.shape[0]
    nh = r.shape[1] // HEAD
    incl = _tri(c, False)
    row2 = lax.broadcasted_iota(jnp.int32, (2 * c, 2 * c), 0)
    col2 = lax.broadcasted_iota(jnp.int32, (2 * c, 2 * c), 1)
    t_row = jnp.where(row2 >= c, row2 - c, row2)
    t_col = jnp.where(col2 >= c, col2 - c, col2)
    mask2 = (t_row > t_col) | ((row2 >= c) & (t_row == t_col))
    right =lax.broadcasted_iota(jnp.int32, (c, 2 * c), 1) >= c
    eye = lax.broadcasted_iota(jnp.int32, (HEAD, HEAD), 0) == lax.broadcasted_iota(jnp.int32, (HEAD, HEAD), 1)
    cl = _mmx(incl.astype(F32), lw)
    last = cl[c - 1:c, :]
    rt = r * jnp.exp(cl)
    at = a * jnp.exp(cl - lw)
    pinv = jnp.exp(-cl)
    bt = b * pinv
    kt = k * pinv
    pend = jnp.exp(last - cl)
    bl = b * pend
    kl = k * pend
    pe_last = jnp.exp(last)
    hs = range(nh)
    sl = [slice(h * HEAD, (h + 1) * HEAD) for h in hs]
    ar = [jnp.concatenate([at[:, sl[h]], rt[:, sl[h]]], axis=0) for h in hs]
    bk = [jnp.concatenate([bt[:, sl[h]], kt[:, sl[h]]], axis=0) for h in hs]
    amat = [jnp.where(mask2, _mm3(ar[h], bk[h], 1, 1), 0.0) for h in hs]
    res = [_mm3(jnp.concatenate([ar[h], amat[h][:, c:]], axis=1),
                jnp.concatenate([st[h], v[:, sl[h]]], axis=0), 1, 0) for h in hs]
    z = [jnp.concatenate([amat[h][:c, :c], res[h][:c]], axis=1) for h in hs]
    for _ in range(max(1, int(np.ceil(np.log2(c))))):
        z = [_mm3(z[h][:, :c], z[h], 1, 0) + jnp.where(right, z[h], 0.0) for h in hs]
    u = [z[h][:, c:] for h in hs]
    ys = [res[h][c:] + _mm3(amat[h][c:, :c], u[h], 1, 0) for h in hs]
    s1s = [_mm3(jnp.concatenate([bl[:, sl[h]], kl[:, sl[h]], jnp.where(eye, pe_last[:, sl[h]], 0.0)], axis=0),
                jnp.concatenate([u[h], v[:, sl[h]], st[h]], axis=0), 0, 0) for h in hs]
    return tuple(ys), tuple(s1s)


def _scan_heads(W):
    return SCAN_HEADS if W % (SCAN_HEADS * HEAD) == 0 else 2


def _scan_fwd(r, lw, k, v, a, b):
    L, W = r.shape
    nh = _scan_heads(W)
    nc, ng = L // CHUNK, W // (nh * HEAD)
    spec = pl.BlockSpec((CHUNK, nh * HEAD), lambda p, c: (c, p))

    def body(r_ref, lw_ref, k_ref, v_ref, a_ref, b_ref, y_ref, s_ref, st):
        @pl.when(pl.program_id(1) == 0)
        def _():
            st[...] = jnp.zeros_like(st)

        s0 = st[...]
        ys, s1s = _scan_step(r_ref[...], lw_ref[...], k_ref[...], v_ref[...], a_ref[...], b_ref[...], s0)
        s_ref[0] = s0
        st[...] = jnp.stack(s1s)
        y_ref[...] = jnp.concatenate(ys, axis=1)

    return pl.pallas_call(
        body, name="wkv7_fwd",
        out_shape=[jax.ShapeDtypeStruct((L, W), F32), jax.ShapeDtypeStruct((nc, nh * ng, HEAD, HEAD), F32)],
        grid=(ng, nc), in_specs=[spec] * 6,
        out_specs=[spec, pl.BlockSpec((1, nh, HEAD, HEAD), lambda p, c: (c, p, 0, 0))],
        scratch_shapes=[pltpu.VMEM((nh, HEAD, HEAD), F32)],
        compiler_params=_params(("parallel", "arbitrary")),
    )(r, lw, k, v, a, b)


def _scan_bwd(r, lw, k, v, a, b, s_all, dy):
    L, W = r.shape
    nh = _scan_heads(W)
    nc, ng = L // CHUNK, W // (nh * HEAD)
    spec = pl.BlockSpec((CHUNK, nh * HEAD), lambda p, c: (nc - 1 - c, p))

    def body(r_ref, lw_ref, k_ref, v_ref, a_ref, b_ref, s_ref, dy_ref,
             dr_ref, dlw_ref, dk_ref, dv_ref, da_ref, db_ref, dst):
        @pl.when(pl.program_id(1) == 0)
        def _():
            dst[...] = jnp.zeros_like(dst)

        _, vjp = jax.vjp(_scan_step, r_ref[...], lw_ref[...], k_ref[...], v_ref[...], a_ref[...], b_ref[...],
                         s_ref[0])
        dys = tuple(dy_ref[:, h * HEAD:(h + 1) * HEAD] for h in range(nh))
        g = vjp((dys, tuple(dst[h] for h in range(nh))))
        for ref, val in zip((dr_ref, dlw_ref, dk_ref, dv_ref, da_ref, db_ref), g[:6]):
            ref[...] = val
        dst[...] = g[6]

    return pl.pallas_call(
        body, name="wkv7_bwd", out_shape=[jax.ShapeDtypeStruct((L, W), F32)] * 6,
        grid=(ng, nc),
        in_specs=[spec] * 6 + [pl.BlockSpec((1, nh, HEAD, HEAD), lambda p, c: (nc - 1 - c, p, 0, 0)), spec],
        out_specs=[spec] * 6,
        scratch_shapes=[pltpu.VMEM((nh, HEAD, HEAD), F32)],
        compiler_params=_params(("parallel", "arbitrary")),
    )(r, lw, k, v, a, b, s_all, dy)


def _cumsum_rows(logf):
    L = logf.shape[0]
    t = LANES

    def body(x_ref, o_ref, carry):
        @pl.when(pl.program_id(0) == 0)
        def _():
            carry[...] = jnp.zeros_like(carry)

        c = _mmx(_tri(t, False).astype(F32), x_ref[...]) + carry[...]
        carry[...] = c[t - 1:t, :]
        o_ref[...] = c.T

    return pl.pallas_call(
        body, name="fox_cumsum", out_shape=jax.ShapeDtypeStruct((LANES, L), F32), grid=(L // t,),
        in_specs=[pl.BlockSpec((t, LANES), lambda i: (i, 0))],
        out_specs=pl.BlockSpec((LANES, t), lambda i: (0, i)),
        scratch_shapes=[pltpu.VMEM((1, LANES), F32)], compiler_params=_params(("arbitrary",)),
    )(logf)


def _rcumsum_cols(dct):
    L = dct.shape[1]
    t = LANES
    n = L // t

    def body(x_ref, o_ref, carry):
        @pl.when(pl.program_id(0) == 0)
        def _():
            carry[...] = jnp.zeros_like(carry)

        rc = _doth(x_ref[...], _tri(t, False).astype(F32)) + carry[...]
        carry[...] = rc[:, 0:1]
        o_ref[...] = rc.T

    return pl.pallas_call(
        body, name="fox_rcumsum", out_shape=jax.ShapeDtypeStruct((L, LANES), F32), grid=(n,),
        in_specs=[pl.BlockSpec((LANES, t), lambda i: (0, n - 1 - i))],
        out_specs=pl.BlockSpec((t, LANES), lambda i: (n - 1 - i, 0)),
        scratch_shapes=[pltpu.VMEM((LANES, 1), F32)], compiler_params=_params(("arbitrary",)),
    )(dct)


def _attn_tile(L):
    return _pick(L, (384, 256, 128))


def _head_lanes(hh, shape):
    return lax.broadcasted_iota(jnp.int32, shape, len(shape) - 1) // HEAD == hh


def _own(hh, block, other=0):
    return jnp.where(_head_lanes(hh, block.shape), block, jnp.asarray(other, block.dtype))


def _attn_scores(q, k, ck, qi, kj, t):
    s = _dot_bnt(q, k) * ATTN_SCALE - ck
    qpos = qi * t + lax.broadcasted_iota(jnp.int32, (t, t), 0)
    kpos = kj * t + lax.broadcasted_iota(jnp.int32, (t, t), 1)
    mask = (kpos <= qpos) & (kpos >= PAD_ROWS)
    return jnp.where(mask, s, NEG), mask


def _dot_bnt(a, b):
    return lax.dot_general(_bf(a), _bf(b), (((1,), (1,)), ((), ())), preferred_element_type=F32)


def _dot_btn(a, b):
    return lax.dot_general(_bf(a), _bf(b), (((0,), (0,)), ((), ())), preferred_element_type=F32)


ATTN_HEADS = 2


def _attn_group(W):
    return ATTN_HEADS if W % (ATTN_HEADS * HEAD) == 0 else 2


def _ck_rows(ct_ref, p, g):
    r0 = (g * p) % 8
    return [ct_ref[pl.ds(r0 + hh, 1), :] for hh in range(g)]


def _carried(plan, refs, n_in, n_out, n_scratch):
    ci = len(plan["arrays"]) if plan else 0
    co = len(plan["out_shapes"]) if plan else 0
    a = n_in + ci
    b = a + n_out + co
    return (refs[:n_in], refs[n_in:a], refs[a:a + n_out], refs[a + n_out:b], refs[b:b + n_scratch],
            refs[b + n_scratch:])


def _attn_fwd(q, k, v, ct, plan=None):
    L, W = q.shape
    t = _attn_tile(L)
    g = _attn_group(W)
    gw = g * HEAD
    nt, npair = L // t, W // gw
    pairs = [(i, j) for i in range(nt) for j in range(i + 1)]
    it = jnp.asarray([i for i, _ in pairs], jnp.int32)
    jt = jnp.asarray([j for _, j in pairs], jnp.int32)
    ns = len(pairs)
    qspec = pl.BlockSpec((t, gw), lambda p, s, it, jt: (it[s], p))
    kspec = pl.BlockSpec((t, gw), lambda p, s, it, jt: (jt[s], p))
    cspec = pl.BlockSpec((8, t), lambda p, s, it, jt: (g * p // 8, jt[s]))

    def body(it_ref, jt_ref, *refs):
        (q_ref, k_ref, v_ref, ct_ref), p_in, (o_ref, lse_ref), p_out, (m_s, acc), sems = _carried(plan, refs, 4, 2, 2)
        p, s = pl.program_id(0), pl.program_id(1)
        i, j = it_ref[s], jt_ref[s]
        if plan:
            @pl.when((p == 0) & (s == 0))
            def _():
                plan["start"](p_in, p_out, *sems)

            if plan["mid"] is not None:
                @pl.when((p == npair // 2) & (s == 0))
                def _():
                    plan["mid"](p_in, p_out, *sems)

        @pl.when(j == 0)
        def _():
            m_s[...] = jnp.full_like(m_s, NEG)
            acc[...] = jnp.zeros_like(acc)

        def accumulate():
            cks = _ck_rows(ct_ref, p, g)
            qb, kb, vf = _bf(q_ref[...]), _bf(k_ref[...]), v_ref[...]
            ss = [_attn_scores(_own(hh, qb), kb, cks[hh], i, j, t)[0] for hh in range(g)]
            prs, alphas = [], []
            for hh in range(g):
                m_old = m_s[hh]
                m_new = jnp.maximum(m_old, jnp.max(ss[hh], axis=-1, keepdims=True))
                alphas.append(jnp.exp(m_old - m_new))
                prs.append(jnp.exp(ss[hh] - m_new))
                m_s[hh] = m_new
            pvs = []
            for hh in range(g):
                p_hi, p_lo = _split2(prs[hh])
                vx = _bf(_own(hh, vf, 1.0))
                pvs.append(jnp.dot(p_hi, vx, preferred_element_type=F32)
                           + jnp.dot(p_lo, vx, preferred_element_type=F32))
            for hh in range(g):
                acc[hh] = alphas[hh] * acc[hh] + pvs[hh]

        accumulate()

        @pl.when(j == i)
        def _():
            lane = lax.broadcasted_iota(jnp.int32, (t, LANES), 1)
            lse = jnp.zeros((t, LANES), F32)
            out = jnp.zeros((t, gw), F32)
            for hh in range(g):
                a = acc[hh]
                row_sum = pltpu.roll(a, HEAD, 1)
                out = jnp.where(_head_lanes(hh, a.shape), a / row_sum, out)
                nb = ((hh + 1) % g) * HEAD
                lse = jnp.where(lane == hh, m_s[hh] + jnp.log(a[:, nb:nb + 1]), lse)
            o_ref[...] = out
            lse_ref[0] = lse

        if plan:
            @pl.when((p == npair - 1) & (s == ns - 1))
            def _():
                plan["wait"](p_in, p_out, *sems)

    extra = plan["arrays"] if plan else []
    extra_out = plan["out_shapes"] if plan else []
    res = pl.pallas_call(
        body, name="fox_attn_fwd",
        out_shape=[jax.ShapeDtypeStruct((L, W), F32), jax.ShapeDtypeStruct((npair, L, LANES), F32)] + extra_out,
        grid_spec=pltpu.PrefetchScalarGridSpec(
            num_scalar_prefetch=2, grid=(npair, ns),
            in_specs=[qspec, kspec, kspec, cspec] + [ANY] * len(extra),
            out_specs=[qspec, pl.BlockSpec((1, t, LANES), lambda p, s, it, jt: (p, it[s], 0))]
            + [ANY] * len(extra_out),
            scratch_shapes=[pltpu.VMEM((g, t, 1), F32), pltpu.VMEM((g, t, gw), F32)]
            + (_dma_sems(plan["nsem"]) if plan else [])),
        compiler_params=_params(("arbitrary",) * 2 if plan else ("parallel", "arbitrary")),
    )(it, jt, q, k, v, ct, *extra)
    return res[0], res[1], list(res[2:])


def _attn_bwd(q, k, v, ct, o, lse, do, plan=None):
    L, W = q.shape
    t = _attn_tile(L)
    g = _attn_group(W)
    gw = g * HEAD
    nt, npair = L // t, W // gw
    pairs = [(i, j) for j in range(nt) for i in range(j, nt)]
    it = jnp.asarray([i for i, _ in pairs], jnp.int32)
    jt = jnp.asarray([j for _, j in pairs], jnp.int32)
    ns = len(pairs)
    kspec = pl.BlockSpec((t, gw), lambda p, s, it, jt: (jt[s], p))
    qspec = pl.BlockSpec((t, gw), lambda p, s, it, jt: (it[s], p))
    cspec = pl.BlockSpec((8, t), lambda p, s, it, jt: (g * p // 8, jt[s]))
    lspec = pl.BlockSpec((1, t, LANES), lambda p, s, it, jt: (p, it[s], 0))

    def body(it_ref, jt_ref, *refs):
        ((q_ref, k_ref, v_ref, ct_ref, o_ref, lse_ref, do_ref), p_in, (dk_ref, dv_ref, dc_ref, dq_ref), p_out,
         (dk_s, dv_s, dc_s), sems) = _carried(plan, refs, 7, 4, 3)
        p, s = pl.program_id(0), pl.program_id(1)
        i, j = it_ref[s], jt_ref[s]
        if plan:
            @pl.when((p == 0) & (s == 0))
            def _():
                plan["start"](p_in, p_out, *sems)

        @pl.when(i == j)
        def _():
            dk_s[...] = jnp.zeros_like(dk_s)
            dv_s[...] = jnp.zeros_like(dv_s)
            dc_s[...] = jnp.zeros_like(dc_s)

        def tile_dq():
            cks = _ck_rows(ct_ref, p, g)
            qb, kb, vb, dob = _bf(q_ref[...]), _bf(k_ref[...]), _bf(v_ref[...]), _bf(do_ref[...])
            of = o_ref[...]
            hs = range(g)
            qm = [_own(hh, qb) for hh in hs]
            dom = [_own(hh, dob) for hh in hs]
            sm = [_attn_scores(qm[hh], kb, cks[hh], i, j, t) for hh in hs]
            dps = [_dot_bnt(dom[hh], vb) for hh in hs]
            prs = [jnp.where(sm[hh][1], jnp.exp(sm[hh][0] - lse_ref[0, :, hh:hh + 1]), 0.0) for hh in hs]
            dss = [prs[hh] * (dps[hh] - jnp.sum(dom[hh].astype(F32) * of, axis=-1, keepdims=True)) for hh in hs]
            dv_s[...] += sum(_dot_btn(prs[hh], dom[hh]) for hh in hs)
            dk_s[...] += sum(_dot_btn(dss[hh], qm[hh]) for hh in hs)
            for hh in hs:
                dc_s[hh] += -jnp.sum(dss[hh], axis=0, keepdims=True)
            return sum(_dotb(dss[hh], _own(hh, kb)) for hh in hs) * ATTN_SCALE

        rows = pl.ds(pl.multiple_of(i * t, t), t)

        @pl.when(j == 0)
        def _():
            dq_ref[rows, :] = tile_dq()

        @pl.when(j > 0)
        def _():
            dq_ref[rows, :] += tile_dq()

        @pl.when(i == nt - 1)
        def _():
            row = lax.broadcasted_iota(jnp.int32, (8, t), 0)
            dc = jnp.zeros((8, t), F32)
            for hh in range(g):
                dc = jnp.where(row == hh, dc_s[hh], dc)
            dk_ref[...] = dk_s[...] * ATTN_SCALE
            dv_ref[...] = dv_s[...]
            dc_ref[0] = dc

        if plan:
            @pl.when((p == npair - 1) & (s == ns - 1))
            def _():
                plan["wait"](p_in, p_out, *sems)

    extra = plan["arrays"] if plan else []
    extra_out = plan["out_shapes"] if plan else []
    res = pl.pallas_call(
        body, name="fox_attn_bwd",
        out_shape=[jax.ShapeDtypeStruct((L, W), F32), jax.ShapeDtypeStruct((L, W), F32),
                   jax.ShapeDtypeStruct((npair, 8, L), F32), jax.ShapeDtypeStruct((L, W), F32)] + extra_out,
        grid_spec=pltpu.PrefetchScalarGridSpec(
            num_scalar_prefetch=2, grid=(npair, ns),
            in_specs=[qspec, kspec, kspec, cspec, qspec, lspec, qspec] + [ANY] * len(extra),
            out_specs=[kspec, kspec, pl.BlockSpec((1, 8, t), lambda p, s, it, jt: (p, 0, jt[s])),
                       pl.BlockSpec((L, gw), lambda p, s, it, jt: (0, p))] + [ANY] * len(extra_out),
            scratch_shapes=[pltpu.VMEM((t, gw), F32), pltpu.VMEM((t, gw), F32), pltpu.VMEM((g, 1, t), F32)]
            + (_dma_sems(plan["nsem"]) if plan else [])),
        compiler_params=_params(("arbitrary",) * 2 if plan else ("parallel", "arbitrary")),
    )(it, jt, q, k, v, ct, o, lse, do, *extra)
    return res[0], res[1], res[2], res[3], list(res[4:])


def _place():
    x, y, c = lax.axis_index("x"), lax.axis_index("y"), lax.axis_index("c")
    chips = [(1 - x, y), (x, 1 - y), (1 - x, 1 - y)]
    return x, y, c, chips


def _remote(src, dst, send_sems, recv_sems, k, to):
    return pltpu.make_async_remote_copy(src_ref=src, dst_ref=dst, send_sem=send_sems.at[k],
                                        recv_sem=recv_sems.at[k], device_id=to, device_id_type=MESH)


def _dma_sems(n):
    return [pltpu.SemaphoreType.DMA((n,)), pltpu.SemaphoreType.DMA((n,))]


def _all_gather(shards, modes):
    return _run_exchange("gather_weights", _gather_plan(shards, modes))


def _gather_plan(shards, modes):
    n = len(shards)

    def out_shape(s, mode):
        r, c = s.shape
        return {"stack": (4, r, c), "cols": (r, 4 * c), "rows": (4 * r, c)}[mode]

    def window(outs, i, chip, cc):
        r, cw = shards[i].shape
        h = r // 2
        if modes[i] == "stack":
            return outs[i].at[chip, pl.ds(cc * h, h), :]
        if modes[i] == "cols":
            return outs[i].at[pl.ds(cc * h, h), pl.ds(pl.multiple_of(chip * cw, LANES), cw)]
        return outs[i].at[pl.ds(pl.multiple_of(chip * r + cc * h, 8), h), :]

    def first(ins, outs, ss, rs):
        x, y, c, chips = _place()
        cps = []
        for i in range(n):
            h = shards[i].shape[0] // 2
            for k, (cx, cy) in enumerate(chips):
                cps.append(_remote(ins[i].at[pl.ds(c * h, h), :], window(outs, i, 2 * x + y, c), ss, rs,
                                   6 * i + k, (cx, cy, c)))
        return cps

    def passed(outs, ss, rs):
        x, y, c, chips = _place()
        cps = []
        for k, (cx, cy) in enumerate(chips):
            for i in range(n):
                landed = window(outs, i, 2 * cx + cy, c)
                cps.append((_remote(landed, landed, ss, rs, 6 * i + k, (x, y, 1 - c)),
                            _remote(landed, landed, ss, rs, 6 * i + 3 + k, (x, y, 1 - c))))
        return cps

    def start(ins, outs, ss, rs):
        for cp in first(ins, outs, ss, rs):
            cp.start()

    def mid(ins, outs, ss, rs):
        for arrival, fwd in passed(outs, ss, rs):
            arrival.wait_recv()
            fwd.start()

    def wait(ins, outs, ss, rs):
        x, y, c, chips = _place()
        for k, (cx, cy) in enumerate(chips):
            for i in range(n):
                other = window(outs, i, 2 * cx + cy, 1 - c)
                _remote(other, other, ss, rs, 6 * i + 3 + k, (x, y, 1 - c)).wait_recv()
        for cp in first(ins, outs, ss, rs) + [fwd for _, fwd in passed(outs, ss, rs)]:
            cp.wait_send()

    return dict(arrays=list(shards), nsem=6 * n, start=start, mid=mid, wait=wait,
                out_shapes=[jax.ShapeDtypeStruct(out_shape(s, m), s.dtype) for s, m in zip(shards, modes)])


def _run_exchange(name, plan):
    n = len(plan["arrays"])

    def body(*refs):
        ins, outs, (ss, rs) = refs[:n], refs[n:n + len(plan["out_shapes"])], refs[n + len(plan["out_shapes"]):]
        plan["start"](ins, outs, ss, rs)
        if plan["mid"] is not None:
            plan["mid"](ins, outs, ss, rs)
        plan["wait"](ins, outs, ss, rs)

    return pl.pallas_call(
        body, name=name, out_shape=plan["out_shapes"], in_specs=[ANY] * n,
        out_specs=[ANY] * len(plan["out_shapes"]), scratch_shapes=_dma_sems(plan["nsem"]),
    )(*plan["arrays"])


def _pair_exchange(name, blocks):
    return _run_exchange(name, _pair_exchange_plan(blocks))


def _pair_exchange_plan(blocks):
    n = len(blocks)

    def copies(ins, outs, ss, rs):
        x, y, c, _ = _place()
        cps = []
        for i in range(n):
            h = blocks[i].shape[1] // 2
            cps.append(_remote(ins[i].at[:, pl.ds((1 - c) * h, h), :], outs[i], ss, rs, i, (x, y, 1 - c)))
        return cps

    def start(ins, outs, ss, rs):
        for cp in copies(ins, outs, ss, rs):
            cp.start()

    def wait(ins, outs, ss, rs):
        for cp in copies(ins, outs, ss, rs):
            cp.wait()

    return dict(arrays=list(blocks), nsem=n, start=start, mid=None, wait=wait,
                out_shapes=[jax.ShapeDtypeStruct((4, b.shape[1] // 2, b.shape[2]), b.dtype) for b in blocks])


def _chip_exchange(parts):
    return _run_exchange("reduce_chip_exchange", _chip_exchange_plan(parts))


def _chip_exchange_plan(parts):
    n = len(parts)

    def sends(ins, outs, ss, rs):
        x, y, c, chips = _place()
        return [_remote(ins[i].at[2 * cx + cy], outs[i].at[2 * x + y], ss, rs, 3 * i + k, (cx, cy, c))
                for i in range(n) for k, (cx, cy) in enumerate(chips)]

    def start(ins, outs, ss, rs):
        for cp in sends(ins, outs, ss, rs):
            cp.start()

    def wait(ins, outs, ss, rs):
        x, y, c, chips = _place()
        for i in range(n):
            for k, (cx, cy) in enumerate(chips):
                slot = outs[i].at[2 * cx + cy]
                _remote(slot, slot, ss, rs, 3 * i + k, (cx, cy, c)).wait_recv()
        for cp in sends(ins, outs, ss, rs):
            cp.wait_send()

    return dict(arrays=list(parts), nsem=3 * n, start=start, mid=None, wait=wait,
                out_shapes=[jax.ShapeDtypeStruct(p.shape, p.dtype) for p in parts])


def _pair_join(tots):
    n = len(tots)

    def body(*refs):
        ins, outs, (send_sems, recv_sems) = refs[:n], refs[n:2 * n], refs[2 * n:]
        x, y, c, _ = _place()
        cps = [_remote(ins[i], outs[i], send_sems, recv_sems, i, (x, y, 1 - c)) for i in range(n)]
        for cp in cps:
            cp.start()
        for cp in cps:
            cp.wait()

    return pl.pallas_call(
        body, name="reduce_pair_join", out_shape=[jax.ShapeDtypeStruct(t.shape, t.dtype) for t in tots],
        in_specs=[ANY] * n, out_specs=[ANY] * n, scratch_shapes=_dma_sems(n),
    )(*tots)


def _add_tile(h, cw):
    cap = max(8, (512 * 1024) // max(cw, 1))
    return _pick(h, tuple(t for t in (1024, 512, 256, 128, 64, 32, 16, 8) if t <= cap))


def _pair_add(name, block, recv):
    n, r, cw = block.shape
    h = r // 2
    tr = _add_tile(h, cw)
    c = lax.axis_index("c").astype(jnp.int32).reshape((1,))

    def body(c_ref, a_ref, b_ref, o_ref):
        o_ref[...] = (a_ref[...] + b_ref[...]).astype(o_ref.dtype)

    return pl.pallas_call(
        body, name=name, out_shape=jax.ShapeDtypeStruct((n, h, cw), BF16),
        grid_spec=pltpu.PrefetchScalarGridSpec(
            num_scalar_prefetch=1, grid=(n, h // tr),
            in_specs=[pl.BlockSpec((1, tr, cw), lambda a, i, cr: (a, cr[0] * (h // tr) + i, 0)),
                      pl.BlockSpec((1, tr, cw), lambda a, i, cr: (a, i, 0))],
            out_specs=pl.BlockSpec((1, tr, cw), lambda a, i, cr: (a, i, 0))),
        compiler_params=_params(("parallel", "parallel")),
    )(c, block, recv)


def _chip_add(name, parts):
    n, h, cw = parts.shape
    tr = _add_tile(h, cw)

    def body(a_ref, o_ref):
        f = lambda i: a_ref[i].astype(F32)
        o_ref[...] = ((f(0) + f(1)) + f(2)) + f(3)

    return pl.pallas_call(
        body, name=name, out_shape=jax.ShapeDtypeStruct((h, cw), F32), grid=(h // tr,),
        in_specs=[pl.BlockSpec((n, tr, cw), lambda i: (0, i, 0))],
        out_specs=pl.BlockSpec((tr, cw), lambda i: (i, 0)),
        compiler_params=_params(("parallel",)),
    )(parts)


def _adamw(name, w, g, m, v):
    R, C = w.shape
    tr = _pick(R, (128, 64, 32, 16, 8))
    spec = pl.BlockSpec((tr, C), lambda i: (i, 0))

    def body(w_ref, g_ref, m_ref, v_ref, d_ref, mo_ref, vo_ref):
        gr = g_ref[...]
        mn = ADAM_B1 * m_ref[...] + (1.0 - ADAM_B1) * gr
        vn = ADAM_B2 * v_ref[...] + (1.0 - ADAM_B2) * jnp.square(gr)
        m_hat = mn / (1.0 - ADAM_B1 ** ADAM_STEP)
        v_hat = vn / (1.0 - ADAM_B2 ** ADAM_STEP)
        d_ref[...] = -ADAM_LR * (m_hat / (jnp.sqrt(v_hat) + ADAM_EPS) + ADAM_WD * w_ref[...])
        mo_ref[...] = mn
        vo_ref[...] = vn

    return pl.pallas_call(
        body, name=name, out_shape=[jax.ShapeDtypeStruct((R, C), F32)] * 3, grid=(R // tr,),
        in_specs=[spec] * 4, out_specs=[spec] * 3, compiler_params=_params(("parallel",)),
    )(w, g, m, v)


def _pack(parts, dtype, row_mult):
    flat = jnp.concatenate([p.reshape(-1).astype(dtype) for p in parts])
    unit = row_mult * PACK_W
    pad = (-flat.shape[0]) % unit
    if pad:
        flat = jnp.concatenate([flat, jnp.zeros((pad,), dtype)])
    return flat.reshape(-1, PACK_W)


def _unpack(flat, shapes):
    out, off = [], 0
    for s in shapes:
        n = int(np.prod(s))
        out.append(flat[off:off + n].reshape(s))
        off += n
    return out


SHARDED = ("w_in", "rwkv_w2", "rwkv_a2", "rwkv_g2", "w_branch_a", "w_branch_b", "w_o", "w_gate_up", "w_down",
           "meta_tokens")
SHARD_AXIS = {"w_in": 1, "rwkv_w2": 1, "rwkv_a2": 1, "rwkv_g2": 1, "w_branch_a": 1, "w_branch_b": 1, "w_o": 0,
              "w_gate_up": 1, "w_down": 0, "meta_tokens": 1}
GATHER_MODE = {"w_in": "stack", "rwkv_w2": "stack", "rwkv_a2": "stack", "rwkv_g2": "stack", "w_branch_a": "cols",
               "w_branch_b": "cols", "w_o": "rows", "w_gate_up": "cols", "w_down": "rows", "meta_tokens": "stack"}
LATE_GATHER = ("w_branch_a", "w_branch_b", "w_o", "w_gate_up", "w_down")
SMALL = ("norm1_g", "rwkv_mu", "rwkv_w0", "rwkv_a0", "rwkv_k_k", "rwkv_k_a", "rwkv_r_k", "rwkv_gn_w",
         "rwkv_gn_b", "fox_q_norm_g", "fox_k_norm_g", "fox_f_bias", "norm2_g")
WEIGHTS = ("meta_tokens", "norm1_g", "w_in", "rwkv_mu", "rwkv_w0", "rwkv_w2", "rwkv_a0", "rwkv_a2", "rwkv_g2",
           "rwkv_k_k", "rwkv_k_a", "rwkv_r_k", "rwkv_gn_w", "rwkv_gn_b", "fox_q_norm_g", "fox_k_norm_g",
           "fox_f_bias", "w_branch_a", "w_branch_b", "w_o", "norm2_g", "w_gate_up", "w_down")


def _pad_rows(t, rows):
    return jnp.concatenate([t, jnp.zeros((rows - t.shape[0],) + t.shape[1:], t.dtype)], axis=0)


def _pad_cols(t, cols):
    return jnp.concatenate([t, jnp.zeros(t.shape[:-1] + (cols - t.shape[-1],), t.dtype)], axis=-1)


def _step(x, tgt, wts, mom1, mom2):
    seq, D = x.shape
    L = SEQ_ROW0 + seq
    RW = wts["rwkv_w0"].shape[-1]
    DL, AL, GL = wts["rwkv_w2"].shape[0], wts["rwkv_a2"].shape[0], wts["rwkv_g2"].shape[0]
    FW = wts["w_branch_b"].shape[0]
    FH = wts["fox_f_bias"].shape[-1]
    DFF = wts["w_down"].shape[0] * 4
    LORA = DL + AL + GL
    LW = -(-(LORA + FH) // 512) * 512
    assert RW == FW and (6 * RW) % D == 0 and (6 * RW + 2 * D) % LW == 0 and LORA % 8 == 0
    xj = lax.axis_index("x")
    yj = lax.axis_index("y")
    chip = 2 * xj + yj

    send = {n: wts[n] if n == "meta_tokens" else wts[n].astype(BF16) for n in SHARDED}
    early = tuple(n for n in SHARDED if n not in LATE_GATHER)
    full, stacked = {}, {}

    def place_own(n, got):
        shard, mode = send[n], GATHER_MODE[n]
        r, cw = shard.shape
        if mode == "stack":
            got = lax.dynamic_update_index_in_dim(got, shard, chip, 0)
            stacked[n] = got
            full[n] = jnp.concatenate([got[j] for j in range(4)], axis=1)
        elif mode == "cols":
            full[n] = lax.dynamic_update_slice(got, shard, (0, chip * cw))
        else:
            full[n] = lax.dynamic_update_slice(got, shard, (chip * r, 0))

    for n, got in zip(early, _all_gather([send[n] for n in early], [GATHER_MODE[n] for n in early])):
        place_own(n, got)
    late_plan = _gather_plan([send[n] for n in LATE_GATHER], [GATHER_MODE[n] for n in LATE_GATHER])
    meta = full["meta_tokens"]
    w_in_blocks = stacked["w_in"]
    blk_w = w_in_blocks.shape[2]

    def ref_cols(lo, hi):
        out = []
        for j in range(4):
            a, b = max(lo, j * blk_w), min(hi, (j + 1) * blk_w)
            if a < b:
                out.append(w_in_blocks[j][:, a - j * blk_w:b - j * blk_w])
        return out

    o = 0
    segs = {}
    for nm, wd in (("r", RW), ("k", RW), ("v", RW), ("wd", DL), ("ad", AL), ("gd", GL),
                   ("fq", FW), ("fk", FW), ("fv", FW), ("ff", FH), ("ga", D), ("gb", D)):
        segs[nm] = ref_cols(o, o + wd)
        o += wd
    order = ("r", "k", "v", "fq", "fk", "fv", "ga", "gb", "wd", "ad", "gd", "ff")
    w1 = jnp.concatenate([c for nm in order for c in segs[nm]]
                         + [jnp.zeros((D, LW - LORA - FH), BF16)], axis=1)
    cb_f = 3
    cb_gate = (6 * RW) // D
    cb_lora = (6 * RW + 2 * D) // LW

    e_m, et_m = _head_mats(RW)
    ft_m = _fold_mat(RW)
    mu = wts["rwkv_mu"]
    mu_rkv = mu[:, :3 * RW]
    mu_l = _pad_cols(mu[:, 3 * RW:], LW)
    w2p = _pad_rows(full["rwkv_w2"].astype(F32), LW)
    a2p = _pad_rows(jnp.concatenate([jnp.zeros((DL, RW), F32), full["rwkv_a2"].astype(F32)], axis=0), LW)
    g2p = _pad_rows(jnp.concatenate([jnp.zeros((DL + AL, RW), F32), full["rwkv_g2"].astype(F32)], axis=0), LW)
    r_k = wts["rwkv_r_k"].reshape(1, RW)
    qg8 = jnp.broadcast_to(wts["fox_q_norm_g"], (8, HEAD))
    kg8 = jnp.broadcast_to(wts["fox_k_norm_g"], (8, HEAD))
    fb = _pad_cols(wts["fox_f_bias"], LANES)
    fmask = (jnp.arange(LANES) < FH).astype(F32).reshape(1, LANES)
    lmask = ((jnp.arange(LW) >= LORA) & (jnp.arange(LW) < LORA + FH)).astype(F32).reshape(1, LW)

    h0 = jnp.concatenate([jnp.zeros((PAD_ROWS, D), F32), meta, x], axis=0)
    n1 = wts["norm1_g"]

    (xn,), _ = _rowcall("rms1_fwd", lambda i, r, b: ([_rms_f(r[0], b[0])], []), L,
                        [(h0, D, 0, "row")], [n1], [(D, BF16)], [])
    proj = _matmul(xn, w1, "nn", F32, "proj_fwd")

    def shift_fn(i, r, b):
        rows = lax.broadcasted_iota(jnp.int32, (ROW_TILE, 1), 0)
        outs = []
        for z, halo, m_ in ((r[0], r[1], b[0]), (r[2], r[3], b[1])):
            first = jnp.where(i == 0, 0.0, halo[7:8, :])
            zp = jnp.where(rows == 0, first, pltpu.roll(z, 1, 0))
            outs.append(z + (zp - z) * m_)
        return outs, []

    rkv_w = 3 * RW
    (x_rkv, x_l), _ = _rowcall(
        "shift_fwd", shift_fn, L,
        [(proj, rkv_w, 0, "row"), (proj, rkv_w, 0, "prev"), (proj, LW, cb_lora, "row"), (proj, LW, cb_lora, "prev")],
        [mu_rkv, mu_l], [(rkv_w, F32), (LW, F32)], [])

    prep_p = [wts["rwkv_w0"], w2p, wts["rwkv_a0"], a2p, g2p, wts["rwkv_k_k"], wts["rwkv_k_a"], e_m, et_m]
    prep_rows = [(x_rkv, RW, 0, "row"), (x_rkv, RW, 1, "row"), (x_rkv, RW, 2, "row"), (x_l, LW, 0, "row")]
    (s_r, s_lw, s_k, s_v, s_a, s_b, gate_g), _ = _rowcall(
        "rwkv_prep_fwd", lambda i, r, b: (list(_prep_f(*r, *b)), []), L, prep_rows, prep_p,
        [(RW, F32)] * 7, [])
    y_scan, s_all = _scan_fwd(s_r, s_lw, s_k, s_v, s_a, s_b)
    post_p = [wts["rwkv_gn_w"], wts["rwkv_gn_b"], r_k, e_m, et_m]
    post_rows = [(y_scan, RW, 0, "row"), (s_r, RW, 0, "row"), (s_k, RW, 0, "row"), (s_v, RW, 0, "row"),
                 (gate_g, RW, 0, "row")]
    (y_a,), _ = _rowcall("rwkv_post_fwd", lambda i, r, b: ([_post_f(*r, *b)], []), L, post_rows, post_p,
                         [(RW, BF16)], [])

    fox_p = [qg8, kg8, fb, e_m, et_m, ft_m, fmask]

    def foxprep_fn(i, r, b):
        fl = _doth(r[2] * b[-1], b[-2])
        return list(_foxprep_f(r[0], r[1], fl, *b[:-2])), []

    sel = (np.arange(LW)[:, None] - LORA == np.arange(LANES)[None, :]).astype(np.float32)
    sel = jnp.asarray(sel)
    fox_rows = [(proj, FW, cb_f, "row"), (proj, FW, cb_f + 1, "row"), (proj, LW, cb_lora, "row")]
    (f_q, f_k, logf), _ = _rowcall("fox_prep_fwd", foxprep_fn, L, fox_rows, fox_p + [sel, lmask],
                                   [(FW, BF16), (FW, BF16), (LANES, F32)], [])
    ct = _cumsum_rows(logf)
    f_v = proj[:, (cb_f + 2) * FW:(cb_f + 3) * FW]
    y_b32, lse, got_late = _attn_fwd(f_q, f_k, f_v, ct, plan=late_plan)
    for n, got in zip(LATE_GATHER, got_late):
        place_own(n, got)
    y_b = y_b32.astype(BF16)

    p_a = _matmul(y_a, full["w_branch_a"], "nn", F32, "branch_a_fwd")
    p_b = _matmul(y_b, full["w_branch_b"], "nn", F32, "branch_b_fwd")
    merge_rows = [(proj, D, cb_gate, "row"), (proj, D, cb_gate + 1, "row"), (p_a, D, 0, "row"), (p_b, D, 0, "row")]
    (merged,), _ = _rowcall("merge_fwd", lambda i, r, b: ([_merge_f(*r)], []), L, merge_rows, [], [(D, BF16)], [])
    h1 = _matmul(merged, full["w_o"], "nn", F32, "wo_fwd", add=h0)
    n2 = wts["norm2_g"]
    (xn2,), _ = _rowcall("rms2_fwd", lambda i, r, b: ([_rms_f(r[0], b[0])], []), L,
                         [(h1, D, 0, "row")], [n2], [(D, BF16)], [])
    gu = _matmul(xn2, full["w_gate_up"], "nn", F32, "gate_up_fwd")
    gu_rows = [(gu, DFF, 0, "row"), (gu, DFF, 1, "row")]
    (act,), _ = _rowcall("swiglu_fwd", lambda i, r, b: ([_swiglu_f(*r)], []), L, gu_rows, [], [(DFF, BF16)], [])
    h2 = _matmul(act, full["w_down"], "nn", F32, "down_fwd", add=h1)

    def loss_fn(i, r, b):
        err = jnp.where(i == 0, 0.0, r[0] - r[1])
        return [err * (1.0 / D)], [jnp.zeros((8, LANES), F32) + 0.5 / D * jnp.sum(err * err)]

    (dh2,), (loss_acc,) = _rowcall("loss", loss_fn, L, [(h2, D, 0, "row"), (tgt, D, 0, "lag")], [],
                                   [(D, F32)], [(8, LANES)])
    loss = lax.psum(loss_acc[0, 0], ("x", "y", "c"))

    dh2b = dh2.astype(BF16)
    g_w_down = _matmul(act, dh2b, "tn", F32, "down_dw")
    d_act = _matmul(dh2b, full["w_down"], "nt", F32, "down_dx")

    def swiglu_bwd(i, r, b):
        _, vjp = jax.vjp(_swiglu_f, r[0], r[1])
        return list(vjp(r[2])), []

    (d_gate, d_up), _ = _rowcall("swiglu_bwd", swiglu_bwd, L, gu_rows + [(d_act, DFF, 0, "row")], [],
                                 [(DFF, BF16), (DFF, BF16)], [])
    d_gu = jnp.concatenate([d_gate, d_up], axis=1)
    g_w_gu = _matmul(xn2, d_gu, "tn", F32, "gate_up_dw", col_blocks=4)
    ffn_blocks = [g_w_gu, g_w_down.reshape(4, -1, D)]
    d_xn2, ffn_recv = _matmul(d_gu, full["w_gate_up"], "nt", F32, "gate_up_dx", plan=_pair_exchange_plan(ffn_blocks))

    def rms_bwd(i, r, b):
        _, vjp = jax.vjp(_rms_f, r[0], b[0])
        dh, dg = vjp(r[1])
        return [dh + r[2]], [dg]

    (dh1,), (g_n2,) = _rowcall("rms2_bwd", rms_bwd, L,
                               [(h1, D, 0, "row"), (d_xn2, D, 0, "row"), (dh2, D, 0, "row")], [n2],
                               [(D, F32)], [(1, D)])
    dh1b = dh1.astype(BF16)
    g_w_o = _matmul(merged, dh1b, "tn", F32, "wo_dw")
    d_merged = _matmul(dh1b, full["w_o"], "nt", F32, "wo_dx")

    def merge_bwd(i, r, b):
        _, vjp = jax.vjp(_merge_f, *r[:4])
        return list(vjp(r[4])), []

    (d_za, d_zb, d_pa, d_pb), _ = _rowcall("merge_bwd", merge_bwd, L, merge_rows + [(d_merged, D, 0, "row")], [],
                                           [(D, BF16)] * 4, [])
    g_w_a = _matmul(y_a, d_pa, "tn", F32, "branch_a_dw", col_blocks=4)
    g_w_b = _matmul(y_b, d_pb, "tn", F32, "branch_b_dw", col_blocks=4)
    d_ya = _matmul(d_pa, full["w_branch_a"], "nt", F32, "branch_a_dx")
    d_yb = _matmul(d_pb, full["w_branch_b"], "nt", F32, "branch_b_dx")

    cj = lax.axis_index("c")
    names1 = ("w_gate_up", "w_down", "w_branch_a", "w_branch_b", "w_o")
    blocks1 = [g_w_a, g_w_b, g_w_o.reshape(4, -1, D)]
    parts1 = [_pair_add("reduce_pair_add_" + n, b, r)
              for n, b, r in zip(names1, ffn_blocks + blocks1,
                                 list(ffn_recv) + list(_pair_exchange("reduce_pair_exchange_1", blocks1)))]
    d_fk, d_fv, dc_rows, d_fq, recv1 = _attn_bwd(f_q, f_k, f_v, ct, y_b32, lse, d_yb,
                                                 plan=_chip_exchange_plan(parts1))
    dct = _pad_rows(dc_rows[:, :_attn_group(FW), :].reshape(-1, L), LANES)
    d_logf = _rcumsum_cols(dct)

    def foxprep_bwd(i, r, b):
        def f(q, k, xl, qg, kg, fbias):
            return _foxprep_f(q, k, _doth(xl * b[-1], b[-2]), qg, kg, fbias, *b[3:7])
        _, vjp = jax.vjp(f, r[0], r[1], r[2], b[0], b[1], b[2])
        dq, dk, dxl, dqg, dkg, dfb = vjp((r[3], r[4], r[5]))
        return [dq, dk, dxl], [dqg, dkg, dfb]

    (d_zfq, d_zfk, d_zl_f), (g_qg8, g_kg8, g_fb) = _rowcall(
        "fox_prep_bwd", foxprep_bwd, L,
        fox_rows + [(d_fq, FW, 0, "row"), (d_fk, FW, 0, "row"), (d_logf, LANES, 0, "row")],
        fox_p + [sel, lmask], [(FW, BF16), (FW, BF16), (LW, F32)], [(8, HEAD), (8, HEAD), (1, LANES)])

    def post_bwd(i, r, b):
        _, vjp = jax.vjp(lambda *a: _post_f(*a, b[3], b[4]), *r[:5], b[0], b[1], b[2])
        g = vjp(r[5])
        return list(g[:5]), list(g[5:])

    (d_y, d_r1, d_k1, d_v1, d_g), (g_gn_w, g_gn_b, g_r_k) = _rowcall(
        "rwkv_post_bwd", post_bwd, L, post_rows + [(d_ya, RW, 0, "row")], post_p,
        [(RW, F32)] * 5, [(1, RW)] * 3)
    d_r2, d_lw, d_k2, d_v2, d_a, d_b = _scan_bwd(s_r, s_lw, s_k, s_v, s_a, s_b, s_all, d_y)

    def prep_bwd(i, r, b):
        _, vjp = jax.vjp(lambda *a: _prep_f(*a, b[7], b[8]), *r[:4], *b[:7])
        cts = (r[4] + r[10], r[5], r[6] + r[11], r[7] + r[12], r[8], r[9], r[13])
        g = vjp(cts)
        return list(g[:4]), list(g[4:])

    bwd_rows = prep_rows + [(d_r2, RW, 0, "row"), (d_lw, RW, 0, "row"), (d_k2, RW, 0, "row"), (d_v2, RW, 0, "row"),
                            (d_a, RW, 0, "row"), (d_b, RW, 0, "row"), (d_r1, RW, 0, "row"), (d_k1, RW, 0, "row"),
                            (d_v1, RW, 0, "row"), (d_g, RW, 0, "row")]
    (d_xr, d_xk, d_xv, d_xl), (g_w0, g_w2p, g_a0, g_a2p, g_g2p, g_kk, g_ka) = _rowcall(
        "rwkv_prep_bwd", prep_bwd, L, bwd_rows, prep_p, [(RW, F32)] * 3 + [(LW, F32)],
        [(1, RW), (LW, RW), (1, RW), (LW, RW), (LW, RW), (1, RW), (1, RW)])

    def shift_bwd(i, r, b):
        last = pl.num_programs(0) - 1
        rows = lax.broadcasted_iota(jnp.int32, (ROW_TILE, 1), 0)
        outs, sums = [], []
        groups = ((r[0], r[1], r[2], r[3], b[0], None), (r[4], r[5], r[6], r[7], b[1], r[8]))
        for d, dnext, z, zhalo, m_, extra in groups:
            nxt = jnp.where(i == last, 0.0, dnext[0:1, :])
            d_up = jnp.where(rows == ROW_TILE - 1, nxt, pltpu.roll(d, ROW_TILE - 1, 0))
            dz = d * (1.0 - m_) + d_up * m_
            if extra is not None:
                dz = dz + extra
            first = jnp.where(i == 0, 0.0, zhalo[7:8, :])
            zp = jnp.where(rows == 0, first, pltpu.roll(z, 1, 0))
            outs.append(dz)
            sums.append(_rowsum(d * (zp - z)))
        return outs, sums

    d_xrkv = jnp.concatenate([d_xr, d_xk, d_xv], axis=1)
    (d_zrkv, d_zl), (g_mu_rkv, g_mu_l) = _rowcall(
        "shift_bwd", shift_bwd, L,
        [(d_xrkv, rkv_w, 0, "row"), (d_xrkv, rkv_w, 0, "next"), (proj, rkv_w, 0, "row"), (proj, rkv_w, 0, "prev"),
         (d_xl, LW, 0, "row"), (d_xl, LW, 0, "next"), (proj, LW, cb_lora, "row"), (proj, LW, cb_lora, "prev"),
         (d_zl_f, LW, 0, "row")],
        [mu_rkv, mu_l], [(rkv_w, BF16), (LW, BF16)], [(1, rkv_w), (1, LW)])

    n_in = stacked["w_in"].shape[2] * 4
    cs = n_in // 4
    cp = -(-cs // LANES) * LANES
    pieces = [d_zrkv, d_zl[:, :LORA], d_zfq, d_zfk, d_fv.astype(BF16), d_zl[:, LORA:LORA + FH], d_za, d_zb]
    cuts, off = [], 0
    for j in range(4):
        lo, hi = j * cs, (j + 1) * cs
        off = 0
        for pc in pieces:
            a, b = max(lo, off), min(hi, off + pc.shape[1])
            if a < b:
                cuts.append(pc[:, a - off:b - off])
            off += pc.shape[1]
        if cp > cs:
            cuts.append(jnp.zeros((L, cp - cs), BF16))
    d_blk = jnp.concatenate(cuts, axis=1)
    w_blk = jnp.concatenate([_pad_cols(stacked["w_in"][j], cp) for j in range(4)], axis=1)
    g_w_in = _matmul(xn, d_blk, "tn", F32, "proj_dw", col_blocks=4)
    part_w_in = _pair_add("reduce_pair_add_w_in", g_w_in, _pair_exchange("reduce_pair_exchange_2", [g_w_in])[0])
    d_xn, recv_w_in = _matmul(d_blk, w_blk, "nt", F32, "proj_dx", plan=_chip_exchange_plan([part_w_in]))
    (dh0,), (g_n1,) = _rowcall("rms1_bwd", rms_bwd, L,
                               [(h0, D, 0, "row"), (d_xn, D, 0, "row"), (dh1, D, 0, "row")], [n1],
                               [(D, F32)], [(1, D)])
    grad_x = dh0[SEQ_ROW0:]
    g_meta = dh0[PAD_ROWS:SEQ_ROW0]

    tiny = {"rwkv_w2": g_w2p[:DL], "rwkv_a2": g_a2p[DL:DL + AL], "rwkv_g2": g_g2p[DL + AL:LORA],
            "meta_tokens": g_meta}
    g_mu = jnp.concatenate([g_mu_rkv, g_mu_l[:, :LORA]], axis=1)
    gsmall = {
        "norm1_g": g_n1, "rwkv_mu": g_mu, "rwkv_w0": g_w0, "rwkv_a0": g_a0, "rwkv_k_k": g_kk, "rwkv_k_a": g_ka,
        "rwkv_r_k": g_r_k.reshape(wts["rwkv_r_k"].shape), "rwkv_gn_w": g_gn_w, "rwkv_gn_b": g_gn_b,
        "fox_q_norm_g": g_qg8[0:1], "fox_k_norm_g": g_kg8[0:1], "fox_f_bias": g_fb[:, :FH], "norm2_g": g_n2,
    }
    small_flat = jnp.concatenate([gsmall[n].reshape(-1) for n in SMALL])

    tiny_names = tuple(tiny)

    def tiny_block(j):
        parts = []
        for n in tiny_names:
            w = tiny[n].shape[1] // 4
            parts.append(tiny[n][:, j * w:(j + 1) * w])
        return _pack(parts + [small_flat], F32, 32)

    small_blocks = jnp.stack([tiny_block(j) for j in range(4)])
    part_small = _pair_add("reduce_pair_add_small", small_blocks,
                           _pair_exchange("reduce_pair_exchange_3", [small_blocks])[0])
    recv_small = _chip_exchange([part_small])
    names = names1 + ("w_in", "small")
    tots = []
    for n, p, r in zip(names, parts1 + [part_w_in, part_small], list(recv1) + list(recv_w_in) + list(recv_small)):
        own = lax.dynamic_index_in_dim(p, chip, 0, keepdims=False)
        tots.append(_chip_add("reduce_chip_add_" + n, lax.dynamic_update_index_in_dim(r, own, chip, 0)))
    others = _pair_join(tots)
    red = [jnp.where(cj == 0, jnp.concatenate([t, o_], axis=0), jnp.concatenate([o_, t], axis=0))
           for t, o_ in zip(tots, others)]
    grads = {n: red[i] for i, n in enumerate(names[:-1])}
    grads["w_in"] = grads["w_in"][:, :cs]
    tiny_shapes = [wts[n].shape for n in tiny_names]
    got = _unpack(red[-1].reshape(-1), tiny_shapes + [small_flat.shape])
    for n, t in zip(tiny_names, got):
        grads[n] = t
    for n, t in zip(SMALL, _unpack(got[-1], [wts[n].shape for n in SMALL])):
        grads[n] = t

    delta, new_m, new_v = {}, {}, {}
    for n in SHARDED:
        delta[n], new_m[n], new_v[n] = _adamw("adamw_" + n, wts[n], grads[n], mom1[n], mom2[n])
    pk = lambda d: _pack([d[n] for n in SMALL], F32, 8)
    ds, ms, vs = _adamw("adamw_small", pk(wts), pk(grads), pk(mom1), pk(mom2))
    small_shapes = [wts[n].shape for n in SMALL]
    for dst, src in ((delta, ds), (new_m, ms), (new_v, vs)):
        for n, t in zip(SMALL, _unpack(src.reshape(-1), small_shapes)):
            dst[n] = t
    return loss, grad_x, grads, delta, new_m, new_v


def kernel(x, meta_tokens, norm1_g, w_in, rwkv_mu, rwkv_w0, rwkv_w2, rwkv_a0, rwkv_a2, rwkv_g2, rwkv_k_k, rwkv_k_a, rwkv_r_k, rwkv_gn_w, rwkv_gn_b, fox_q_norm_g, fox_k_norm_g, fox_f_bias, w_branch_a, w_branch_b, w_o, norm2_g, w_gate_up, w_down, loss_target, m_meta_tokens, m_norm1_g, m_w_in, m_rwkv_mu, m_rwkv_w0, m_rwkv_w2, m_rwkv_a0, m_rwkv_a2, m_rwkv_g2, m_rwkv_k_k, m_rwkv_k_a, m_rwkv_r_k, m_rwkv_gn_w, m_rwkv_gn_b, m_fox_q_norm_g, m_fox_k_norm_g, m_fox_f_bias, m_w_branch_a, m_w_branch_b, m_w_o, m_norm2_g, m_w_gate_up, m_w_down, v_meta_tokens, v_norm1_g, v_w_in, v_rwkv_mu, v_rwkv_w0, v_rwkv_w2, v_rwkv_a0, v_rwkv_a2, v_rwkv_g2, v_rwkv_k_k, v_rwkv_k_a, v_rwkv_r_k, v_rwkv_gn_w, v_rwkv_gn_b, v_fox_q_norm_g, v_fox_k_norm_g, v_fox_f_bias, v_w_branch_a, v_w_branch_b, v_w_o, v_norm2_g, v_w_gate_up, v_w_down):
    args = dict(locals())
    shapes = {n: args[n].shape for n in WEIGHTS}

    def drop_depth(t, n):
        if n == "meta_tokens":
            return t
        if n == "rwkv_r_k":
            return t.reshape(1, -1)
        return t.reshape(t.shape[1:]) if t.ndim == 3 else t

    wts = {n: drop_depth(args[n], n) for n in WEIGHTS}
    mom1 = {n: drop_depth(args["m_" + n], n) for n in WEIGHTS}
    mom2 = {n: drop_depth(args["v_" + n], n) for n in WEIGHTS}
    loss, grad_x, grads, delta, new_m, new_v = _step(x[0], loss_target[0], wts, mom1, mom2)
    outs = [loss, grad_x[None]]
    for d in (grads, delta, new_m, new_v):
        outs += [d[n].reshape(shapes[n]) for n in WEIGHTS]
    return tuple(outs)
```

```python
import functools

import jax
import jax.numpy as jnp
import numpy as np
from jax import lax
from jax.experimental import pallas as pl
from jax.experimental.pallas import tpu as pltpu

F32 = jnp.float32
BF16 = jnp.bfloat16
MESH = pl.DeviceIdType.MESH
ANY = pl.BlockSpec(memory_space=pl.ANY)

N_META = 16
HEAD = 64
ROW_TILE = 128
PAD_ROWS = ROW_TILE - N_META
SEQ_ROW0 = ROW_TILE
CHUNK = 64
LANES = 128
PACK_W = 1024
RMS_EPS = 1e-6
GN_EPS = 64e-5
ATTN_SCALE = HEAD ** -0.5
NEG = -1e30
VMEM_LIMIT_V7X = 56 * 1024 * 1024

ADAM_LR = 0.001
ADAM_B1 = 0.9
ADAM_B2 = 0.999
ADAM_EPS = 1e-08
ADAM_WD = 0.01
ADAM_STEP = 10


def _params(sem=None):
    return pltpu.CompilerParams(dimension_semantics=sem, vmem_limit_bytes=VMEM_LIMIT_V7X)


def _pick(n, cands):
    for c in cands:
        if n % c == 0:
            return c
    return n


def _bf(t):
    return t.astype(BF16)


def _dotb(a, b):
    return jnp.dot(_bf(a), _bf(b), preferred_element_type=F32)


def _split3(x):
    x1 = x.astype(BF16)
    r1 = x - x1.astype(F32)
    x2 = r1.astype(BF16)
    return x1, x2, (r1 - x2.astype(F32)).astype(BF16)


def _mme_raw(x, e, ce):
    eb = e.astype(BF16)
    dot = lambda q: lax.dot_general(q, eb, (((1,), (ce,)), ((), ())), preferred_element_type=F32)
    x1, x2, x3 = _split3(x)
    return dot(x1) + (dot(x2) + dot(x3))


@jax.custom_vjp
def _doth(x, e):
    return _mme_raw(x, e, 0)


def _doth_fwd(x, e):
    return _mme_raw(x, e, 0), e


def _doth_bwd(e, ct):
    return _mme_raw(ct, e, 1), jnp.zeros_like(e)


_doth.defvjp(_doth_fwd, _doth_bwd)


_BIG = (2816, 2048, 1536, 1408, 1024, 768, 704, 512, 384, 256, 128)


def _matmul(a, b, mode, out_dtype, name, add=None, col_blocks=1, plan=None):
    if mode == "nn":
        (M, R), (_, N) = a.shape, b.shape
        dims = (((1,), (0,)), ((), ()))
    elif mode == "nt":
        (M, R), (N, _) = a.shape, b.shape
        dims = (((1,), (1,)), ((), ()))
    else:
        (R, M), (_, N) = a.shape, b.shape
        dims = (((0,), (0,)), ((), ()))
    tm = _pick(M, (1408, 1024, 768, 512, 384, 256, 128))
    nb = N // col_blocks
    tn = _pick(nb, (1408, 1024, 896, 768, 704, 512, 384, 256, 128)) if mode == "tn" else _pick(nb, (512, 384, 256, 128))
    per = nb // tn
    tr = _pick(R, (1408, 1056, 768, 512, 384, 256, 128)) if mode == "tn" else _pick(R, _BIG)
    nr = R // tr

    if mode == "nn":
        a_spec = pl.BlockSpec((tm, tr), lambda i, j, r: (i, r))
        b_spec = pl.BlockSpec((tr, tn), lambda i, j, r: (r, j))
    elif mode == "nt":
        a_spec = pl.BlockSpec((tm, tr), lambda i, j, r: (i, r))
        b_spec = pl.BlockSpec((tn, tr), lambda i, j, r: (j, r))
    else:
        a_spec = pl.BlockSpec((tr, tm), lambda i, j, r: (r, i))
        b_spec = pl.BlockSpec((tr, tn), lambda i, j, r: (r, j))
    if col_blocks == 1:
        o_spec = pl.BlockSpec((tm, tn), lambda i, j, r: (i, j))
        o_shape = (M, N)
    else:
        o_spec = pl.BlockSpec((1, tm, tn), lambda i, j, r: (j // per, i, j % per))
        o_shape = (col_blocks, M, nb)
    has_add = add is not None

    grid = (M // tm, N // tn, nr)

    def body(*refs):
        own_in, p_in, (o_ref,), p_out, (acc,), sems = _carried(plan, refs, 3 if has_add else 2, 1, 1)
        a_ref, b_ref = own_in[:2]
        i, j, r = pl.program_id(0), pl.program_id(1), pl.program_id(2)
        if plan:
            @pl.when((i == 0) & (j == 0) & (r == 0))
            def _():
                plan["start"](p_in, p_out, *sems)

        @pl.when(r == 0)
        def _():
            acc[...] = jnp.zeros_like(acc)

        acc[...] += lax.dot_general(_bf(a_ref[...]), _bf(b_ref[...]), dims, preferred_element_type=F32)

        @pl.when(r == nr - 1)
        def _():
            res = acc[...]
            if has_add:
                res = res + own_in[2][...]
            o_ref[...] = res.astype(o_ref.dtype).reshape(o_ref.shape)

        if plan:
            @pl.when((i == grid[0] - 1) & (j == grid[1] - 1) & (r == nr - 1))
            def _():
                plan["wait"](p_in, p_out, *sems)

    ins = [a, b] + ([add] if has_add else [])
    specs = [a_spec, b_spec] + ([o_spec] if has_add else [])
    extra = plan["arrays"] if plan else []
    extra_out = plan["out_shapes"] if plan else []
    res = pl.pallas_call(
        body, name=name, out_shape=[jax.ShapeDtypeStruct(o_shape, out_dtype)] + extra_out,
        grid=grid, in_specs=specs + [ANY] * len(extra), out_specs=[o_spec] + [ANY] * len(extra_out),
        scratch_shapes=[pltpu.VMEM((tm, tn), F32)] + (_dma_sems(plan["nsem"]) if plan else []),
        compiler_params=_params(("arbitrary",) * 3 if plan else ("parallel", "parallel", "arbitrary")),
    )(*ins, *extra)
    return (res[0], list(res[1:])) if plan else res[0]


def _rowcall(name, fn, L, row_ins, bc_ins, row_outs, acc_outs, tm=ROW_TILE):
    nt = L // tm
    specs = []
    for arr, w, cb, kind in row_ins:
        if kind == "row":
            specs.append(pl.BlockSpec((tm, w), lambda i, cb=cb: (i, cb)))
        elif kind == "lag":
            specs.append(pl.BlockSpec((tm, w), lambda i, cb=cb: (jnp.maximum(i - 1, 0), cb)))
        elif kind == "prev":
            specs.append(pl.BlockSpec((8, w), lambda i, cb=cb: (jnp.maximum(i * (tm // 8) - 1, 0), cb)))
        else:
            specs.append(pl.BlockSpec((8, w), lambda i, cb=cb: (jnp.minimum((i + 1) * (tm // 8), L // 8 - 1), cb)))
    for arr in bc_ins:
        specs.append(pl.BlockSpec(arr.shape, lambda i, nd=arr.ndim: (0,) * nd))
    out_shapes = [jax.ShapeDtypeStruct((L, w), dt) for w, dt in row_outs]
    out_specs = [pl.BlockSpec((tm, w), lambda i: (i, 0)) for w, dt in row_outs]
    out_shapes += [jax.ShapeDtypeStruct(s, F32) for s in acc_outs]
    out_specs += [pl.BlockSpec(s, lambda i, nd=len(s): (0,) * nd) for s in acc_outs]
    n_row, n_bc, n_ro = len(row_ins), len(bc_ins), len(row_outs)

    def body(*refs):
        i = pl.program_id(0)
        vals = [r[...] for r in refs[: n_row + n_bc]]
        outs, sums = fn(i, vals[:n_row], vals[n_row:])
        o_refs = refs[n_row + n_bc:]
        for r, v in zip(o_refs[:n_ro], outs):
            r[...] = v.astype(r.dtype)

        @pl.when(i == 0)
        def _():
            for r in o_refs[n_ro:]:
                r[...] = jnp.zeros_like(r)

        for r, v in zip(o_refs[n_ro:], sums):
            r[...] += v

    res = pl.pallas_call(
        body, name=name, out_shape=out_shapes, grid=(nt,), in_specs=specs, out_specs=out_specs,
        compiler_params=_params(("arbitrary",)),
    )(*[a for a, _, _, _ in row_ins], *bc_ins)
    return list(res[:n_ro]), list(res[n_ro:])


def _rowsum(t):
    return jnp.sum(t, axis=0, keepdims=True)


def _head_mats(width):
    e = (np.arange(width)[:, None] // HEAD == np.arange(LANES)[None, :]).astype(np.float32)
    return jnp.asarray(e), jnp.asarray(e.T)


def _fold_mat(width):
    ft = (np.arange(HEAD)[:, None] == np.arange(width)[None, :] % HEAD).astype(np.float32)
    return jnp.asarray(ft)


def _rms_f(h, g):
    return (h * lax.rsqrt(jnp.mean(h * h, axis=-1, keepdims=True) + RMS_EPS)) * g


def _prep_f(xr, xk, xv, xl, w0, w2p, a0, a2p, g2p, k_k, k_a, e, et):
    w_log = -jax.nn.softplus(-(w0 + _dotb(jnp.tanh(xl), w2p))) - 0.5
    lw = -jnp.exp(w_log)
    a = jax.nn.sigmoid(a0 + _dotb(xl, a2p))
    g = _dotb(jax.nn.sigmoid(xl), g2p)
    kkr = xk * k_k
    inv = lax.rsqrt(jnp.maximum(_doth(kkr * kkr, e), 1e-24))
    kk = kkr * _doth(inv, et)
    kf = xk * (1.0 + (a - 1.0) * k_a)
    return xr, lw, kf, xv, -kk, kk * a, g


def _post_f(y, r, kf, v, g, gn_w, gn_b, r_k, e, et):
    mu = _doth(y, e) * (1.0 / HEAD)
    yc = y - _doth(mu, et)
    var = _doth(yc * yc, e) * (1.0 / HEAD)
    yn = yc * _doth(lax.rsqrt(var + GN_EPS), et) * gn_w + gn_b
    bonus = _doth(r * kf * r_k, e)
    return (yn + _doth(bonus, et) * v) * g


def _foxprep_f(q, k, fl, qg8, kg8, fb, e, et, ft, fmask):
    def norm(t, g8):
        ms = _doth(t * t, e) * (1.0 / HEAD)
        return t * _doth(lax.rsqrt(ms + RMS_EPS), et) * _doth(g8, ft)[0:1]
    logf = jax.nn.log_sigmoid(fl + fb) * fmask
    return norm(q, qg8), norm(k, kg8), logf


def _merge_f(za, zb, pa, pb):
    return jax.nn.sigmoid(za) * pa + jax.nn.sigmoid(zb) * pb


def _swiglu_f(gate, up):
    return jax.nn.silu(gate) * up


def _tri(n, strict):
    row = lax.broadcasted_iota(jnp.int32, (n, n), 0)
    col = lax.broadcasted_iota(jnp.int32, (n, n), 1)
    return (row > col) if strict else (row >= col)


def _split2(x):
    hi = x.astype(BF16)
    return hi, (x - hi.astype(F32)).astype(BF16)


def _mm3_raw(a, b, ca, cb):
    dn = (((ca,), (cb,)), ((), ()))
    ah, al = _split2(a)
    bh, bl = _split2(b)
    dot = lambda p, q: lax.dot_general(p, q, dn, preferred_element_type=F32)
    return dot(ah, bh) + (dot(al, bh) + dot(ah, bl))


@functools.partial(jax.custom_vjp, nondiff_argnums=(2, 3))
def _mm3(a, b, ca, cb):
    return _mm3_raw(a, b, ca, cb)


def _mm3_fwd(a, b, ca, cb):
    return _mm3_raw(a, b, ca, cb), (a, b)


def _mm3_bwd(ca, cb, res, ct):
    a, b = res
    da = _mm3_raw(ct, b, 1, 1 - cb) if ca == 1 else _mm3_raw(b, ct, 1 - cb, 1)
    db = _mm3_raw(a, ct, 1 - ca, 0) if cb == 0 else _mm3_raw(ct, a, 0, 1 - ca)
    return da, db


_mm3.defvjp(_mm3_fwd, _mm3_bwd)


def _mmx_raw(t, x, ct):
    x1 = x.astype(BF16)
    r1 = x - x1.astype(F32)
    x2 = r1.astype(BF16)
    x3 = (r1 - x2.astype(F32)).astype(BF16)
    tb = t.astype(BF16)
    dot = lambda q: lax.dot_general(tb, q, (((ct,), (0,)), ((), ())), preferred_element_type=F32)
    return dot(x1) + (dot(x2) + dot(x3))


@jax.custom_vjp
def _mmx(t, x):
    return _mmx_raw(t, x, 1)


def _mmx_fwd(t, x):
    return _mmx_raw(t, x, 1), t


def _mmx_bwd(t, ct):
    return jnp.zeros_like(t), _mmx_raw(t, ct, 0)


_mmx.defvjp(_mmx_fwd, _mmx_bwd)

SCAN_HEADS = 16


def _scan_step(r, lw, k, v, a, b, st):
    c = r.shape[0]
    nh = r.shape[1] // HEAD
    incl = _tri(c, False)
    row2 = lax.broadcasted_iota(jnp.int32, (2 * c, 2 * c), 0)
    col2 = lax.broadcasted_iota(jnp.int32, (2 * c, 2 * c), 1)
    t_row = jnp.where(row2 >= c, row2 - c, row2)
    t_col = jnp.where(col2 >= c, col2 - c, col2)
    mask2 = (t_row > t_col) | ((row2 >= c) & (t_row == t_col))
    right =lax.broadcasted_iota(jnp.int32, (c, 2 * c), 1) >= c
    eye = lax.broadcasted_iota(jnp.int32, (HEAD, HEAD), 0) == lax.broadcasted_iota(jnp.int32, (HEAD, HEAD), 1)
    cl = _mmx(incl.astype(F32), lw)
    last = cl[c - 1:c, :]
    rt = r * jnp.exp(cl)
    at = a * jnp.exp(cl - lw)
    pinv = jnp.exp(-cl)
    bt = b * pinv
    kt = k * pinv
    pend = jnp.exp(last - cl)
    bl = b * pend
    kl = k * pend
    pe_last = jnp.exp(last)
    hs = range(nh)
    sl = [slice(h * HEAD, (h + 1) * HEAD) for h in hs]
    ar = [jnp.concatenate([at[:, sl[h]], rt[:, sl[h]]], axis=0) for h in hs]
    bk = [jnp.concatenate([bt[:, sl[h]], kt[:, sl[h]]], axis=0) for h in hs]
    amat = [jnp.where(mask2, _mm3(ar[h], bk[h], 1, 1), 0.0) for h in hs]
    res = [_mm3(jnp.concatenate([ar[h], amat[h][:, c:]], axis=1),
                jnp.concatenate([st[h], v[:, sl[h]]], axis=0), 1, 0) for h in hs]
    z = [jnp.concatenate([amat[h][:c, :c], res[h][:c]], axis=1) for h in hs]
    for _ in range(max(1, int(np.ceil(np.log2(c))))):
        z = [_mm3(z[h][:, :c], z[h], 1, 0) + jnp.where(right, z[h], 0.0) for h in hs]
    u = [z[h][:, c:] for h in hs]
    ys = [res[h][c:] + _mm3(amat[h][c:, :c], u[h], 1, 0) for h in hs]
    s1s = [_mm3(jnp.concatenate([bl[:, sl[h]], kl[:, sl[h]], jnp.where(eye, pe_last[:, sl[h]], 0.0)], axis=0),
                jnp.concatenate([u[h], v[:, sl[h]], st[h]], axis=0), 0, 0) for h in hs]
    return tuple(ys), tuple(s1s)


def _scan_heads(W):
    return SCAN_HEADS if W % (SCAN_HEADS * HEAD) == 0 else 2


def _scan_fwd(r, lw, k, v, a, b):
    L, W = r.shape
    nh = _scan_heads(W)
    nc, ng = L // CHUNK, W // (nh * HEAD)
    spec = pl.BlockSpec((CHUNK, nh * HEAD), lambda p, c: (c, p))

    def body(r_ref, lw_ref, k_ref, v_ref, a_ref, b_ref, y_ref, s_ref, st):
        @pl.when(pl.program_id(1) == 0)
        def _():
            st[...] = jnp.zeros_like(st)

        s0 = st[...]
        ys, s1s = _scan_step(r_ref[...], lw_ref[...], k_ref[...], v_ref[...], a_ref[...], b_ref[...], s0)
        s_ref[0] = s0
        st[...] = jnp.stack(s1s)
        y_ref[...] = jnp.concatenate(ys, axis=1)

    return pl.pallas_call(
        body, name="wkv7_fwd",
        out_shape=[jax.ShapeDtypeStruct((L, W), F32), jax.ShapeDtypeStruct((nc, nh * ng, HEAD, HEAD), F32)],
        grid=(ng, nc), in_specs=[spec] * 6,
        out_specs=[spec, pl.BlockSpec((1, nh, HEAD, HEAD), lambda p, c: (c, p, 0, 0))],
        scratch_shapes=[pltpu.VMEM((nh, HEAD, HEAD), F32)],
        compiler_params=_params(("parallel", "arbitrary")),
    )(r, lw, k, v, a, b)


def _scan_bwd(r, lw, k, v, a, b, s_all, dy):
    L, W = r.shape
    nh = _scan_heads(W)
    nc, ng = L // CHUNK, W // (nh * HEAD)
    spec = pl.BlockSpec((CHUNK, nh * HEAD), lambda p, c: (nc - 1 - c, p))

    def body(r_ref, lw_ref, k_ref, v_ref, a_ref, b_ref, s_ref, dy_ref,
             dr_ref, dlw_ref, dk_ref, dv_ref, da_ref, db_ref, dst):
        @pl.when(pl.program_id(1) == 0)
        def _():
            dst[...] = jnp.zeros_like(dst)

        _, vjp = jax.vjp(_scan_step, r_ref[...], lw_ref[...], k_ref[...], v_ref[...], a_ref[...], b_ref[...],
                         s_ref[0])
        dys = tuple(dy_ref[:, h * HEAD:(h + 1) * HEAD] for h in range(nh))
        g = vjp((dys, tuple(dst[h] for h in range(nh))))
        for ref, val in zip((dr_ref, dlw_ref, dk_ref, dv_ref, da_ref, db_ref), g[:6]):
            ref[...] = val
        dst[...] = g[6]

    return pl.pallas_call(
        body, name="wkv7_bwd", out_shape=[jax.ShapeDtypeStruct((L, W), F32)] * 6,
        grid=(ng, nc),
        in_specs=[spec] * 6 + [pl.BlockSpec((1, nh, HEAD, HEAD), lambda p, c: (nc - 1 - c, p, 0, 0)), spec],
        out_specs=[spec] * 6,
        scratch_shapes=[pltpu.VMEM((nh, HEAD, HEAD), F32)],
        compiler_params=_params(("parallel", "arbitrary")),
    )(r, lw, k, v, a, b, s_all, dy)


def _cumsum_rows(logf):
    L = logf.shape[0]
    t = LANES

    def body(x_ref, o_ref, carry):
        @pl.when(pl.program_id(0) == 0)
        def _():
            carry[...] = jnp.zeros_like(carry)

        c = _mmx(_tri(t, False).astype(F32), x_ref[...]) + carry[...]
        carry[...] = c[t - 1:t, :]
        o_ref[...] = c.T

    return pl.pallas_call(
        body, name="fox_cumsum", out_shape=jax.ShapeDtypeStruct((LANES, L), F32), grid=(L // t,),
        in_specs=[pl.BlockSpec((t, LANES), lambda i: (i, 0))],
        out_specs=pl.BlockSpec((LANES, t), lambda i: (0, i)),
        scratch_shapes=[pltpu.VMEM((1, LANES), F32)], compiler_params=_params(("arbitrary",)),
    )(logf)


def _rcumsum_cols(dct):
    L = dct.shape[1]
    t = LANES
    n = L // t

    def body(x_ref, o_ref, carry):
        @pl.when(pl.program_id(0) == 0)
        def _():
            carry[...] = jnp.zeros_like(carry)

        rc = _doth(x_ref[...], _tri(t, False).astype(F32)) + carry[...]
        carry[...] = rc[:, 0:1]
        o_ref[...] = rc.T

    return pl.pallas_call(
        body, name="fox_rcumsum", out_shape=jax.ShapeDtypeStruct((L, LANES), F32), grid=(n,),
        in_specs=[pl.BlockSpec((LANES, t), lambda i: (0, n - 1 - i))],
        out_specs=pl.BlockSpec((t, LANES), lambda i: (n - 1 - i, 0)),
        scratch_shapes=[pltpu.VMEM((LANES, 1), F32)], compiler_params=_params(("arbitrary",)),
    )(dct)


def _attn_tile(L):
    return _pick(L, (384, 256, 128))


def _head_lanes(hh, shape):
    return lax.broadcasted_iota(jnp.int32, shape, len(shape) - 1) // HEAD == hh


def _own(hh, block, other=0):
    return jnp.where(_head_lanes(hh, block.shape), block, jnp.asarray(other, block.dtype))


def _attn_scores(q, k, ck, qi, kj, t):
    s = _dot_bnt(q, k) * ATTN_SCALE - ck
    qpos = qi * t + lax.broadcasted_iota(jnp.int32, (t, t), 0)
    kpos = kj * t + lax.broadcasted_iota(jnp.int32, (t, t), 1)
    mask = (kpos <= qpos) & (kpos >= PAD_ROWS)
    return jnp.where(mask, s, NEG), mask


def _dot_bnt(a, b):
    return lax.dot_general(_bf(a), _bf(b), (((1,), (1,)), ((), ())), preferred_element_type=F32)


def _dot_btn(a, b):
    return lax.dot_general(_bf(a), _bf(b), (((0,), (0,)), ((), ())), preferred_element_type=F32)


ATTN_HEADS = 2


def _attn_group(W):
    return ATTN_HEADS if W % (ATTN_HEADS * HEAD) == 0 else 2


def _ck_rows(ct_ref, p, g):
    r0 = (g * p) % 8
    return [ct_ref[pl.ds(r0 + hh, 1), :] for hh in range(g)]


def _carried(plan, refs, n_in, n_out, n_scratch):
    ci = len(plan["arrays"]) if plan else 0
    co = len(plan["out_shapes"]) if plan else 0
    a = n_in + ci
    b = a + n_out + co
    return (refs[:n_in], refs[n_in:a], refs[a:a + n_out], refs[a + n_out:b], refs[b:b + n_scratch],
            refs[b + n_scratch:])


def _attn_fwd(q, k, v, ct, plan=None):
    L, W = q.shape
    t = _attn_tile(L)
    g = _attn_group(W)
    gw = g * HEAD
    nt, npair = L // t, W // gw
    pairs = [(i, j) for i in range(nt) for j in range(i + 1)]
    it = jnp.asarray([i for i, _ in pairs], jnp.int32)
    jt = jnp.asarray([j for _, j in pairs], jnp.int32)
    ns = len(pairs)
    qspec = pl.BlockSpec((t, gw), lambda p, s, it, jt: (it[s], p))
    kspec = pl.BlockSpec((t, gw), lambda p, s, it, jt: (jt[s], p))
    cspec = pl.BlockSpec((8, t), lambda p, s, it, jt: (g * p // 8, jt[s]))

    def body(it_ref, jt_ref, *refs):
        (q_ref, k_ref, v_ref, ct_ref), p_in, (o_ref, lse_ref), p_out, (m_s, acc), sems = _carried(plan, refs, 4, 2, 2)
        p, s = pl.program_id(0), pl.program_id(1)
        i, j = it_ref[s], jt_ref[s]
        if plan:
            @pl.when((p == 0) & (s == 0))
            def _():
                plan["start"](p_in, p_out, *sems)

            if plan["mid"] is not None:
                @pl.when((p == npair // 2) & (s == 0))
                def _():
                    plan["mid"](p_in, p_out, *sems)

        @pl.when(j == 0)
        def _():
            m_s[...] = jnp.full_like(m_s, NEG)
            acc[...] = jnp.zeros_like(acc)

        def accumulate():
            cks = _ck_rows(ct_ref, p, g)
            qb, kb, vf = _bf(q_ref[...]), _bf(k_ref[...]), v_ref[...]
            ss = [_attn_scores(_own(hh, qb), kb, cks[hh], i, j, t)[0] for hh in range(g)]
            prs, alphas = [], []
            for hh in range(g):
                m_old = m_s[hh]
                m_new = jnp.maximum(m_old, jnp.max(ss[hh], axis=-1, keepdims=True))
                alphas.append(jnp.exp(m_old - m_new))
                prs.append(jnp.exp(ss[hh] - m_new))
                m_s[hh] = m_new
            pvs = []
            for hh in range(g):
                p_hi, p_lo = _split2(prs[hh])
                vx = _bf(_own(hh, vf, 1.0))
                pvs.append(jnp.dot(p_hi, vx, preferred_element_type=F32)
                           + jnp.dot(p_lo, vx, preferred_element_type=F32))
            for hh in range(g):
                acc[hh] = alphas[hh] * acc[hh] + pvs[hh]

        accumulate()

        @pl.when(j == i)
        def _():
            lane = lax.broadcasted_iota(jnp.int32, (t, LANES), 1)
            lse = jnp.zeros((t, LANES), F32)
            out = jnp.zeros((t, gw), F32)
            for hh in range(g):
                a = acc[hh]
                row_sum = pltpu.roll(a, HEAD, 1)
                out = jnp.where(_head_lanes(hh, a.shape), a / row_sum, out)
                nb = ((hh + 1) % g) * HEAD
                lse = jnp.where(lane == hh, m_s[hh] + jnp.log(a[:, nb:nb + 1]), lse)
            o_ref[...] = out
            lse_ref[0] = lse

        if plan:
            @pl.when((p == npair - 1) & (s == ns - 1))
            def _():
                plan["wait"](p_in, p_out, *sems)

    extra = plan["arrays"] if plan else []
    extra_out = plan["out_shapes"] if plan else []
    res = pl.pallas_call(
        body, name="fox_attn_fwd",
        out_shape=[jax.ShapeDtypeStruct((L, W), F32), jax.ShapeDtypeStruct((npair, L, LANES), F32)] + extra_out,
        grid_spec=pltpu.PrefetchScalarGridSpec(
            num_scalar_prefetch=2, grid=(npair, ns),
            in_specs=[qspec, kspec, kspec, cspec] + [ANY] * len(extra),
            out_specs=[qspec, pl.BlockSpec((1, t, LANES), lambda p, s, it, jt: (p, it[s], 0))]
            + [ANY] * len(extra_out),
            scratch_shapes=[pltpu.VMEM((g, t, 1), F32), pltpu.VMEM((g, t, gw), F32)]
            + (_dma_sems(plan["nsem"]) if plan else [])),
        compiler_params=_params(("arbitrary",) * 2 if plan else ("parallel", "arbitrary")),
    )(it, jt, q, k, v, ct, *extra)
    return res[0], res[1], list(res[2:])


def _attn_bwd(q, k, v, ct, o, lse, do, plan=None):
    L, W = q.shape
    t = _attn_tile(L)
    g = _attn_group(W)
    gw = g * HEAD
    nt, npair = L // t, W // gw
    pairs = [(i, j) for j in range(nt) for i in range(j, nt)]
    it = jnp.asarray([i for i, _ in pairs], jnp.int32)
    jt = jnp.asarray([j for _, j in pairs], jnp.int32)
    ns = len(pairs)
    kspec = pl.BlockSpec((t, gw), lambda p, s, it, jt: (jt[s], p))
    qspec = pl.BlockSpec((t, gw), lambda p, s, it, jt: (it[s], p))
    cspec = pl.BlockSpec((8, t), lambda p, s, it, jt: (g * p // 8, jt[s]))
    lspec = pl.BlockSpec((1, t, LANES), lambda p, s, it, jt: (p, it[s], 0))

    def body(it_ref, jt_ref, *refs):
        ((q_ref, k_ref, v_ref, ct_ref, o_ref, lse_ref, do_ref), p_in, (dk_ref, dv_ref, dc_ref, dq_ref), p_out,
         (dk_s, dv_s, dc_s), sems) = _carried(plan, refs, 7, 4, 3)
        p, s = pl.program_id(0), pl.program_id(1)
        i, j = it_ref[s], jt_ref[s]
        if plan:
            @pl.when((p == 0) & (s == 0))
            def _():
                plan["start"](p_in, p_out, *sems)

        @pl.when(i == j)
        def _():
            dk_s[...] = jnp.zeros_like(dk_s)
            dv_s[...] = jnp.zeros_like(dv_s)
            dc_s[...] = jnp.zeros_like(dc_s)

        def tile_dq():
            cks = _ck_rows(ct_ref, p, g)
            qb, kb, vb, dob = _bf(q_ref[...]), _bf(k_ref[...]), _bf(v_ref[...]), _bf(do_ref[...])
            of = o_ref[...]
            hs = range(g)
            qm = [_own(hh, qb) for hh in hs]
            dom = [_own(hh, dob) for hh in hs]
            sm = [_attn_scores(qm[hh], kb, cks[hh], i, j, t) for hh in hs]
            dps = [_dot_bnt(dom[hh], vb) for hh in hs]
            prs = [jnp.where(sm[hh][1], jnp.exp(sm[hh][0] - lse_ref[0, :, hh:hh + 1]), 0.0) for hh in hs]
            dss = [prs[hh] * (dps[hh] - jnp.sum(dom[hh].astype(F32) * of, axis=-1, keepdims=True)) for hh in hs]
            dv_s[...] += sum(_dot_btn(prs[hh], dom[hh]) for hh in hs)
            dk_s[...] += sum(_dot_btn(dss[hh], qm[hh]) for hh in hs)
            for hh in hs:
                dc_s[hh] += -jnp.sum(dss[hh], axis=0, keepdims=True)
            return sum(_dotb(dss[hh], _own(hh, kb)) for hh in hs) * ATTN_SCALE

        rows = pl.ds(pl.multiple_of(i * t, t), t)

        @pl.when(j == 0)
        def _():
            dq_ref[rows, :] = tile_dq()

        @pl.when(j > 0)
        def _():
            dq_ref[rows, :] += tile_dq()

        @pl.when(i == nt - 1)
        def _():
            row = lax.broadcasted_iota(jnp.int32, (8, t), 0)
            dc = jnp.zeros((8, t), F32)
            for hh in range(g):
                dc = jnp.where(row == hh, dc_s[hh], dc)
            dk_ref[...] = dk_s[...] * ATTN_SCALE
            dv_ref[...] = dv_s[...]
            dc_ref[0] = dc

        if plan:
            @pl.when((p == npair - 1) & (s == ns - 1))
            def _():
                plan["wait"](p_in, p_out, *sems)

    extra = plan["arrays"] if plan else []
    extra_out = plan["out_shapes"] if plan else []
    res = pl.pallas_call(
        body, name="fox_attn_bwd",
        out_shape=[jax.ShapeDtypeStruct((L, W), F32), jax.ShapeDtypeStruct((L, W), F32),
                   jax.ShapeDtypeStruct((npair, 8, L), F32), jax.ShapeDtypeStruct((L, W), F32)] + extra_out,
        grid_spec=pltpu.PrefetchScalarGridSpec(
            num_scalar_prefetch=2, grid=(npair, ns),
            in_specs=[qspec, kspec, kspec, cspec, qspec, lspec, qspec] + [ANY] * len(extra),
            out_specs=[kspec, kspec, pl.BlockSpec((1, 8, t), lambda p, s, it, jt: (p, 0, jt[s])),
                       pl.BlockSpec((L, gw), lambda p, s, it, jt: (0, p))] + [ANY] * len(extra_out),
            scratch_shapes=[pltpu.VMEM((t, gw), F32), pltpu.VMEM((t, gw), F32), pltpu.VMEM((g, 1, t), F32)]
            + (_dma_sems(plan["nsem"]) if plan else [])),
        compiler_params=_params(("arbitrary",) * 2 if plan else ("parallel", "arbitrary")),
    )(it, jt, q, k, v, ct, o, lse, do, *extra)
    return res[0], res[1], res[2], res[3], list(res[4:])


def _place():
    x, y, c = lax.axis_index("x"), lax.axis_index("y"), lax.axis_index("c")
    chips = [(1 - x, y), (x, 1 - y), (1 - x, 1 - y)]
    return x, y, c, chips


def _remote(src, dst, send_sems, recv_sems, k, to):
    return pltpu.make_async_remote_copy(src_ref=src, dst_ref=dst, send_sem=send_sems.at[k],
                                        recv_sem=recv_sems.at[k], device_id=to, device_id_type=MESH)


def _dma_sems(n):
    return [pltpu.SemaphoreType.DMA((n,)), pltpu.SemaphoreType.DMA((n,))]


def _all_gather(shards, modes):
    return _run_exchange("gather_weights", _gather_plan(shards, modes))


def _gather_plan(shards, modes):
    n = len(shards)

    def out_shape(s, mode):
        r, c = s.shape
        return {"stack": (4, r, c), "cols": (r, 4 * c), "rows": (4 * r, c)}[mode]

    def window(outs, i, chip, cc):
        r, cw = shards[i].shape
        h = r // 2
        if modes[i] == "stack":
            return outs[i].at[chip, pl.ds(cc * h, h), :]
        if modes[i] == "cols":
            return outs[i].at[pl.ds(cc * h, h), pl.ds(pl.multiple_of(chip * cw, LANES), cw)]
        return outs[i].at[pl.ds(pl.multiple_of(chip * r + cc * h, 8), h), :]

    def first(ins, outs, ss, rs):
        x, y, c, chips = _place()
        cps = []
        for i in range(n):
            h = shards[i].shape[0] // 2
            for k, (cx, cy) in enumerate(chips):
                cps.append(_remote(ins[i].at[pl.ds(c * h, h), :], window(outs, i, 2 * x + y, c), ss, rs,
                                   6 * i + k, (cx, cy, c)))
        return cps

    def passed(outs, ss, rs):
        x, y, c, chips = _place()
        cps = []
        for k, (cx, cy) in enumerate(chips):
            for i in range(n):
                landed = window(outs, i, 2 * cx + cy, c)
                cps.append((_remote(landed, landed, ss, rs, 6 * i + k, (x, y, 1 - c)),
                            _remote(landed, landed, ss, rs, 6 * i + 3 + k, (x, y, 1 - c))))
        return cps

    def start(ins, outs, ss, rs):
        for cp in first(ins, outs, ss, rs):
            cp.start()

    def mid(ins, outs, ss, rs):
        for arrival, fwd in passed(outs, ss, rs):
            arrival.wait_recv()
            fwd.start()

    def wait(ins, outs, ss, rs):
        x, y, c, chips = _place()
        for k, (cx, cy) in enumerate(chips):
            for i in range(n):
                other = window(outs, i, 2 * cx + cy, 1 - c)
                _remote(other, other, ss, rs, 6 * i + 3 + k, (x, y, 1 - c)).wait_recv()
        for cp in first(ins, outs, ss, rs) + [fwd for _, fwd in passed(outs, ss, rs)]:
            cp.wait_send()

    return dict(arrays=list(shards), nsem=6 * n, start=start, mid=mid, wait=wait,
                out_shapes=[jax.ShapeDtypeStruct(out_shape(s, m), s.dtype) for s, m in zip(shards, modes)])


def _run_exchange(name, plan):
    n = len(plan["arrays"])

    def body(*refs):
        ins, outs, (ss, rs) = refs[:n], refs[n:n + len(plan["out_shapes"])], refs[n + len(plan["out_shapes"]):]
        plan["start"](ins, outs, ss, rs)
        if plan["mid"] is not None:
            plan["mid"](ins, outs, ss, rs)
        plan["wait"](ins, outs, ss, rs)

    return pl.pallas_call(
        body, name=name, out_shape=plan["out_shapes"], in_specs=[ANY] * n,
        out_specs=[ANY] * len(plan["out_shapes"]), scratch_shapes=_dma_sems(plan["nsem"]),
    )(*plan["arrays"])


def _pair_exchange(name, blocks):
    return _run_exchange(name, _pair_exchange_plan(blocks))


def _pair_exchange_plan(blocks):
    n = len(blocks)

    def copies(ins, outs, ss, rs):
        x, y, c, _ = _place()
        cps = []
        for i in range(n):
            h = blocks[i].shape[1] // 2
            cps.append(_remote(ins[i].at[:, pl.ds((1 - c) * h, h), :], outs[i], ss, rs, i, (x, y, 1 - c)))
        return cps

    def start(ins, outs, ss, rs):
        for cp in copies(ins, outs, ss, rs):
            cp.start()

    def wait(ins, outs, ss, rs):
        for cp in copies(ins, outs, ss, rs):
            cp.wait()

    return dict(arrays=list(blocks), nsem=n, start=start, mid=None, wait=wait,
                out_shapes=[jax.ShapeDtypeStruct((4, b.shape[1] // 2, b.shape[2]), b.dtype) for b in blocks])


def _chip_exchange(parts):
    return _run_exchange("reduce_chip_exchange", _chip_exchange_plan(parts))


def _chip_exchange_plan(parts):
    n = len(parts)

    def sends(ins, outs, ss, rs):
        x, y, c, chips = _place()
        return [_remote(ins[i].at[2 * cx + cy], outs[i].at[2 * x + y], ss, rs, 3 * i + k, (cx, cy, c))
                for i in range(n) for k, (cx, cy) in enumerate(chips)]

    def start(ins, outs, ss, rs):
        for cp in sends(ins, outs, ss, rs):
            cp.start()

    def wait(ins, outs, ss, rs):
        x, y, c, chips = _place()
        for i in range(n):
            for k, (cx, cy) in enumerate(chips):
                slot = outs[i].at[2 * cx + cy]
                _remote(slot, slot, ss, rs, 3 * i + k, (cx, cy, c)).wait_recv()
        for cp in sends(ins, outs, ss, rs):
            cp.wait_send()

    return dict(arrays=list(parts), nsem=3 * n, start=start, mid=None, wait=wait,
                out_shapes=[jax.ShapeDtypeStruct(p.shape, p.dtype) for p in parts])


def _pair_join(tots):
    n = len(tots)

    def body(*refs):
        ins, outs, (send_sems, recv_sems) = refs[:n], refs[n:2 * n], refs[2 * n:]
        x, y, c, _ = _place()
        cps = [_remote(ins[i], outs[i], send_sems, recv_sems, i, (x, y, 1 - c)) for i in range(n)]
        for cp in cps:
            cp.start()
        for cp in cps:
            cp.wait()

    return pl.pallas_call(
        body, name="reduce_pair_join", out_shape=[jax.ShapeDtypeStruct(t.shape, t.dtype) for t in tots],
        in_specs=[ANY] * n, out_specs=[ANY] * n, scratch_shapes=_dma_sems(n),
    )(*tots)


def _add_tile(h, cw):
    cap = max(8, (512 * 1024) // max(cw, 1))
    return _pick(h, tuple(t for t in (1024, 512, 256, 128, 64, 32, 16, 8) if t <= cap))


def _pair_add(name, block, recv):
    n, r, cw = block.shape
    h = r // 2
    tr = _add_tile(h, cw)
    c = lax.axis_index("c").astype(jnp.int32).reshape((1,))

    def body(c_ref, a_ref, b_ref, o_ref):
        o_ref[...] = (a_ref[...] + b_ref[...]).astype(o_ref.dtype)

    return pl.pallas_call(
        body, name=name, out_shape=jax.ShapeDtypeStruct((n, h, cw), BF16),
        grid_spec=pltpu.PrefetchScalarGridSpec(
            num_scalar_prefetch=1, grid=(n, h // tr),
            in_specs=[pl.BlockSpec((1, tr, cw), lambda a, i, cr: (a, cr[0] * (h // tr) + i, 0)),
                      pl.BlockSpec((1, tr, cw), lambda a, i, cr: (a, i, 0))],
            out_specs=pl.BlockSpec((1, tr, cw), lambda a, i, cr: (a, i, 0))),
        compiler_params=_params(("parallel", "parallel")),
    )(c, block, recv)


def _chip_add(name, parts):
    n, h, cw = parts.shape
    tr = _add_tile(h, cw)

    def body(a_ref, o_ref):
        f = lambda i: a_ref[i].astype(F32)
        o_ref[...] = ((f(0) + f(1)) + f(2)) + f(3)

    return pl.pallas_call(
        body, name=name, out_shape=jax.ShapeDtypeStruct((h, cw), F32), grid=(h // tr,),
        in_specs=[pl.BlockSpec((n, tr, cw), lambda i: (0, i, 0))],
        out_specs=pl.BlockSpec((tr, cw), lambda i: (i, 0)),
        compiler_params=_params(("parallel",)),
    )(parts)


def _adamw(name, w, g, m, v):
    R, C = w.shape
    tr = _pick(R, (128, 64, 32, 16, 8))
    spec = pl.BlockSpec((tr, C), lambda i: (i, 0))

    def body(w_ref, g_ref, m_ref, v_ref, d_ref, mo_ref, vo_ref):
        gr = g_ref[...]
        mn = ADAM_B1 * m_ref[...] + (1.0 - ADAM_B1) * gr
        vn = ADAM_B2 * v_ref[...] + (1.0 - ADAM_B2) * jnp.square(gr)
        m_hat = mn / (1.0 - ADAM_B1 ** ADAM_STEP)
        v_hat = vn / (1.0 - ADAM_B2 ** ADAM_STEP)
        d_ref[...] = -ADAM_LR * (m_hat / (jnp.sqrt(v_hat) + ADAM_EPS) + ADAM_WD * w_ref[...])
        mo_ref[...] = mn
        vo_ref[...] = vn

    return pl.pallas_call(
        body, name=name, out_shape=[jax.ShapeDtypeStruct((R, C), F32)] * 3, grid=(R // tr,),
        in_specs=[spec] * 4, out_specs=[spec] * 3, compiler_params=_params(("parallel",)),
    )(w, g, m, v)


def _pack(parts, dtype, row_mult):
    flat = jnp.concatenate([p.reshape(-1).astype(dtype) for p in parts])
    unit = row_mult * PACK_W
    pad = (-flat.shape[0]) % unit
    if pad:
        flat = jnp.concatenate([flat, jnp.zeros((pad,), dtype)])
    return flat.reshape(-1, PACK_W)


def _unpack(flat, shapes):
    out, off = [], 0
    for s in shapes:
        n = int(np.prod(s))
        out.append(flat[off:off + n].reshape(s))
        off += n
    return out


SHARDED = ("w_in", "rwkv_w2", "rwkv_a2", "rwkv_g2", "w_branch_a", "w_branch_b", "w_o", "w_gate_up", "w_down",
           "meta_tokens")
SHARD_AXIS = {"w_in": 1, "rwkv_w2": 1, "rwkv_a2": 1, "rwkv_g2": 1, "w_branch_a": 1, "w_branch_b": 1, "w_o": 0,
              "w_gate_up": 1, "w_down": 0, "meta_tokens": 1}
GATHER_MODE = {"w_in": "stack", "rwkv_w2": "stack", "rwkv_a2": "stack", "rwkv_g2": "stack", "w_branch_a": "stack",
               "w_branch_b": "stack", "w_o": "rows", "w_gate_up": "stack", "w_down": "rows", "meta_tokens": "stack"}
LATE_GATHER = ("w_branch_a", "w_branch_b", "w_o", "w_gate_up", "w_down")
SMALL = ("norm1_g", "rwkv_mu", "rwkv_w0", "rwkv_a0", "rwkv_k_k", "rwkv_k_a", "rwkv_r_k", "rwkv_gn_w",
         "rwkv_gn_b", "fox_q_norm_g", "fox_k_norm_g", "fox_f_bias", "norm2_g")
WEIGHTS = ("meta_tokens", "norm1_g", "w_in", "rwkv_mu", "rwkv_w0", "rwkv_w2", "rwkv_a0", "rwkv_a2", "rwkv_g2",
           "rwkv_k_k", "rwkv_k_a", "rwkv_r_k", "rwkv_gn_w", "rwkv_gn_b", "fox_q_norm_g", "fox_k_norm_g",
           "fox_f_bias", "w_branch_a", "w_branch_b", "w_o", "norm2_g", "w_gate_up", "w_down")


def _pad_rows(t, rows):
    return jnp.concatenate([t, jnp.zeros((rows - t.shape[0],) + t.shape[1:], t.dtype)], axis=0)


def _pad_cols(t, cols):
    return jnp.concatenate([t, jnp.zeros(t.shape[:-1] + (cols - t.shape[-1],), t.dtype)], axis=-1)


def _step(x, tgt, wts, mom1, mom2):
    seq, D = x.shape
    L = SEQ_ROW0 + seq
    RW = wts["rwkv_w0"].shape[-1]
    DL, AL, GL = wts["rwkv_w2"].shape[0], wts["rwkv_a2"].shape[0], wts["rwkv_g2"].shape[0]
    FW = wts["w_branch_b"].shape[0]
    FH = wts["fox_f_bias"].shape[-1]
    DFF = wts["w_down"].shape[0] * 4
    LORA = DL + AL + GL
    LW = -(-(LORA + FH) // 512) * 512
    assert RW == FW and (6 * RW) % D == 0 and (6 * RW + 2 * D) % LW == 0 and LORA % 8 == 0
    xj = lax.axis_index("x")
    yj = lax.axis_index("y")
    chip = 2 * xj + yj

    send = {n: wts[n] if n == "meta_tokens" else wts[n].astype(BF16) for n in SHARDED}
    early = tuple(n for n in SHARDED if n not in LATE_GATHER)
    full, stacked = {}, {}

    def place_own(n, got):
        shard, mode = send[n], GATHER_MODE[n]
        r, cw = shard.shape
        if mode == "stack":
            got = lax.dynamic_update_index_in_dim(got, shard, chip, 0)
            stacked[n] = got
            full[n] = jnp.concatenate([got[j] for j in range(4)], axis=1)
        elif mode == "cols":
            full[n] = lax.dynamic_update_slice(got, shard, (0, chip * cw))
        else:
            full[n] = lax.dynamic_update_slice(got, shard, (chip * r, 0))

    for n, got in zip(early, _all_gather([send[n] for n in early], [GATHER_MODE[n] for n in early])):
        place_own(n, got)
    late_plan = _gather_plan([send[n] for n in LATE_GATHER], [GATHER_MODE[n] for n in LATE_GATHER])
    meta = full["meta_tokens"]
    w_in_blocks = stacked["w_in"]
    blk_w = w_in_blocks.shape[2]

    def ref_cols(lo, hi):
        out = []
        for j in range(4):
            a, b = max(lo, j * blk_w), min(hi, (j + 1) * blk_w)
            if a < b:
                out.append(w_in_blocks[j][:, a - j * blk_w:b - j * blk_w])
        return out

    o = 0
    segs = {}
    for nm, wd in (("r", RW), ("k", RW), ("v", RW), ("wd", DL), ("ad", AL), ("gd", GL),
                   ("fq", FW), ("fk", FW), ("fv", FW), ("ff", FH), ("ga", D), ("gb", D)):
        segs[nm] = ref_cols(o, o + wd)
        o += wd
    order = ("r", "k", "v", "fq", "fk", "fv", "ga", "gb", "wd", "ad", "gd", "ff")
    w1 = jnp.concatenate([c for nm in order for c in segs[nm]]
                         + [jnp.zeros((D, LW - LORA - FH), BF16)], axis=1)
    cb_f = 3
    cb_gate = (6 * RW) // D
    cb_lora = (6 * RW + 2 * D) // LW

    e_m, et_m = _head_mats(RW)
    ft_m = _fold_mat(RW)
    mu = wts["rwkv_mu"]
    mu_rkv = mu[:, :3 * RW]
    mu_l = _pad_cols(mu[:, 3 * RW:], LW)
    w2p = _pad_rows(full["rwkv_w2"].astype(F32), LW)
    a2p = _pad_rows(jnp.concatenate([jnp.zeros((DL, RW), F32), full["rwkv_a2"].astype(F32)], axis=0), LW)
    g2p = _pad_rows(jnp.concatenate([jnp.zeros((DL + AL, RW), F32), full["rwkv_g2"].astype(F32)], axis=0), LW)
    r_k = wts["rwkv_r_k"].reshape(1, RW)
    qg8 = jnp.broadcast_to(wts["fox_q_norm_g"], (8, HEAD))
    kg8 = jnp.broadcast_to(wts["fox_k_norm_g"], (8, HEAD))
    fb = _pad_cols(wts["fox_f_bias"], LANES)
    fmask = (jnp.arange(LANES) < FH).astype(F32).reshape(1, LANES)
    lmask = ((jnp.arange(LW) >= LORA) & (jnp.arange(LW) < LORA + FH)).astype(F32).reshape(1, LW)

    h0 = jnp.concatenate([jnp.zeros((PAD_ROWS, D), F32), meta, x], axis=0)
    n1 = wts["norm1_g"]

    (xn,), _ = _rowcall("rms1_fwd", lambda i, r, b: ([_rms_f(r[0], b[0])], []), L,
                        [(h0, D, 0, "row")], [n1], [(D, BF16)], [])
    proj = _matmul(xn, w1, "nn", F32, "proj_fwd")

    def shift_fn(i, r, b):
        rows = lax.broadcasted_iota(jnp.int32, (ROW_TILE, 1), 0)
        outs = []
        for z, halo, m_ in ((r[0], r[1], b[0]), (r[2], r[3], b[1])):
            first = jnp.where(i == 0, 0.0, halo[7:8, :])
            zp = jnp.where(rows == 0, first, pltpu.roll(z, 1, 0))
            outs.append(z + (zp - z) * m_)
        return outs, []

    rkv_w = 3 * RW
    (x_rkv, x_l), _ = _rowcall(
        "shift_fwd", shift_fn, L,
        [(proj, rkv_w, 0, "row"), (proj, rkv_w, 0, "prev"), (proj, LW, cb_lora, "row"), (proj, LW, cb_lora, "prev")],
        [mu_rkv, mu_l], [(rkv_w, F32), (LW, F32)], [])

    prep_p = [wts["rwkv_w0"], w2p, wts["rwkv_a0"], a2p, g2p, wts["rwkv_k_k"], wts["rwkv_k_a"], e_m, et_m]
    prep_rows = [(x_rkv, RW, 0, "row"), (x_rkv, RW, 1, "row"), (x_rkv, RW, 2, "row"), (x_l, LW, 0, "row")]
    (s_r, s_lw, s_k, s_v, s_a, s_b, gate_g), _ = _rowcall(
        "rwkv_prep_fwd", lambda i, r, b: (list(_prep_f(*r, *b)), []), L, prep_rows, prep_p,
        [(RW, F32)] * 7, [])
    y_scan, s_all = _scan_fwd(s_r, s_lw, s_k, s_v, s_a, s_b)
    post_p = [wts["rwkv_gn_w"], wts["rwkv_gn_b"], r_k, e_m, et_m]
    post_rows = [(y_scan, RW, 0, "row"), (s_r, RW, 0, "row"), (s_k, RW, 0, "row"), (s_v, RW, 0, "row"),
                 (gate_g, RW, 0, "row")]
    (y_a,), _ = _rowcall("rwkv_post_fwd", lambda i, r, b: ([_post_f(*r, *b)], []), L, post_rows, post_p,
                         [(RW, BF16)], [])

    fox_p = [qg8, kg8, fb, e_m, et_m, ft_m, fmask]

    def foxprep_fn(i, r, b):
        fl = _doth(r[2] * b[-1], b[-2])
        return list(_foxprep_f(r[0], r[1], fl, *b[:-2])), []

    sel = (np.arange(LW)[:, None] - LORA == np.arange(LANES)[None, :]).astype(np.float32)
    sel = jnp.asarray(sel)
    fox_rows = [(proj, FW, cb_f, "row"), (proj, FW, cb_f + 1, "row"), (proj, LW, cb_lora, "row")]
    (f_q, f_k, logf), _ = _rowcall("fox_prep_fwd", foxprep_fn, L, fox_rows, fox_p + [sel, lmask],
                                   [(FW, BF16), (FW, BF16), (LANES, F32)], [])
    ct = _cumsum_rows(logf)
    f_v = proj[:, (cb_f + 2) * FW:(cb_f + 3) * FW]
    y_b32, lse, got_late = _attn_fwd(f_q, f_k, f_v, ct, plan=late_plan)
    for n, got in zip(LATE_GATHER, got_late):
        place_own(n, got)
    y_b = y_b32.astype(BF16)

    p_a = _matmul(y_a, full["w_branch_a"], "nn", F32, "branch_a_fwd")
    p_b = _matmul(y_b, full["w_branch_b"], "nn", F32, "branch_b_fwd")
    merge_rows = [(proj, D, cb_gate, "row"), (proj, D, cb_gate + 1, "row"), (p_a, D, 0, "row"), (p_b, D, 0, "row")]
    (merged,), _ = _rowcall("merge_fwd", lambda i, r, b: ([_merge_f(*r)], []), L, merge_rows, [], [(D, BF16)], [])
    h1 = _matmul(merged, full["w_o"], "nn", F32, "wo_fwd", add=h0)
    n2 = wts["norm2_g"]
    (xn2,), _ = _rowcall("rms2_fwd", lambda i, r, b: ([_rms_f(r[0], b[0])], []), L,
                         [(h1, D, 0, "row")], [n2], [(D, BF16)], [])
    gu = _matmul(xn2, full["w_gate_up"], "nn", F32, "gate_up_fwd")
    gu_rows = [(gu, DFF, 0, "row"), (gu, DFF, 1, "row")]
    (act,), _ = _rowcall("swiglu_fwd", lambda i, r, b: ([_swiglu_f(*r)], []), L, gu_rows, [], [(DFF, BF16)], [])
    h2 = _matmul(act, full["w_down"], "nn", F32, "down_fwd", add=h1)

    def loss_fn(i, r, b):
        err = jnp.where(i == 0, 0.0, r[0] - r[1])
        return [err * (1.0 / D)], [jnp.zeros((8, LANES), F32) + 0.5 / D * jnp.sum(err * err)]

    (dh2,), (loss_acc,) = _rowcall("loss", loss_fn, L, [(h2, D, 0, "row"), (tgt, D, 0, "lag")], [],
                                   [(D, F32)], [(8, LANES)])
    loss = lax.psum(loss_acc[0, 0], ("x", "y", "c"))

    dh2b = dh2.astype(BF16)
    g_w_down = _matmul(act, dh2b, "tn", F32, "down_dw")
    d_act = _matmul(dh2b, full["w_down"], "nt", F32, "down_dx")

    def swiglu_bwd(i, r, b):
        _, vjp = jax.vjp(_swiglu_f, r[0], r[1])
        return list(vjp(r[2])), []

    (d_gate, d_up), _ = _rowcall("swiglu_bwd", swiglu_bwd, L, gu_rows + [(d_act, DFF, 0, "row")], [],
                                 [(DFF, BF16), (DFF, BF16)], [])
    d_gu = jnp.concatenate([d_gate, d_up], axis=1)
    g_w_gu = _matmul(xn2, d_gu, "tn", F32, "gate_up_dw", col_blocks=4)
    ffn_blocks = [g_w_gu, g_w_down.reshape(4, -1, D)]
    d_xn2, ffn_recv = _matmul(d_gu, full["w_gate_up"], "nt", F32, "gate_up_dx", plan=_pair_exchange_plan(ffn_blocks))

    def rms_bwd(i, r, b):
        _, vjp = jax.vjp(_rms_f, r[0], b[0])
        dh, dg = vjp(r[1])
        return [dh + r[2]], [dg]

    (dh1,), (g_n2,) = _rowcall("rms2_bwd", rms_bwd, L,
                               [(h1, D, 0, "row"), (d_xn2, D, 0, "row"), (dh2, D, 0, "row")], [n2],
                               [(D, F32)], [(1, D)])
    dh1b = dh1.astype(BF16)
    g_w_o = _matmul(merged, dh1b, "tn", F32, "wo_dw")
    d_merged = _matmul(dh1b, full["w_o"], "nt", F32, "wo_dx")

    def merge_bwd(i, r, b):
        _, vjp = jax.vjp(_merge_f, *r[:4])
        return list(vjp(r[4])), []

    (d_za, d_zb, d_pa, d_pb), _ = _rowcall("merge_bwd", merge_bwd, L, merge_rows + [(d_merged, D, 0, "row")], [],
                                           [(D, BF16)] * 4, [])
    g_w_a = _matmul(y_a, d_pa, "tn", F32, "branch_a_dw", col_blocks=4)
    g_w_b = _matmul(y_b, d_pb, "tn", F32, "branch_b_dw", col_blocks=4)
    d_ya = _matmul(d_pa, full["w_branch_a"], "nt", F32, "branch_a_dx")
    d_yb = _matmul(d_pb, full["w_branch_b"], "nt", F32, "branch_b_dx")

    cj = lax.axis_index("c")
    names1 = ("w_gate_up", "w_down", "w_branch_a", "w_branch_b", "w_o")
    blocks1 = [g_w_a, g_w_b, g_w_o.reshape(4, -1, D)]
    parts1 = [_pair_add("reduce_pair_add_" + n, b, r)
              for n, b, r in zip(names1, ffn_blocks + blocks1,
                                 list(ffn_recv) + list(_pair_exchange("reduce_pair_exchange_1", blocks1)))]
    d_fk, d_fv, dc_rows, d_fq, recv1 = _attn_bwd(f_q, f_k, f_v, ct, y_b32, lse, d_yb,
                                                 plan=_chip_exchange_plan(parts1))
    dct = _pad_rows(dc_rows[:, :_attn_group(FW), :].reshape(-1, L), LANES)
    d_logf = _rcumsum_cols(dct)

    def foxprep_bwd(i, r, b):
        def f(q, k, xl, qg, kg, fbias):
            return _foxprep_f(q, k, _doth(xl * b[-1], b[-2]), qg, kg, fbias, *b[3:7])
        _, vjp = jax.vjp(f, r[0], r[1], r[2], b[0], b[1], b[2])
        dq, dk, dxl, dqg, dkg, dfb = vjp((r[3], r[4], r[5]))
        return [dq, dk, dxl], [dqg, dkg, dfb]

    (d_zfq, d_zfk, d_zl_f), (g_qg8, g_kg8, g_fb) = _rowcall(
        "fox_prep_bwd", foxprep_bwd, L,
        fox_rows + [(d_fq, FW, 0, "row"), (d_fk, FW, 0, "row"), (d_logf, LANES, 0, "row")],
        fox_p + [sel, lmask], [(FW, BF16), (FW, BF16), (LW, F32)], [(8, HEAD), (8, HEAD), (1, LANES)])

    def post_bwd(i, r, b):
        _, vjp = jax.vjp(lambda *a: _post_f(*a, b[3], b[4]), *r[:5], b[0], b[1], b[2])
        g = vjp(r[5])
        return list(g[:5]), list(g[5:])

    (d_y, d_r1, d_k1, d_v1, d_g), (g_gn_w, g_gn_b, g_r_k) = _rowcall(
        "rwkv_post_bwd", post_bwd, L, post_rows + [(d_ya, RW, 0, "row")], post_p,
        [(RW, F32)] * 5, [(1, RW)] * 3)
    d_r2, d_lw, d_k2, d_v2, d_a, d_b = _scan_bwd(s_r, s_lw, s_k, s_v, s_a, s_b, s_all, d_y)

    def prep_bwd(i, r, b):
        _, vjp = jax.vjp(lambda *a: _prep_f(*a, b[7], b[8]), *r[:4], *b[:7])
        cts = (r[4] + r[10], r[5], r[6] + r[11], r[7] + r[12], r[8], r[9], r[13])
        g = vjp(cts)
        return list(g[:4]), list(g[4:])

    bwd_rows = prep_rows + [(d_r2, RW, 0, "row"), (d_lw, RW, 0, "row"), (d_k2, RW, 0, "row"), (d_v2, RW, 0, "row"),
                            (d_a, RW, 0, "row"), (d_b, RW, 0, "row"), (d_r1, RW, 0, "row"), (d_k1, RW, 0, "row"),
                            (d_v1, RW, 0, "row"), (d_g, RW, 0, "row")]
    (d_xr, d_xk, d_xv, d_xl), (g_w0, g_w2p, g_a0, g_a2p, g_g2p, g_kk, g_ka) = _rowcall(
        "rwkv_prep_bwd", prep_bwd, L, bwd_rows, prep_p, [(RW, F32)] * 3 + [(LW, F32)],
        [(1, RW), (LW, RW), (1, RW), (LW, RW), (LW, RW), (1, RW), (1, RW)])

    def shift_bwd(i, r, b):
        last = pl.num_programs(0) - 1
        rows = lax.broadcasted_iota(jnp.int32, (ROW_TILE, 1), 0)
        outs, sums = [], []
        groups = ((r[0], r[1], r[2], r[3], b[0], None), (r[4], r[5], r[6], r[7], b[1], r[8]))
        for d, dnext, z, zhalo, m_, extra in groups:
            nxt = jnp.where(i == last, 0.0, dnext[0:1, :])
            d_up = jnp.where(rows == ROW_TILE - 1, nxt, pltpu.roll(d, ROW_TILE - 1, 0))
            dz = d * (1.0 - m_) + d_up * m_
            if extra is not None:
                dz = dz + extra
            first = jnp.where(i == 0, 0.0, zhalo[7:8, :])
            zp = jnp.where(rows == 0, first, pltpu.roll(z, 1, 0))
            outs.append(dz)
            sums.append(_rowsum(d * (zp - z)))
        return outs, sums

    d_xrkv = jnp.concatenate([d_xr, d_xk, d_xv], axis=1)
    (d_zrkv, d_zl), (g_mu_rkv, g_mu_l) = _rowcall(
        "shift_bwd", shift_bwd, L,
        [(d_xrkv, rkv_w, 0, "row"), (d_xrkv, rkv_w, 0, "next"), (proj, rkv_w, 0, "row"), (proj, rkv_w, 0, "prev"),
         (d_xl, LW, 0, "row"), (d_xl, LW, 0, "next"), (proj, LW, cb_lora, "row"), (proj, LW, cb_lora, "prev"),
         (d_zl_f, LW, 0, "row")],
        [mu_rkv, mu_l], [(rkv_w, BF16), (LW, BF16)], [(1, rkv_w), (1, LW)])

    n_in = stacked["w_in"].shape[2] * 4
    cs = n_in // 4
    cp = -(-cs // LANES) * LANES
    pieces = [d_zrkv, d_zl[:, :LORA], d_zfq, d_zfk, d_fv.astype(BF16), d_zl[:, LORA:LORA + FH], d_za, d_zb]
    cuts, off = [], 0
    for j in range(4):
        lo, hi = j * cs, (j + 1) * cs
        off = 0
        for pc in pieces:
            a, b = max(lo, off), min(hi, off + pc.shape[1])
            if a < b:
                cuts.append(pc[:, a - off:b - off])
            off += pc.shape[1]
        if cp > cs:
            cuts.append(jnp.zeros((L, cp - cs), BF16))
    d_blk = jnp.concatenate(cuts, axis=1)
    w_blk = jnp.concatenate([_pad_cols(stacked["w_in"][j], cp) for j in range(4)], axis=1)
    g_w_in = _matmul(xn, d_blk, "tn", F32, "proj_dw", col_blocks=4)
    part_w_in = _pair_add("reduce_pair_add_w_in", g_w_in, _pair_exchange("reduce_pair_exchange_2", [g_w_in])[0])
    d_xn, recv_w_in = _matmul(d_blk, w_blk, "nt", F32, "proj_dx", plan=_chip_exchange_plan([part_w_in]))
    (dh0,), (g_n1,) = _rowcall("rms1_bwd", rms_bwd, L,
                               [(h0, D, 0, "row"), (d_xn, D, 0, "row"), (dh1, D, 0, "row")], [n1],
                               [(D, F32)], [(1, D)])
    grad_x = dh0[SEQ_ROW0:]
    g_meta = dh0[PAD_ROWS:SEQ_ROW0]

    tiny = {"rwkv_w2": g_w2p[:DL], "rwkv_a2": g_a2p[DL:DL + AL], "rwkv_g2": g_g2p[DL + AL:LORA],
            "meta_tokens": g_meta}
    g_mu = jnp.concatenate([g_mu_rkv, g_mu_l[:, :LORA]], axis=1)
    gsmall = {
        "norm1_g": g_n1, "rwkv_mu": g_mu, "rwkv_w0": g_w0, "rwkv_a0": g_a0, "rwkv_k_k": g_kk, "rwkv_k_a": g_ka,
        "rwkv_r_k": g_r_k.reshape(wts["rwkv_r_k"].shape), "rwkv_gn_w": g_gn_w, "rwkv_gn_b": g_gn_b,
        "fox_q_norm_g": g_qg8[0:1], "fox_k_norm_g": g_kg8[0:1], "fox_f_bias": g_fb[:, :FH], "norm2_g": g_n2,
    }
    small_flat = jnp.concatenate([gsmall[n].reshape(-1) for n in SMALL])

    tiny_names = tuple(tiny)

    def tiny_block(j):
        parts = []
        for n in tiny_names:
            w = tiny[n].shape[1] // 4
            parts.append(tiny[n][:, j * w:(j + 1) * w])
        return _pack(parts + [small_flat], F32, 32)

    small_blocks = jnp.stack([tiny_block(j) for j in range(4)])
    part_small = _pair_add("reduce_pair_add_small", small_blocks,
                           _pair_exchange("reduce_pair_exchange_3", [small_blocks])[0])
    recv_small = _chip_exchange([part_small])
    names = names1 + ("w_in", "small")
    tots = []
    for n, p, r in zip(names, parts1 + [part_w_in, part_small], list(recv1) + list(recv_w_in) + list(recv_small)):
        own = lax.dynamic_index_in_dim(p, chip, 0, keepdims=False)
        tots.append(_chip_add("reduce_chip_add_" + n, lax.dynamic_update_index_in_dim(r, own, chip, 0)))
    others = _pair_join(tots)
    red = [jnp.where(cj == 0, jnp.concatenate([t, o_], axis=0), jnp.concatenate([o_, t], axis=0))
           for t, o_ in zip(tots, others)]
    grads = {n: red[i] for i, n in enumerate(names[:-1])}
    grads["w_in"] = grads["w_in"][:, :cs]
    tiny_shapes = [wts[n].shape for n in tiny_names]
    got = _unpack(red[-1].reshape(-1), tiny_shapes + [small_flat.shape])
    for n, t in zip(tiny_names, got):
        grads[n] = t
    for n, t in zip(SMALL, _unpack(got[-1], [wts[n].shape for n in SMALL])):
        grads[n] = t

    delta, new_m, new_v = {}, {}, {}
    for n in SHARDED:
        delta[n], new_m[n], new_v[n] = _adamw("adamw_" + n, wts[n], grads[n], mom1[n], mom2[n])
    pk = lambda d: _pack([d[n] for n in SMALL], F32, 8)
    ds, ms, vs = _adamw("adamw_small", pk(wts), pk(grads), pk(mom1), pk(mom2))
    small_shapes = [wts[n].shape for n in SMALL]
    for dst, src in ((delta, ds), (new_m, ms), (new_v, vs)):
        for n, t in zip(SMALL, _unpack(src.reshape(-1), small_shapes)):
            dst[n] = t
    return loss, grad_x, grads, delta, new_m, new_v


def kernel(x, meta_tokens, norm1_g, w_in, rwkv_mu, rwkv_w0, rwkv_w2, rwkv_a0, rwkv_a2, rwkv_g2, rwkv_k_k, rwkv_k_a, rwkv_r_k, rwkv_gn_w, rwkv_gn_b, fox_q_norm_g, fox_k_norm_g, fox_f_bias, w_branch_a, w_branch_b, w_o, norm2_g, w_gate_up, w_down, loss_target, m_meta_tokens, m_norm1_g, m_w_in, m_rwkv_mu, m_rwkv_w0, m_rwkv_w2, m_rwkv_a0, m_rwkv_a2, m_rwkv_g2, m_rwkv_k_k, m_rwkv_k_a, m_rwkv_r_k, m_rwkv_gn_w, m_rwkv_gn_b, m_fox_q_norm_g, m_fox_k_norm_g, m_fox_f_bias, m_w_branch_a, m_w_branch_b, m_w_o, m_norm2_g, m_w_gate_up, m_w_down, v_meta_tokens, v_norm1_g, v_w_in, v_rwkv_mu, v_rwkv_w0, v_rwkv_w2, v_rwkv_a0, v_rwkv_a2, v_rwkv_g2, v_rwkv_k_k, v_rwkv_k_a, v_rwkv_r_k, v_rwkv_gn_w, v_rwkv_gn_b, v_fox_q_norm_g, v_fox_k_norm_g, v_fox_f_bias, v_w_branch_a, v_w_branch_b, v_w_o, v_norm2_g, v_w_gate_up, v_w_down):
    args = dict(locals())
    shapes = {n: args[n].shape for n in WEIGHTS}

    def drop_depth(t, n):
        if n == "meta_tokens":
            return t
        if n == "rwkv_r_k":
            return t.reshape(1, -1)
        return t.reshape(t.shape[1:]) if t.ndim == 3 else t

    wts = {n: drop_depth(args[n], n) for n in WEIGHTS}
    mom1 = {n: drop_depth(args["m_" + n], n) for n in WEIGHTS}
    mom2 = {n: drop_depth(args["v_" + n], n) for n in WEIGHTS}
    loss, grad_x, grads, delta, new_m, new_v = _step(x[0], loss_target[0], wts, mom1, mom2)
    outs = [loss, grad_x[None]]
    for d in (grads, delta, new_m, new_v):
        outs += [d[n].reshape(shapes[n]) for n in WEIGHTS]
    return tuple(outs)
```

```python
import functools

import jax
import jax.numpy as jnp
import numpy as np
from jax import lax
from jax.experimental import pallas as pl
from jax.experimental.pallas import tpu as pltpu

F32 = jnp.float32
BF16 = jnp.bfloat16
MESH = pl.DeviceIdType.MESH
ANY = pl.BlockSpec(memory_space=pl.ANY)

N_META = 16
HEAD = 64
ROW_TILE = 128
PAD_ROWS = ROW_TILE - N_META
SEQ_ROW0 = ROW_TILE
CHUNK = 64
LANES = 128
PACK_W = 1024
RMS_EPS = 1e-6
GN_EPS = 64e-5
ATTN_SCALE = HEAD ** -0.5
NEG = -1e30
VMEM_LIMIT_V7X = 56 * 1024 * 1024

ADAM_LR = 0.001
ADAM_B1 = 0.9
ADAM_B2 = 0.999
ADAM_EPS = 1e-08
ADAM_WD = 0.01
ADAM_STEP = 10


def _params(sem=None):
    return pltpu.CompilerParams(dimension_semantics=sem, vmem_limit_bytes=VMEM_LIMIT_V7X)


def _pick(n, cands):
    for c in cands:
        if n % c == 0:
            return c
    return n


def _bf(t):
    return t.astype(BF16)


def _dotb(a, b):
    return jnp.dot(_bf(a), _bf(b), preferred_element_type=F32)


def _split3(x):
    x1 = x.astype(BF16)
    r1 = x - x1.astype(F32)
    x2 = r1.astype(BF16)
    return x1, x2, (r1 - x2.astype(F32)).astype(BF16)


def _mme_raw(x, e, ce):
    eb = e.astype(BF16)
    dot = lambda q: lax.dot_general(q, eb, (((1,), (ce,)), ((), ())), preferred_element_type=F32)
    x1, x2, x3 = _split3(x)
    return dot(x1) + (dot(x2) + dot(x3))


@jax.custom_vjp
def _doth(x, e):
    return _mme_raw(x, e, 0)


def _doth_fwd(x, e):
    return _mme_raw(x, e, 0), e


def _doth_bwd(e, ct):
    return _mme_raw(ct, e, 1), jnp.zeros_like(e)


_doth.defvjp(_doth_fwd, _doth_bwd)


_BIG = (2816, 2048, 1536, 1408, 1024, 768, 704, 512, 384, 256, 128)


def _matmul(a, b, mode, out_dtype, name, add=None, col_blocks=1, plan=None):
    if mode == "nn":
        (M, R), (_, N) = a.shape, b.shape
        dims = (((1,), (0,)), ((), ()))
    elif mode == "nt":
        (M, R), (N, _) = a.shape, b.shape
        dims = (((1,), (1,)), ((), ()))
    else:
        (R, M), (_, N) = a.shape, b.shape
        dims = (((0,), (0,)), ((), ()))
    tm = _pick(M, (1408, 1024, 768, 512, 384, 256, 128))
    nb = N // col_blocks
    tn = _pick(nb, (1408, 1024, 896, 768, 704, 512, 384, 256, 128)) if mode == "tn" else _pick(nb, (512, 384, 256, 128))
    per = nb // tn
    tr = _pick(R, (1408, 1056, 768, 512, 384, 256, 128)) if mode == "tn" else _pick(R, _BIG)
    nr = R // tr

    if mode == "nn":
        a_spec = pl.BlockSpec((tm, tr), lambda i, j, r: (i, r))
        b_spec = pl.BlockSpec((tr, tn), lambda i, j, r: (r, j))
    elif mode == "nt":
        a_spec = pl.BlockSpec((tm, tr), lambda i, j, r: (i, r))
        b_spec = pl.BlockSpec((tn, tr), lambda i, j, r: (j, r))
    else:
        a_spec = pl.BlockSpec((tr, tm), lambda i, j, r: (r, i))
        b_spec = pl.BlockSpec((tr, tn), lambda i, j, r: (r, j))
    if col_blocks == 1:
        o_spec = pl.BlockSpec((tm, tn), lambda i, j, r: (i, j))
        o_shape = (M, N)
    else:
        o_spec = pl.BlockSpec((1, tm, tn), lambda i, j, r: (j // per, i, j % per))
        o_shape = (col_blocks, M, nb)
    has_add = add is not None

    grid = (M // tm, N // tn, nr)

    def body(*refs):
        own_in, p_in, (o_ref,), p_out, (acc,), sems = _carried(plan, refs, 3 if has_add else 2, 1, 1)
        a_ref, b_ref = own_in[:2]
        i, j, r = pl.program_id(0), pl.program_id(1), pl.program_id(2)
        if plan:
            @pl.when((i == 0) & (j == 0) & (r == 0))
            def _():
                plan["start"](p_in, p_out, *sems)

        @pl.when(r == 0)
        def _():
            acc[...] = jnp.zeros_like(acc)

        acc[...] += lax.dot_general(_bf(a_ref[...]), _bf(b_ref[...]), dims, preferred_element_type=F32)

        @pl.when(r == nr - 1)
        def _():
            res = acc[...]
            if has_add:
                res = res + own_in[2][...]
            o_ref[...] = res.astype(o_ref.dtype).reshape(o_ref.shape)

        if plan:
            @pl.when((i == grid[0] - 1) & (j == grid[1] - 1) & (r == nr - 1))
            def _():
                plan["wait"](p_in, p_out, *sems)

    ins = [a, b] + ([add] if has_add else [])
    specs = [a_spec, b_spec] + ([o_spec] if has_add else [])
    extra = plan["arrays"] if plan else []
    extra_out = plan["out_shapes"] if plan else []
    res = pl.pallas_call(
        body, name=name, out_shape=[jax.ShapeDtypeStruct(o_shape, out_dtype)] + extra_out,
        grid=grid, in_specs=specs + [ANY] * len(extra), out_specs=[o_spec] + [ANY] * len(extra_out),
        scratch_shapes=[pltpu.VMEM((tm, tn), F32)] + (_dma_sems(plan["nsem"]) if plan else []),
        compiler_params=_params(("arbitrary",) * 3 if plan else ("parallel", "parallel", "arbitrary")),
    )(*ins, *extra)
    return (res[0], list(res[1:])) if plan else res[0]


def _rowcall(name, fn, L, row_ins, bc_ins, row_outs, acc_outs, tm=ROW_TILE):
    nt = L // tm
    specs = []
    for arr, w, cb, kind in row_ins:
        if kind == "row":
            specs.append(pl.BlockSpec((tm, w), lambda i, cb=cb: (i, cb)))
        elif kind == "lag":
            specs.append(pl.BlockSpec((tm, w), lambda i, cb=cb: (jnp.maximum(i - 1, 0), cb)))
        elif kind == "prev":
            specs.append(pl.BlockSpec((8, w), lambda i, cb=cb: (jnp.maximum(i * (tm // 8) - 1, 0), cb)))
        else:
            specs.append(pl.BlockSpec((8, w), lambda i, cb=cb: (jnp.minimum((i + 1) * (tm // 8), L // 8 - 1), cb)))
    for arr in bc_ins:
        specs.append(pl.BlockSpec(arr.shape, lambda i, nd=arr.ndim: (0,) * nd))
    out_shapes = [jax.ShapeDtypeStruct((L, w), dt) for w, dt in row_outs]
    out_specs = [pl.BlockSpec((tm, w), lambda i: (i, 0)) for w, dt in row_outs]
    out_shapes += [jax.ShapeDtypeStruct(s, F32) for s in acc_outs]
    out_specs += [pl.BlockSpec(s, lambda i, nd=len(s): (0,) * nd) for s in acc_outs]
    n_row, n_bc, n_ro = len(row_ins), len(bc_ins), len(row_outs)

    def body(*refs):
        i = pl.program_id(0)
        vals = [r[...] for r in refs[: n_row + n_bc]]
        outs, sums = fn(i, vals[:n_row], vals[n_row:])
        o_refs = refs[n_row + n_bc:]
        for r, v in zip(o_refs[:n_ro], outs):
            r[...] = v.astype(r.dtype)

        @pl.when(i == 0)
        def _():
            for r in o_refs[n_ro:]:
                r[...] = jnp.zeros_like(r)

        for r, v in zip(o_refs[n_ro:], sums):
            r[...] += v

    res = pl.pallas_call(
        body, name=name, out_shape=out_shapes, grid=(nt,), in_specs=specs, out_specs=out_specs,
        compiler_params=_params(("arbitrary",)),
    )(*[a for a, _, _, _ in row_ins], *bc_ins)
    return list(res[:n_ro]), list(res[n_ro:])


def _rowsum(t):
    return jnp.sum(t, axis=0, keepdims=True)


def _head_mats(width):
    e = (np.arange(width)[:, None] // HEAD == np.arange(LANES)[None, :]).astype(np.float32)
    return jnp.asarray(e), jnp.asarray(e.T)


def _fold_mat(width):
    ft = (np.arange(HEAD)[:, None] == np.arange(width)[None, :] % HEAD).astype(np.float32)
    return jnp.asarray(ft)


def _rms_f(h, g):
    return (h * lax.rsqrt(jnp.mean(h * h, axis=-1, keepdims=True) + RMS_EPS)) * g


def _prep_f(xr, xk, xv, xl, w0, w2p, a0, a2p, g2p, k_k, k_a, e, et):
    w_log = -jax.nn.softplus(-(w0 + _dotb(jnp.tanh(xl), w2p))) - 0.5
    lw = -jnp.exp(w_log)
    a = jax.nn.sigmoid(a0 + _dotb(xl, a2p))
    g = _dotb(jax.nn.sigmoid(xl), g2p)
    kkr = xk * k_k
    inv = lax.rsqrt(jnp.maximum(_doth(kkr * kkr, e), 1e-24))
    kk = kkr * _doth(inv, et)
    kf = xk * (1.0 + (a - 1.0) * k_a)
    return xr, lw, kf, xv, -kk, kk * a, g


def _post_f(y, r, kf, v, g, gn_w, gn_b, r_k, e, et):
    mu = _doth(y, e) * (1.0 / HEAD)
    yc = y - _doth(mu, et)
    var = _doth(yc * yc, e) * (1.0 / HEAD)
    yn = yc * _doth(lax.rsqrt(var + GN_EPS), et) * gn_w + gn_b
    bonus = _doth(r * kf * r_k, e)
    return (yn + _doth(bonus, et) * v) * g


def _foxprep_f(q, k, fl, qg8, kg8, fb, e, et, ft, fmask):
    def norm(t, g8):
        ms = _doth(t * t, e) * (1.0 / HEAD)
        return t * _doth(lax.rsqrt(ms + RMS_EPS), et) * _doth(g8, ft)[0:1]
    logf = jax.nn.log_sigmoid(fl + fb) * fmask
    return norm(q, qg8), norm(k, kg8), logf


def _merge_f(za, zb, pa, pb):
    return jax.nn.sigmoid(za) * pa + jax.nn.sigmoid(zb) * pb


def _swiglu_f(gate, up):
    return jax.nn.silu(gate) * up


def _tri(n, strict):
    row = lax.broadcasted_iota(jnp.int32, (n, n), 0)
    col = lax.broadcasted_iota(jnp.int32, (n, n), 1)
    return (row > col) if strict else (row >= col)


def _split2(x):
    hi = x.astype(BF16)
    return hi, (x - hi.astype(F32)).astype(BF16)


def _mm3_raw(a, b, ca, cb):
    dn = (((ca,), (cb,)), ((), ()))
    ah, al = _split2(a)
    bh, bl = _split2(b)
    dot = lambda p, q: lax.dot_general(p, q, dn, preferred_element_type=F32)
    return dot(ah, bh) + (dot(al, bh) + dot(ah, bl))


@functools.partial(jax.custom_vjp, nondiff_argnums=(2, 3))
def _mm3(a, b, ca, cb):
    return _mm3_raw(a, b, ca, cb)


def _mm3_fwd(a, b, ca, cb):
    return _mm3_raw(a, b, ca, cb), (a, b)


def _mm3_bwd(ca, cb, res, ct):
    a, b = res
    da = _mm3_raw(ct, b, 1, 1 - cb) if ca == 1 else _mm3_raw(b, ct, 1 - cb, 1)
    db = _mm3_raw(a, ct, 1 - ca, 0) if cb == 0 else _mm3_raw(ct, a, 0, 1 - ca)
    return da, db


_mm3.defvjp(_mm3_fwd, _mm3_bwd)


def _mmx_raw(t, x, ct):
    x1 = x.astype(BF16)
    r1 = x - x1.astype(F32)
    x2 = r1.astype(BF16)
    x3 = (r1 - x2.astype(F32)).astype(BF16)
    tb = t.astype(BF16)
    dot = lambda q: lax.dot_general(tb, q, (((ct,), (0,)), ((), ())), preferred_element_type=F32)
    return dot(x1) + (dot(x2) + dot(x3))


@jax.custom_vjp
def _mmx(t, x):
    return _mmx_raw(t, x, 1)


def _mmx_fwd(t, x):
    return _mmx_raw(t, x, 1), t


def _mmx_bwd(t, ct):
    return jnp.zeros_like(t), _mmx_raw(t, ct, 0)


_mmx.defvjp(_mmx_fwd, _mmx_bwd)

SCAN_HEADS = 16


def _scan_step(r, lw, k, v, a, b, st):
    c = r.shape[0]
    nh = r.shape[1] // HEAD
    incl = _tri(c, False)
    row2 = lax.broadcasted_iota(jnp.int32, (2 * c, 2 * c), 0)
    col2 = lax.broadcasted_iota(jnp.int32, (2 * c, 2 * c), 1)
    t_row = jnp.where(row2 >= c, row2 - c, row2)
    t_col = jnp.where(col2 >= c, col2 - c, col2)
    mask2 = (t_row > t_col) | ((row2 >= c) & (t_row == t_col))
    right =lax.broadcasted_iota(jnp.int32, (c, 2 * c), 1) >= c
    eye = lax.broadcasted_iota(jnp.int32, (HEAD, HEAD), 0) == lax.broadcasted_iota(jnp.int32, (HEAD, HEAD), 1)
    cl = _mmx(incl.astype(F32), lw)
    last = cl[c - 1:c, :]
    rt = r * jnp.exp(cl)
    at = a * jnp.exp(cl - lw)
    pinv = jnp.exp(-cl)
    bt = b * pinv
    kt = k * pinv
    pend = jnp.exp(last - cl)
    bl = b * pend
    kl = k * pend
    pe_last = jnp.exp(last)
    hs = range(nh)
    sl = [slice(h * HEAD, (h + 1) * HEAD) for h in hs]
    ar = [jnp.concatenate([at[:, sl[h]], rt[:, sl[h]]], axis=0) for h in hs]
    bk = [jnp.concatenate([bt[:, sl[h]], kt[:, sl[h]]], axis=0) for h in hs]
    amat = [jnp.where(mask2, _mm3(ar[h], bk[h], 1, 1), 0.0) for h in hs]
    res = [_mm3(jnp.concatenate([ar[h], amat[h][:, c:]], axis=1),
                jnp.concatenate([st[h], v[:, sl[h]]], axis=0), 1, 0) for h in hs]
    z = [jnp.concatenate([amat[h][:c, :c], res[h][:c]], axis=1) for h in hs]
    for _ in range(max(1, int(np.ceil(np.log2(c))))):
        z = [_mm3(z[h][:, :c], z[h], 1, 0) + jnp.where(right, z[h], 0.0) for h in hs]
    u = [z[h][:, c:] for h in hs]
    ys = [res[h][c:] + _mm3(amat[h][c:, :c], u[h], 1, 0) for h in hs]
    s1s = [_mm3(jnp.concatenate([bl[:, sl[h]], kl[:, sl[h]], jnp.where(eye, pe_last[:, sl[h]], 0.0)], axis=0),
                jnp.concatenate([u[h], v[:, sl[h]], st[h]], axis=0), 0, 0) for h in hs]
    return tuple(ys), tuple(s1s)


def _scan_heads(W):
    return SCAN_HEADS if W % (SCAN_HEADS * HEAD) == 0 else 2


def _scan_fwd(r, lw, k, v, a, b):
    L, W = r.shape
    nh = _scan_heads(W)
    nc, ng = L // CHUNK, W // (nh * HEAD)
    spec = pl.BlockSpec((CHUNK, nh * HEAD), lambda p, c: (c, p))

    def body(r_ref, lw_ref, k_ref, v_ref, a_ref, b_ref, y_ref, s_ref, st):
        @pl.when(pl.program_id(1) == 0)
        def _():
            st[...] = jnp.zeros_like(st)

        s0 = st[...]
        ys, s1s = _scan_step(r_ref[...], lw_ref[...], k_ref[...], v_ref[...], a_ref[...], b_ref[...], s0)
        s_ref[0] = s0
        st[...] = jnp.stack(s1s)
        y_ref[...] = jnp.concatenate(ys, axis=1)

    return pl.pallas_call(
        body, name="wkv7_fwd",
        out_shape=[jax.ShapeDtypeStruct((L, W), F32), jax.ShapeDtypeStruct((nc, nh * ng, HEAD, HEAD), F32)],
        grid=(ng, nc), in_specs=[spec] * 6,
        out_specs=[spec, pl.BlockSpec((1, nh, HEAD, HEAD), lambda p, c: (c, p, 0, 0))],
        scratch_shapes=[pltpu.VMEM((nh, HEAD, HEAD), F32)],
        compiler_params=_params(("parallel", "arbitrary")),
    )(r, lw, k, v, a, b)


def _scan_bwd(r, lw, k, v, a, b, s_all, dy):
    L, W = r.shape
    nh = _scan_heads(W)
    nc, ng = L // CHUNK, W // (nh * HEAD)
    spec = pl.BlockSpec((CHUNK, nh * HEAD), lambda p, c: (nc - 1 - c, p))

    def body(r_ref, lw_ref, k_ref, v_ref, a_ref, b_ref, s_ref, dy_ref,
             dr_ref, dlw_ref, dk_ref, dv_ref, da_ref, db_ref, dst):
        @pl.when(pl.program_id(1) == 0)
        def _():
            dst[...] = jnp.zeros_like(dst)

        _, vjp = jax.vjp(_scan_step, r_ref[...], lw_ref[...], k_ref[...], v_ref[...], a_ref[...], b_ref[...],
                         s_ref[0])
        dys = tuple(dy_ref[:, h * HEAD:(h + 1) * HEAD] for h in range(nh))
        g = vjp((dys, tuple(dst[h] for h in range(nh))))
        for ref, val in zip((dr_ref, dlw_ref, dk_ref, dv_ref, da_ref, db_ref), g[:6]):
            ref[...] = val
        dst[...] = g[6]

    return pl.pallas_call(
        body, name="wkv7_bwd", out_shape=[jax.ShapeDtypeStruct((L, W), F32)] * 6,
        grid=(ng, nc),
        in_specs=[spec] * 6 + [pl.BlockSpec((1, nh, HEAD, HEAD), lambda p, c: (nc - 1 - c, p, 0, 0)), spec],
        out_specs=[spec] * 6,
        scratch_shapes=[pltpu.VMEM((nh, HEAD, HEAD), F32)],
        compiler_params=_params(("parallel", "arbitrary")),
    )(r, lw, k, v, a, b, s_all, dy)


def _cumsum_rows(logf):
    L = logf.shape[0]
    t = LANES

    def body(x_ref, o_ref, carry):
        @pl.when(pl.program_id(0) == 0)
        def _():
            carry[...] = jnp.zeros_like(carry)

        c = _mmx(_tri(t, False).astype(F32), x_ref[...]) + carry[...]
        carry[...] = c[t - 1:t, :]
        o_ref[...] = c.T

    return pl.pallas_call(
        body, name="fox_cumsum", out_shape=jax.ShapeDtypeStruct((LANES, L), F32), grid=(L // t,),
        in_specs=[pl.BlockSpec((t, LANES), lambda i: (i, 0))],
        out_specs=pl.BlockSpec((LANES, t), lambda i: (0, i)),
        scratch_shapes=[pltpu.VMEM((1, LANES), F32)], compiler_params=_params(("arbitrary",)),
    )(logf)


def _rcumsum_cols(dct):
    L = dct.shape[1]
    t = LANES
    n = L // t

    def body(x_ref, o_ref, carry):
        @pl.when(pl.program_id(0) == 0)
        def _():
            carry[...] = jnp.zeros_like(carry)

        rc = _doth(x_ref[...], _tri(t, False).astype(F32)) + carry[...]
        carry[...] = rc[:, 0:1]
        o_ref[...] = rc.T

    return pl.pallas_call(
        body, name="fox_rcumsum", out_shape=jax.ShapeDtypeStruct((L, LANES), F32), grid=(n,),
        in_specs=[pl.BlockSpec((LANES, t), lambda i: (0, n - 1 - i))],
        out_specs=pl.BlockSpec((t, LANES), lambda i: (n - 1 - i, 0)),
        scratch_shapes=[pltpu.VMEM((LANES, 1), F32)], compiler_params=_params(("arbitrary",)),
    )(dct)


def _attn_tile(L):
    return _pick(L, (384, 256, 128))


def _head_lanes(hh, shape):
    return lax.broadcasted_iota(jnp.int32, shape, len(shape) - 1) // HEAD == hh


def _own(hh, block, other=0):
    return jnp.where(_head_lanes(hh, block.shape), block, jnp.asarray(other, block.dtype))


def _attn_scores(q, k, ck, qi, kj, t):
    s = _dot_bnt(q, k) * ATTN_SCALE - ck
    qpos = qi * t + lax.broadcasted_iota(jnp.int32, (t, t), 0)
    kpos = kj * t + lax.broadcasted_iota(jnp.int32, (t, t), 1)
    mask = (kpos <= qpos) & (kpos >= PAD_ROWS)
    return jnp.where(mask, s, NEG), mask


def _dot_bnt(a, b):
    return lax.dot_general(_bf(a), _bf(b), (((1,), (1,)), ((), ())), preferred_element_type=F32)


def _dot_btn(a, b):
    return lax.dot_general(_bf(a), _bf(b), (((0,), (0,)), ((), ())), preferred_element_type=F32)


ATTN_HEADS = 2


def _attn_group(W):
    return ATTN_HEADS if W % (ATTN_HEADS * HEAD) == 0 else 2


def _ck_rows(ct_ref, p, g):
    r0 = (g * p) % 8
    return [ct_ref[pl.ds(r0 + hh, 1), :] for hh in range(g)]


def _carried(plan, refs, n_in, n_out, n_scratch):
    ci = len(plan["arrays"]) if plan else 0
    co = len(plan["out_shapes"]) if plan else 0
    a = n_in + ci
    b = a + n_out + co
    return (refs[:n_in], refs[n_in:a], refs[a:a + n_out], refs[a + n_out:b], refs[b:b + n_scratch],
            refs[b + n_scratch:])


def _attn_fwd(q, k, v, ct, plan=None):
    L, W = q.shape
    t = _attn_tile(L)
    g = _attn_group(W)
    gw = g * HEAD
    nt, npair = L // t, W // gw
    pairs = [(i, j) for i in range(nt) for j in range(i + 1)]
    it = jnp.asarray([i for i, _ in pairs], jnp.int32)
    jt = jnp.asarray([j for _, j in pairs], jnp.int32)
    ns = len(pairs)
    qspec = pl.BlockSpec((t, gw), lambda p, s, it, jt: (it[s], p))
    kspec = pl.BlockSpec((t, gw), lambda p, s, it, jt: (jt[s], p))
    cspec = pl.BlockSpec((8, t), lambda p, s, it, jt: (g * p // 8, jt[s]))

    def body(it_ref, jt_ref, *refs):
        (q_ref, k_ref, v_ref, ct_ref), p_in, (o_ref, lse_ref), p_out, (m_s, acc), sems = _carried(plan, refs, 4, 2, 2)
        p, s = pl.program_id(0), pl.program_id(1)
        i, j = it_ref[s], jt_ref[s]
        if plan:
            @pl.when((p == 0) & (s == 0))
            def _():
                plan["start"](p_in, p_out, *sems)

            if plan["mid"] is not None:
                @pl.when((p == npair // 2) & (s == 0))
                def _():
                    plan["mid"](p_in, p_out, *sems)

        @pl.when(j == 0)
        def _():
            m_s[...] = jnp.full_like(m_s, NEG)
            acc[...] = jnp.zeros_like(acc)

        def accumulate():
            cks = _ck_rows(ct_ref, p, g)
            qb, kb, vf = _bf(q_ref[...]), _bf(k_ref[...]), v_ref[...]
            ss = [_attn_scores(_own(hh, qb), kb, cks[hh], i, j, t)[0] for hh in range(g)]
            prs, alphas = [], []
            for hh in range(g):
                m_old = m_s[hh]
                m_new = jnp.maximum(m_old, jnp.max(ss[hh], axis=-1, keepdims=True))
                alphas.append(jnp.exp(m_old - m_new))
                prs.append(jnp.exp(ss[hh] - m_new))
                m_s[hh] = m_new
            pvs = []
            for hh in range(g):
                p_hi, p_lo = _split2(prs[hh])
                vx = _bf(_own(hh, vf, 1.0))
                pvs.append(jnp.dot(p_hi, vx, preferred_element_type=F32)
                           + jnp.dot(p_lo, vx, preferred_element_type=F32))
            for hh in range(g):
                acc[hh] = alphas[hh] * acc[hh] + pvs[hh]

        accumulate()

        @pl.when(j == i)
        def _():
            lane = lax.broadcasted_iota(jnp.int32, (t, LANES), 1)
            lse = jnp.zeros((t, LANES), F32)
            out = jnp.zeros((t, gw), F32)
            for hh in range(g):
                a = acc[hh]
                row_sum = pltpu.roll(a, HEAD, 1)
                out = jnp.where(_head_lanes(hh, a.shape), a / row_sum, out)
                nb = ((hh + 1) % g) * HEAD
                lse = jnp.where(lane == hh, m_s[hh] + jnp.log(a[:, nb:nb + 1]), lse)
            o_ref[...] = out
            lse_ref[0] = lse

        if plan:
            @pl.when((p == npair - 1) & (s == ns - 1))
            def _():
                plan["wait"](p_in, p_out, *sems)

    extra = plan["arrays"] if plan else []
    extra_out = plan["out_shapes"] if plan else []
    res = pl.pallas_call(
        body, name="fox_attn_fwd",
        out_shape=[jax.ShapeDtypeStruct((L, W), F32), jax.ShapeDtypeStruct((npair, L, LANES), F32)] + extra_out,
        grid_spec=pltpu.PrefetchScalarGridSpec(
            num_scalar_prefetch=2, grid=(npair, ns),
            in_specs=[qspec, kspec, kspec, cspec] + [ANY] * len(extra),
            out_specs=[qspec, pl.BlockSpec((1, t, LANES), lambda p, s, it, jt: (p, it[s], 0))]
            + [ANY] * len(extra_out),
            scratch_shapes=[pltpu.VMEM((g, t, 1), F32), pltpu.VMEM((g, t, gw), F32)]
            + (_dma_sems(plan["nsem"]) if plan else [])),
        compiler_params=_params(("arbitrary",) * 2 if plan else ("parallel", "arbitrary")),
    )(it, jt, q, k, v, ct, *extra)
    return res[0], res[1], list(res[2:])


def _attn_bwd(q, k, v, ct, o, lse, do, plan=None):
    L, W = q.shape
    t = _attn_tile(L)
    g = _attn_group(W)
    gw = g * HEAD
    nt, npair = L // t, W // gw
    pairs = [(i, j) for j in range(nt) for i in range(j, nt)]
    it = jnp.asarray([i for i, _ in pairs], jnp.int32)
    jt = jnp.asarray([j for _, j in pairs], jnp.int32)
    ns = len(pairs)
    kspec = pl.BlockSpec((t, gw), lambda p, s, it, jt: (jt[s], p))
    qspec = pl.BlockSpec((t, gw), lambda p, s, it, jt: (it[s], p))
    cspec = pl.BlockSpec((8, t), lambda p, s, it, jt: (g * p // 8, jt[s]))
    lspec = pl.BlockSpec((1, t, LANES), lambda p, s, it, jt: (p, it[s], 0))

    def body(it_ref, jt_ref, *refs):
        ((q_ref, k_ref, v_ref, ct_ref, o_ref, lse_ref, do_ref), p_in, (dk_ref, dv_ref, dc_ref, dq_ref), p_out,
         (dk_s, dv_s, dc_s), sems) = _carried(plan, refs, 7, 4, 3)
        p, s = pl.program_id(0), pl.program_id(1)
        i, j = it_ref[s], jt_ref[s]
        if plan:
            @pl.when((p == 0) & (s == 0))
            def _():
                plan["start"](p_in, p_out, *sems)

        @pl.when(i == j)
        def _():
            dk_s[...] = jnp.zeros_like(dk_s)
            dv_s[...] = jnp.zeros_like(dv_s)
            dc_s[...] = jnp.zeros_like(dc_s)

        def tile_dq():
            cks = _ck_rows(ct_ref, p, g)
            qb, kb, vb, dob = _bf(q_ref[...]), _bf(k_ref[...]), _bf(v_ref[...]), _bf(do_ref[...])
            of = o_ref[...]
            hs = range(g)
            qm = [_own(hh, qb) for hh in hs]
            dom = [_own(hh, dob) for hh in hs]
            sm = [_attn_scores(qm[hh], kb, cks[hh], i, j, t) for hh in hs]
            dps = [_dot_bnt(dom[hh], vb) for hh in hs]
            prs = [jnp.where(sm[hh][1], jnp.exp(sm[hh][0] - lse_ref[0, :, hh:hh + 1]), 0.0) for hh in hs]
            dss = [prs[hh] * (dps[hh] - jnp.sum(dom[hh].astype(F32) * of, axis=-1, keepdims=True)) for hh in hs]
            dv_s[...] += sum(_dot_btn(prs[hh], dom[hh]) for hh in hs)
            dk_s[...] += sum(_dot_btn(dss[hh], qm[hh]) for hh in hs)
            for hh in hs:
                dc_s[hh] += -jnp.sum(dss[hh], axis=0, keepdims=True)
            return sum(_dotb(dss[hh], _own(hh, kb)) for hh in hs) * ATTN_SCALE

        rows = pl.ds(pl.multiple_of(i * t, t), t)

        @pl.when(j == 0)
        def _():
            dq_ref[rows, :] = tile_dq()

        @pl.when(j > 0)
        def _():
            dq_ref[rows, :] += tile_dq()

        @pl.when(i == nt - 1)
        def _():
            row = lax.broadcasted_iota(jnp.int32, (8, t), 0)
            dc = jnp.zeros((8, t), F32)
            for hh in range(g):
                dc = jnp.where(row == hh, dc_s[hh], dc)
            dk_ref[...] = dk_s[...] * ATTN_SCALE
            dv_ref[...] = dv_s[...]
            dc_ref[0] = dc

        if plan:
            @pl.when((p == npair - 1) & (s == ns - 1))
            def _():
                plan["wait"](p_in, p_out, *sems)

    extra = plan["arrays"] if plan else []
    extra_out = plan["out_shapes"] if plan else []
    res = pl.pallas_call(
        body, name="fox_attn_bwd",
        out_shape=[jax.ShapeDtypeStruct((L, W), F32), jax.ShapeDtypeStruct((L, W), F32),
                   jax.ShapeDtypeStruct((npair, 8, L), F32), jax.ShapeDtypeStruct((L, W), F32)] + extra_out,
        grid_spec=pltpu.PrefetchScalarGridSpec(
            num_scalar_prefetch=2, grid=(npair, ns),
            in_specs=[qspec, kspec, kspec, cspec, qspec, lspec, qspec] + [ANY] * len(extra),
            out_specs=[kspec, kspec, pl.BlockSpec((1, 8, t), lambda p, s, it, jt: (p, 0, jt[s])),
                       pl.BlockSpec((L, gw), lambda p, s, it, jt: (0, p))] + [ANY] * len(extra_out),
            scratch_shapes=[pltpu.VMEM((t, gw), F32), pltpu.VMEM((t, gw), F32), pltpu.VMEM((g, 1, t), F32)]
            + (_dma_sems(plan["nsem"]) if plan else [])),
        compiler_params=_params(("arbitrary",) * 2 if plan else ("parallel", "arbitrary")),
    )(it, jt, q, k, v, ct, o, lse, do, *extra)
    return res[0], res[1], res[2], res[3], list(res[4:])


def _place():
    x, y, c = lax.axis_index("x"), lax.axis_index("y"), lax.axis_index("c")
    chips = [(1 - x, y), (x, 1 - y), (1 - x, 1 - y)]
    return x, y, c, chips


def _remote(src, dst, send_sems, recv_sems, k, to):
    return pltpu.make_async_remote_copy(src_ref=src, dst_ref=dst, send_sem=send_sems.at[k],
                                        recv_sem=recv_sems.at[k], device_id=to, device_id_type=MESH)


def _dma_sems(n):
    return [pltpu.SemaphoreType.DMA((n,)), pltpu.SemaphoreType.DMA((n,))]


def _all_gather(shards, modes):
    return _run_exchange("gather_weights", _gather_plan(shards, modes))


def _gather_plan(shards, modes):
    n = len(shards)

    def out_shape(s, mode):
        r, c = s.shape
        return {"stack": (4, r, c), "cols": (r, 4 * c), "rows": (4 * r, c)}[mode]

    def window(outs, i, chip, cc):
        r, cw = shards[i].shape
        h = r // 2
        if modes[i] == "stack":
            return outs[i].at[chip, pl.ds(cc * h, h), :]
        if modes[i] == "cols":
            return outs[i].at[pl.ds(cc * h, h), pl.ds(pl.multiple_of(chip * cw, LANES), cw)]
        return outs[i].at[pl.ds(pl.multiple_of(chip * r + cc * h, 8), h), :]

    def first(ins, outs, ss, rs):
        x, y, c, chips = _place()
        cps = []
        for i in range(n):
            h = shards[i].shape[0] // 2
            for k, (cx, cy) in enumerate(chips):
                cps.append(_remote(ins[i].at[pl.ds(c * h, h), :], window(outs, i, 2 * x + y, c), ss, rs,
                                   6 * i + k, (cx, cy, c)))
        return cps

    def passed(outs, ss, rs):
        x, y, c, chips = _place()
        cps = []
        for k, (cx, cy) in enumerate(chips):
            for i in range(n):
                landed = window(outs, i, 2 * cx + cy, c)
                cps.append((_remote(landed, landed, ss, rs, 6 * i + k, (x, y, 1 - c)),
                            _remote(landed, landed, ss, rs, 6 * i + 3 + k, (x, y, 1 - c))))
        return cps

    def start(ins, outs, ss, rs):
        for cp in first(ins, outs, ss, rs):
            cp.start()

    def mid(ins, outs, ss, rs):
        for arrival, fwd in passed(outs, ss, rs):
            arrival.wait_recv()
            fwd.start()

    def wait(ins, outs, ss, rs):
        x, y, c, chips = _place()
        for k, (cx, cy) in enumerate(chips):
            for i in range(n):
                other = window(outs, i, 2 * cx + cy, 1 - c)
                _remote(other, other, ss, rs, 6 * i + 3 + k, (x, y, 1 - c)).wait_recv()
        for cp in first(ins, outs, ss, rs) + [fwd for _, fwd in passed(outs, ss, rs)]:
            cp.wait_send()

    return dict(arrays=list(shards), nsem=6 * n, start=start, mid=mid, wait=wait,
                out_shapes=[jax.ShapeDtypeStruct(out_shape(s, m), s.dtype) for s, m in zip(shards, modes)])


def _run_exchange(name, plan):
    n = len(plan["arrays"])

    def body(*refs):
        ins, outs, (ss, rs) = refs[:n], refs[n:n + len(plan["out_shapes"])], refs[n + len(plan["out_shapes"]):]
        plan["start"](ins, outs, ss, rs)
        if plan["mid"] is not None:
            plan["mid"](ins, outs, ss, rs)
        plan["wait"](ins, outs, ss, rs)

    return pl.pallas_call(
        body, name=name, out_shape=plan["out_shapes"], in_specs=[ANY] * n,
        out_specs=[ANY] * len(plan["out_shapes"]), scratch_shapes=_dma_sems(plan["nsem"]),
    )(*plan["arrays"])


def _pair_exchange(name, blocks):
    return _run_exchange(name, _pair_exchange_plan(blocks))


def _pair_exchange_plan(blocks):
    n = len(blocks)

    def copies(ins, outs, ss, rs):
        x, y, c, _ = _place()
        cps = []
        for i in range(n):
            h = blocks[i].shape[1] // 2
            cps.append(_remote(ins[i].at[:, pl.ds((1 - c) * h, h), :], outs[i], ss, rs, i, (x, y, 1 - c)))
        return cps

    def start(ins, outs, ss, rs):
        for cp in copies(ins, outs, ss, rs):
            cp.start()

    def wait(ins, outs, ss, rs):
        for cp in copies(ins, outs, ss, rs):
            cp.wait()

    return dict(arrays=list(blocks), nsem=n, start=start, mid=None, wait=wait,
                out_shapes=[jax.ShapeDtypeStruct((4, b.shape[1] // 2, b.shape[2]), b.dtype) for b in blocks])


def _chip_exchange(parts):
    return _run_exchange("reduce_chip_exchange", _chip_exchange_plan(parts))


def _chip_exchange_plan(parts):
    n = len(parts)

    def sends(ins, outs, ss, rs):
        x, y, c, chips = _place()
        return [_remote(ins[i].at[2 * cx + cy], outs[i].at[2 * x + y], ss, rs, 3 * i + k, (cx, cy, c))
                for i in range(n) for k, (cx, cy) in enumerate(chips)]

    def start(ins, outs, ss, rs):
        for cp in sends(ins, outs, ss, rs):
            cp.start()

    def wait(ins, outs, ss, rs):
        x, y, c, chips = _place()
        for i in range(n):
            for k, (cx, cy) in enumerate(chips):
                slot = outs[i].at[2 * cx + cy]
                _remote(slot, slot, ss, rs, 3 * i + k, (cx, cy, c)).wait_recv()
        for cp in sends(ins, outs, ss, rs):
            cp.wait_send()

    return dict(arrays=list(parts), nsem=3 * n, start=start, mid=None, wait=wait,
                out_shapes=[jax.ShapeDtypeStruct(p.shape, p.dtype) for p in parts])


def _pair_join(tots):
    n = len(tots)

    def body(*refs):
        ins, outs, (send_sems, recv_sems) = refs[:n], refs[n:2 * n], refs[2 * n:]
        x, y, c, _ = _place()
        cps = [_remote(ins[i], outs[i], send_sems, recv_sems, i, (x, y, 1 - c)) for i in range(n)]
        for cp in cps:
            cp.start()
        for cp in cps:
            cp.wait()

    return pl.pallas_call(
        body, name="reduce_pair_join", out_shape=[jax.ShapeDtypeStruct(t.shape, t.dtype) for t in tots],
        in_specs=[ANY] * n, out_specs=[ANY] * n, scratch_shapes=_dma_sems(n),
    )(*tots)


def _add_tile(h, cw):
    cap = max(8, (512 * 1024) // max(cw, 1))
    return _pick(h, tuple(t for t in (1024, 512, 256, 128, 64, 32, 16, 8) if t <= cap))


def _pair_add(name, block, recv):
    n, r, cw = block.shape
    h = r // 2
    tr = _add_tile(h, cw)
    c = lax.axis_index("c").astype(jnp.int32).reshape((1,))

    def body(c_ref, a_ref, b_ref, o_ref):
        o_ref[...] = (a_ref[...] + b_ref[...]).astype(o_ref.dtype)

    return pl.pallas_call(
        body, name=name, out_shape=jax.ShapeDtypeStruct((n, h, cw), BF16),
        grid_spec=pltpu.PrefetchScalarGridSpec(
            num_scalar_prefetch=1, grid=(n, h // tr),
            in_specs=[pl.BlockSpec((1, tr, cw), lambda a, i, cr: (a, cr[0] * (h // tr) + i, 0)),
                      pl.BlockSpec((1, tr, cw), lambda a, i, cr: (a, i, 0))],
            out_specs=pl.BlockSpec((1, tr, cw), lambda a, i, cr: (a, i, 0))),
        compiler_params=_params(("parallel", "parallel")),
    )(c, block, recv)


def _chip_add(name, parts):
    n, h, cw = parts.shape
    tr = _add_tile(h, cw)

    def body(a_ref, o_ref):
        f = lambda i: a_ref[i].astype(F32)
        o_ref[...] = ((f(0) + f(1)) + f(2)) + f(3)

    return pl.pallas_call(
        body, name=name, out_shape=jax.ShapeDtypeStruct((h, cw), F32), grid=(h // tr,),
        in_specs=[pl.BlockSpec((n, tr, cw), lambda i: (0, i, 0))],
        out_specs=pl.BlockSpec((tr, cw), lambda i: (i, 0)),
        compiler_params=_params(("parallel",)),
    )(parts)


def _adamw(name, w, g, m, v):
    R, C = w.shape
    tr = _pick(R, (128, 64, 32, 16, 8))
    spec = pl.BlockSpec((tr, C), lambda i: (i, 0))

    def body(w_ref, g_ref, m_ref, v_ref, d_ref, mo_ref, vo_ref):
        gr = g_ref[...]
        mn = ADAM_B1 * m_ref[...] + (1.0 - ADAM_B1) * gr
        vn = ADAM_B2 * v_ref[...] + (1.0 - ADAM_B2) * jnp.square(gr)
        m_hat = mn / (1.0 - ADAM_B1 ** ADAM_STEP)
        v_hat = vn / (1.0 - ADAM_B2 ** ADAM_STEP)
        d_ref[...] = -ADAM_LR * (m_hat / (jnp.sqrt(v_hat) + ADAM_EPS) + ADAM_WD * w_ref[...])
        mo_ref[...] = mn
        vo_ref[...] = vn

    return pl.pallas_call(
        body, name=name, out_shape=[jax.ShapeDtypeStruct((R, C), F32)] * 3, grid=(R // tr,),
        in_specs=[spec] * 4, out_specs=[spec] * 3, compiler_params=_params(("parallel",)),
    )(w, g, m, v)


def _pack(parts, dtype, row_mult):
    flat = jnp.concatenate([p.reshape(-1).astype(dtype) for p in parts])
    unit = row_mult * PACK_W
    pad = (-flat.shape[0]) % unit
    if pad:
        flat = jnp.concatenate([flat, jnp.zeros((pad,), dtype)])
    return flat.reshape(-1, PACK_W)


def _unpack(flat, shapes):
    out, off = [], 0
    for s in shapes:
        n = int(np.prod(s))
        out.append(flat[off:off + n].reshape(s))
        off += n
    return out


SHARDED = ("w_in", "rwkv_w2", "rwkv_a2", "rwkv_g2", "w_branch_a", "w_branch_b", "w_o", "w_gate_up", "w_down",
           "meta_tokens")
SHARD_AXIS = {"w_in": 1, "rwkv_w2": 1, "rwkv_a2": 1, "rwkv_g2": 1, "w_branch_a": 1, "w_branch_b": 1, "w_o": 0,
              "w_gate_up": 1, "w_down": 0, "meta_tokens": 1}
GATHER_MODE = {"w_in": "stack", "rwkv_w2": "stack", "rwkv_a2": "stack", "rwkv_g2": "stack", "w_branch_a": "stack",
               "w_branch_b": "stack", "w_o": "rows", "w_gate_up": "stack", "w_down": "rows", "meta_tokens": "stack"}
LATE_GATHER = ("w_branch_a", "w_branch_b", "w_o", "w_gate_up", "w_down")
SMALL = ("norm1_g", "rwkv_mu", "rwkv_w0", "rwkv_a0", "rwkv_k_k", "rwkv_k_a", "rwkv_r_k", "rwkv_gn_w",
         "rwkv_gn_b", "fox_q_norm_g", "fox_k_norm_g", "fox_f_bias", "norm2_g")
WEIGHTS = ("meta_tokens", "norm1_g", "w_in", "rwkv_mu", "rwkv_w0", "rwkv_w2", "rwkv_a0", "rwkv_a2", "rwkv_g2",
           "rwkv_k_k", "rwkv_k_a", "rwkv_r_k", "rwkv_gn_w", "rwkv_gn_b", "fox_q_norm_g", "fox_k_norm_g",
           "fox_f_bias", "w_branch_a", "w_branch_b", "w_o", "norm2_g", "w_gate_up", "w_down")


def _pad_rows(t, rows):
    return jnp.concatenate([t, jnp.zeros((rows - t.shape[0],) + t.shape[1:], t.dtype)], axis=0)


def _pad_cols(t, cols):
    return jnp.concatenate([t, jnp.zeros(t.shape[:-1] + (cols - t.shape[-1],), t.dtype)], axis=-1)


def _step(x, tgt, wts, mom1, mom2):
    seq, D = x.shape
    L = SEQ_ROW0 + seq
    RW = wts["rwkv_w0"].shape[-1]
    DL, AL, GL = wts["rwkv_w2"].shape[0], wts["rwkv_a2"].shape[0], wts["rwkv_g2"].shape[0]
    FW = wts["w_branch_b"].shape[0]
    FH = wts["fox_f_bias"].shape[-1]
    DFF = wts["w_down"].shape[0] * 4
    LORA = DL + AL + GL
    LW = -(-(LORA + FH) // 512) * 512
    assert RW == FW and (6 * RW) % D == 0 and (6 * RW + 2 * D) % LW == 0 and LORA % 8 == 0
    xj = lax.axis_index("x")
    yj = lax.axis_index("y")
    chip = 2 * xj + yj

    send = {n: wts[n] if n == "meta_tokens" else wts[n].astype(BF16) for n in SHARDED}
    early = tuple(n for n in SHARDED if n not in LATE_GATHER)
    full, stacked = {}, {}

    def place_own(n, got):
        shard, mode = send[n], GATHER_MODE[n]
        r, cw = shard.shape
        if mode == "stack":
            got = lax.dynamic_update_index_in_dim(got, shard, chip, 0)
            stacked[n] = got
            full[n] = jnp.concatenate([got[j] for j in range(4)], axis=1)
        elif mode == "cols":
            full[n] = lax.dynamic_update_slice(got, shard, (0, chip * cw))
        else:
            full[n] = lax.dynamic_update_slice(got, shard, (chip * r, 0))

    for n, got in zip(early, _all_gather([send[n] for n in early], [GATHER_MODE[n] for n in early])):
        place_own(n, got)
    late_plan = _gather_plan([send[n] for n in LATE_GATHER], [GATHER_MODE[n] for n in LATE_GATHER])
    meta = full["meta_tokens"]
    w_in_blocks = stacked["w_in"]
    blk_w = w_in_blocks.shape[2]

    def ref_cols(lo, hi):
        out = []
        for j in range(4):
            a, b = max(lo, j * blk_w), min(hi, (j + 1) * blk_w)
            if a < b:
                out.append(w_in_blocks[j][:, a - j * blk_w:b - j * blk_w])
        return out

    o = 0
    segs = {}
    for nm, wd in (("r", RW), ("k", RW), ("v", RW), ("wd", DL), ("ad", AL), ("gd", GL),
                   ("fq", FW), ("fk", FW), ("fv", FW), ("ff", FH), ("ga", D), ("gb", D)):
        segs[nm] = ref_cols(o, o + wd)
        o += wd
    order = ("r", "k", "v", "fq", "fk", "fv", "ga", "gb", "wd", "ad", "gd", "ff")
    w1 = jnp.concatenate([c for nm in order for c in segs[nm]]
                         + [jnp.zeros((D, LW - LORA - FH), BF16)], axis=1)
    cb_f = 3
    cb_gate = (6 * RW) // D
    cb_lora = (6 * RW + 2 * D) // LW

    e_m, et_m = _head_mats(RW)
    ft_m = _fold_mat(RW)
    mu = wts["rwkv_mu"]
    mu_rkv = mu[:, :3 * RW]
    mu_l = _pad_cols(mu[:, 3 * RW:], LW)
    w2p = _pad_rows(full["rwkv_w2"].astype(F32), LW)
    a2p = _pad_rows(jnp.concatenate([jnp.zeros((DL, RW), F32), full["rwkv_a2"].astype(F32)], axis=0), LW)
    g2p = _pad_rows(jnp.concatenate([jnp.zeros((DL + AL, RW), F32), full["rwkv_g2"].astype(F32)], axis=0), LW)
    r_k = wts["rwkv_r_k"].reshape(1, RW)
    qg8 = jnp.broadcast_to(wts["fox_q_norm_g"], (8, HEAD))
    kg8 = jnp.broadcast_to(wts["fox_k_norm_g"], (8, HEAD))
    fb = _pad_cols(wts["fox_f_bias"], LANES)
    fmask = (jnp.arange(LANES) < FH).astype(F32).reshape(1, LANES)
    lmask = ((jnp.arange(LW) >= LORA) & (jnp.arange(LW) < LORA + FH)).astype(F32).reshape(1, LW)

    h0 = jnp.concatenate([jnp.zeros((PAD_ROWS, D), F32), meta, x], axis=0)
    n1 = wts["norm1_g"]

    wide = _pick(L, (3 * ROW_TILE, ROW_TILE))
    (xn,), _ = _rowcall("rms1_fwd", lambda i, r, b: ([_rms_f(r[0], b[0])], []), L,
                        [(h0, D, 0, "row")], [n1], [(D, BF16)], [], tm=wide)
    proj = _matmul(xn, w1, "nn", F32, "proj_fwd")

    def shift_fn(i, r, b):
        rows = lax.broadcasted_iota(jnp.int32, (ROW_TILE, 1), 0)
        outs = []
        for z, halo, m_ in ((r[0], r[1], b[0]), (r[2], r[3], b[1])):
            first = jnp.where(i == 0, 0.0, halo[7:8, :])
            zp = jnp.where(rows == 0, first, pltpu.roll(z, 1, 0))
            outs.append(z + (zp - z) * m_)
        return outs, []

    rkv_w = 3 * RW
    (x_rkv, x_l), _ = _rowcall(
        "shift_fwd", shift_fn, L,
        [(proj, rkv_w, 0, "row"), (proj, rkv_w, 0, "prev"), (proj, LW, cb_lora, "row"), (proj, LW, cb_lora, "prev")],
        [mu_rkv, mu_l], [(rkv_w, F32), (LW, F32)], [])

    prep_p = [wts["rwkv_w0"], w2p, wts["rwkv_a0"], a2p, g2p, wts["rwkv_k_k"], wts["rwkv_k_a"], e_m, et_m]
    prep_rows = [(x_rkv, RW, 0, "row"), (x_rkv, RW, 1, "row"), (x_rkv, RW, 2, "row"), (x_l, LW, 0, "row")]
    (s_r, s_lw, s_k, s_v, s_a, s_b, gate_g), _ = _rowcall(
        "rwkv_prep_fwd", lambda i, r, b: (list(_prep_f(*r, *b)), []), L, prep_rows, prep_p,
        [(RW, F32)] * 7, [])
    y_scan, s_all = _scan_fwd(s_r, s_lw, s_k, s_v, s_a, s_b)
    post_p = [wts["rwkv_gn_w"], wts["rwkv_gn_b"], r_k, e_m, et_m]
    post_rows = [(y_scan, RW, 0, "row"), (s_r, RW, 0, "row"), (s_k, RW, 0, "row"), (s_v, RW, 0, "row"),
                 (gate_g, RW, 0, "row")]
    (y_a,), _ = _rowcall("rwkv_post_fwd", lambda i, r, b: ([_post_f(*r, *b)], []), L, post_rows, post_p,
                         [(RW, BF16)], [])

    fox_p = [qg8, kg8, fb, e_m, et_m, ft_m, fmask]

    def foxprep_fn(i, r, b):
        fl = _doth(r[2] * b[-1], b[-2])
        return list(_foxprep_f(r[0], r[1], fl, *b[:-2])), []

    sel = (np.arange(LW)[:, None] - LORA == np.arange(LANES)[None, :]).astype(np.float32)
    sel = jnp.asarray(sel)
    fox_rows = [(proj, FW, cb_f, "row"), (proj, FW, cb_f + 1, "row"), (proj, LW, cb_lora, "row")]
    (f_q, f_k, logf), _ = _rowcall("fox_prep_fwd", foxprep_fn, L, fox_rows, fox_p + [sel, lmask],
                                   [(FW, BF16), (FW, BF16), (LANES, F32)], [])
    ct = _cumsum_rows(logf)
    f_v = proj[:, (cb_f + 2) * FW:(cb_f + 3) * FW]
    y_b32, lse, got_late = _attn_fwd(f_q, f_k, f_v, ct, plan=late_plan)
    for n, got in zip(LATE_GATHER, got_late):
        place_own(n, got)
    y_b = y_b32.astype(BF16)

    p_a = _matmul(y_a, full["w_branch_a"], "nn", F32, "branch_a_fwd")
    p_b = _matmul(y_b, full["w_branch_b"], "nn", F32, "branch_b_fwd")
    merge_rows = [(proj, D, cb_gate, "row"), (proj, D, cb_gate + 1, "row"), (p_a, D, 0, "row"), (p_b, D, 0, "row")]
    (merged,), _ = _rowcall("merge_fwd", lambda i, r, b: ([_merge_f(*r)], []), L, merge_rows, [], [(D, BF16)], [],
                            tm=wide)
    h1 = _matmul(merged, full["w_o"], "nn", F32, "wo_fwd", add=h0)
    n2 = wts["norm2_g"]
    (xn2,), _ = _rowcall("rms2_fwd", lambda i, r, b: ([_rms_f(r[0], b[0])], []), L,
                         [(h1, D, 0, "row")], [n2], [(D, BF16)], [], tm=wide)
    gu = _matmul(xn2, full["w_gate_up"], "nn", F32, "gate_up_fwd")
    gu_rows = [(gu, DFF, 0, "row"), (gu, DFF, 1, "row")]
    (act,), _ = _rowcall("swiglu_fwd", lambda i, r, b: ([_swiglu_f(*r)], []), L, gu_rows, [], [(DFF, BF16)], [])
    h2 = _matmul(act, full["w_down"], "nn", F32, "down_fwd", add=h1)

    def loss_fn(i, r, b):
        err = jnp.where(i == 0, 0.0, r[0] - r[1])
        return [err * (1.0 / D)], [jnp.zeros((8, LANES), F32) + 0.5 / D * jnp.sum(err * err)]

    (dh2,), (loss_acc,) = _rowcall("loss", loss_fn, L, [(h2, D, 0, "row"), (tgt, D, 0, "lag")], [],
                                   [(D, F32)], [(8, LANES)])
    loss = lax.psum(loss_acc[0, 0], ("x", "y", "c"))

    dh2b = dh2.astype(BF16)
    g_w_down = _matmul(act, dh2b, "tn", F32, "down_dw")
    d_act = _matmul(dh2b, full["w_down"], "nt", F32, "down_dx")

    def swiglu_bwd(i, r, b):
        _, vjp = jax.vjp(_swiglu_f, r[0], r[1])
        return list(vjp(r[2])), []

    (d_gate, d_up), _ = _rowcall("swiglu_bwd", swiglu_bwd, L, gu_rows + [(d_act, DFF, 0, "row")], [],
                                 [(DFF, BF16), (DFF, BF16)], [])
    d_gu = jnp.concatenate([d_gate, d_up], axis=1)
    g_w_gu = _matmul(xn2, d_gu, "tn", F32, "gate_up_dw", col_blocks=4)
    ffn_blocks = [g_w_gu, g_w_down.reshape(4, -1, D)]
    d_xn2, ffn_recv = _matmul(d_gu, full["w_gate_up"], "nt", F32, "gate_up_dx", plan=_pair_exchange_plan(ffn_blocks))

    def rms_bwd(i, r, b):
        _, vjp = jax.vjp(_rms_f, r[0], b[0])
        dh, dg = vjp(r[1])
        return [dh + r[2]], [dg]

    (dh1,), (g_n2,) = _rowcall("rms2_bwd", rms_bwd, L,
                               [(h1, D, 0, "row"), (d_xn2, D, 0, "row"), (dh2, D, 0, "row")], [n2],
                               [(D, F32)], [(1, D)], tm=wide)
    dh1b = dh1.astype(BF16)
    g_w_o = _matmul(merged, dh1b, "tn", F32, "wo_dw")
    d_merged = _matmul(dh1b, full["w_o"], "nt", F32, "wo_dx")

    def merge_bwd(i, r, b):
        _, vjp = jax.vjp(_merge_f, *r[:4])
        return list(vjp(r[4])), []

    (d_za, d_zb, d_pa, d_pb), _ = _rowcall("merge_bwd", merge_bwd, L, merge_rows + [(d_merged, D, 0, "row")], [],
                                           [(D, BF16)] * 4, [])
    g_w_a = _matmul(y_a, d_pa, "tn", F32, "branch_a_dw", col_blocks=4)
    g_w_b = _matmul(y_b, d_pb, "tn", F32, "branch_b_dw", col_blocks=4)
    d_ya = _matmul(d_pa, full["w_branch_a"], "nt", F32, "branch_a_dx")
    d_yb = _matmul(d_pb, full["w_branch_b"], "nt", F32, "branch_b_dx")

    cj = lax.axis_index("c")
    names1 = ("w_gate_up", "w_down", "w_branch_a", "w_branch_b", "w_o")
    blocks1 = [g_w_a, g_w_b, g_w_o.reshape(4, -1, D)]
    parts1 = [_pair_add("reduce_pair_add_" + n, b, r)
              for n, b, r in zip(names1, ffn_blocks + blocks1,
                                 list(ffn_recv) + list(_pair_exchange("reduce_pair_exchange_1", blocks1)))]
    d_fk, d_fv, dc_rows, d_fq, recv1 = _attn_bwd(f_q, f_k, f_v, ct, y_b32, lse, d_yb,
                                                 plan=_chip_exchange_plan(parts1))
    dct = _pad_rows(dc_rows[:, :_attn_group(FW), :].reshape(-1, L), LANES)
    d_logf = _rcumsum_cols(dct)

    def foxprep_bwd(i, r, b):
        def f(q, k, xl, qg, kg, fbias):
            return _foxprep_f(q, k, _doth(xl * b[-1], b[-2]), qg, kg, fbias, *b[3:7])
        _, vjp = jax.vjp(f, r[0], r[1], r[2], b[0], b[1], b[2])
        dq, dk, dxl, dqg, dkg, dfb = vjp((r[3], r[4], r[5]))
        return [dq, dk, dxl], [dqg, dkg, dfb]

    (d_zfq, d_zfk, d_zl_f), (g_qg8, g_kg8, g_fb) = _rowcall(
        "fox_prep_bwd", foxprep_bwd, L,
        fox_rows + [(d_fq, FW, 0, "row"), (d_fk, FW, 0, "row"), (d_logf, LANES, 0, "row")],
        fox_p + [sel, lmask], [(FW, BF16), (FW, BF16), (LW, F32)], [(8, HEAD), (8, HEAD), (1, LANES)])

    def post_bwd(i, r, b):
        _, vjp = jax.vjp(lambda *a: _post_f(*a, b[3], b[4]), *r[:5], b[0], b[1], b[2])
        g = vjp(r[5])
        return list(g[:5]), list(g[5:])

    (d_y, d_r1, d_k1, d_v1, d_g), (g_gn_w, g_gn_b, g_r_k) = _rowcall(
        "rwkv_post_bwd", post_bwd, L, post_rows + [(d_ya, RW, 0, "row")], post_p,
        [(RW, F32)] * 5, [(1, RW)] * 3)
    d_r2, d_lw, d_k2, d_v2, d_a, d_b = _scan_bwd(s_r, s_lw, s_k, s_v, s_a, s_b, s_all, d_y)

    def prep_bwd(i, r, b):
        _, vjp = jax.vjp(lambda *a: _prep_f(*a, b[7], b[8]), *r[:4], *b[:7])
        cts = (r[4] + r[10], r[5], r[6] + r[11], r[7] + r[12], r[8], r[9], r[13])
        g = vjp(cts)
        return list(g[:4]), list(g[4:])

    bwd_rows = prep_rows + [(d_r2, RW, 0, "row"), (d_lw, RW, 0, "row"), (d_k2, RW, 0, "row"), (d_v2, RW, 0, "row"),
                            (d_a, RW, 0, "row"), (d_b, RW, 0, "row"), (d_r1, RW, 0, "row"), (d_k1, RW, 0, "row"),
                            (d_v1, RW, 0, "row"), (d_g, RW, 0, "row")]
    (d_xr, d_xk, d_xv, d_xl), (g_w0, g_w2p, g_a0, g_a2p, g_g2p, g_kk, g_ka) = _rowcall(
        "rwkv_prep_bwd", prep_bwd, L, bwd_rows, prep_p, [(RW, F32)] * 3 + [(LW, F32)],
        [(1, RW), (LW, RW), (1, RW), (LW, RW), (LW, RW), (1, RW), (1, RW)])

    def shift_bwd(i, r, b):
        last = pl.num_programs(0) - 1
        rows = lax.broadcasted_iota(jnp.int32, (ROW_TILE, 1), 0)
        outs, sums = [], []
        groups = ((r[0], r[1], r[2], r[3], b[0], None), (r[4], r[5], r[6], r[7], b[1], r[8]))
        for d, dnext, z, zhalo, m_, extra in groups:
            nxt = jnp.where(i == last, 0.0, dnext[0:1, :])
            d_up = jnp.where(rows == ROW_TILE - 1, nxt, pltpu.roll(d, ROW_TILE - 1, 0))
            dz = d * (1.0 - m_) + d_up * m_
            if extra is not None:
                dz = dz + extra
            first = jnp.where(i == 0, 0.0, zhalo[7:8, :])
            zp = jnp.where(rows == 0, first, pltpu.roll(z, 1, 0))
            outs.append(dz)
            sums.append(_rowsum(d * (zp - z)))
        return outs, sums

    d_xrkv = jnp.concatenate([d_xr, d_xk, d_xv], axis=1)
    (d_zrkv, d_zl), (g_mu_rkv, g_mu_l) = _rowcall(
        "shift_bwd", shift_bwd, L,
        [(d_xrkv, rkv_w, 0, "row"), (d_xrkv, rkv_w, 0, "next"), (proj, rkv_w, 0, "row"), (proj, rkv_w, 0, "prev"),
         (d_xl, LW, 0, "row"), (d_xl, LW, 0, "next"), (proj, LW, cb_lora, "row"), (proj, LW, cb_lora, "prev"),
         (d_zl_f, LW, 0, "row")],
        [mu_rkv, mu_l], [(rkv_w, BF16), (LW, BF16)], [(1, rkv_w), (1, LW)])

    n_in = stacked["w_in"].shape[2] * 4
    cs = n_in // 4
    cp = -(-cs // LANES) * LANES
    pieces = [d_zrkv, d_zl[:, :LORA], d_zfq, d_zfk, d_fv.astype(BF16), d_zl[:, LORA:LORA + FH], d_za, d_zb]
    cuts, off = [], 0
    for j in range(4):
        lo, hi = j * cs, (j + 1) * cs
        off = 0
        for pc in pieces:
            a, b = max(lo, off), min(hi, off + pc.shape[1])
            if a < b:
                cuts.append(pc[:, a - off:b - off])
            off += pc.shape[1]
        if cp > cs:
            cuts.append(jnp.zeros((L, cp - cs), BF16))
    d_blk = jnp.concatenate(cuts, axis=1)
    w_blk = jnp.concatenate([_pad_cols(stacked["w_in"][j], cp) for j in range(4)], axis=1)
    g_w_in = _matmul(xn, d_blk, "tn", F32, "proj_dw", col_blocks=4)
    part_w_in = _pair_add("reduce_pair_add_w_in", g_w_in, _pair_exchange("reduce_pair_exchange_2", [g_w_in])[0])
    d_xn, recv_w_in = _matmul(d_blk, w_blk, "nt", F32, "proj_dx", plan=_chip_exchange_plan([part_w_in]))
    (dh0,), (g_n1,) = _rowcall("rms1_bwd", rms_bwd, L,
                               [(h0, D, 0, "row"), (d_xn, D, 0, "row"), (dh1, D, 0, "row")], [n1],
                               [(D, F32)], [(1, D)], tm=wide)
    grad_x = dh0[SEQ_ROW0:]
    g_meta = dh0[PAD_ROWS:SEQ_ROW0]

    tiny = {"rwkv_w2": g_w2p[:DL], "rwkv_a2": g_a2p[DL:DL + AL], "rwkv_g2": g_g2p[DL + AL:LORA],
            "meta_tokens": g_meta}
    g_mu = jnp.concatenate([g_mu_rkv, g_mu_l[:, :LORA]], axis=1)
    gsmall = {
        "norm1_g": g_n1, "rwkv_mu": g_mu, "rwkv_w0": g_w0, "rwkv_a0": g_a0, "rwkv_k_k": g_kk, "rwkv_k_a": g_ka,
        "rwkv_r_k": g_r_k.reshape(wts["rwkv_r_k"].shape), "rwkv_gn_w": g_gn_w, "rwkv_gn_b": g_gn_b,
        "fox_q_norm_g": g_qg8[0:1], "fox_k_norm_g": g_kg8[0:1], "fox_f_bias": g_fb[:, :FH], "norm2_g": g_n2,
    }
    small_flat = jnp.concatenate([gsmall[n].reshape(-1) for n in SMALL])

    tiny_names = tuple(tiny)

    def tiny_block(j):
        parts = []
        for n in tiny_names:
            w = tiny[n].shape[1] // 4
            parts.append(tiny[n][:, j * w:(j + 1) * w])
        return _pack(parts + [small_flat], F32, 32)

    small_blocks = jnp.stack([tiny_block(j) for j in range(4)])
    part_small = _pair_add("reduce_pair_add_small", small_blocks,
                           _pair_exchange("reduce_pair_exchange_3", [small_blocks])[0])
    recv_small = _chip_exchange([part_small])
    names = names1 + ("w_in", "small")
    tots = []
    for n, p, r in zip(names, parts1 + [part_w_in, part_small], list(recv1) + list(recv_w_in) + list(recv_small)):
        own = lax.dynamic_index_in_dim(p, chip, 0, keepdims=False)
        tots.append(_chip_add("reduce_chip_add_" + n, lax.dynamic_update_index_in_dim(r, own, chip, 0)))
    others = _pair_join(tots)
    red = [jnp.where(cj == 0, jnp.concatenate([t, o_], axis=0), jnp.concatenate([o_, t], axis=0))
           for t, o_ in zip(tots, others)]
    grads = {n: red[i] for i, n in enumerate(names[:-1])}
    grads["w_in"] = grads["w_in"][:, :cs]
    tiny_shapes = [wts[n].shape for n in tiny_names]
    got = _unpack(red[-1].reshape(-1), tiny_shapes + [small_flat.shape])
    for n, t in zip(tiny_names, got):
        grads[n] = t
    for n, t in zip(SMALL, _unpack(got[-1], [wts[n].shape for n in SMALL])):
        grads[n] = t

    delta, new_m, new_v = {}, {}, {}
    for n in SHARDED:
        delta[n], new_m[n], new_v[n] = _adamw("adamw_" + n, wts[n], grads[n], mom1[n], mom2[n])
    pk = lambda d: _pack([d[n] for n in SMALL], F32, 8)
    ds, ms, vs = _adamw("adamw_small", pk(wts), pk(grads), pk(mom1), pk(mom2))
    small_shapes = [wts[n].shape for n in SMALL]
    for dst, src in ((delta, ds), (new_m, ms), (new_v, vs)):
        for n, t in zip(SMALL, _unpack(src.reshape(-1), small_shapes)):
            dst[n] = t
    return loss, grad_x, grads, delta, new_m, new_v


def kernel(x, meta_tokens, norm1_g, w_in, rwkv_mu, rwkv_w0, rwkv_w2, rwkv_a0, rwkv_a2, rwkv_g2, rwkv_k_k, rwkv_k_a, rwkv_r_k, rwkv_gn_w, rwkv_gn_b, fox_q_norm_g, fox_k_norm_g, fox_f_bias, w_branch_a, w_branch_b, w_o, norm2_g, w_gate_up, w_down, loss_target, m_meta_tokens, m_norm1_g, m_w_in, m_rwkv_mu, m_rwkv_w0, m_rwkv_w2, m_rwkv_a0, m_rwkv_a2, m_rwkv_g2, m_rwkv_k_k, m_rwkv_k_a, m_rwkv_r_k, m_rwkv_gn_w, m_rwkv_gn_b, m_fox_q_norm_g, m_fox_k_norm_g, m_fox_f_bias, m_w_branch_a, m_w_branch_b, m_w_o, m_norm2_g, m_w_gate_up, m_w_down, v_meta_tokens, v_norm1_g, v_w_in, v_rwkv_mu, v_rwkv_w0, v_rwkv_w2, v_rwkv_a0, v_rwkv_a2, v_rwkv_g2, v_rwkv_k_k, v_rwkv_k_a, v_rwkv_r_k, v_rwkv_gn_w, v_rwkv_gn_b, v_fox_q_norm_g, v_fox_k_norm_g, v_fox_f_bias, v_w_branch_a, v_w_branch_b, v_w_o, v_norm2_g, v_w_gate_up, v_w_down):
    args = dict(locals())
    shapes = {n: args[n].shape for n in WEIGHTS}

    def drop_depth(t, n):
        if n == "meta_tokens":
            return t
        if n == "rwkv_r_k":
            return t.reshape(1, -1)
        return t.reshape(t.shape[1:]) if t.ndim == 3 else t

    wts = {n: drop_depth(args[n], n) for n in WEIGHTS}
    mom1 = {n: drop_depth(args["m_" + n], n) for n in WEIGHTS}
    mom2 = {n: drop_depth(args["v_" + n], n) for n in WEIGHTS}
    loss, grad_x, grads, delta, new_m, new_v = _step(x[0], loss_target[0], wts, mom1, mom2)
    outs = [loss, grad_x[None]]
    for d in (grads, delta, new_m, new_v):
        outs += [d[n].reshape(shapes[n]) for n in WEIGHTS]
    return tuple(outs)
```
